```python
import math
import jax, jax.numpy as jnp
from jax import lax
import numpy as np

D_MODEL = 2048
BATCH = 4
SEQ = 4096
DEPTH = 2

HEAD_DIM = 128
N_HEADS = D_MODEL // HEAD_DIM
A_Q_HEADS = 6
A_KV_HEADS = 2
CMP_BLOCK = 32
CMP_STRIDE = 16
SLC_BLOCK = 64
SLC_TOPK = 16
NSA_WINDOW = 512
SLC_Q_CHUNK = 64
FORCED_SCORE = 1.0e4
B_Q_HEADS = 4
B_KV_HEADS = 2
SWA_WINDOW = 128
DIL_PAIRS = ((128, 1), (512, 4), (2048, 16))
C_HEADS_PER_PAIR = 2
C_Q_HEADS = C_HEADS_PER_PAIR * len(DIL_PAIRS)
C_KV_HEADS = len(DIL_PAIRS)
REL_BUCKETS = 32
REL_MAX_EXACT = 16
REL_MAX_DIST = 2048
BAND_BLOCK = 128
SCALE = HEAD_DIM ** -0.5
N_QK_NORMS = 8
D_FF = -(-8 * D_MODEL // (3 * 256)) * 256
SPLIT_SIZES = ((A_Q_HEADS * HEAD_DIM,) + (A_KV_HEADS * HEAD_DIM,) * 6 + (A_Q_HEADS * 3,)
               + (B_Q_HEADS * HEAD_DIM, B_KV_HEADS * HEAD_DIM, B_KV_HEADS * HEAD_DIM)
               + (C_Q_HEADS * HEAD_DIM, C_KV_HEADS * HEAD_DIM, C_KV_HEADS * HEAD_DIM))
N_IN = sum(SPLIT_SIZES)

kernel_name = "hybrid_nsa_swa_sink_dilated_block"


def rms_norm(x, g, eps=1e-6):
    xf = x.astype(jnp.float32)
    y = xf * lax.rsqrt(jnp.mean(xf * xf, axis=-1, keepdims=True) + eps)
    return (y * g.astype(jnp.float32)).astype(x.dtype)


def rel_bucket(dist):
    dist = jnp.maximum(dist, 0)
    far = jnp.maximum(dist, REL_MAX_EXACT).astype(jnp.float32)
    log_b = REL_MAX_EXACT + (jnp.log(far / REL_MAX_EXACT) / math.log(REL_MAX_DIST / REL_MAX_EXACT)
                             * (REL_BUCKETS - REL_MAX_EXACT)).astype(jnp.int32)
    log_b = jnp.minimum(log_b, REL_BUCKETS - 1)
    return jnp.where(dist < REL_MAX_EXACT, dist, log_b)


def banded_attention(q, k, v, head_bias, max_dist, dist_scale=1, sinks=None):
    n, L, hq, hd = q.shape
    hk = k.shape[2]
    grp = hq // hk
    nb = -(-L // BAND_BLOCK)
    Lp = nb * BAND_BLOCK
    n_prev = -(-max_dist // BAND_BLOCK)
    W = (n_prev + 1) * BAND_BLOCK
    pad_end = Lp - L
    qb = jnp.pad(q, ((0, 0), (0, pad_end), (0, 0), (0, 0))).reshape(n, nb, BAND_BLOCK, hk, grp, hd)
    kv_pad = ((0, 0), (n_prev * BAND_BLOCK, pad_end), (0, 0), (0, 0))
    kb = jnp.pad(k, kv_pad).reshape(n, nb + n_prev, BAND_BLOCK, hk, hd)
    vb = jnp.pad(v, kv_pad).reshape(n, nb + n_prev, BAND_BLOCK, hk, hd)
    kw = jnp.concatenate([kb[:, s:s + nb] for s in range(n_prev + 1)], axis=2)
    vw = jnp.concatenate([vb[:, s:s + nb] for s in range(n_prev + 1)], axis=2)
    qpos = jnp.arange(nb)[:, None] * BAND_BLOCK + jnp.arange(BAND_BLOCK)[None, :]
    kpos = (jnp.arange(nb)[:, None] - n_prev) * BAND_BLOCK + jnp.arange(W)[None, :]
    dist = qpos[:, :, None] - kpos[:, None, :]
    valid = (dist >= 0) & (dist <= max_dist) & (kpos[:, None, :] >= 0)
    bias = head_bias.astype(jnp.float32)[rel_bucket(dist * dist_scale)]
    bias = bias.reshape(nb, BAND_BLOCK, W, hk, grp).transpose(0, 3, 4, 1, 2)
    s = jnp.einsum('nbqkgd,nbckd->nbkgqc', qb, kw, preferred_element_type=jnp.float32) * SCALE + bias
    s = jnp.where(valid[:, None, None], s, -jnp.inf)
    lse = jax.nn.logsumexp(s, axis=-1)
    if sinks is None:
        total = lse
    else:
        total = jnp.logaddexp(lse, sinks.astype(jnp.float32).reshape(1, 1, hk, grp, 1))
    p = jnp.exp(s - total[..., None])
    out = jnp.einsum('nbkgqc,nbckd->nbqkgd', p.astype(v.dtype), vw).reshape(n, Lp, hq, hd)[:, :L]
    lse = lse.transpose(0, 1, 4, 2, 3).reshape(n, Lp, hq)[:, :L]
    return out.astype(q.dtype), lse


def nsa_compress(x, pe, w1, w2):
    B, T, hk, hd = x.shape
    n_cmp = (T - CMP_BLOCK) // CMP_STRIDE + 1
    tok = jnp.arange(n_cmp)[:, None] * CMP_STRIDE + jnp.arange(CMP_BLOCK)[None, :]
    blk = x[:, tok] + pe[:, None, :].astype(x.dtype)
    blk = blk.transpose(0, 1, 3, 2, 4).reshape(B, n_cmp, hk, CMP_BLOCK * hd)
    return jax.nn.gelu(blk @ w1) @ w2


def nsa_mixer(q, k_cmp, v_cmp, k_slc, v_slc, k_win, v_win, gates, k_cmp_gain, cmp_pe, cmp_w1, cmp_w2, head_bias):
    B, T, hq, hd = q.shape
    hk = k_cmp.shape[2]
    grp = hq // hk
    qg = q.reshape(B, T, hk, grp, hd)
    t_pos = jnp.arange(T)
    kc = rms_norm(nsa_compress(k_cmp, cmp_pe[0], cmp_w1[0], cmp_w2[0]), k_cmp_gain)
    vc = nsa_compress(v_cmp, cmp_pe[1], cmp_w1[1], cmp_w2[1])
    n_cmp = kc.shape[1]
    c_end = jnp.arange(n_cmp) * CMP_STRIDE + CMP_BLOCK - 1
    dist_c = t_pos[:, None] - c_end[None, :]
    bias_c = head_bias.astype(jnp.float32)[rel_bucket(dist_c)].transpose(2, 0, 1).reshape(hk, grp, T, n_cmp)
    s_c = jnp.einsum('btkgd,bnkd->bkgtn', qg, kc, preferred_element_type=jnp.float32) * SCALE + bias_c
    s_c = jnp.where(dist_c >= 0, s_c, -jnp.inf)
    m = jnp.max(s_c, axis=-1, keepdims=True)
    e = jnp.exp(s_c - jnp.where(jnp.isfinite(m), m, 0.0))
    den = jnp.sum(e, axis=-1, keepdims=True)
    p_c = e / jnp.where(den > 0, den, 1.0)
    o_cmp = jnp.einsum('bkgtn,bnkd->btkgd', p_c, vc.astype(jnp.float32)).reshape(B, T, hq, hd)
    n_slc = T // SLC_BLOCK
    k_sel = min(SLC_TOPK, n_slc)
    c0 = np.arange(n_cmp)[:, None] * CMP_STRIDE
    s0 = np.arange(n_slc)[None, :] * SLC_BLOCK
    overlap = np.clip(np.minimum(c0 + CMP_BLOCK, s0 + SLC_BLOCK) - np.maximum(c0, s0), 0, None) / CMP_BLOCK
    imp = jnp.einsum('bkgtn,nj->bktj', p_c, jnp.asarray(overlap, dtype=jnp.float32))
    blk_id = jnp.arange(n_slc)[None, :]
    cur = (t_pos // SLC_BLOCK)[:, None]
    forced = (blk_id == 0) | (blk_id == cur) | (blk_id == cur - 1)
    causal_blk = blk_id * SLC_BLOCK <= t_pos[:, None]
    imp = jnp.where(causal_blk, jnp.where(forced, FORCED_SCORE, imp), -jnp.inf)
    _, idx = lax.top_k(imp, k_sel)
    kb = k_slc.reshape(B, n_slc, SLC_BLOCK, hk, hd).transpose(0, 3, 1, 2, 4)
    vb = v_slc.reshape(B, n_slc, SLC_BLOCK, hk, hd).transpose(0, 3, 1, 2, 4)
    n_chunk = T // SLC_Q_CHUNK
    q_ch = qg.reshape(B, n_chunk, SLC_Q_CHUNK, hk, grp, hd).transpose(1, 0, 3, 2, 4, 5)
    idx_ch = idx.reshape(B, hk, n_chunk, SLC_Q_CHUNK, k_sel).transpose(2, 0, 1, 3, 4)
    starts = jnp.arange(n_chunk) * SLC_Q_CHUNK
    bi = jnp.arange(B)[:, None, None, None]
    ki = jnp.arange(hk)[None, :, None, None]
    hb = head_bias.astype(jnp.float32).reshape(REL_BUCKETS, hk, grp).transpose(1, 0, 2)

    def chunk_fn(args):
        qc, ic, t0 = args
        kg = kb[bi, ki, ic]
        vg = vb[bi, ki, ic]
        tq = t0 + jnp.arange(SLC_Q_CHUNK)
        kpos = ic[..., None] * SLC_BLOCK + jnp.arange(SLC_BLOCK)
        dist = tq[None, None, :, None, None] - kpos
        bias = hb[ki[..., None], rel_bucket(dist)].transpose(0, 1, 2, 5, 3, 4)
        s = jnp.einsum('bkqgd,bkqjpd->bkqgjp', qc, kg, preferred_element_type=jnp.float32) * SCALE + bias
        s = jnp.where(dist[:, :, :, None] >= 0, s, -jnp.inf)
        p = jax.nn.softmax(s.reshape(s.shape[:4] + (k_sel * SLC_BLOCK,)), axis=-1).reshape(s.shape)
        return jnp.einsum('bkqgjp,bkqjpd->bkqgd', p.astype(vg.dtype), vg)

    o_slc = lax.map(chunk_fn, (q_ch, idx_ch, starts))
    o_slc = o_slc.transpose(1, 0, 3, 2, 4, 5).reshape(B, T, hq, hd)
    o_win, _ = banded_attention(q, k_win, v_win, head_bias, NSA_WINDOW - 1)
    g = jax.nn.sigmoid(gates.astype(jnp.float32))
    o = (g[..., 0:1] * o_cmp + g[..., 1:2] * o_slc.astype(jnp.float32) + g[..., 2:3] * o_win.astype(jnp.float32))
    return o.astype(q.dtype)


def strided_window_attention(q, k, v, head_bias, window, dilation):
    B, T, hq, hd = q.shape
    Tp = -(-T // dilation) * dilation
    Ls = Tp // dilation

    def to_sub(t):
        t = jnp.pad(t, ((0, 0), (0, Tp - T), (0, 0), (0, 0)))
        h = t.shape[2]
        return t.reshape(B, Ls, dilation, h, hd).transpose(0, 2, 1, 3, 4).reshape(B * dilation, Ls, h, hd)

    o, lse = banded_attention(to_sub(q), to_sub(k), to_sub(v), head_bias, window // dilation, dist_scale=dilation)
    o = o.reshape(B, dilation, Ls, hq, hd).transpose(0, 2, 1, 3, 4).reshape(B, Tp, hq, hd)[:, :T]
    lse = lse.reshape(B, dilation, Ls, hq).transpose(0, 2, 1, 3).reshape(B, Tp, hq)[:, :T]
    return o, lse


def dilated_mixer(q, k, v, head_bias):
    B, T, _, hd = q.shape
    outs, lses = [], []
    for g, (w, d) in enumerate(DIL_PAIRS):
        sl = slice(g * C_HEADS_PER_PAIR, (g + 1) * C_HEADS_PER_PAIR)
        o, lse = strided_window_attention(q[:, :, sl], k[:, :, g:g + 1], v[:, :, g:g + 1], head_bias[:, sl], w, d)
        outs.append(o)
        lses.append(lse)
    alpha = jax.nn.softmax(jnp.stack(lses, axis=2), axis=2)
    o = jnp.stack(outs, axis=2).astype(jnp.float32) * alpha[..., None]
    return o.reshape(B, T, C_Q_HEADS, hd).astype(q.dtype)


def setup_inputs(seed: int = 0) -> dict:
    key = jax.random.key(seed)
    ks = jax.random.split(key, 14)
    f = jnp.float32
    nrm = jax.random.normal
    return {
        "x": nrm(ks[0], (BATCH, SEQ, D_MODEL), f),
        "norm_attn": 1.0 + 0.01 * nrm(ks[1], (DEPTH, D_MODEL), f),
        "w_in": nrm(ks[2], (DEPTH, D_MODEL, N_IN), f) * D_MODEL ** -0.5,
        "qk_gain": 1.0 + 0.01 * nrm(ks[3], (DEPTH, N_QK_NORMS, HEAD_DIM), f),
        "cmp_pe": 0.02 * nrm(ks[4], (DEPTH, 2, CMP_BLOCK, HEAD_DIM), f),
        "cmp_w1": nrm(ks[5], (DEPTH, 2, CMP_BLOCK * HEAD_DIM, HEAD_DIM), f) * (CMP_BLOCK * HEAD_DIM) ** -0.5,
        "cmp_w2": nrm(ks[6], (DEPTH, 2, HEAD_DIM, HEAD_DIM), f) * HEAD_DIM ** -0.5,
        "sinks": 0.5 * nrm(ks[7], (DEPTH, B_Q_HEADS), f),
        "rel_bias": 0.5 * nrm(ks[8], (REL_BUCKETS, N_HEADS), f),
        "w_out": nrm(ks[9], (DEPTH, D_MODEL, D_MODEL), f) * D_MODEL ** -0.5,
        "norm_ffn": 1.0 + 0.01 * nrm(ks[10], (DEPTH, D_MODEL), f),
        "w_gate": nrm(ks[11], (DEPTH, D_MODEL, D_FF), f) * D_MODEL ** -0.5,
        "w_up": nrm(ks[12], (DEPTH, D_MODEL, D_FF), f) * D_MODEL ** -0.5,
        "w_down": nrm(ks[13], (DEPTH, D_FF, D_MODEL), f) * D_FF ** -0.5,
    }


def reference(x, norm_attn, w_in, qk_gain, cmp_pe, cmp_w1, cmp_w2, sinks, rel_bias, w_out, norm_ffn, w_gate, w_up, w_down):
    B, T, _ = x.shape
    offsets = np.cumsum(SPLIT_SIZES)[:-1].tolist()
    bias_a = rel_bias[:, :A_Q_HEADS]
    bias_b = rel_bias[:, A_Q_HEADS:A_Q_HEADS + B_Q_HEADS]
    bias_c = rel_bias[:, A_Q_HEADS + B_Q_HEADS:]

    def heads(t, n):
        return t.reshape(B, T, n, HEAD_DIM)

    for l in range(DEPTH):
        g = qk_gain[l]
        h = rms_norm(x, norm_attn[l])
        proj = jnp.einsum('btd,dn->btn', h, w_in[l])
        (qa, kca, vca, ksa, vsa, kwa, vwa, ga, qb, kb, vb, qc, kc, vc) = jnp.split(proj, offsets, axis=-1)
        o_a = nsa_mixer(rms_norm(heads(qa, A_Q_HEADS), g[0]),
                        heads(kca, A_KV_HEADS), heads(vca, A_KV_HEADS),
                        rms_norm(heads(ksa, A_KV_HEADS), g[2]), heads(vsa, A_KV_HEADS),
                        rms_norm(heads(kwa, A_KV_HEADS), g[3]), heads(vwa, A_KV_HEADS),
                        ga.reshape(B, T, A_Q_HEADS, 3), g[1], cmp_pe[l], cmp_w1[l], cmp_w2[l], bias_a)
        o_b, _ = banded_attention(rms_norm(heads(qb, B_Q_HEADS), g[4]), rms_norm(heads(kb, B_KV_HEADS), g[5]),
                                  heads(vb, B_KV_HEADS), bias_b, SWA_WINDOW - 1, sinks=sinks[l])
        o_c = dilated_mixer(rms_norm(heads(qc, C_Q_HEADS), g[6]), rms_norm(heads(kc, C_KV_HEADS), g[7]),
                            heads(vc, C_KV_HEADS), bias_c)
        mix = jnp.concatenate([o_a.reshape(B, T, -1), o_b.reshape(B, T, -1), o_c.reshape(B, T, -1)], axis=-1)
        x = x + jnp.einsum('btm,md->btd', mix.astype(x.dtype), w_out[l])
        h = rms_norm(x, norm_ffn[l])
        x = x + jnp.einsum('btf,fd->btd', jax.nn.silu(h @ w_gate[l]) * (h @ w_up[l]), w_down[l])
    return x
```

```python
import functools
import math

import numpy as np
import jax
import jax.numpy as jnp
from jax import lax
from jax.experimental import pallas as pl
from jax.experimental.pallas import tpu as pltpu

F32 = jnp.float32
BF16 = jnp.bfloat16

HEAD_DIM = 128
LANES = 128
QBLK = 128
A_Q_HEADS, A_KV_HEADS = 6, 2
A_GRP = A_Q_HEADS // A_KV_HEADS
B_Q_HEADS, B_KV_HEADS = 4, 2
B_GRP = B_Q_HEADS // B_KV_HEADS
DIL_PAIRS = ((128, 1), (512, 4), (2048, 16))
C_GRP = 2
C_Q_HEADS = C_GRP * len(DIL_PAIRS)
CMP_BLOCK, CMP_STRIDE = 32, 16
SLC_BLOCK, SLC_TOPK = 64, 16
NSA_WINDOW, SWA_WINDOW = 512, 128
FORCED_SCORE = 1.0e4
REL_BUCKETS, REL_MAX_EXACT, REL_MAX_DIST = 32, 16, 2048
SCALE = HEAD_DIM ** -0.5
EPS = 1e-6
NEG = -1e30
VMEM_LIMIT = 56 * 1024 * 1024

COL_QA, COL_KCA, COL_VCA, COL_KSA, COL_VSA, COL_KWA, COL_VWA = 0, 6, 8, 10, 12, 14, 16
COL_QB, COL_KB, COL_VB, COL_QC, COL_KC, COL_VC = 18, 22, 24, 26, 32, 35
N_MAIN_BLOCKS = 38
N_MAIN = N_MAIN_BLOCKS * LANES
GATE_START = 2304
N_GATES = A_Q_HEADS * 3


def _bucket_starts():
    d = np.arange(0, 1 << 17)
    out = []
    for dt in (np.float32, np.float64):
        far = np.maximum(d, REL_MAX_EXACT).astype(dt)
        lb = REL_MAX_EXACT + (np.log(far / dt(REL_MAX_EXACT)) / dt(math.log(REL_MAX_DIST / REL_MAX_EXACT))
                              * dt(REL_BUCKETS - REL_MAX_EXACT)).astype(np.int64)
        out.append(np.where(d < REL_MAX_EXACT, d, np.minimum(lb, REL_BUCKETS - 1)))
    assert (out[0] == out[1]).all() and (np.diff(out[0]) >= 0).all()
    return [int(np.argmax(out[0] >= b)) for b in range(REL_BUCKETS)]


BUCKET_START = _bucket_starts()
SAT_DIST = BUCKET_START[REL_BUCKETS - 1]


def _cparams(sem, vmem=VMEM_LIMIT):
    return pltpu.CompilerParams(dimension_semantics=sem, vmem_limit_bytes=vmem)


def _bias_tile_kernel(tab_ref, o_ref, *, head0, width, kstride, koff, dscale, max_dist, ncols):
    h = pl.program_id(0) + head0
    t = pl.program_id(1)
    r = lax.broadcasted_iota(jnp.int32, (QBLK, width), 0)
    c = lax.broadcasted_iota(jnp.int32, (QBLK, width), 1)
    dist = t * QBLK + r - kstride * c - koff
    d = dist * dscale
    val = jnp.full((QBLK, width), tab_ref[REL_BUCKETS - 1, h], F32)
    for b in range(REL_BUCKETS - 2, -1, -1):
        val = jnp.where(d < BUCKET_START[b + 1], tab_ref[b, h], val)
    val = jnp.where(dist >= 0, val, NEG)
    if max_dist is not None:
        val = jnp.where(dist <= max_dist, val, NEG)
    if ncols < width:
        val = jnp.where(c < ncols, val, NEG)
    o_ref[0, 0] = val


def _bias_tiles(rel_bias, head0, nheads, ntiles, *, width=LANES, kstride=1, koff=0, dscale=1,
                max_dist=None, ncols=None):
    ncols = width if ncols is None else ncols
    kern = functools.partial(_bias_tile_kernel, head0=head0, width=width, kstride=kstride, koff=koff,
                             dscale=dscale, max_dist=max_dist, ncols=ncols)
    return pl.pallas_call(
        kern,
        grid=(nheads, ntiles),
        in_specs=[pl.BlockSpec(memory_space=pltpu.SMEM)],
        out_specs=pl.BlockSpec((1, 1, QBLK, width), lambda h, t: (h, t, 0, 0)),
        out_shape=jax.ShapeDtypeStruct((nheads, ntiles, QBLK, width), F32),
        compiler_params=_cparams(("parallel", "parallel")),
        name="bias_tiles",
    )(rel_bias)


def _rms(x, w):
    ms = jnp.mean(x * x, axis=-1, keepdims=True)
    return x * lax.rsqrt(ms + EPS) * w


def _proj_kernel(x_ref, nw_ref, w_ref, gain_ref, flag_ref, o_ref, h_ref):
    @pl.when(pl.program_id(1) == 0)
    def _():
        h_ref[...] = _rms(x_ref[...], nw_ref[...]).astype(BF16)

    acc = jnp.dot(h_ref[...], w_ref[...], preferred_element_type=F32)
    for c in range(acc.shape[1] // LANES):
        sl = slice(c * LANES, (c + 1) * LANES)
        y = acc[:, sl]
        ms = jnp.mean(y * y, axis=-1, keepdims=True)
        sc = jnp.where(flag_ref[:, sl] > 0, lax.rsqrt(ms + EPS), 1.0)
        o_ref[:, sl] = (y * sc * gain_ref[:, sl]).astype(o_ref.dtype)


def _proj(x2d, norm_w, w, gain, flag, *, out_dtype, tm, tn):
    m, d = x2d.shape
    n = w.shape[1]
    return pl.pallas_call(
        _proj_kernel,
        grid=(m // tm, n // tn),
        in_specs=[
            pl.BlockSpec((tm, d), lambda i, j: (i, 0)),
            pl.BlockSpec((1, d), lambda i, j: (0, 0)),
            pl.BlockSpec((d, tn), lambda i, j: (0, j)),
            pl.BlockSpec((1, tn), lambda i, j: (0, j)),
            pl.BlockSpec((1, tn), lambda i, j: (0, j)),
        ],
        out_specs=pl.BlockSpec((tm, tn), lambda i, j: (i, j)),
        out_shape=jax.ShapeDtypeStruct((m, n), out_dtype),
        scratch_shapes=[pltpu.VMEM((tm, d), BF16)],
        compiler_params=_cparams(("parallel", "arbitrary")),
        name="in_proj",
    )(x2d, norm_w, w, gain, flag)


def _outproj_kernel(x_ref, a_ref, w_ref, o_ref):
    o_ref[...] = x_ref[...] + jnp.dot(a_ref[...], w_ref[...], preferred_element_type=F32)


def _outproj(x2d, a, w, *, tm, tn):
    m, d = x2d.shape
    k = a.shape[1]
    return pl.pallas_call(
        _outproj_kernel,
        grid=(m // tm, d // tn),
        in_specs=[
            pl.BlockSpec((tm, tn), lambda i, j: (i, j)),
            pl.BlockSpec((tm, k), lambda i, j: (i, 0)),
            pl.BlockSpec((k, tn), lambda i, j: (0, j)),
        ],
        out_specs=pl.BlockSpec((tm, tn), lambda i, j: (i, j)),
        out_shape=jax.ShapeDtypeStruct((m, d), F32),
        compiler_params=_cparams(("parallel", "parallel")),
        name="out_proj",
    )(x2d, a, w)


def _ffn_kernel(x_ref, nw_ref, wg_ref, wu_ref, wd_ref, o_ref, h_ref, acc_ref):
    f = pl.program_id(1)

    @pl.when(f == 0)
    def _():
        h_ref[...] = _rms(x_ref[...], nw_ref[...]).astype(BF16)
        acc_ref[...] = jnp.zeros_like(acc_ref)

    h = h_ref[...]
    g = jnp.dot(h, wg_ref[...], preferred_element_type=F32)
    u = jnp.dot(h, wu_ref[...], preferred_element_type=F32)
    a = (g * (1.0 / (1.0 + jnp.exp(-g))) * u).astype(BF16)
    acc_ref[...] += jnp.dot(a, wd_ref[...], preferred_element_type=F32)

    @pl.when(f == pl.num_programs(1) - 1)
    def _():
        o_ref[...] = x_ref[...] + acc_ref[...]


def _ffn(x2d, norm_w, wg, wu, wd, *, tm, tf):
    m, d = x2d.shape
    dff = wg.shape[1]
    return pl.pallas_call(
        _ffn_kernel,
        grid=(m // tm, dff // tf),
        in_specs=[
            pl.BlockSpec((tm, d), lambda i, f: (i, 0)),
            pl.BlockSpec((1, d), lambda i, f: (0, 0)),
            pl.BlockSpec((d, tf), lambda i, f: (0, f)),
            pl.BlockSpec((d, tf), lambda i, f: (0, f)),
            pl.BlockSpec((tf, d), lambda i, f: (f, 0)),
        ],
        out_specs=pl.BlockSpec((tm, d), lambda i, f: (i, 0)),
        out_shape=jax.ShapeDtypeStruct((m, d), F32),
        scratch_shapes=[pltpu.VMEM((tm, d), BF16), pltpu.VMEM((tm, d), F32)],
        compiler_params=_cparams(("parallel", "arbitrary")),
        name="ffn",
    )(x2d, norm_w, wg, wu, wd)


def _band_kernel(*refs, grp, n_off, has_sinks, with_lse):
    if has_sinks:
        sink_ref, q_ref, k_ref, v_ref, b_ref = refs[:5]
        outs = refs[5:]
    else:
        q_ref, k_ref, v_ref, b_ref = refs[:4]
        outs = refs[4:]
    o_ref = outs[0]
    g = pl.program_id(1)
    i = pl.program_id(2)
    kts, vts, tidx = [], [], []
    for off in range(n_off):
        kb = jnp.maximum(i - off, 0)
        start = pl.multiple_of(kb * QBLK, QBLK)
        kts.append(k_ref[0, pl.ds(start, QBLK), :])
        vts.append(v_ref[0, pl.ds(start, QBLK), :])
        tidx.append(jnp.where(i - off >= 0, off, n_off))
    for h in range(grp):
        sl = slice(h * HEAD_DIM, (h + 1) * HEAD_DIM)
        qh = q_ref[0, :, sl]
        s = [lax.dot_general(qh, kts[off], (((1,), (1,)), ((), ())), preferred_element_type=F32)
             + b_ref[h, tidx[off]] for off in range(n_off)]
        m = functools.reduce(jnp.maximum, [jnp.max(x, axis=-1, keepdims=True) for x in s])
        p = [jnp.exp(x - m) for x in s]
        l = functools.reduce(jnp.add, [jnp.sum(x, axis=-1, keepdims=True) for x in p])
        o = functools.reduce(jnp.add, [jnp.dot(p[off].astype(BF16), vts[off], preferred_element_type=F32)
                                       for off in range(n_off)])
        den = l
        if has_sinks:
            den = l + jnp.exp(sink_ref[g * grp + h] - m)
        o_ref[0, :, sl] = (o / den).astype(o_ref.dtype)
        if with_lse:
            outs[1][0, :, sl] = jnp.broadcast_to(m + jnp.log(l), (QBLK, HEAD_DIM))


def _band_attn(q_arr, k_arr, v_arr, bias, *, n_kv, grp, q_col, k_col, v_col, sinks=None, with_lse=False):
    n, L, _ = q_arr.shape
    n_off = bias.shape[1] - 1
    gw = grp * HEAD_DIM
    kern = functools.partial(_band_kernel, grp=grp, n_off=n_off, has_sinks=sinks is not None, with_lse=with_lse)
    in_specs = [
        pl.BlockSpec((1, QBLK, gw), lambda b, g, i: (b, i, q_col + g)),
        pl.BlockSpec((1, L, HEAD_DIM), lambda b, g, i: (b, 0, k_col + g)),
        pl.BlockSpec((1, L, HEAD_DIM), lambda b, g, i: (b, 0, v_col + g)),
        pl.BlockSpec((grp, n_off + 1, QBLK, LANES), lambda b, g, i: (g, 0, 0, 0)),
    ]
    args = [q_arr, k_arr, v_arr, bias]
    if sinks is not None:
        in_specs = [pl.BlockSpec(memory_space=pltpu.SMEM)] + in_specs
        args = [sinks] + args
    o_spec = pl.BlockSpec((1, QBLK, gw), lambda b, g, i: (b, i, g))
    o_shape = jax.ShapeDtypeStruct((n, L, n_kv * gw), F32)
    return pl.pallas_call(
        kern,
        grid=(n, n_kv, L // QBLK),
        in_specs=in_specs,
        out_specs=(o_spec, o_spec) if with_lse else o_spec,
        out_shape=(o_shape, o_shape) if with_lse else o_shape,
        compiler_params=_cparams(("parallel", "parallel", "arbitrary")),
        name="band_attn",
    )(*args)


def _compress_kernel(x_ref, pe_ref, w1_ref, w2_ref, gain_ref, o_ref, *, n_cmp):
    half = w1_ref.shape[1] // 2
    x = x_ref[0, 0, 0]
    y0 = jnp.dot(x, w1_ref[0, :half], preferred_element_type=F32)
    y1 = jnp.dot(x, w1_ref[0, half:], preferred_element_type=F32)
    pe = jnp.dot(pe_ref[0], w1_ref[0], preferred_element_type=F32)[0:1]
    rows = x.shape[0]
    c = y0 + pltpu.roll(y1, rows - 1, 0) + pe
    gl = 0.5 * c * (1.0 + jnp.tanh(math.sqrt(2.0 / math.pi) * (c + 0.044715 * (c * c * c))))
    out = jnp.dot(gl.astype(BF16), w2_ref[0], preferred_element_type=F32)
    out = jnp.where(pl.program_id(0) == 0, _rms(out, gain_ref[...]), out)
    valid = lax.broadcasted_iota(jnp.int32, out.shape, 0) < n_cmp
    o_ref[0, 0, 0] = jnp.where(valid, out, 0.0).astype(o_ref.dtype)


def _compress(xr, pe, w1, w2, gain, n_cmp):
    _, B, hk, rows, wide = xr.shape
    return pl.pallas_call(
        functools.partial(_compress_kernel, n_cmp=n_cmp),
        grid=(2, B, hk),
        in_specs=[
            pl.BlockSpec((1, 1, 1, rows, wide), lambda s, b, g: (s, b, g, 0, 0)),
            pl.BlockSpec((1, 8, 2 * wide), lambda s, b, g: (s, 0, 0)),
            pl.BlockSpec((1, 2 * wide, HEAD_DIM), lambda s, b, g: (s, 0, 0)),
            pl.BlockSpec((1, HEAD_DIM, HEAD_DIM), lambda s, b, g: (s, 0, 0)),
            pl.BlockSpec((1, HEAD_DIM), lambda s, b, g: (0, 0)),
        ],
        out_specs=pl.BlockSpec((1, 1, 1, rows, HEAD_DIM), lambda s, b, g: (s, b, g, 0, 0)),
        out_shape=jax.ShapeDtypeStruct((2, B, hk, rows, HEAD_DIM), BF16),
        compiler_params=_cparams(("arbitrary", "arbitrary", "arbitrary")),
        name="nsa_compress",
    )(xr, pe, w1, w2, gain)


def _cmp_select_kernel(q_ref, kc_ref, vc_ref, b_ref, ovl_ref, o_ref, sel_ref, *, n_slc):
    i = pl.program_id(2)
    kc = kc_ref[0, 0, 0]
    vc = vc_ref[0, 0, 0]
    imp = jnp.zeros((QBLK, LANES), F32)
    for h in range(A_GRP):
        sl = slice(h * HEAD_DIM, (h + 1) * HEAD_DIM)
        s = lax.dot_general(q_ref[0, :, sl], kc, (((1,), (1,)), ((), ())), preferred_element_type=F32) + b_ref[h]
        m = jnp.max(s, axis=-1, keepdims=True)
        e = jnp.exp(s - jnp.where(m > 0.5 * NEG, m, 0.0))
        den = jnp.sum(e, axis=-1, keepdims=True)
        p = (e / jnp.where(den > 0, den, 1.0)).astype(BF16)
        o_ref[0, :, sl] = jnp.dot(p, vc, preferred_element_type=F32)
        imp = imp + jnp.dot(p, ovl_ref[...], preferred_element_type=F32)
    t = i * QBLK + lax.broadcasted_iota(jnp.int32, (QBLK, LANES), 0)
    blk = lax.broadcasted_iota(jnp.int32, (QBLK, LANES), 1)
    cur = t // SLC_BLOCK
    imp = jnp.where(blk == 0, FORCED_SCORE, imp)
    imp = jnp.where(blk == cur, FORCED_SCORE, imp)
    imp = jnp.where(blk == cur - 1, FORCED_SCORE, imp)
    imp = jnp.where(blk * SLC_BLOCK <= t, imp, NEG)
    imp = jnp.where(blk < n_slc, imp, 2.0 * NEG)
    imp_t = imp.T
    nrow = min(LANES, -(-n_slc // 8) * 8)
    cand = imp_t[:nrow]
    row_id = lax.broadcasted_iota(jnp.int32, (nrow, QBLK), 0)
    rank = jnp.zeros((nrow, QBLK), F32)
    for j in range(n_slc):
        other = imp_t[j:j + 1, :]
        ge = jnp.where(other >= cand, 1.0, 0.0)
        gt = jnp.where(other > cand, 1.0, 0.0)
        rank = rank + jnp.where(row_id > j, ge, gt)
    sel_t = jnp.where(rank < float(min(SLC_TOPK, n_slc)), 1.0, 0.0)
    if nrow < LANES:
        sel_t = jnp.concatenate([sel_t, jnp.zeros((LANES - nrow, QBLK), F32)], axis=0)
    sel_ref[0, 0] = sel_t.T.astype(sel_ref.dtype)


def _cmp_select(proj3, kvc, bias_c, ovl, n_slc):
    B, T, _ = proj3.shape
    ncp = kvc.shape[3]
    return pl.pallas_call(
        functools.partial(_cmp_select_kernel, n_slc=n_slc),
        grid=(B, A_KV_HEADS, T // QBLK),
        in_specs=[
            pl.BlockSpec((1, QBLK, A_GRP * HEAD_DIM), lambda b, g, i: (b, i, g)),
            pl.BlockSpec((1, 1, 1, ncp, HEAD_DIM), lambda b, g, i: (0, b, g, 0, 0)),
            pl.BlockSpec((1, 1, 1, ncp, HEAD_DIM), lambda b, g, i: (1, b, g, 0, 0)),
            pl.BlockSpec((A_GRP, QBLK, ncp), lambda b, g, i: (g, i, 0)),
            pl.BlockSpec((ncp, LANES), lambda b, g, i: (0, 0)),
        ],
        out_specs=(
            pl.BlockSpec((1, QBLK, A_GRP * HEAD_DIM), lambda b, g, i: (b, i, g)),
            pl.BlockSpec((1, 1, QBLK, LANES), lambda b, g, i: (b, g, i, 0)),
        ),
        out_shape=(
            jax.ShapeDtypeStruct((B, T, A_Q_HEADS * HEAD_DIM), F32),
            jax.ShapeDtypeStruct((B, A_KV_HEADS, T, LANES), BF16),
        ),
        compiler_params=_cparams(("parallel", "parallel", "arbitrary")),
        name="nsa_cmp_select",
    )(proj3, kvc, kvc, bias_c, ovl)


def _slc_kernel(q_ref, k_ref, v_ref, sel_ref, e_ref, b_ref, o_ref, m_ref, l_ref, acc_ref, *, n_tiles):
    i = pl.program_id(2)
    m_ref[...] = jnp.full(m_ref.shape, NEG, F32)
    l_ref[...] = jnp.zeros(l_ref.shape, F32)
    acc_ref[...] = jnp.zeros(acc_ref.shape, F32)
    sel = sel_ref[0, 0]

    def body(kb, carry):
        start = pl.multiple_of(kb * QBLK, QBLK)
        kt = k_ref[0, pl.ds(start, QBLK), :]
        vt = v_ref[0, pl.ds(start, QBLK), :]
        chosen = jnp.dot(sel, e_ref[:, pl.ds(start, QBLK)], preferred_element_type=F32)
        madd = (chosen - 1.0) * (-NEG)
        tidx = jnp.minimum(i - kb, n_tiles - 1)
        for h in range(A_GRP):
            sl = slice(h * HEAD_DIM, (h + 1) * HEAD_DIM)
            s = lax.dot_general(q_ref[0, :, sl], kt, (((1,), (1,)), ((), ())), preferred_element_type=F32)
            s = s + b_ref[h, tidx] + madd
            m_old = m_ref[h]
            m_new = jnp.maximum(m_old, jnp.max(s, axis=-1, keepdims=True))
            alpha = jnp.exp(m_old - m_new)
            p = jnp.exp(s - m_new)
            l_ref[h] = alpha * l_ref[h] + jnp.sum(p, axis=-1, keepdims=True)
            acc_ref[h] = alpha * acc_ref[h] + jnp.dot(p.astype(BF16), vt, preferred_element_type=F32)
            m_ref[h] = m_new
        return carry

    lax.fori_loop(0, i + 1, body, 0)
    for h in range(A_GRP):
        o_ref[0, :, h * HEAD_DIM:(h + 1) * HEAD_DIM] = acc_ref[h] / l_ref[h]


def _slc_attn(proj3, sel, expand, bias):
    B, T, _ = proj3.shape
    n_tiles = bias.shape[1]
    return pl.pallas_call(
        functools.partial(_slc_kernel, n_tiles=n_tiles),
        grid=(B, A_KV_HEADS, T // QBLK),
        in_specs=[
            pl.BlockSpec((1, QBLK, A_GRP * HEAD_DIM), lambda b, g, i: (b, i, g)),
            pl.BlockSpec((1, T, HEAD_DIM), lambda b, g, i: (b, 0, COL_KSA + g)),
            pl.BlockSpec((1, T, HEAD_DIM), lambda b, g, i: (b, 0, COL_VSA + g)),
            pl.BlockSpec((1, 1, QBLK, LANES), lambda b, g, i: (b, g, i, 0)),
            pl.BlockSpec((LANES, T), lambda b, g, i: (0, 0)),
            pl.BlockSpec((A_GRP, n_tiles, QBLK, LANES), lambda b, g, i: (g, 0, 0, 0)),
        ],
        out_specs=pl.BlockSpec((1, QBLK, A_GRP * HEAD_DIM), lambda b, g, i: (b, i, g)),
        out_shape=jax.ShapeDtypeStruct((B, T, A_Q_HEADS * HEAD_DIM), F32),
        scratch_shapes=[
            pltpu.VMEM((A_GRP, QBLK, 1), F32),
            pltpu.VMEM((A_GRP, QBLK, 1), F32),
            pltpu.VMEM((A_GRP, QBLK, HEAD_DIM), F32),
        ],
        compiler_params=_cparams(("parallel", "parallel", "arbitrary")),
        name="nsa_slc_attn",
    )(proj3, proj3, proj3, sel, expand, bias)


def _combine_kernel(gate_ref, ocmp_ref, oslc_ref, owin_ref, ob_ref, *rest):
    npair = len(DIL_PAIRS)
    oc_refs, lse_refs, o_ref = rest[:npair], rest[npair:2 * npair], rest[2 * npair]
    gate = 1.0 / (1.0 + jnp.exp(-gate_ref[...]))
    for h in range(A_Q_HEADS):
        sl = slice(h * HEAD_DIM, (h + 1) * HEAD_DIM)
        o = (gate[:, 3 * h:3 * h + 1] * ocmp_ref[:, sl] + gate[:, 3 * h + 1:3 * h + 2] * oslc_ref[:, sl]
             + gate[:, 3 * h + 2:3 * h + 3] * owin_ref[:, sl])
        o_ref[:, sl] = o.astype(o_ref.dtype)
    base = A_Q_HEADS * HEAD_DIM
    width = B_Q_HEADS * HEAD_DIM
    o_ref[:, base:base + width] = ob_ref[...].astype(o_ref.dtype)
    base += width
    lses = [r[...] for r in lse_refs]
    mx = functools.reduce(jnp.maximum, lses)
    ws = [jnp.exp(x - mx) for x in lses]
    tot = functools.reduce(jnp.add, ws)
    for gidx in range(npair):
        w = C_GRP * HEAD_DIM
        o_ref[:, base + gidx * w:base + (gidx + 1) * w] = (oc_refs[gidx][...] * (ws[gidx] / tot)).astype(o_ref.dtype)


def _combine(gates, o_cmp, o_slc, o_win, o_b, o_cs, lses, *, tm):
    m = gates.shape[0]
    d_out = (A_Q_HEADS + B_Q_HEADS + C_Q_HEADS) * HEAD_DIM
    row = lambda w: pl.BlockSpec((tm, w), lambda i: (i, 0))
    ins = [gates, o_cmp, o_slc, o_win, o_b, *o_cs, *lses]
    return pl.pallas_call(
        _combine_kernel,
        grid=(m // tm,),
        in_specs=[row(a.shape[1]) for a in ins],
        out_specs=row(d_out),
        out_shape=jax.ShapeDtypeStruct((m, d_out), BF16),
        compiler_params=_cparams(("parallel",)),
        name="combine",
    )(*ins)


def _to_sub(a, d):
    B, T, c = a.shape
    return a.reshape(B, T // d, d, c).transpose(0, 2, 1, 3).reshape(B * d, T // d, c)


def _from_sub(a, d, B):
    _, ls, c = a.shape
    return a.reshape(B, d, ls, c).transpose(0, 2, 1, 3).reshape(B, ls * d, c)


def _mixers(proj3, gates, layer_params, tables):
    B, T, _ = proj3.shape
    cmp_pe, cmp_w1, cmp_w2, kc_gain, sinks = layer_params
    n_cmp = (T - CMP_BLOCK) // CMP_STRIDE + 1
    n_slc = T // SLC_BLOCK
    ncp = T // CMP_STRIDE

    def cmp_rows(col):
        a = proj3[:, :, col * LANES:(col + A_KV_HEADS) * LANES].reshape(B, ncp, CMP_STRIDE, A_KV_HEADS, HEAD_DIM)
        return a.transpose(0, 3, 1, 2, 4).reshape(B, A_KV_HEADS, ncp, CMP_STRIDE * HEAD_DIM)

    xr = jnp.stack([cmp_rows(COL_KCA), cmp_rows(COL_VCA)])
    pe = jnp.broadcast_to(cmp_pe.reshape(2, 1, CMP_BLOCK * HEAD_DIM), (2, 8, CMP_BLOCK * HEAD_DIM)).astype(BF16)
    kvc = _compress(xr, pe, cmp_w1.astype(BF16), cmp_w2.astype(BF16), kc_gain.reshape(1, HEAD_DIM), n_cmp)

    o_cmp, sel = _cmp_select(proj3, kvc, tables["bias_c"], tables["ovl"], n_slc)
    o_slc = _slc_attn(proj3, sel, tables["expand"], tables["bias_slc"])
    o_win = _band_attn(proj3, proj3, proj3, tables["bias_win"], n_kv=A_KV_HEADS, grp=A_GRP,
                       q_col=COL_QA // A_GRP, k_col=COL_KWA, v_col=COL_VWA)
    o_b = _band_attn(proj3, proj3, proj3, tables["bias_b"], n_kv=B_KV_HEADS, grp=B_GRP,
                     q_col=COL_QB // B_GRP, k_col=COL_KB, v_col=COL_VB, sinks=sinks)
    o_cs, lses = [], []
    for gidx, (_, dil) in enumerate(DIL_PAIRS):
        if dil == 1:
            o, lse = _band_attn(proj3, proj3, proj3, tables["bias_c%d" % gidx], n_kv=1, grp=C_GRP,
                                q_col=COL_QC // C_GRP + gidx, k_col=COL_KC + gidx, v_col=COL_VC + gidx,
                                with_lse=True)
        else:
            qs = _to_sub(proj3[:, :, (COL_QC + C_GRP * gidx) * LANES:(COL_QC + C_GRP * (gidx + 1)) * LANES], dil)
            ks = _to_sub(proj3[:, :, (COL_KC + gidx) * LANES:(COL_KC + gidx + 1) * LANES], dil)
            vs = _to_sub(proj3[:, :, (COL_VC + gidx) * LANES:(COL_VC + gidx + 1) * LANES], dil)
            o, lse = _band_attn(qs, ks, vs, tables["bias_c%d" % gidx], n_kv=1, grp=C_GRP,
                                q_col=0, k_col=0, v_col=0, with_lse=True)
            o, lse = _from_sub(o, dil, B), _from_sub(lse, dil, B)
        o_cs.append(o.reshape(B * T, -1))
        lses.append(lse.reshape(B * T, -1))
    return _combine(gates, o_cmp.reshape(B * T, -1), o_slc.reshape(B * T, -1), o_win.reshape(B * T, -1),
                    o_b.reshape(B * T, -1), o_cs, lses, tm=256)


def _build_tables(rel_bias, T):
    n_cmp = (T - CMP_BLOCK) // CMP_STRIDE + 1
    n_slc = T // SLC_BLOCK
    ncp = T // CMP_STRIDE
    nq = T // QBLK
    tables = {}
    tables["bias_c"] = _bias_tiles(rel_bias, 0, A_Q_HEADS, nq, width=ncp, kstride=CMP_STRIDE,
                                   koff=CMP_BLOCK - 1, ncols=n_cmp).reshape(A_Q_HEADS, T, ncp)
    n_sat = -(-(SAT_DIST + QBLK - 1) // QBLK) + 1
    tables["bias_slc"] = _bias_tiles(rel_bias, 0, A_Q_HEADS, min(n_sat, nq))
    win = NSA_WINDOW - 1
    tables["bias_win"] = _bias_tiles(rel_bias, 0, A_Q_HEADS, -(-win // QBLK) + 2, max_dist=win)
    swa = SWA_WINDOW - 1
    tables["bias_b"] = _bias_tiles(rel_bias, A_Q_HEADS, B_Q_HEADS, -(-swa // QBLK) + 2, max_dist=swa)
    for gidx, (w, dil) in enumerate(DIL_PAIRS):
        md = w // dil
        tables["bias_c%d" % gidx] = _bias_tiles(rel_bias, A_Q_HEADS + B_Q_HEADS + C_GRP * gidx, C_GRP,
                                                -(-md // QBLK) + 2, dscale=dil, max_dist=md)
    c0 = np.arange(ncp)[:, None] * CMP_STRIDE
    s0 = np.arange(LANES)[None, :] * SLC_BLOCK
    ovl = np.clip(np.minimum(c0 + CMP_BLOCK, s0 + SLC_BLOCK) - np.maximum(c0, s0), 0, None) / CMP_BLOCK
    ovl = ovl * (np.arange(ncp)[:, None] < n_cmp) * (np.arange(LANES)[None, :] < n_slc)
    tables["ovl"] = jnp.asarray(ovl, BF16)
    expand = (np.arange(T)[None, :] // SLC_BLOCK) == np.arange(LANES)[:, None]
    tables["expand"] = jnp.asarray(expand, BF16)
    return tables


def _proj_gain_flag(g):
    ones = jnp.ones((HEAD_DIM,), F32)
    spec = [(g[0] * SCALE, 6, 1), (ones, 2, 0), (ones, 2, 0), (g[2], 2, 1), (ones, 2, 0), (g[3], 2, 1), (ones, 2, 0),
            (g[4] * SCALE, 4, 1), (g[5], 2, 1), (ones, 2, 0), (g[6] * SCALE, 6, 1), (g[7], 3, 1), (ones, 3, 0)]
    gain = jnp.concatenate([jnp.tile(v, n) for v, n, _ in spec]).reshape(1, N_MAIN)
    flag = np.concatenate([np.full(n * HEAD_DIM, f, np.float32) for _, n, f in spec]).reshape(1, N_MAIN)
    return gain, jnp.asarray(flag)


def kernel(x, norm_attn, w_in, qk_gain, cmp_pe, cmp_w1, cmp_w2, sinks, rel_bias, w_out, norm_ffn, w_gate, w_up, w_down):
    B, T, D = x.shape
    depth = w_in.shape[0]
    tables = _build_tables(rel_bias, T)
    x2 = x.reshape(B * T, D)
    tm = min(1024, B * T)
    for l in range(depth):
        w_main = jnp.concatenate([w_in[l][:, :GATE_START], w_in[l][:, GATE_START + N_GATES:]], axis=1).astype(BF16)
        w_g = jnp.pad(w_in[l][:, GATE_START:GATE_START + N_GATES], ((0, 0), (0, LANES - N_GATES))).astype(BF16)
        gain, flag = _proj_gain_flag(qk_gain[l])
        nw = norm_attn[l].reshape(1, D)
        proj = _proj(x2, nw, w_main, gain, flag, out_dtype=BF16, tm=tm, tn=256)
        gates = _proj(x2, nw, w_g, jnp.ones((1, LANES), F32), jnp.zeros((1, LANES), F32), out_dtype=F32, tm=tm, tn=LANES)
        mix = _mixers(proj.reshape(B, T, N_MAIN), gates,
                      (cmp_pe[l], cmp_w1[l], cmp_w2[l], qk_gain[l][1], sinks[l]), tables)
        x2 = _outproj(x2, mix, w_out[l].astype(BF16), tm=tm, tn=512)
        x2 = _ffn(x2, norm_ffn[l].reshape(1, D), w_gate[l].astype(BF16), w_up[l].astype(BF16),
                  w_down[l].astype(BF16), tm=min(512, B * T), tf=512)
    return x2.reshape(B, T, D)
```

```python
import functools
import math

import numpy as np
import jax
import jax.numpy as jnp
from jax import lax
from jax.experimental import pallas as pl
from jax.experimental.pallas import tpu as pltpu

F32 = jnp.float32
BF16 = jnp.bfloat16

HEAD_DIM = 128
LANES = 128
QBLK = 128
A_Q_HEADS, A_KV_HEADS = 6, 2
A_GRP = A_Q_HEADS // A_KV_HEADS
B_Q_HEADS, B_KV_HEADS = 4, 2
B_GRP = B_Q_HEADS // B_KV_HEADS
DIL_PAIRS = ((128, 1), (512, 4), (2048, 16))
C_GRP = 2
C_Q_HEADS = C_GRP * len(DIL_PAIRS)
CMP_BLOCK, CMP_STRIDE = 32, 16
SLC_BLOCK, SLC_TOPK = 64, 16
SLC_CHUNK = 4
NSA_WINDOW, SWA_WINDOW = 512, 128
FORCED_SCORE = 1.0e4
REL_BUCKETS, REL_MAX_EXACT, REL_MAX_DIST = 32, 16, 2048
SCALE = HEAD_DIM ** -0.5
EPS = 1e-6
NEG = -1e30
VMEM_LIMIT = 56 * 1024 * 1024

COL_QA, COL_KCA, COL_VCA, COL_KSA, COL_VSA, COL_KWA, COL_VWA = 0, 6, 8, 10, 12, 14, 16
COL_QB, COL_KB, COL_VB, COL_QC, COL_KC, COL_VC = 18, 22, 24, 26, 32, 35
N_MAIN_BLOCKS = 38
N_MAIN = N_MAIN_BLOCKS * LANES
GATE_START = 2304
N_GATES = A_Q_HEADS * 3


def _bucket_starts():
    d = np.arange(0, 1 << 17)
    out = []
    for dt in (np.float32, np.float64):
        far = np.maximum(d, REL_MAX_EXACT).astype(dt)
        lb = REL_MAX_EXACT + (np.log(far / dt(REL_MAX_EXACT)) / dt(math.log(REL_MAX_DIST / REL_MAX_EXACT))
                              * dt(REL_BUCKETS - REL_MAX_EXACT)).astype(np.int64)
        out.append(np.where(d < REL_MAX_EXACT, d, np.minimum(lb, REL_BUCKETS - 1)))
    assert (out[0] == out[1]).all() and (np.diff(out[0]) >= 0).all()
    return [int(np.argmax(out[0] >= b)) for b in range(REL_BUCKETS)]


BUCKET_START = _bucket_starts()
SAT_DIST = BUCKET_START[REL_BUCKETS - 1]


def _cparams(sem, vmem=VMEM_LIMIT):
    return pltpu.CompilerParams(dimension_semantics=sem, vmem_limit_bytes=vmem)


def _bias_tile_kernel(tab_ref, o_ref, *, head0, width, kstride, koff, dscale, max_dist, ncols):
    h = pl.program_id(0) + head0
    t = pl.program_id(1)
    r = lax.broadcasted_iota(jnp.int32, (QBLK, width), 0)
    c = lax.broadcasted_iota(jnp.int32, (QBLK, width), 1)
    dist = t * QBLK + r - kstride * c - koff
    d = dist * dscale
    val = jnp.full((QBLK, width), tab_ref[REL_BUCKETS - 1, h], F32)
    for b in range(REL_BUCKETS - 2, -1, -1):
        val = jnp.where(d < BUCKET_START[b + 1], tab_ref[b, h], val)
    val = jnp.where(dist >= 0, val, NEG)
    if max_dist is not None:
        val = jnp.where(dist <= max_dist, val, NEG)
    if ncols < width:
        val = jnp.where(c < ncols, val, NEG)
    o_ref[0, 0] = val


def _bias_tiles(rel_bias, head0, nheads, ntiles, *, width=LANES, kstride=1, koff=0, dscale=1,
                max_dist=None, ncols=None):
    ncols = width if ncols is None else ncols
    kern = functools.partial(_bias_tile_kernel, head0=head0, width=width, kstride=kstride, koff=koff,
                             dscale=dscale, max_dist=max_dist, ncols=ncols)
    return pl.pallas_call(
        kern,
        grid=(nheads, ntiles),
        in_specs=[pl.BlockSpec(memory_space=pltpu.SMEM)],
        out_specs=pl.BlockSpec((1, 1, QBLK, width), lambda h, t: (h, t, 0, 0)),
        out_shape=jax.ShapeDtypeStruct((nheads, ntiles, QBLK, width), F32),
        compiler_params=_cparams(("parallel", "parallel")),
        name="bias_tiles",
    )(rel_bias)


def _rms(x, w):
    ms = jnp.mean(x * x, axis=-1, keepdims=True)
    return x * lax.rsqrt(ms + EPS) * w


def _proj_kernel(x_ref, nw_ref, w_ref, gain_ref, flag_ref, o_ref, h_ref):
    @pl.when(pl.program_id(1) == 0)
    def _():
        h_ref[...] = _rms(x_ref[...], nw_ref[...]).astype(BF16)

    acc = jnp.dot(h_ref[...], w_ref[...], preferred_element_type=F32)
    for c in range(acc.shape[1] // LANES):
        sl = slice(c * LANES, (c + 1) * LANES)
        y = acc[:, sl]
        ms = jnp.mean(y * y, axis=-1, keepdims=True)
        sc = jnp.where(flag_ref[:, sl] > 0, lax.rsqrt(ms + EPS), 1.0)
        o_ref[:, sl] = (y * sc * gain_ref[:, sl]).astype(o_ref.dtype)


def _proj(x2d, norm_w, w, gain, flag, *, out_dtype, tm, tn):
    m, d = x2d.shape
    n = w.shape[1]
    return pl.pallas_call(
        _proj_kernel,
        grid=(m // tm, n // tn),
        in_specs=[
            pl.BlockSpec((tm, d), lambda i, j: (i, 0)),
            pl.BlockSpec((1, d), lambda i, j: (0, 0)),
            pl.BlockSpec((d, tn), lambda i, j: (0, j)),
            pl.BlockSpec((1, tn), lambda i, j: (0, j)),
            pl.BlockSpec((1, tn), lambda i, j: (0, j)),
        ],
        out_specs=pl.BlockSpec((tm, tn), lambda i, j: (i, j)),
        out_shape=jax.ShapeDtypeStruct((m, n), out_dtype),
        scratch_shapes=[pltpu.VMEM((tm, d), BF16)],
        compiler_params=_cparams(("parallel", "arbitrary")),
        name="in_proj",
    )(x2d, norm_w, w, gain, flag)


def _outproj_kernel(x_ref, a_ref, w_ref, o_ref):
    o_ref[...] = x_ref[...] + jnp.dot(a_ref[...], w_ref[...], preferred_element_type=F32)


def _outproj(x2d, a, w, *, tm, tn):
    m, d = x2d.shape
    k = a.shape[1]
    return pl.pallas_call(
        _outproj_kernel,
        grid=(m // tm, d // tn),
        in_specs=[
            pl.BlockSpec((tm, tn), lambda i, j: (i, j)),
            pl.BlockSpec((tm, k), lambda i, j: (i, 0)),
            pl.BlockSpec((k, tn), lambda i, j: (0, j)),
        ],
        out_specs=pl.BlockSpec((tm, tn), lambda i, j: (i, j)),
        out_shape=jax.ShapeDtypeStruct((m, d), F32),
        compiler_params=_cparams(("parallel", "parallel")),
        name="out_proj",
    )(x2d, a, w)


def _ffn_kernel(x_ref, nw_ref, wg_ref, wu_ref, wd_ref, o_ref, h_ref, acc_ref):
    f = pl.program_id(1)

    @pl.when(f == 0)
    def _():
        h_ref[...] = _rms(x_ref[...], nw_ref[...]).astype(BF16)
        acc_ref[...] = jnp.zeros_like(acc_ref)

    h = h_ref[...]
    g = jnp.dot(h, wg_ref[...], preferred_element_type=F32)
    u = jnp.dot(h, wu_ref[...], preferred_element_type=F32)
    a = (g * (1.0 / (1.0 + jnp.exp(-g))) * u).astype(BF16)
    acc_ref[...] += jnp.dot(a, wd_ref[...], preferred_element_type=F32)

    @pl.when(f == pl.num_programs(1) - 1)
    def _():
        o_ref[...] = x_ref[...] + acc_ref[...]


def _ffn(x2d, norm_w, wg, wu, wd, *, tm, tf):
    m, d = x2d.shape
    dff = wg.shape[1]
    return pl.pallas_call(
        _ffn_kernel,
        grid=(m // tm, dff // tf),
        in_specs=[
            pl.BlockSpec((tm, d), lambda i, f: (i, 0)),
            pl.BlockSpec((1, d), lambda i, f: (0, 0)),
            pl.BlockSpec((d, tf), lambda i, f: (0, f)),
            pl.BlockSpec((d, tf), lambda i, f: (0, f)),
            pl.BlockSpec((tf, d), lambda i, f: (f, 0)),
        ],
        out_specs=pl.BlockSpec((tm, d), lambda i, f: (i, 0)),
        out_shape=jax.ShapeDtypeStruct((m, d), F32),
        scratch_shapes=[pltpu.VMEM((tm, d), BF16), pltpu.VMEM((tm, d), F32)],
        compiler_params=_cparams(("parallel", "arbitrary")),
        name="ffn",
    )(x2d, norm_w, wg, wu, wd)


def _band_kernel(*refs, grp, n_off, has_sinks, with_lse):
    if has_sinks:
        sink_ref, q_ref, k_ref, v_ref, b_ref = refs[:5]
        outs = refs[5:]
    else:
        q_ref, k_ref, v_ref, b_ref = refs[:4]
        outs = refs[4:]
    o_ref = outs[0]
    g = pl.program_id(1)
    i = pl.program_id(2)
    kts, vts, tidx = [], [], []
    for off in range(n_off):
        kb = jnp.maximum(i - off, 0)
        start = pl.multiple_of(kb * QBLK, QBLK)
        kts.append(k_ref[0, pl.ds(start, QBLK), :])
        vts.append(v_ref[0, pl.ds(start, QBLK), :])
        tidx.append(jnp.where(i - off >= 0, off, n_off))
    for h in range(grp):
        sl = slice(h * HEAD_DIM, (h + 1) * HEAD_DIM)
        qh = q_ref[0, :, sl]
        s = [lax.dot_general(qh, kts[off], (((1,), (1,)), ((), ())), preferred_element_type=F32)
             + b_ref[h, tidx[off]] for off in range(n_off)]
        m = functools.reduce(jnp.maximum, [jnp.max(x, axis=-1, keepdims=True) for x in s])
        p = [jnp.exp(x - m) for x in s]
        l = functools.reduce(jnp.add, [jnp.sum(x, axis=-1, keepdims=True) for x in p])
        o = functools.reduce(jnp.add, [jnp.dot(p[off].astype(BF16), vts[off], preferred_element_type=F32)
                                       for off in range(n_off)])
        den = l
        if has_sinks:
            den = l + jnp.exp(sink_ref[g * grp + h] - m)
        o_ref[0, :, sl] = (o / den).astype(o_ref.dtype)
        if with_lse:
            outs[1][0, :, sl] = jnp.broadcast_to(m + jnp.log(l), (QBLK, HEAD_DIM))


def _band_attn(q_arr, k_arr, v_arr, bias, *, n_kv, grp, q_col, k_col, v_col, sinks=None, with_lse=False):
    n, L, _ = q_arr.shape
    n_off = bias.shape[1] - 1
    gw = grp * HEAD_DIM
    kern = functools.partial(_band_kernel, grp=grp, n_off=n_off, has_sinks=sinks is not None, with_lse=with_lse)
    in_specs = [
        pl.BlockSpec((1, QBLK, gw), lambda b, g, i: (b, i, q_col + g)),
        pl.BlockSpec((1, L, HEAD_DIM), lambda b, g, i: (b, 0, k_col + g)),
        pl.BlockSpec((1, L, HEAD_DIM), lambda b, g, i: (b, 0, v_col + g)),
        pl.BlockSpec((grp, n_off + 1, QBLK, LANES), lambda b, g, i: (g, 0, 0, 0)),
    ]
    args = [q_arr, k_arr, v_arr, bias]
    if sinks is not None:
        in_specs = [pl.BlockSpec(memory_space=pltpu.SMEM)] + in_specs
        args = [sinks] + args
    o_spec = pl.BlockSpec((1, QBLK, gw), lambda b, g, i: (b, i, g))
    o_shape = jax.ShapeDtypeStruct((n, L, n_kv * gw), F32)
    return pl.pallas_call(
        kern,
        grid=(n, n_kv, L // QBLK),
        in_specs=in_specs,
        out_specs=(o_spec, o_spec) if with_lse else o_spec,
        out_shape=(o_shape, o_shape) if with_lse else o_shape,
        compiler_params=_cparams(("parallel", "parallel", "arbitrary")),
        name="band_attn",
    )(*args)


def _compress_kernel(x_ref, pe_ref, w1_ref, w2_ref, gain_ref, o_ref, *, n_cmp):
    half = w1_ref.shape[1] // 2
    x = x_ref[0, 0, 0]
    y0 = jnp.dot(x, w1_ref[0, :half], preferred_element_type=F32)
    y1 = jnp.dot(x, w1_ref[0, half:], preferred_element_type=F32)
    pe = jnp.dot(pe_ref[0], w1_ref[0], preferred_element_type=F32)[0:1]
    rows = x.shape[0]
    c = y0 + pltpu.roll(y1, rows - 1, 0) + pe
    gl = 0.5 * c * (1.0 + jnp.tanh(math.sqrt(2.0 / math.pi) * (c + 0.044715 * (c * c * c))))
    out = jnp.dot(gl.astype(BF16), w2_ref[0], preferred_element_type=F32)
    out = jnp.where(pl.program_id(0) == 0, _rms(out, gain_ref[...]), out)
    valid = lax.broadcasted_iota(jnp.int32, out.shape, 0) < n_cmp
    o_ref[0, 0, 0] = jnp.where(valid, out, 0.0).astype(o_ref.dtype)


def _compress(xr, pe, w1, w2, gain, n_cmp):
    _, B, hk, rows, wide = xr.shape
    return pl.pallas_call(
        functools.partial(_compress_kernel, n_cmp=n_cmp),
        grid=(2, B, hk),
        in_specs=[
            pl.BlockSpec((1, 1, 1, rows, wide), lambda s, b, g: (s, b, g, 0, 0)),
            pl.BlockSpec((1, 8, 2 * wide), lambda s, b, g: (s, 0, 0)),
            pl.BlockSpec((1, 2 * wide, HEAD_DIM), lambda s, b, g: (s, 0, 0)),
            pl.BlockSpec((1, HEAD_DIM, HEAD_DIM), lambda s, b, g: (s, 0, 0)),
            pl.BlockSpec((1, HEAD_DIM), lambda s, b, g: (0, 0)),
        ],
        out_specs=pl.BlockSpec((1, 1, 1, rows, HEAD_DIM), lambda s, b, g: (s, b, g, 0, 0)),
        out_shape=jax.ShapeDtypeStruct((2, B, hk, rows, HEAD_DIM), BF16),
        compiler_params=_cparams(("arbitrary", "arbitrary", "arbitrary")),
        name="nsa_compress",
    )(xr, pe, w1, w2, gain)


def _cmp_select_kernel(q_ref, kc_ref, vc_ref, b_ref, ovl_ref, o_ref, sel_ref, *, n_slc):
    i = pl.program_id(2)
    kc = kc_ref[0, 0, 0]
    vc = vc_ref[0, 0, 0]
    imp = jnp.zeros((QBLK, LANES), F32)
    for h in range(A_GRP):
        sl = slice(h * HEAD_DIM, (h + 1) * HEAD_DIM)
        s = lax.dot_general(q_ref[0, :, sl], kc, (((1,), (1,)), ((), ())), preferred_element_type=F32) + b_ref[h]
        m = jnp.max(s, axis=-1, keepdims=True)
        e = jnp.exp(s - jnp.where(m > 0.5 * NEG, m, 0.0))
        den = jnp.sum(e, axis=-1, keepdims=True)
        p = (e / jnp.where(den > 0, den, 1.0)).astype(BF16)
        o_ref[0, :, sl] = jnp.dot(p, vc, preferred_element_type=F32)
        imp = imp + jnp.dot(p, ovl_ref[...], preferred_element_type=F32)
    t = i * QBLK + lax.broadcasted_iota(jnp.int32, (QBLK, LANES), 0)
    blk = lax.broadcasted_iota(jnp.int32, (QBLK, LANES), 1)
    cur = t // SLC_BLOCK
    imp = jnp.where(blk == 0, FORCED_SCORE, imp)
    imp = jnp.where(blk == cur, FORCED_SCORE, imp)
    imp = jnp.where(blk == cur - 1, FORCED_SCORE, imp)
    imp = jnp.where(blk * SLC_BLOCK <= t, imp, NEG)
    imp = jnp.where(blk < n_slc, imp, 2.0 * NEG)
    imp_t = imp.T
    nrow = min(LANES, -(-n_slc // 8) * 8)
    cand = imp_t[:nrow]
    row_id = lax.broadcasted_iota(jnp.int32, (nrow, QBLK), 0)
    rank = jnp.zeros((nrow, QBLK), F32)
    for j in range(n_slc):
        other = imp_t[j:j + 1, :]
        ge = jnp.where(other >= cand, 1.0, 0.0)
        gt = jnp.where(other > cand, 1.0, 0.0)
        rank = rank + jnp.where(row_id > j, ge, gt)
    sel_t = jnp.where(rank < float(min(SLC_TOPK, n_slc)), 1.0, 0.0)
    if nrow < LANES:
        sel_t = jnp.concatenate([sel_t, jnp.zeros((LANES - nrow, QBLK), F32)], axis=0)
    sel_ref[0, 0] = sel_t.T.astype(sel_ref.dtype)


def _cmp_select(proj3, kvc, bias_c, ovl, n_slc):
    B, T, _ = proj3.shape
    ncp = kvc.shape[3]
    return pl.pallas_call(
        functools.partial(_cmp_select_kernel, n_slc=n_slc),
        grid=(B, A_KV_HEADS, T // QBLK),
        in_specs=[
            pl.BlockSpec((1, QBLK, A_GRP * HEAD_DIM), lambda b, g, i: (b, i, g)),
            pl.BlockSpec((1, 1, 1, ncp, HEAD_DIM), lambda b, g, i: (0, b, g, 0, 0)),
            pl.BlockSpec((1, 1, 1, ncp, HEAD_DIM), lambda b, g, i: (1, b, g, 0, 0)),
            pl.BlockSpec((A_GRP, QBLK, ncp), lambda b, g, i: (g, i, 0)),
            pl.BlockSpec((ncp, LANES), lambda b, g, i: (0, 0)),
        ],
        out_specs=(
            pl.BlockSpec((1, QBLK, A_GRP * HEAD_DIM), lambda b, g, i: (b, i, g)),
            pl.BlockSpec((1, 1, QBLK, LANES), lambda b, g, i: (b, g, i, 0)),
        ),
        out_shape=(
            jax.ShapeDtypeStruct((B, T, A_Q_HEADS * HEAD_DIM), F32),
            jax.ShapeDtypeStruct((B, A_KV_HEADS, T, LANES), BF16),
        ),
        compiler_params=_cparams(("parallel", "parallel", "arbitrary")),
        name="nsa_cmp_select",
    )(proj3, kvc, kvc, bias_c, ovl)


def _slc_kernel(q_ref, k_ref, v_ref, sel_ref, e_ref, b_ref, o_ref, m_ref, l_ref, acc_ref, *, n_tiles):
    i = pl.program_id(2)
    m_ref[...] = jnp.full(m_ref.shape, NEG, F32)
    l_ref[...] = jnp.zeros(l_ref.shape, F32)
    acc_ref[...] = jnp.zeros(acc_ref.shape, F32)
    sel = sel_ref[0, 0]
    q = jnp.concatenate([q_ref[0, :, h * HEAD_DIM:(h + 1) * HEAD_DIM] for h in range(A_GRP)], axis=0)
    kw = SLC_CHUNK * QBLK

    def body(c, carry):
        start = pl.multiple_of(c * kw, kw)
        kt = k_ref[0, pl.ds(start, kw), :]
        vt = v_ref[0, pl.ds(start, kw), :]
        chosen = jnp.dot(sel, e_ref[:, pl.ds(start, kw)], preferred_element_type=F32)
        madd = (chosen - 1.0) * (-NEG)
        s = lax.dot_general(q, kt, (((1,), (1,)), ((), ())), preferred_element_type=F32)
        tidx = [jnp.clip(i - (c * SLC_CHUNK + j) + 1, 0, n_tiles - 1) for j in range(SLC_CHUNK)]
        rows = []
        for h in range(A_GRP):
            rows.append([s[h * QBLK:(h + 1) * QBLK, j * QBLK:(j + 1) * QBLK] + b_ref[h, tidx[j]]
                         + madd[:, j * QBLK:(j + 1) * QBLK] for j in range(SLC_CHUNK)])
        tile_max = jnp.concatenate([functools.reduce(jnp.maximum, r) for r in rows], axis=0)
        m_old = m_ref[...]
        m_new = jnp.maximum(m_old, jnp.max(tile_max, axis=-1, keepdims=True))
        alpha = jnp.exp(m_old - m_new)
        p = [[jnp.exp(rows[h][j] - m_new[h * QBLK:(h + 1) * QBLK]) for j in range(SLC_CHUNK)] for h in range(A_GRP)]
        tile_sum = jnp.concatenate([functools.reduce(jnp.add, r) for r in p], axis=0)
        l_ref[...] = alpha * l_ref[...] + jnp.sum(tile_sum, axis=-1, keepdims=True)
        pb = jnp.concatenate([jnp.concatenate([t.astype(BF16) for t in r], axis=1) for r in p], axis=0)
        acc_ref[...] = alpha * acc_ref[...] + jnp.dot(pb, vt, preferred_element_type=F32)
        m_ref[...] = m_new
        return carry

    lax.fori_loop(0, i // SLC_CHUNK + 1, body, 0)
    out = acc_ref[...] / l_ref[...]
    for h in range(A_GRP):
        o_ref[0, :, h * HEAD_DIM:(h + 1) * HEAD_DIM] = out[h * QBLK:(h + 1) * QBLK]


def _slc_attn(proj3, sel, expand, bias):
    B, T, _ = proj3.shape
    n_tiles = bias.shape[1]
    return pl.pallas_call(
        functools.partial(_slc_kernel, n_tiles=n_tiles),
        grid=(B, A_KV_HEADS, T // QBLK),
        in_specs=[
            pl.BlockSpec((1, QBLK, A_GRP * HEAD_DIM), lambda b, g, i: (b, i, g)),
            pl.BlockSpec((1, T, HEAD_DIM), lambda b, g, i: (b, 0, COL_KSA + g)),
            pl.BlockSpec((1, T, HEAD_DIM), lambda b, g, i: (b, 0, COL_VSA + g)),
            pl.BlockSpec((1, 1, QBLK, LANES), lambda b, g, i: (b, g, i, 0)),
            pl.BlockSpec((LANES, T), lambda b, g, i: (0, 0)),
            pl.BlockSpec((A_GRP, n_tiles, QBLK, LANES), lambda b, g, i: (g, 0, 0, 0)),
        ],
        out_specs=pl.BlockSpec((1, QBLK, A_GRP * HEAD_DIM), lambda b, g, i: (b, i, g)),
        out_shape=jax.ShapeDtypeStruct((B, T, A_Q_HEADS * HEAD_DIM), F32),
        scratch_shapes=[
            pltpu.VMEM((A_GRP * QBLK, 1), F32),
            pltpu.VMEM((A_GRP * QBLK, 1), F32),
            pltpu.VMEM((A_GRP * QBLK, HEAD_DIM), F32),
        ],
        compiler_params=_cparams(("parallel", "parallel", "arbitrary")),
        name="nsa_slc_attn",
    )(proj3, proj3, proj3, sel, expand, bias)


def _combine_kernel(gate_ref, ocmp_ref, oslc_ref, owin_ref, ob_ref, *rest):
    npair = len(DIL_PAIRS)
    oc_refs, lse_refs, o_ref = rest[:npair], rest[npair:2 * npair], rest[2 * npair]
    gate = 1.0 / (1.0 + jnp.exp(-gate_ref[...]))
    for h in range(A_Q_HEADS):
        sl = slice(h * HEAD_DIM, (h + 1) * HEAD_DIM)
        o = (gate[:, 3 * h:3 * h + 1] * ocmp_ref[:, sl] + gate[:, 3 * h + 1:3 * h + 2] * oslc_ref[:, sl]
             + gate[:, 3 * h + 2:3 * h + 3] * owin_ref[:, sl])
        o_ref[:, sl] = o.astype(o_ref.dtype)
    base = A_Q_HEADS * HEAD_DIM
    width = B_Q_HEADS * HEAD_DIM
    o_ref[:, base:base + width] = ob_ref[...].astype(o_ref.dtype)
    base += width
    lses = [r[...] for r in lse_refs]
    mx = functools.reduce(jnp.maximum, lses)
    ws = [jnp.exp(x - mx) for x in lses]
    tot = functools.reduce(jnp.add, ws)
    for gidx in range(npair):
        w = C_GRP * HEAD_DIM
        o_ref[:, base + gidx * w:base + (gidx + 1) * w] = (oc_refs[gidx][...] * (ws[gidx] / tot)).astype(o_ref.dtype)


def _combine(gates, o_cmp, o_slc, o_win, o_b, o_cs, lses, *, tm):
    m = gates.shape[0]
    d_out = (A_Q_HEADS + B_Q_HEADS + C_Q_HEADS) * HEAD_DIM
    row = lambda w: pl.BlockSpec((tm, w), lambda i: (i, 0))
    ins = [gates, o_cmp, o_slc, o_win, o_b, *o_cs, *lses]
    return pl.pallas_call(
        _combine_kernel,
        grid=(m // tm,),
        in_specs=[row(a.shape[1]) for a in ins],
        out_specs=row(d_out),
        out_shape=jax.ShapeDtypeStruct((m, d_out), BF16),
        compiler_params=_cparams(("parallel",)),
        name="combine",
    )(*ins)


def _to_sub(a, d):
    B, T, c = a.shape
    return a.reshape(B, T // d, d, c).transpose(0, 2, 1, 3).reshape(B * d, T // d, c)


def _from_sub(a, d, B):
    _, ls, c = a.shape
    return a.reshape(B, d, ls, c).transpose(0, 2, 1, 3).reshape(B, ls * d, c)


def _mixers(proj3, gates, layer_params, tables):
    B, T, _ = proj3.shape
    cmp_pe, cmp_w1, cmp_w2, kc_gain, sinks = layer_params
    n_cmp = (T - CMP_BLOCK) // CMP_STRIDE + 1
    n_slc = T // SLC_BLOCK
    ncp = T // CMP_STRIDE

    def cmp_rows(col):
        a = proj3[:, :, col * LANES:(col + A_KV_HEADS) * LANES].reshape(B, ncp, CMP_STRIDE, A_KV_HEADS, HEAD_DIM)
        return a.transpose(0, 3, 1, 2, 4).reshape(B, A_KV_HEADS, ncp, CMP_STRIDE * HEAD_DIM)

    xr = jnp.stack([cmp_rows(COL_KCA), cmp_rows(COL_VCA)])
    pe = jnp.broadcast_to(cmp_pe.reshape(2, 1, CMP_BLOCK * HEAD_DIM), (2, 8, CMP_BLOCK * HEAD_DIM)).astype(BF16)
    kvc = _compress(xr, pe, cmp_w1.astype(BF16), cmp_w2.astype(BF16), kc_gain.reshape(1, HEAD_DIM), n_cmp)

    o_cmp, sel = _cmp_select(proj3, kvc, tables["bias_c"], tables["ovl"], n_slc)
    o_slc = _slc_attn(proj3, sel, tables["expand"], tables["bias_slc"])
    o_win = _band_attn(proj3, proj3, proj3, tables["bias_win"], n_kv=A_KV_HEADS, grp=A_GRP,
                       q_col=COL_QA // A_GRP, k_col=COL_KWA, v_col=COL_VWA)
    o_b = _band_attn(proj3, proj3, proj3, tables["bias_b"], n_kv=B_KV_HEADS, grp=B_GRP,
                     q_col=COL_QB // B_GRP, k_col=COL_KB, v_col=COL_VB, sinks=sinks)
    o_cs, lses = [], []
    for gidx, (_, dil) in enumerate(DIL_PAIRS):
        if dil == 1:
            o, lse = _band_attn(proj3, proj3, proj3, tables["bias_c%d" % gidx], n_kv=1, grp=C_GRP,
                                q_col=COL_QC // C_GRP + gidx, k_col=COL_KC + gidx, v_col=COL_VC + gidx,
                                with_lse=True)
        else:
            qs = _to_sub(proj3[:, :, (COL_QC + C_GRP * gidx) * LANES:(COL_QC + C_GRP * (gidx + 1)) * LANES], dil)
            ks = _to_sub(proj3[:, :, (COL_KC + gidx) * LANES:(COL_KC + gidx + 1) * LANES], dil)
            vs = _to_sub(proj3[:, :, (COL_VC + gidx) * LANES:(COL_VC + gidx + 1) * LANES], dil)
            o, lse = _band_attn(qs, ks, vs, tables["bias_c%d" % gidx], n_kv=1, grp=C_GRP,
                                q_col=0, k_col=0, v_col=0, with_lse=True)
            o, lse = _from_sub(o, dil, B), _from_sub(lse, dil, B)
        o_cs.append(o.reshape(B * T, -1))
        lses.append(lse.reshape(B * T, -1))
    return _combine(gates, o_cmp.reshape(B * T, -1), o_slc.reshape(B * T, -1), o_win.reshape(B * T, -1),
                    o_b.reshape(B * T, -1), o_cs, lses, tm=256)


def _build_tables(rel_bias, T):
    n_cmp = (T - CMP_BLOCK) // CMP_STRIDE + 1
    n_slc = T // SLC_BLOCK
    ncp = T // CMP_STRIDE
    nq = T // QBLK
    tables = {}
    tables["bias_c"] = _bias_tiles(rel_bias, 0, A_Q_HEADS, nq, width=ncp, kstride=CMP_STRIDE,
                                   koff=CMP_BLOCK - 1, ncols=n_cmp).reshape(A_Q_HEADS, T, ncp)
    n_sat = -(-(SAT_DIST + QBLK - 1) // QBLK) + 1
    tables["bias_slc"] = _bias_tiles(rel_bias, 0, A_Q_HEADS, min(n_sat, nq) + 1, koff=QBLK)
    win = NSA_WINDOW - 1
    tables["bias_win"] = _bias_tiles(rel_bias, 0, A_Q_HEADS, -(-win // QBLK) + 2, max_dist=win)
    swa = SWA_WINDOW - 1
    tables["bias_b"] = _bias_tiles(rel_bias, A_Q_HEADS, B_Q_HEADS, -(-swa // QBLK) + 2, max_dist=swa)
    for gidx, (w, dil) in enumerate(DIL_PAIRS):
        md = w // dil
        tables["bias_c%d" % gidx] = _bias_tiles(rel_bias, A_Q_HEADS + B_Q_HEADS + C_GRP * gidx, C_GRP,
                                                -(-md // QBLK) + 2, dscale=dil, max_dist=md)
    c0 = np.arange(ncp)[:, None] * CMP_STRIDE
    s0 = np.arange(LANES)[None, :] * SLC_BLOCK
    ovl = np.clip(np.minimum(c0 + CMP_BLOCK, s0 + SLC_BLOCK) - np.maximum(c0, s0), 0, None) / CMP_BLOCK
    ovl = ovl * (np.arange(ncp)[:, None] < n_cmp) * (np.arange(LANES)[None, :] < n_slc)
    tables["ovl"] = jnp.asarray(ovl, BF16)
    expand = (np.arange(T)[None, :] // SLC_BLOCK) == np.arange(LANES)[:, None]
    tables["expand"] = jnp.asarray(expand, BF16)
    return tables


def _proj_gain_flag(g):
    ones = jnp.ones((HEAD_DIM,), F32)
    spec = [(g[0] * SCALE, 6, 1), (ones, 2, 0), (ones, 2, 0), (g[2], 2, 1), (ones, 2, 0), (g[3], 2, 1), (ones, 2, 0),
            (g[4] * SCALE, 4, 1), (g[5], 2, 1), (ones, 2, 0), (g[6] * SCALE, 6, 1), (g[7], 3, 1), (ones, 3, 0)]
    gain = jnp.concatenate([jnp.tile(v, n) for v, n, _ in spec]).reshape(1, N_MAIN)
    flag = np.concatenate([np.full(n * HEAD_DIM, f, np.float32) for _, n, f in spec]).reshape(1, N_MAIN)
    return gain, jnp.asarray(flag)


def kernel(x, norm_attn, w_in, qk_gain, cmp_pe, cmp_w1, cmp_w2, sinks, rel_bias, w_out, norm_ffn, w_gate, w_up, w_down):
    B, T, D = x.shape
    depth = w_in.shape[0]
    tables = _build_tables(rel_bias, T)
    x2 = x.reshape(B * T, D)
    tm = min(1024, B * T)
    for l in range(depth):
        w_main = jnp.concatenate([w_in[l][:, :GATE_START], w_in[l][:, GATE_START + N_GATES:]], axis=1).astype(BF16)
        w_g = jnp.pad(w_in[l][:, GATE_START:GATE_START + N_GATES], ((0, 0), (0, LANES - N_GATES))).astype(BF16)
        gain, flag = _proj_gain_flag(qk_gain[l])
        nw = norm_attn[l].reshape(1, D)
        proj = _proj(x2, nw, w_main, gain, flag, out_dtype=BF16, tm=tm, tn=256)
        gates = _proj(x2, nw, w_g, jnp.ones((1, LANES), F32), jnp.zeros((1, LANES), F32), out_dtype=F32, tm=tm, tn=LANES)
        mix = _mixers(proj.reshape(B, T, N_MAIN), gates,
                      (cmp_pe[l], cmp_w1[l], cmp_w2[l], qk_gain[l][1], sinks[l]), tables)
        x2 = _outproj(x2, mix, w_out[l].astype(BF16), tm=tm, tn=512)
        x2 = _ffn(x2, norm_ffn[l].reshape(1, D), w_gate[l].astype(BF16), w_up[l].astype(BF16),
                  w_down[l].astype(BF16), tm=min(512, B * T), tf=512)
    return x2.reshape(B, T, D)
```

```python
import functools
import math

import numpy as np
import jax
import jax.numpy as jnp
from jax import lax
from jax.experimental import pallas as pl
from jax.experimental.pallas import tpu as pltpu

F32 = jnp.float32
BF16 = jnp.bfloat16

HEAD_DIM = 128
LANES = 128
SUBLANES = 8
QBLK = 128
BAND_SUBBLOCKS = 4
CMP_SUBBLOCKS = 4
A_Q_HEADS, A_KV_HEADS = 6, 2
A_GRP = A_Q_HEADS // A_KV_HEADS
B_Q_HEADS, B_KV_HEADS = 4, 2
B_GRP = B_Q_HEADS // B_KV_HEADS
DIL_PAIRS = ((128, 1), (512, 4), (2048, 16))
C_GRP = 2
C_Q_HEADS = C_GRP * len(DIL_PAIRS)
CMP_BLOCK, CMP_STRIDE = 32, 16
SLC_BLOCK, SLC_TOPK = 64, 16
SLC_CHUNK = 4
NSA_WINDOW, SWA_WINDOW = 512, 128
FORCED_SCORE = 1.0e4
REL_BUCKETS, REL_MAX_EXACT, REL_MAX_DIST = 32, 16, 2048
SCALE = HEAD_DIM ** -0.5
EPS = 1e-6
NEG = -1e30
VMEM_LIMIT = 56 * 1024 * 1024

COL_QA, COL_KCA, COL_VCA, COL_KSA, COL_VSA, COL_KWA, COL_VWA = 0, 6, 8, 10, 12, 14, 16
COL_QB, COL_KB, COL_VB, COL_QC, COL_KC, COL_VC = 18, 22, 24, 26, 32, 35
N_MAIN_BLOCKS = 38
N_MAIN = N_MAIN_BLOCKS * LANES
GATE_START = 2304
N_GATES = A_Q_HEADS * 3


def _bucket_starts():
    d = np.arange(0, 1 << 17)
    out = []
    for dt in (np.float32, np.float64):
        far = np.maximum(d, REL_MAX_EXACT).astype(dt)
        lb = REL_MAX_EXACT + (np.log(far / dt(REL_MAX_EXACT)) / dt(math.log(REL_MAX_DIST / REL_MAX_EXACT))
                              * dt(REL_BUCKETS - REL_MAX_EXACT)).astype(np.int64)
        out.append(np.where(d < REL_MAX_EXACT, d, np.minimum(lb, REL_BUCKETS - 1)))
    assert (out[0] == out[1]).all() and (np.diff(out[0]) >= 0).all()
    return [int(np.argmax(out[0] >= b)) for b in range(REL_BUCKETS)]


BUCKET_START = _bucket_starts()
SAT_DIST = BUCKET_START[REL_BUCKETS - 1]


def _cparams(sem, vmem=VMEM_LIMIT):
    return pltpu.CompilerParams(dimension_semantics=sem, vmem_limit_bytes=vmem)


def _bias_tile_kernel(tab_ref, o_ref, *, head0, width, kstride, koff, dscale, max_dist, ncols):
    h = pl.program_id(0) + head0
    t = pl.program_id(1)
    r = lax.broadcasted_iota(jnp.int32, (QBLK, width), 0)
    c = lax.broadcasted_iota(jnp.int32, (QBLK, width), 1)
    dist = t * QBLK + r - kstride * c - koff
    d = dist * dscale
    val = jnp.full((QBLK, width), tab_ref[REL_BUCKETS - 1, h], F32)
    for b in range(REL_BUCKETS - 2, -1, -1):
        val = jnp.where(d < BUCKET_START[b + 1], tab_ref[b, h], val)
    val = jnp.where(dist >= 0, val, NEG)
    if max_dist is not None:
        val = jnp.where(dist <= max_dist, val, NEG)
    if ncols < width:
        val = jnp.where(c < ncols, val, NEG)
    o_ref[0, 0] = val


def _bias_tiles(rel_bias, head0, nheads, ntiles, *, width=LANES, kstride=1, koff=0, dscale=1,
                max_dist=None, ncols=None):
    ncols = width if ncols is None else ncols
    kern = functools.partial(_bias_tile_kernel, head0=head0, width=width, kstride=kstride, koff=koff,
                             dscale=dscale, max_dist=max_dist, ncols=ncols)
    return pl.pallas_call(
        kern,
        grid=(nheads, ntiles),
        in_specs=[pl.BlockSpec(memory_space=pltpu.SMEM)],
        out_specs=pl.BlockSpec((1, 1, QBLK, width), lambda h, t: (h, t, 0, 0)),
        out_shape=jax.ShapeDtypeStruct((nheads, ntiles, QBLK, width), F32),
        compiler_params=_cparams(("parallel", "parallel")),
        name="bias_tiles",
    )(rel_bias)


def _rms(x, w):
    ms = jnp.mean(x * x, axis=-1, keepdims=True)
    return x * lax.rsqrt(ms + EPS) * w


def _wprep_kernel(w_ref, main_ref, gate_ref):
    main_ref[0, :, :GATE_START] = w_ref[0, :, :GATE_START].astype(BF16)
    main_ref[0, :, GATE_START:] = w_ref[0, :, GATE_START + N_GATES:].astype(BF16)
    gate_ref[0] = w_ref[0, :, GATE_START:GATE_START + LANES].astype(BF16)


def _wprep(w_in, *, tr):
    depth, d, n_in = w_in.shape
    return pl.pallas_call(
        _wprep_kernel,
        grid=(depth, d // tr),
        in_specs=[pl.BlockSpec((1, tr, n_in), lambda l, i: (l, i, 0))],
        out_specs=(pl.BlockSpec((1, tr, N_MAIN), lambda l, i: (l, i, 0)),
                   pl.BlockSpec((1, tr, LANES), lambda l, i: (l, i, 0))),
        out_shape=(jax.ShapeDtypeStruct((depth, d, N_MAIN), BF16), jax.ShapeDtypeStruct((depth, d, LANES), BF16)),
        compiler_params=_cparams(("parallel", "parallel")),
        name="w_in_prep",
    )(w_in)


def _proj_kernel(x_ref, nw_ref, w_ref, gain_ref, flag_ref, o_ref, h_ref):
    @pl.when(pl.program_id(1) == 0)
    def _():
        h_ref[...] = _rms(x_ref[...], nw_ref[...]).astype(BF16)

    acc = jnp.dot(h_ref[...], w_ref[...], preferred_element_type=F32)
    for c in range(acc.shape[1] // LANES):
        sl = slice(c * LANES, (c + 1) * LANES)
        y = acc[:, sl]
        ms = jnp.mean(y * y, axis=-1, keepdims=True)
        sc = jnp.where(flag_ref[:, sl] > 0, lax.rsqrt(ms + EPS), 1.0)
        o_ref[:, sl] = (y * sc * gain_ref[:, sl]).astype(o_ref.dtype)


def _proj(x2d, norm_w, w, gain, flag, *, out_dtype, tm, tn):
    m, d = x2d.shape
    n = w.shape[1]
    return pl.pallas_call(
        _proj_kernel,
        grid=(m // tm, n // tn),
        in_specs=[
            pl.BlockSpec((tm, d), lambda i, j: (i, 0)),
            pl.BlockSpec((1, d), lambda i, j: (0, 0)),
            pl.BlockSpec((d, tn), lambda i, j: (0, j)),
            pl.BlockSpec((1, tn), lambda i, j: (0, j)),
            pl.BlockSpec((1, tn), lambda i, j: (0, j)),
        ],
        out_specs=pl.BlockSpec((tm, tn), lambda i, j: (i, j)),
        out_shape=jax.ShapeDtypeStruct((m, n), out_dtype),
        scratch_shapes=[pltpu.VMEM((tm, d), BF16)],
        compiler_params=_cparams(("parallel", "arbitrary")),
        name="in_proj",
    )(x2d, norm_w, w, gain, flag)


def _outproj_kernel(x_ref, a_ref, w_ref, o_ref):
    o_ref[...] = x_ref[...] + jnp.dot(a_ref[...], w_ref[...], preferred_element_type=F32)


def _outproj(x2d, a, w, *, tm, tn):
    m, d = x2d.shape
    k = a.shape[1]
    return pl.pallas_call(
        _outproj_kernel,
        grid=(m // tm, d // tn),
        in_specs=[
            pl.BlockSpec((tm, tn), lambda i, j: (i, j)),
            pl.BlockSpec((tm, k), lambda i, j: (i, 0)),
            pl.BlockSpec((k, tn), lambda i, j: (0, j)),
        ],
        out_specs=pl.BlockSpec((tm, tn), lambda i, j: (i, j)),
        out_shape=jax.ShapeDtypeStruct((m, d), F32),
        compiler_params=_cparams(("parallel", "parallel")),
        name="out_proj",
    )(x2d, a, w)


def _ffn_kernel(x_ref, nw_ref, wg_ref, wu_ref, wd_ref, o_ref, h_ref, acc_ref):
    f = pl.program_id(1)

    @pl.when(f == 0)
    def _():
        h_ref[...] = _rms(x_ref[...], nw_ref[...]).astype(BF16)
        acc_ref[...] = jnp.zeros_like(acc_ref)

    h = h_ref[...]
    g = jnp.dot(h, wg_ref[...], preferred_element_type=F32)
    u = jnp.dot(h, wu_ref[...], preferred_element_type=F32)
    a = (g * (1.0 / (1.0 + jnp.exp(-g))) * u).astype(BF16)
    acc_ref[...] += jnp.dot(a, wd_ref[...], preferred_element_type=F32)

    @pl.when(f == pl.num_programs(1) - 1)
    def _():
        o_ref[...] = x_ref[...] + acc_ref[...]


def _ffn(x2d, norm_w, wg, wu, wd, *, tm, tf):
    m, d = x2d.shape
    dff = wg.shape[1]
    return pl.pallas_call(
        _ffn_kernel,
        grid=(m // tm, dff // tf),
        in_specs=[
            pl.BlockSpec((tm, d), lambda i, f: (i, 0)),
            pl.BlockSpec((1, d), lambda i, f: (0, 0)),
            pl.BlockSpec((d, tf), lambda i, f: (0, f)),
            pl.BlockSpec((d, tf), lambda i, f: (0, f)),
            pl.BlockSpec((tf, d), lambda i, f: (f, 0)),
        ],
        out_specs=pl.BlockSpec((tm, d), lambda i, f: (i, 0)),
        out_shape=jax.ShapeDtypeStruct((m, d), F32),
        scratch_shapes=[pltpu.VMEM((tm, d), BF16), pltpu.VMEM((tm, d), F32)],
        compiler_params=_cparams(("parallel", "arbitrary")),
        name="ffn",
    )(x2d, norm_w, wg, wu, wd)


def _band_kernel(*refs, grp, n_off, nsub, has_sinks, with_lse):
    if has_sinks:
        sink_ref, q_ref, k_ref, v_ref, b_ref = refs[:5]
        outs = refs[5:]
    else:
        q_ref, k_ref, v_ref, b_ref = refs[:4]
        outs = refs[4:]
    o_ref = outs[0]
    g = pl.program_id(1)
    i0 = pl.program_id(2) * nsub
    tiles = {}

    def kv_tile(rel):
        if rel not in tiles:
            start = pl.multiple_of(jnp.maximum(i0 + rel, 0) * QBLK, QBLK)
            tiles[rel] = (k_ref[0, pl.ds(start, QBLK), :], v_ref[0, pl.ds(start, QBLK), :])
        return tiles[rel]

    def head_rows(x, h):
        return x[h * QBLK:(h + 1) * QBLK]

    for j in range(nsub):
        rs = slice(j * QBLK, (j + 1) * QBLK)
        kv = [kv_tile(j - off) for off in range(n_off)]
        kcat = jnp.concatenate([t[0] for t in kv], axis=0)
        vcat = jnp.concatenate([t[1] for t in kv], axis=0)
        tidx = [jnp.where(i0 + j - off >= 0, off, n_off) for off in range(n_off)]
        q = jnp.concatenate([q_ref[0, rs, h * HEAD_DIM:(h + 1) * HEAD_DIM] for h in range(grp)], axis=0)
        s = lax.dot_general(q, kcat, (((1,), (1,)), ((), ())), preferred_element_type=F32)
        st = [[head_rows(s, h)[:, off * QBLK:(off + 1) * QBLK] + b_ref[h, tidx[off]] for off in range(n_off)]
              for h in range(grp)]
        m = jnp.max(jnp.concatenate([functools.reduce(jnp.maximum, r) for r in st], axis=0), axis=-1, keepdims=True)
        p = [[jnp.exp(st[h][off] - head_rows(m, h)) for off in range(n_off)] for h in range(grp)]
        l = jnp.sum(jnp.concatenate([functools.reduce(jnp.add, r) for r in p], axis=0), axis=-1, keepdims=True)
        pb = jnp.concatenate([jnp.concatenate([t.astype(BF16) for t in r], axis=1) for r in p], axis=0)
        o = jnp.dot(pb, vcat, preferred_element_type=F32)
        den = l
        if has_sinks:
            sink = jnp.concatenate([jnp.full((QBLK, 1), sink_ref[g * grp + h], F32) for h in range(grp)], axis=0)
            den = l + jnp.exp(sink - m)
        o = o / den
        for h in range(grp):
            sl = slice(h * HEAD_DIM, (h + 1) * HEAD_DIM)
            o_ref[0, rs, sl] = head_rows(o, h).astype(o_ref.dtype)
            if with_lse:
                outs[1][0, rs, sl] = jnp.broadcast_to(head_rows(m + jnp.log(l), h), (QBLK, HEAD_DIM))


def _band_attn(q_arr, k_arr, v_arr, bias, *, n_kv, grp, q_col, k_col, v_col, sinks=None, with_lse=False):
    n, L, _ = q_arr.shape
    n_off = bias.shape[1] - 1
    gw = grp * HEAD_DIM
    nsub = math.gcd(BAND_SUBBLOCKS, L // QBLK)
    rows = nsub * QBLK
    kern = functools.partial(_band_kernel, grp=grp, n_off=n_off, nsub=nsub, has_sinks=sinks is not None,
                             with_lse=with_lse)
    in_specs = [
        pl.BlockSpec((1, rows, gw), lambda b, g, i: (b, i, q_col + g)),
        pl.BlockSpec((1, L, HEAD_DIM), lambda b, g, i: (b, 0, k_col + g)),
        pl.BlockSpec((1, L, HEAD_DIM), lambda b, g, i: (b, 0, v_col + g)),
        pl.BlockSpec((grp, n_off + 1, QBLK, LANES), lambda b, g, i: (g, 0, 0, 0)),
    ]
    args = [q_arr, k_arr, v_arr, bias]
    if sinks is not None:
        in_specs = [pl.BlockSpec(memory_space=pltpu.SMEM)] + in_specs
        args = [sinks] + args
    o_spec = pl.BlockSpec((1, rows, gw), lambda b, g, i: (b, i, g))
    o_shape = jax.ShapeDtypeStruct((n, L, n_kv * gw), F32)
    return pl.pallas_call(
        kern,
        grid=(n, n_kv, L // rows),
        in_specs=in_specs,
        out_specs=(o_spec, o_spec) if with_lse else o_spec,
        out_shape=(o_shape, o_shape) if with_lse else o_shape,
        compiler_params=_cparams(("parallel", "parallel", "arbitrary")),
        name="band_attn",
    )(*args)


def _compress_kernel(x_ref, pe_ref, w1_ref, w2_ref, gain_ref, o_ref, *, n_cmp):
    half = w1_ref.shape[1] // 2
    x = x_ref[0, 0, 0]
    y0 = jnp.dot(x, w1_ref[0, :half], preferred_element_type=F32)
    y1 = jnp.dot(x, w1_ref[0, half:], preferred_element_type=F32)
    pe = jnp.dot(pe_ref[0], w1_ref[0], preferred_element_type=F32)[0:1]
    rows = x.shape[0]
    c = y0 + pltpu.roll(y1, rows - 1, 0) + pe
    gl = 0.5 * c * (1.0 + jnp.tanh(math.sqrt(2.0 / math.pi) * (c + 0.044715 * (c * c * c))))
    out = jnp.dot(gl.astype(BF16), w2_ref[0], preferred_element_type=F32)
    out = jnp.where(pl.program_id(0) == 0, _rms(out, gain_ref[...]), out)
    valid = lax.broadcasted_iota(jnp.int32, out.shape, 0) < n_cmp
    o_ref[0, 0, 0] = jnp.where(valid, out, 0.0).astype(o_ref.dtype)


def _compress(xr, pe, w1, w2, gain, n_cmp):
    _, B, hk, rows, wide = xr.shape
    return pl.pallas_call(
        functools.partial(_compress_kernel, n_cmp=n_cmp),
        grid=(2, B, hk),
        in_specs=[
            pl.BlockSpec((1, 1, 1, rows, wide), lambda s, b, g: (s, b, g, 0, 0)),
            pl.BlockSpec((1, 8, 2 * wide), lambda s, b, g: (s, 0, 0)),
            pl.BlockSpec((1, 2 * wide, HEAD_DIM), lambda s, b, g: (s, 0, 0)),
            pl.BlockSpec((1, HEAD_DIM, HEAD_DIM), lambda s, b, g: (s, 0, 0)),
            pl.BlockSpec((1, HEAD_DIM), lambda s, b, g: (0, 0)),
        ],
        out_specs=pl.BlockSpec((1, 1, 1, rows, HEAD_DIM), lambda s, b, g: (s, b, g, 0, 0)),
        out_shape=jax.ShapeDtypeStruct((2, B, hk, rows, HEAD_DIM), BF16),
        compiler_params=_cparams(("arbitrary", "arbitrary", "arbitrary")),
        name="nsa_compress",
    )(xr, pe, w1, w2, gain)


def _cmp_select_kernel(q_ref, kc_ref, vc_ref, b_ref, ovl_ref, o_ref, sel_ref, *, n_slc, nsub):
    i0 = pl.program_id(2) * nsub
    rows = nsub * QBLK
    kc = kc_ref[0, 0, 0]
    vc = vc_ref[0, 0, 0]
    imp = jnp.zeros((rows, LANES), F32)
    for h in range(A_GRP):
        sl = slice(h * HEAD_DIM, (h + 1) * HEAD_DIM)
        s = lax.dot_general(q_ref[0, :, sl], kc, (((1,), (1,)), ((), ())), preferred_element_type=F32) + b_ref[h]
        m = jnp.max(s, axis=-1, keepdims=True)
        e = jnp.exp(s - jnp.where(m > 0.5 * NEG, m, 0.0))
        den = jnp.sum(e, axis=-1, keepdims=True)
        p = (e / jnp.where(den > 0, den, 1.0)).astype(BF16)
        o_ref[0, :, sl] = jnp.dot(p, vc, preferred_element_type=F32)
        imp = imp + jnp.dot(p, ovl_ref[...], preferred_element_type=F32)
    t = i0 * QBLK + lax.broadcasted_iota(jnp.int32, (rows, LANES), 0)
    blk = lax.broadcasted_iota(jnp.int32, (rows, LANES), 1)
    cur = t // SLC_BLOCK
    imp = jnp.where(blk == 0, FORCED_SCORE, imp)
    imp = jnp.where(blk == cur, FORCED_SCORE, imp)
    imp = jnp.where(blk == cur - 1, FORCED_SCORE, imp)
    imp = jnp.where(blk * SLC_BLOCK <= t, imp, NEG)
    imp = jnp.where(blk < n_slc, imp, 2.0 * NEG)
    imp_t = jnp.concatenate([imp[j * QBLK:(j + 1) * QBLK].T for j in range(nsub)], axis=1)
    ngrp = -(-n_slc // SUBLANES)
    cands = [imp_t[r * SUBLANES:(r + 1) * SUBLANES] for r in range(ngrp)]
    row_id = lax.broadcasted_iota(jnp.int32, (SUBLANES, rows), 0)
    ranks = [jnp.zeros((SUBLANES, rows), F32) for _ in range(ngrp)]
    for j in range(n_slc):
        other = imp_t[j:j + 1, :]
        for r in range(ngrp):
            if r * SUBLANES > j:
                beats = other >= cands[r]
            elif (r + 1) * SUBLANES <= j:
                beats = other > cands[r]
            else:
                beats = jnp.where(row_id > j - r * SUBLANES, jnp.where(other >= cands[r], 1.0, 0.0),
                                  jnp.where(other > cands[r], 1.0, 0.0)) > 0.5
            ranks[r] = ranks[r] + jnp.where(beats, 1.0, 0.0)
    sel_t = jnp.concatenate([jnp.where(r < float(min(SLC_TOPK, n_slc)), 1.0, 0.0) for r in ranks], axis=0)
    if ngrp * SUBLANES < LANES:
        sel_t = jnp.concatenate([sel_t, jnp.zeros((LANES - ngrp * SUBLANES, rows), F32)], axis=0)
    for j in range(nsub):
        sel_ref[0, 0, j * QBLK:(j + 1) * QBLK, :] = sel_t[:, j * QBLK:(j + 1) * QBLK].T.astype(sel_ref.dtype)


def _cmp_select(proj3, kvc, bias_c, ovl, n_slc):
    B, T, _ = proj3.shape
    ncp = kvc.shape[3]
    nsub = math.gcd(CMP_SUBBLOCKS, T // QBLK)
    rows = nsub * QBLK
    return pl.pallas_call(
        functools.partial(_cmp_select_kernel, n_slc=n_slc, nsub=nsub),
        grid=(B, A_KV_HEADS, T // rows),
        in_specs=[
            pl.BlockSpec((1, rows, A_GRP * HEAD_DIM), lambda b, g, i: (b, i, g)),
            pl.BlockSpec((1, 1, 1, ncp, HEAD_DIM), lambda b, g, i: (0, b, g, 0, 0)),
            pl.BlockSpec((1, 1, 1, ncp, HEAD_DIM), lambda b, g, i: (1, b, g, 0, 0)),
            pl.BlockSpec((A_GRP, rows, ncp), lambda b, g, i: (g, i, 0)),
            pl.BlockSpec((ncp, LANES), lambda b, g, i: (0, 0)),
        ],
        out_specs=(
            pl.BlockSpec((1, rows, A_GRP * HEAD_DIM), lambda b, g, i: (b, i, g)),
            pl.BlockSpec((1, 1, rows, LANES), lambda b, g, i: (b, g, i, 0)),
        ),
        out_shape=(
            jax.ShapeDtypeStruct((B, T, A_Q_HEADS * HEAD_DIM), F32),
            jax.ShapeDtypeStruct((B, A_KV_HEADS, T, LANES), BF16),
        ),
        compiler_params=_cparams(("parallel", "parallel", "arbitrary")),
        name="nsa_cmp_select",
    )(proj3, kvc, kvc, bias_c, ovl)


def _slc_kernel(q_ref, k_ref, v_ref, sel_ref, e_ref, b_ref, o_ref, m_ref, l_ref, acc_ref, *, n_tiles):
    i = pl.program_id(2)
    m_ref[...] = jnp.full(m_ref.shape, NEG, F32)
    l_ref[...] = jnp.zeros(l_ref.shape, F32)
    acc_ref[...] = jnp.zeros(acc_ref.shape, F32)
    sel = sel_ref[0, 0]
    q = jnp.concatenate([q_ref[0, :, h * HEAD_DIM:(h + 1) * HEAD_DIM] for h in range(A_GRP)], axis=0)
    kw = SLC_CHUNK * QBLK

    def body(c, carry):
        start = pl.multiple_of(c * kw, kw)
        kt = k_ref[0, pl.ds(start, kw), :]
        vt = v_ref[0, pl.ds(start, kw), :]
        chosen = jnp.dot(sel, e_ref[:, pl.ds(start, kw)], preferred_element_type=F32)
        madd = (chosen - 1.0) * (-NEG)
        s = lax.dot_general(q, kt, (((1,), (1,)), ((), ())), preferred_element_type=F32)
        tidx = [jnp.clip(i - (c * SLC_CHUNK + j) + 1, 0, n_tiles - 1) for j in range(SLC_CHUNK)]
        rows = []
        for h in range(A_GRP):
            rows.append([s[h * QBLK:(h + 1) * QBLK, j * QBLK:(j + 1) * QBLK] + b_ref[h, tidx[j]]
                         + madd[:, j * QBLK:(j + 1) * QBLK] for j in range(SLC_CHUNK)])
        tile_max = jnp.concatenate([functools.reduce(jnp.maximum, r) for r in rows], axis=0)
        m_old = m_ref[...]
        m_new = jnp.maximum(m_old, jnp.max(tile_max, axis=-1, keepdims=True))
        alpha = jnp.exp(m_old - m_new)
        p = [[jnp.exp(rows[h][j] - m_new[h * QBLK:(h + 1) * QBLK]) for j in range(SLC_CHUNK)] for h in range(A_GRP)]
        tile_sum = jnp.concatenate([functools.reduce(jnp.add, r) for r in p], axis=0)
        l_ref[...] = alpha * l_ref[...] + jnp.sum(tile_sum, axis=-1, keepdims=True)
        pb = jnp.concatenate([jnp.concatenate([t.astype(BF16) for t in r], axis=1) for r in p], axis=0)
        acc_ref[...] = alpha * acc_ref[...] + jnp.dot(pb, vt, preferred_element_type=F32)
        m_ref[...] = m_new
        return carry

    lax.fori_loop(0, i // SLC_CHUNK + 1, body, 0)
    out = acc_ref[...] / l_ref[...]
    for h in range(A_GRP):
        o_ref[0, :, h * HEAD_DIM:(h + 1) * HEAD_DIM] = out[h * QBLK:(h + 1) * QBLK]


def _slc_attn(proj3, sel, expand, bias):
    B, T, _ = proj3.shape
    n_tiles = bias.shape[1]
    return pl.pallas_call(
        functools.partial(_slc_kernel, n_tiles=n_tiles),
        grid=(B, A_KV_HEADS, T // QBLK),
        in_specs=[
            pl.BlockSpec((1, QBLK, A_GRP * HEAD_DIM), lambda b, g, i: (b, i, g)),
            pl.BlockSpec((1, T, HEAD_DIM), lambda b, g, i: (b, 0, COL_KSA + g)),
            pl.BlockSpec((1, T, HEAD_DIM), lambda b, g, i: (b, 0, COL_VSA + g)),
            pl.BlockSpec((1, 1, QBLK, LANES), lambda b, g, i: (b, g, i, 0)),
            pl.BlockSpec((LANES, T), lambda b, g, i: (0, 0)),
            pl.BlockSpec((A_GRP, n_tiles, QBLK, LANES), lambda b, g, i: (g, 0, 0, 0)),
        ],
        out_specs=pl.BlockSpec((1, QBLK, A_GRP * HEAD_DIM), lambda b, g, i: (b, i, g)),
        out_shape=jax.ShapeDtypeStruct((B, T, A_Q_HEADS * HEAD_DIM), F32),
        scratch_shapes=[
            pltpu.VMEM((A_GRP * QBLK, 1), F32),
            pltpu.VMEM((A_GRP * QBLK, 1), F32),
            pltpu.VMEM((A_GRP * QBLK, HEAD_DIM), F32),
        ],
        compiler_params=_cparams(("parallel", "parallel", "arbitrary")),
        name="nsa_slc_attn",
    )(proj3, proj3, proj3, sel, expand, bias)


def _combine_kernel(gate_ref, ocmp_ref, oslc_ref, owin_ref, ob_ref, *rest):
    npair = len(DIL_PAIRS)
    oc_refs, lse_refs, o_ref = rest[:npair], rest[npair:2 * npair], rest[2 * npair]
    gate = 1.0 / (1.0 + jnp.exp(-gate_ref[...]))
    for h in range(A_Q_HEADS):
        sl = slice(h * HEAD_DIM, (h + 1) * HEAD_DIM)
        o = (gate[:, 3 * h:3 * h + 1] * ocmp_ref[:, sl] + gate[:, 3 * h + 1:3 * h + 2] * oslc_ref[:, sl]
             + gate[:, 3 * h + 2:3 * h + 3] * owin_ref[:, sl])
        o_ref[:, sl] = o.astype(o_ref.dtype)
    base = A_Q_HEADS * HEAD_DIM
    width = B_Q_HEADS * HEAD_DIM
    o_ref[:, base:base + width] = ob_ref[...].astype(o_ref.dtype)
    base += width
    lses = [r[...] for r in lse_refs]
    mx = functools.reduce(jnp.maximum, lses)
    ws = [jnp.exp(x - mx) for x in lses]
    tot = functools.reduce(jnp.add, ws)
    for gidx in range(npair):
        w = C_GRP * HEAD_DIM
        o_ref[:, base + gidx * w:base + (gidx + 1) * w] = (oc_refs[gidx][...] * (ws[gidx] / tot)).astype(o_ref.dtype)


def _combine(gates, o_cmp, o_slc, o_win, o_b, o_cs, lses, *, tm):
    m = gates.shape[0]
    d_out = (A_Q_HEADS + B_Q_HEADS + C_Q_HEADS) * HEAD_DIM
    row = lambda w: pl.BlockSpec((tm, w), lambda i: (i, 0))
    ins = [gates, o_cmp, o_slc, o_win, o_b, *o_cs, *lses]
    return pl.pallas_call(
        _combine_kernel,
        grid=(m // tm,),
        in_specs=[row(a.shape[1]) for a in ins],
        out_specs=row(d_out),
        out_shape=jax.ShapeDtypeStruct((m, d_out), BF16),
        compiler_params=_cparams(("parallel",)),
        name="combine",
    )(*ins)


def _to_sub(a, d):
    B, T, c = a.shape
    return a.reshape(B, T // d, d, c).transpose(0, 2, 1, 3).reshape(B * d, T // d, c)


def _from_sub(a, d, B):
    _, ls, c = a.shape
    return a.reshape(B, d, ls, c).transpose(0, 2, 1, 3).reshape(B, ls * d, c)


def _mixers(proj3, gates, layer_params, tables):
    B, T, _ = proj3.shape
    cmp_pe, cmp_w1, cmp_w2, kc_gain, sinks = layer_params
    n_cmp = (T - CMP_BLOCK) // CMP_STRIDE + 1
    n_slc = T // SLC_BLOCK
    ncp = T // CMP_STRIDE

    def cmp_rows(col):
        a = proj3[:, :, col * LANES:(col + A_KV_HEADS) * LANES].reshape(B, ncp, CMP_STRIDE, A_KV_HEADS, HEAD_DIM)
        return a.transpose(0, 3, 1, 2, 4).reshape(B, A_KV_HEADS, ncp, CMP_STRIDE * HEAD_DIM)

    xr = jnp.stack([cmp_rows(COL_KCA), cmp_rows(COL_VCA)])
    pe = jnp.broadcast_to(cmp_pe.reshape(2, 1, CMP_BLOCK * HEAD_DIM), (2, 8, CMP_BLOCK * HEAD_DIM)).astype(BF16)
    kvc = _compress(xr, pe, cmp_w1.astype(BF16), cmp_w2.astype(BF16), kc_gain.reshape(1, HEAD_DIM), n_cmp)

    o_cmp, sel = _cmp_select(proj3, kvc, tables["bias_c"], tables["ovl"], n_slc)
    o_slc = _slc_attn(proj3, sel, tables["expand"], tables["bias_slc"])
    o_win = _band_attn(proj3, proj3, proj3, tables["bias_win"], n_kv=A_KV_HEADS, grp=A_GRP,
                       q_col=COL_QA // A_GRP, k_col=COL_KWA, v_col=COL_VWA)
    o_b = _band_attn(proj3, proj3, proj3, tables["bias_b"], n_kv=B_KV_HEADS, grp=B_GRP,
                     q_col=COL_QB // B_GRP, k_col=COL_KB, v_col=COL_VB, sinks=sinks)
    o_cs, lses = [], []
    for gidx, (_, dil) in enumerate(DIL_PAIRS):
        if dil == 1:
            o, lse = _band_attn(proj3, proj3, proj3, tables["bias_c%d" % gidx], n_kv=1, grp=C_GRP,
                                q_col=COL_QC // C_GRP + gidx, k_col=COL_KC + gidx, v_col=COL_VC + gidx,
                                with_lse=True)
        else:
            qs = _to_sub(proj3[:, :, (COL_QC + C_GRP * gidx) * LANES:(COL_QC + C_GRP * (gidx + 1)) * LANES], dil)
            ks = _to_sub(proj3[:, :, (COL_KC + gidx) * LANES:(COL_KC + gidx + 1) * LANES], dil)
            vs = _to_sub(proj3[:, :, (COL_VC + gidx) * LANES:(COL_VC + gidx + 1) * LANES], dil)
            o, lse = _band_attn(qs, ks, vs, tables["bias_c%d" % gidx], n_kv=1, grp=C_GRP,
                                q_col=0, k_col=0, v_col=0, with_lse=True)
            o, lse = _from_sub(o, dil, B), _from_sub(lse, dil, B)
        o_cs.append(o.reshape(B * T, -1))
        lses.append(lse.reshape(B * T, -1))
    return _combine(gates, o_cmp.reshape(B * T, -1), o_slc.reshape(B * T, -1), o_win.reshape(B * T, -1),
                    o_b.reshape(B * T, -1), o_cs, lses, tm=256)


def _build_tables(rel_bias, T):
    n_cmp = (T - CMP_BLOCK) // CMP_STRIDE + 1
    n_slc = T // SLC_BLOCK
    ncp = T // CMP_STRIDE
    nq = T // QBLK
    tables = {}
    tables["bias_c"] = _bias_tiles(rel_bias, 0, A_Q_HEADS, nq, width=ncp, kstride=CMP_STRIDE,
                                   koff=CMP_BLOCK - 1, ncols=n_cmp).reshape(A_Q_HEADS, T, ncp)
    n_sat = -(-(SAT_DIST + QBLK - 1) // QBLK) + 1
    tables["bias_slc"] = _bias_tiles(rel_bias, 0, A_Q_HEADS, min(n_sat, nq) + 1, koff=QBLK)
    win = NSA_WINDOW - 1
    tables["bias_win"] = _bias_tiles(rel_bias, 0, A_Q_HEADS, -(-win // QBLK) + 2, max_dist=win)
    swa = SWA_WINDOW - 1
    tables["bias_b"] = _bias_tiles(rel_bias, A_Q_HEADS, B_Q_HEADS, -(-swa // QBLK) + 2, max_dist=swa)
    for gidx, (w, dil) in enumerate(DIL_PAIRS):
        md = w // dil
        tables["bias_c%d" % gidx] = _bias_tiles(rel_bias, A_Q_HEADS + B_Q_HEADS + C_GRP * gidx, C_GRP,
                                                -(-md // QBLK) + 2, dscale=dil, max_dist=md)
    c0 = np.arange(ncp)[:, None] * CMP_STRIDE
    s0 = np.arange(LANES)[None, :] * SLC_BLOCK
    ovl = np.clip(np.minimum(c0 + CMP_BLOCK, s0 + SLC_BLOCK) - np.maximum(c0, s0), 0, None) / CMP_BLOCK
    ovl = ovl * (np.arange(ncp)[:, None] < n_cmp) * (np.arange(LANES)[None, :] < n_slc)
    tables["ovl"] = jnp.asarray(ovl, BF16)
    expand = (np.arange(T)[None, :] // SLC_BLOCK) == np.arange(LANES)[:, None]
    tables["expand"] = jnp.asarray(expand, BF16)
    return tables


def _proj_gain_flag(g):
    ones = jnp.ones((HEAD_DIM,), F32)
    spec = [(g[0] * SCALE, 6, 1), (ones, 2, 0), (ones, 2, 0), (g[2], 2, 1), (ones, 2, 0), (g[3], 2, 1), (ones, 2, 0),
            (g[4] * SCALE, 4, 1), (g[5], 2, 1), (ones, 2, 0), (g[6] * SCALE, 6, 1), (g[7], 3, 1), (ones, 3, 0)]
    gain = jnp.concatenate([jnp.tile(v, n) for v, n, _ in spec]).reshape(1, N_MAIN)
    flag = np.concatenate([np.full(n * HEAD_DIM, f, np.float32) for _, n, f in spec]).reshape(1, N_MAIN)
    return gain, jnp.asarray(flag)


def kernel(x, norm_attn, w_in, qk_gain, cmp_pe, cmp_w1, cmp_w2, sinks, rel_bias, w_out, norm_ffn, w_gate, w_up, w_down):
    B, T, D = x.shape
    depth = w_in.shape[0]
    tables = _build_tables(rel_bias, T)
    x2 = x.reshape(B * T, D)
    tm = min(1024, B * T)
    w_main_all, w_gate_all = _wprep(w_in, tr=256)
    for l in range(depth):
        w_main, w_g = w_main_all[l], w_gate_all[l]
        gain, flag = _proj_gain_flag(qk_gain[l])
        nw = norm_attn[l].reshape(1, D)
        proj = _proj(x2, nw, w_main, gain, flag, out_dtype=BF16, tm=tm, tn=256)
        gates = _proj(x2, nw, w_g, jnp.ones((1, LANES), F32), jnp.zeros((1, LANES), F32), out_dtype=F32, tm=tm, tn=LANES)
        mix = _mixers(proj.reshape(B, T, N_MAIN), gates,
                      (cmp_pe[l], cmp_w1[l], cmp_w2[l], qk_gain[l][1], sinks[l]), tables)
        x2 = _outproj(x2, mix, w_out[l].astype(BF16), tm=tm, tn=512)
        x2 = _ffn(x2, norm_ffn[l].reshape(1, D), w_gate[l].astype(BF16), w_up[l].astype(BF16),
                  w_down[l].astype(BF16), tm=min(512, B * T), tf=512)
    return x2.reshape(B, T, D)
```

```python
import functools
import math

import numpy as np
import jax
import jax.numpy as jnp
from jax import lax
from jax.experimental import pallas as pl
from jax.experimental.pallas import tpu as pltpu

F32 = jnp.float32
BF16 = jnp.bfloat16

HEAD_DIM = 128
LANES = 128
SUBLANES = 8
QBLK = 128
BAND_SUBBLOCKS = 4
CMP_SUBBLOCKS = 4
A_Q_HEADS, A_KV_HEADS = 6, 2
A_GRP = A_Q_HEADS // A_KV_HEADS
B_Q_HEADS, B_KV_HEADS = 4, 2
B_GRP = B_Q_HEADS // B_KV_HEADS
DIL_PAIRS = ((128, 1), (512, 4), (2048, 16))
C_GRP = 2
C_Q_HEADS = C_GRP * len(DIL_PAIRS)
CMP_BLOCK, CMP_STRIDE = 32, 16
SLC_BLOCK, SLC_TOPK = 64, 16
SLC_CHUNK = 4
SLC_QTILES = 1
NSA_WINDOW, SWA_WINDOW = 512, 128
FORCED_SCORE = 1.0e4
REL_BUCKETS, REL_MAX_EXACT, REL_MAX_DIST = 32, 16, 2048
SCALE = HEAD_DIM ** -0.5
EPS = 1e-6
NEG = -1e30
VMEM_LIMIT = 56 * 1024 * 1024

COL_QA, COL_KCA, COL_VCA, COL_KSA, COL_VSA, COL_KWA, COL_VWA = 0, 6, 8, 10, 12, 14, 16
COL_QB, COL_KB, COL_VB, COL_QC, COL_KC, COL_VC = 18, 22, 24, 26, 32, 35
N_MAIN_BLOCKS = 38
N_MAIN = N_MAIN_BLOCKS * LANES
GATE_START = 2304
N_GATES = A_Q_HEADS * 3
PROJ_NORMED = ([True] * 6 + [False] * 4 + [True] * 2 + [False] * 2 + [True] * 2 + [False] * 2
               + [True] * 6 + [False] * 2 + [True] * 9 + [False] * 3)
PROJ_SUBTILE = 256
OUT_SUBTILE = 512


def _bucket_starts():
    d = np.arange(0, 1 << 17)
    out = []
    for dt in (np.float32, np.float64):
        far = np.maximum(d, REL_MAX_EXACT).astype(dt)
        lb = REL_MAX_EXACT + (np.log(far / dt(REL_MAX_EXACT)) / dt(math.log(REL_MAX_DIST / REL_MAX_EXACT))
                              * dt(REL_BUCKETS - REL_MAX_EXACT)).astype(np.int64)
        out.append(np.where(d < REL_MAX_EXACT, d, np.minimum(lb, REL_BUCKETS - 1)))
    assert (out[0] == out[1]).all() and (np.diff(out[0]) >= 0).all()
    return [int(np.argmax(out[0] >= b)) for b in range(REL_BUCKETS)]


BUCKET_START = _bucket_starts()
SAT_DIST = BUCKET_START[REL_BUCKETS - 1]


def _cparams(sem, vmem=VMEM_LIMIT):
    return pltpu.CompilerParams(dimension_semantics=sem, vmem_limit_bytes=vmem)


def _bias_tile_kernel(tab_ref, o_ref, *, head0, width, kstride, koff, dscale, max_dist, ncols):
    h = pl.program_id(0) + head0
    t = pl.program_id(1)
    r = lax.broadcasted_iota(jnp.int32, (QBLK, width), 0)
    c = lax.broadcasted_iota(jnp.int32, (QBLK, width), 1)
    dist = t * QBLK + r - kstride * c - koff
    d = dist * dscale
    val = jnp.full((QBLK, width), tab_ref[REL_BUCKETS - 1, h], F32)
    for b in range(REL_BUCKETS - 2, -1, -1):
        val = jnp.where(d < BUCKET_START[b + 1], tab_ref[b, h], val)
    val = jnp.where(dist >= 0, val, NEG)
    if max_dist is not None:
        val = jnp.where(dist <= max_dist, val, NEG)
    if ncols < width:
        val = jnp.where(c < ncols, val, NEG)
    o_ref[0, 0] = val


def _bias_tiles(rel_bias, head0, nheads, ntiles, *, width=LANES, kstride=1, koff=0, dscale=1,
                max_dist=None, ncols=None):
    ncols = width if ncols is None else ncols
    kern = functools.partial(_bias_tile_kernel, head0=head0, width=width, kstride=kstride, koff=koff,
                             dscale=dscale, max_dist=max_dist, ncols=ncols)
    return pl.pallas_call(
        kern,
        grid=(nheads, ntiles),
        in_specs=[pl.BlockSpec(memory_space=pltpu.SMEM)],
        out_specs=pl.BlockSpec((1, 1, QBLK, width), lambda h, t: (h, t, 0, 0)),
        out_shape=jax.ShapeDtypeStruct((nheads, ntiles, QBLK, width), F32),
        compiler_params=_cparams(("parallel", "parallel")),
        name="bias_tiles",
    )(rel_bias)


def _rms(x, w):
    ms = jnp.mean(x * x, axis=-1, keepdims=True)
    return x * lax.rsqrt(ms + EPS) * w


def _wprep_kernel(w_ref, o_ref):
    o_ref[0, :, :GATE_START] = w_ref[0, :, :GATE_START].astype(BF16)
    o_ref[0, :, GATE_START:N_MAIN] = w_ref[0, :, GATE_START + N_GATES:].astype(BF16)
    o_ref[0, :, N_MAIN:] = w_ref[0, :, GATE_START:GATE_START + LANES].astype(BF16)


def _wprep(w_in, *, tr):
    depth, d, n_in = w_in.shape
    return pl.pallas_call(
        _wprep_kernel,
        grid=(depth, d // tr),
        in_specs=[pl.BlockSpec((1, tr, n_in), lambda l, i: (l, i, 0))],
        out_specs=pl.BlockSpec((1, tr, N_MAIN + LANES), lambda l, i: (l, i, 0)),
        out_shape=jax.ShapeDtypeStruct((depth, d, N_MAIN + LANES), BF16),
        compiler_params=_cparams(("parallel", "parallel")),
        name="w_in_prep",
    )(w_in)


def _proj_kernel(x_ref, nw_ref, w_ref, gain_ref, o_ref, gate_ref):
    h = _rms(x_ref[...], nw_ref[...]).astype(BF16)
    for c0 in range(0, N_MAIN, PROJ_SUBTILE):
        width = min(PROJ_SUBTILE, N_MAIN - c0)
        acc = jnp.dot(h, w_ref[:, c0:c0 + width], preferred_element_type=F32)
        for c in range(width // LANES):
            blk = c0 // LANES + c
            sl = slice(blk * LANES, (blk + 1) * LANES)
            y = acc[:, c * LANES:(c + 1) * LANES]
            if PROJ_NORMED[blk]:
                ms = jnp.mean(y * y, axis=-1, keepdims=True)
                y = y * lax.rsqrt(ms + EPS) * gain_ref[:, sl]
            o_ref[:, sl] = y.astype(o_ref.dtype)
    gate_ref[...] = jnp.dot(h, w_ref[:, N_MAIN:], preferred_element_type=F32)


def _proj(x2d, norm_w, w, gain, *, tm):
    m, d = x2d.shape
    n = w.shape[1]
    return pl.pallas_call(
        _proj_kernel,
        grid=(m // tm,),
        in_specs=[
            pl.BlockSpec((tm, d), lambda i: (i, 0)),
            pl.BlockSpec((1, d), lambda i: (0, 0)),
            pl.BlockSpec((d, n), lambda i: (0, 0), pipeline_mode=pl.Buffered(1)),
            pl.BlockSpec((1, N_MAIN), lambda i: (0, 0)),
        ],
        out_specs=(pl.BlockSpec((tm, N_MAIN), lambda i: (i, 0)), pl.BlockSpec((tm, LANES), lambda i: (i, 0))),
        out_shape=(jax.ShapeDtypeStruct((m, N_MAIN), BF16), jax.ShapeDtypeStruct((m, LANES), F32)),
        compiler_params=_cparams(("parallel",)),
        name="in_proj",
    )(x2d, norm_w, w, gain)


def _outproj_kernel(x_ref, a_ref, w_ref, o_ref):
    a = a_ref[...]
    for c0 in range(0, o_ref.shape[1], OUT_SUBTILE):
        sl = slice(c0, c0 + OUT_SUBTILE)
        o_ref[:, sl] = x_ref[:, sl] + jnp.dot(a, w_ref[:, sl], preferred_element_type=F32)


def _outproj(x2d, a, w, *, tm):
    m, d = x2d.shape
    k = a.shape[1]
    return pl.pallas_call(
        _outproj_kernel,
        grid=(m // tm,),
        in_specs=[
            pl.BlockSpec((tm, d), lambda i: (i, 0)),
            pl.BlockSpec((tm, k), lambda i: (i, 0)),
            pl.BlockSpec((k, d), lambda i: (0, 0), pipeline_mode=pl.Buffered(1)),
        ],
        out_specs=pl.BlockSpec((tm, d), lambda i: (i, 0)),
        out_shape=jax.ShapeDtypeStruct((m, d), F32),
        compiler_params=_cparams(("parallel",)),
        name="out_proj",
    )(x2d, a, w)


def _ffn_kernel(x_ref, nw_ref, wg_ref, wu_ref, wd_ref, o_ref, h_ref, acc_ref):
    f = pl.program_id(1)

    @pl.when(f == 0)
    def _():
        h_ref[...] = _rms(x_ref[...], nw_ref[...]).astype(BF16)
        acc_ref[...] = jnp.zeros_like(acc_ref)

    h = h_ref[...]
    g = jnp.dot(h, wg_ref[...], preferred_element_type=F32)
    u = jnp.dot(h, wu_ref[...], preferred_element_type=F32)
    a = (g * (1.0 / (1.0 + jnp.exp(-g))) * u).astype(BF16)
    acc_ref[...] += jnp.dot(a, wd_ref[...], preferred_element_type=F32)

    @pl.when(f == pl.num_programs(1) - 1)
    def _():
        o_ref[...] = x_ref[...] + acc_ref[...]


def _ffn(x2d, norm_w, wg, wu, wd, *, tm, tf):
    m, d = x2d.shape
    dff = wg.shape[1]
    return pl.pallas_call(
        _ffn_kernel,
        grid=(m // tm, dff // tf),
        in_specs=[
            pl.BlockSpec((tm, d), lambda i, f: (i, 0)),
            pl.BlockSpec((1, d), lambda i, f: (0, 0)),
            pl.BlockSpec((d, tf), lambda i, f: (0, f)),
            pl.BlockSpec((d, tf), lambda i, f: (0, f)),
            pl.BlockSpec((tf, d), lambda i, f: (f, 0)),
        ],
        out_specs=pl.BlockSpec((tm, d), lambda i, f: (i, 0)),
        out_shape=jax.ShapeDtypeStruct((m, d), F32),
        scratch_shapes=[pltpu.VMEM((tm, d), BF16), pltpu.VMEM((tm, d), F32)],
        compiler_params=_cparams(("parallel", "arbitrary")),
        name="ffn",
    )(x2d, norm_w, wg, wu, wd)


def _band_kernel(*refs, grp, n_off, nsub, has_sinks, with_lse):
    if has_sinks:
        sink_ref, q_ref, k_ref, v_ref, b_ref = refs[:5]
        outs = refs[5:]
    else:
        q_ref, k_ref, v_ref, b_ref = refs[:4]
        outs = refs[4:]
    o_ref = outs[0]
    g = pl.program_id(1)
    i0 = pl.program_id(2) * nsub
    tiles = {}

    def kv_tile(rel):
        if rel not in tiles:
            start = pl.multiple_of(jnp.maximum(i0 + rel, 0) * QBLK, QBLK)
            tiles[rel] = (k_ref[0, pl.ds(start, QBLK), :], v_ref[0, pl.ds(start, QBLK), :])
        return tiles[rel]

    def head_rows(x, h):
        return x[h * QBLK:(h + 1) * QBLK]

    for j in range(nsub):
        rs = slice(j * QBLK, (j + 1) * QBLK)
        kv = [kv_tile(j - off) for off in range(n_off)]
        kcat = jnp.concatenate([t[0] for t in kv], axis=0)
        vcat = jnp.concatenate([t[1] for t in kv], axis=0)
        tidx = [jnp.where(i0 + j - off >= 0, off, n_off) for off in range(n_off)]
        q = jnp.concatenate([q_ref[0, rs, h * HEAD_DIM:(h + 1) * HEAD_DIM] for h in range(grp)], axis=0)
        s = lax.dot_general(q, kcat, (((1,), (1,)), ((), ())), preferred_element_type=F32)
        st = [[head_rows(s, h)[:, off * QBLK:(off + 1) * QBLK] + b_ref[h, tidx[off]] for off in range(n_off)]
              for h in range(grp)]
        m = jnp.max(jnp.concatenate([functools.reduce(jnp.maximum, r) for r in st], axis=0), axis=-1, keepdims=True)
        p = [[jnp.exp(st[h][off] - head_rows(m, h)) for off in range(n_off)] for h in range(grp)]
        l = jnp.sum(jnp.concatenate([functools.reduce(jnp.add, r) for r in p], axis=0), axis=-1, keepdims=True)
        pb = jnp.concatenate([jnp.concatenate([t.astype(BF16) for t in r], axis=1) for r in p], axis=0)
        o = jnp.dot(pb, vcat, preferred_element_type=F32)
        den = l
        if has_sinks:
            sink = jnp.concatenate([jnp.full((QBLK, 1), sink_ref[g * grp + h], F32) for h in range(grp)], axis=0)
            den = l + jnp.exp(sink - m)
        o = o / den
        for h in range(grp):
            sl = slice(h * HEAD_DIM, (h + 1) * HEAD_DIM)
            o_ref[0, rs, sl] = head_rows(o, h).astype(o_ref.dtype)
            if with_lse:
                outs[1][0, rs, sl] = jnp.broadcast_to(head_rows(m + jnp.log(l), h), (QBLK, HEAD_DIM))


def _band_attn(q_arr, k_arr, v_arr, bias, *, n_kv, grp, q_col, k_col, v_col, sinks=None, with_lse=False):
    n, L, _ = q_arr.shape
    n_off = bias.shape[1] - 1
    gw = grp * HEAD_DIM
    nsub = math.gcd(BAND_SUBBLOCKS, L // QBLK)
    rows = nsub * QBLK
    kern = functools.partial(_band_kernel, grp=grp, n_off=n_off, nsub=nsub, has_sinks=sinks is not None,
                             with_lse=with_lse)
    in_specs = [
        pl.BlockSpec((1, rows, gw), lambda b, g, i: (b, i, q_col + g)),
        pl.BlockSpec((1, L, HEAD_DIM), lambda b, g, i: (b, 0, k_col + g)),
        pl.BlockSpec((1, L, HEAD_DIM), lambda b, g, i: (b, 0, v_col + g)),
        pl.BlockSpec((grp, n_off + 1, QBLK, LANES), lambda b, g, i: (g, 0, 0, 0)),
    ]
    args = [q_arr, k_arr, v_arr, bias]
    if sinks is not None:
        in_specs = [pl.BlockSpec(memory_space=pltpu.SMEM)] + in_specs
        args = [sinks] + args
    o_spec = pl.BlockSpec((1, rows, gw), lambda b, g, i: (b, i, g))
    o_shape = jax.ShapeDtypeStruct((n, L, n_kv * gw), F32)
    return pl.pallas_call(
        kern,
        grid=(n, n_kv, L // rows),
        in_specs=in_specs,
        out_specs=(o_spec, o_spec) if with_lse else o_spec,
        out_shape=(o_shape, o_shape) if with_lse else o_shape,
        compiler_params=_cparams(("parallel", "parallel", "arbitrary")),
        name="band_attn",
    )(*args)


def _compress_kernel(x_ref, pe_ref, w1_ref, w2_ref, gain_ref, o_ref, *, n_cmp):
    half = w1_ref.shape[1] // 2
    x = x_ref[0, 0, 0]
    y0 = jnp.dot(x, w1_ref[0, :half], preferred_element_type=F32)
    y1 = jnp.dot(x, w1_ref[0, half:], preferred_element_type=F32)
    pe = jnp.dot(pe_ref[0], w1_ref[0], preferred_element_type=F32)[0:1]
    rows = x.shape[0]
    c = y0 + pltpu.roll(y1, rows - 1, 0) + pe
    gl = 0.5 * c * (1.0 + jnp.tanh(math.sqrt(2.0 / math.pi) * (c + 0.044715 * (c * c * c))))
    out = jnp.dot(gl.astype(BF16), w2_ref[0], preferred_element_type=F32)
    out = jnp.where(pl.program_id(0) == 0, _rms(out, gain_ref[...]), out)
    valid = lax.broadcasted_iota(jnp.int32, out.shape, 0) < n_cmp
    o_ref[0, 0, 0] = jnp.where(valid, out, 0.0).astype(o_ref.dtype)


def _compress(xr, pe, w1, w2, gain, n_cmp):
    _, B, hk, rows, wide = xr.shape
    return pl.pallas_call(
        functools.partial(_compress_kernel, n_cmp=n_cmp),
        grid=(2, B, hk),
        in_specs=[
            pl.BlockSpec((1, 1, 1, rows, wide), lambda s, b, g: (s, b, g, 0, 0)),
            pl.BlockSpec((1, 8, 2 * wide), lambda s, b, g: (s, 0, 0)),
            pl.BlockSpec((1, 2 * wide, HEAD_DIM), lambda s, b, g: (s, 0, 0)),
            pl.BlockSpec((1, HEAD_DIM, HEAD_DIM), lambda s, b, g: (s, 0, 0)),
            pl.BlockSpec((1, HEAD_DIM), lambda s, b, g: (0, 0)),
        ],
        out_specs=pl.BlockSpec((1, 1, 1, rows, HEAD_DIM), lambda s, b, g: (s, b, g, 0, 0)),
        out_shape=jax.ShapeDtypeStruct((2, B, hk, rows, HEAD_DIM), BF16),
        compiler_params=_cparams(("arbitrary", "arbitrary", "arbitrary")),
        name="nsa_compress",
    )(xr, pe, w1, w2, gain)


def _cmp_select_kernel(q_ref, kc_ref, vc_ref, b_ref, ovl_ref, o_ref, sel_ref, *, n_slc, nsub):
    i0 = pl.program_id(2) * nsub
    rows = nsub * QBLK
    kc = kc_ref[0, 0, 0]
    vc = vc_ref[0, 0, 0]
    imp = jnp.zeros((rows, LANES), F32)
    for h in range(A_GRP):
        sl = slice(h * HEAD_DIM, (h + 1) * HEAD_DIM)
        s = lax.dot_general(q_ref[0, :, sl], kc, (((1,), (1,)), ((), ())), preferred_element_type=F32) + b_ref[h]
        m = jnp.max(s, axis=-1, keepdims=True)
        e = jnp.exp(s - jnp.where(m > 0.5 * NEG, m, 0.0))
        den = jnp.sum(e, axis=-1, keepdims=True)
        p = (e / jnp.where(den > 0, den, 1.0)).astype(BF16)
        o_ref[0, :, sl] = jnp.dot(p, vc, preferred_element_type=F32)
        imp = imp + jnp.dot(p, ovl_ref[...], preferred_element_type=F32)
    t = i0 * QBLK + lax.broadcasted_iota(jnp.int32, (rows, LANES), 0)
    blk = lax.broadcasted_iota(jnp.int32, (rows, LANES), 1)
    cur = t // SLC_BLOCK
    imp = jnp.where(blk == 0, FORCED_SCORE, imp)
    imp = jnp.where(blk == cur, FORCED_SCORE, imp)
    imp = jnp.where(blk == cur - 1, FORCED_SCORE, imp)
    imp = jnp.where(blk * SLC_BLOCK <= t, imp, NEG)
    imp = jnp.where(blk < n_slc, imp, 2.0 * NEG)
    imp_t = jnp.concatenate([imp[j * QBLK:(j + 1) * QBLK].T for j in range(nsub)], axis=1)
    ngrp = -(-n_slc // SUBLANES)
    cands = [imp_t[r * SUBLANES:(r + 1) * SUBLANES] for r in range(ngrp)]
    row_id = lax.broadcasted_iota(jnp.int32, (SUBLANES, rows), 0)
    ranks = [jnp.zeros((SUBLANES, rows), F32) for _ in range(ngrp)]
    for j in range(n_slc):
        other = imp_t[j:j + 1, :]
        for r in range(ngrp):
            if r * SUBLANES > j:
                beats = other >= cands[r]
            elif (r + 1) * SUBLANES <= j:
                beats = other > cands[r]
            else:
                beats = jnp.where(row_id > j - r * SUBLANES, jnp.where(other >= cands[r], 1.0, 0.0),
                                  jnp.where(other > cands[r], 1.0, 0.0)) > 0.5
            ranks[r] = ranks[r] + jnp.where(beats, 1.0, 0.0)
    sel_t = jnp.concatenate([jnp.where(r < float(min(SLC_TOPK, n_slc)), 1.0, 0.0) for r in ranks], axis=0)
    if ngrp * SUBLANES < LANES:
        sel_t = jnp.concatenate([sel_t, jnp.zeros((LANES - ngrp * SUBLANES, rows), F32)], axis=0)
    for j in range(nsub):
        sel_ref[0, 0, j * QBLK:(j + 1) * QBLK, :] = sel_t[:, j * QBLK:(j + 1) * QBLK].T.astype(sel_ref.dtype)


def _cmp_select(proj3, kvc, bias_c, ovl, n_slc):
    B, T, _ = proj3.shape
    ncp = kvc.shape[3]
    nsub = math.gcd(CMP_SUBBLOCKS, T // QBLK)
    rows = nsub * QBLK
    return pl.pallas_call(
        functools.partial(_cmp_select_kernel, n_slc=n_slc, nsub=nsub),
        grid=(B, A_KV_HEADS, T // rows),
        in_specs=[
            pl.BlockSpec((1, rows, A_GRP * HEAD_DIM), lambda b, g, i: (b, i, g)),
            pl.BlockSpec((1, 1, 1, ncp, HEAD_DIM), lambda b, g, i: (0, b, g, 0, 0)),
            pl.BlockSpec((1, 1, 1, ncp, HEAD_DIM), lambda b, g, i: (1, b, g, 0, 0)),
            pl.BlockSpec((A_GRP, rows, ncp), lambda b, g, i: (g, i, 0)),
            pl.BlockSpec((ncp, LANES), lambda b, g, i: (0, 0)),
        ],
        out_specs=(
            pl.BlockSpec((1, rows, A_GRP * HEAD_DIM), lambda b, g, i: (b, i, g)),
            pl.BlockSpec((1, 1, rows, LANES), lambda b, g, i: (b, g, i, 0)),
        ),
        out_shape=(
            jax.ShapeDtypeStruct((B, T, A_Q_HEADS * HEAD_DIM), F32),
            jax.ShapeDtypeStruct((B, A_KV_HEADS, T, LANES), BF16),
        ),
        compiler_params=_cparams(("parallel", "parallel", "arbitrary")),
        name="nsa_cmp_select",
    )(proj3, kvc, kvc, bias_c, ovl)


def _slc_kernel(q_ref, k_ref, v_ref, sel_ref, e_ref, b_ref, o_ref, *scratch, n_tiles, nq):
    ig = pl.program_id(1)
    chains = [(g, a) for g in range(A_KV_HEADS) for a in range(nq)]
    m_refs, l_refs, acc_refs = scratch[0::3], scratch[1::3], scratch[2::3]
    qs, sels = [], []
    for ci, (g, a) in enumerate(chains):
        m_refs[ci][...] = jnp.full(m_refs[ci].shape, NEG, F32)
        l_refs[ci][...] = jnp.zeros(l_refs[ci].shape, F32)
        acc_refs[ci][...] = jnp.zeros(acc_refs[ci].shape, F32)
        rs = slice(a * QBLK, (a + 1) * QBLK)
        qs.append(jnp.concatenate([q_ref[0, rs, (g * A_GRP + h) * HEAD_DIM:(g * A_GRP + h + 1) * HEAD_DIM]
                                   for h in range(A_GRP)], axis=0))
        sels.append(sel_ref[0, g, rs, :])
    kw = SLC_CHUNK * QBLK

    def body(c, carry):
        start = pl.multiple_of(c * kw, kw)
        base = ig * nq - c * SLC_CHUNK + 1
        tidx = {d: jnp.clip(base + d, 0, n_tiles - 1) for d in range(-(SLC_CHUNK - 1), nq)}
        kts = [k_ref[0, pl.ds(start, kw), g * HEAD_DIM:(g + 1) * HEAD_DIM] for g in range(A_KV_HEADS)]
        vts = [v_ref[0, pl.ds(start, kw), g * HEAD_DIM:(g + 1) * HEAD_DIM] for g in range(A_KV_HEADS)]
        ech = e_ref[:, pl.ds(start, kw)]
        ss = [lax.dot_general(qs[ci], kts[g], (((1,), (1,)), ((), ())), preferred_element_type=F32)
              for ci, (g, a) in enumerate(chains)]
        chosens = [jnp.dot(sels[ci], ech, preferred_element_type=F32) for ci in range(len(chains))]
        pbs, alphas = [], []
        for ci, (g, a) in enumerate(chains):
            madd = (chosens[ci] - 1.0) * (-NEG)
            s = ss[ci]
            rows = [[s[h * QBLK:(h + 1) * QBLK, j * QBLK:(j + 1) * QBLK] + b_ref[g * A_GRP + h, tidx[a - j]]
                     + madd[:, j * QBLK:(j + 1) * QBLK] for j in range(SLC_CHUNK)] for h in range(A_GRP)]
            tile_max = jnp.concatenate([functools.reduce(jnp.maximum, r) for r in rows], axis=0)
            m_old = m_refs[ci][...]
            m_new = jnp.maximum(m_old, jnp.max(tile_max, axis=-1, keepdims=True))
            alpha = jnp.exp(m_old - m_new)
            p = [[jnp.exp(t - m_new[h * QBLK:(h + 1) * QBLK]) for t in r] for h, r in enumerate(rows)]
            tile_sum = jnp.concatenate([functools.reduce(jnp.add, r) for r in p], axis=0)
            l_refs[ci][...] = alpha * l_refs[ci][...] + jnp.sum(tile_sum, axis=-1, keepdims=True)
            pbs.append(jnp.concatenate([jnp.concatenate([t.astype(BF16) for t in r], axis=1) for r in p], axis=0))
            alphas.append(alpha)
            m_refs[ci][...] = m_new
        for ci, (g, a) in enumerate(chains):
            acc_refs[ci][...] = alphas[ci] * acc_refs[ci][...] + jnp.dot(pbs[ci], vts[g], preferred_element_type=F32)
        return carry

    lax.fori_loop(0, (ig * nq + nq - 1) // SLC_CHUNK + 1, body, 0)
    for ci, (g, a) in enumerate(chains):
        out = acc_refs[ci][...] / l_refs[ci][...]
        for h in range(A_GRP):
            col = (g * A_GRP + h) * HEAD_DIM
            o_ref[0, a * QBLK:(a + 1) * QBLK, col:col + HEAD_DIM] = out[h * QBLK:(h + 1) * QBLK]


def _slc_attn(proj3, sel, expand, bias):
    B, T, _ = proj3.shape
    n_tiles = bias.shape[1]
    nq = SLC_QTILES
    qrows = nq * QBLK
    assert SLC_CHUNK % nq == 0 and T % (SLC_CHUNK * QBLK) == 0
    chain_rows = A_GRP * QBLK
    n_chains = A_KV_HEADS * nq
    return pl.pallas_call(
        functools.partial(_slc_kernel, n_tiles=n_tiles, nq=nq),
        grid=(B, T // qrows),
        in_specs=[
            pl.BlockSpec((1, qrows, A_Q_HEADS * HEAD_DIM), lambda b, i: (b, i, COL_QA // A_Q_HEADS)),
            pl.BlockSpec((1, T, A_KV_HEADS * HEAD_DIM), lambda b, i: (b, 0, COL_KSA // A_KV_HEADS)),
            pl.BlockSpec((1, T, A_KV_HEADS * HEAD_DIM), lambda b, i: (b, 0, COL_VSA // A_KV_HEADS)),
            pl.BlockSpec((1, A_KV_HEADS, qrows, LANES), lambda b, i: (b, 0, i, 0)),
            pl.BlockSpec((LANES, T), lambda b, i: (0, 0)),
            pl.BlockSpec((A_Q_HEADS, n_tiles, QBLK, LANES), lambda b, i: (0, 0, 0, 0)),
        ],
        out_specs=pl.BlockSpec((1, qrows, A_Q_HEADS * HEAD_DIM), lambda b, i: (b, i, 0)),
        out_shape=jax.ShapeDtypeStruct((B, T, A_Q_HEADS * HEAD_DIM), F32),
        scratch_shapes=[pltpu.VMEM((chain_rows, 1), F32), pltpu.VMEM((chain_rows, 1), F32),
                        pltpu.VMEM((chain_rows, HEAD_DIM), F32)] * n_chains,
        compiler_params=_cparams(("parallel", "arbitrary")),
        name="nsa_slc_attn",
    )(proj3, proj3, proj3, sel, expand, bias)


def _combine_kernel(gate_ref, ocmp_ref, oslc_ref, owin_ref, ob_ref, *rest):
    npair = len(DIL_PAIRS)
    oc_refs, lse_refs, o_ref = rest[:npair], rest[npair:2 * npair], rest[2 * npair]
    gate = 1.0 / (1.0 + jnp.exp(-gate_ref[...]))
    for h in range(A_Q_HEADS):
        sl = slice(h * HEAD_DIM, (h + 1) * HEAD_DIM)
        o = (gate[:, 3 * h:3 * h + 1] * ocmp_ref[:, sl] + gate[:, 3 * h + 1:3 * h + 2] * oslc_ref[:, sl]
             + gate[:, 3 * h + 2:3 * h + 3] * owin_ref[:, sl])
        o_ref[:, sl] = o.astype(o_ref.dtype)
    base = A_Q_HEADS * HEAD_DIM
    width = B_Q_HEADS * HEAD_DIM
    o_ref[:, base:base + width] = ob_ref[...].astype(o_ref.dtype)
    base += width
    lses = [r[...] for r in lse_refs]
    mx = functools.reduce(jnp.maximum, lses)
    ws = [jnp.exp(x - mx) for x in lses]
    tot = functools.reduce(jnp.add, ws)
    for gidx in range(npair):
        w = C_GRP * HEAD_DIM
        o_ref[:, base + gidx * w:base + (gidx + 1) * w] = (oc_refs[gidx][...] * (ws[gidx] / tot)).astype(o_ref.dtype)


def _combine(gates, o_cmp, o_slc, o_win, o_b, o_cs, lses, *, tm):
    m = gates.shape[0]
    d_out = (A_Q_HEADS + B_Q_HEADS + C_Q_HEADS) * HEAD_DIM
    row = lambda w: pl.BlockSpec((tm, w), lambda i: (i, 0))
    ins = [gates, o_cmp, o_slc, o_win, o_b, *o_cs, *lses]
    return pl.pallas_call(
        _combine_kernel,
        grid=(m // tm,),
        in_specs=[row(a.shape[1]) for a in ins],
        out_specs=row(d_out),
        out_shape=jax.ShapeDtypeStruct((m, d_out), BF16),
        compiler_params=_cparams(("parallel",)),
        name="combine",
    )(*ins)


def _to_sub(a, d):
    B, T, c = a.shape
    return a.reshape(B, T // d, d, c).transpose(0, 2, 1, 3).reshape(B * d, T // d, c)


def _from_sub(a, d, B):
    _, ls, c = a.shape
    return a.reshape(B, d, ls, c).transpose(0, 2, 1, 3).reshape(B, ls * d, c)


def _mixers(proj3, gates, layer_params, tables):
    B, T, _ = proj3.shape
    cmp_pe, cmp_w1, cmp_w2, kc_gain, sinks = layer_params
    n_cmp = (T - CMP_BLOCK) // CMP_STRIDE + 1
    n_slc = T // SLC_BLOCK
    ncp = T // CMP_STRIDE

    def cmp_rows(col):
        a = proj3[:, :, col * LANES:(col + A_KV_HEADS) * LANES].reshape(B, ncp, CMP_STRIDE, A_KV_HEADS, HEAD_DIM)
        return a.transpose(0, 3, 1, 2, 4).reshape(B, A_KV_HEADS, ncp, CMP_STRIDE * HEAD_DIM)

    xr = jnp.stack([cmp_rows(COL_KCA), cmp_rows(COL_VCA)])
    pe = jnp.broadcast_to(cmp_pe.reshape(2, 1, CMP_BLOCK * HEAD_DIM), (2, 8, CMP_BLOCK * HEAD_DIM)).astype(BF16)
    kvc = _compress(xr, pe, cmp_w1.astype(BF16), cmp_w2.astype(BF16), kc_gain.reshape(1, HEAD_DIM), n_cmp)

    o_cmp, sel = _cmp_select(proj3, kvc, tables["bias_c"], tables["ovl"], n_slc)
    o_slc = _slc_attn(proj3, sel, tables["expand"], tables["bias_slc"])
    o_win = _band_attn(proj3, proj3, proj3, tables["bias_win"], n_kv=A_KV_HEADS, grp=A_GRP,
                       q_col=COL_QA // A_GRP, k_col=COL_KWA, v_col=COL_VWA)
    o_b = _band_attn(proj3, proj3, proj3, tables["bias_b"], n_kv=B_KV_HEADS, grp=B_GRP,
                     q_col=COL_QB // B_GRP, k_col=COL_KB, v_col=COL_VB, sinks=sinks)
    o_cs, lses = [], []
    for gidx, (_, dil) in enumerate(DIL_PAIRS):
        if dil == 1:
            o, lse = _band_attn(proj3, proj3, proj3, tables["bias_c%d" % gidx], n_kv=1, grp=C_GRP,
                                q_col=COL_QC // C_GRP + gidx, k_col=COL_KC + gidx, v_col=COL_VC + gidx,
                                with_lse=True)
        else:
            qs = _to_sub(proj3[:, :, (COL_QC + C_GRP * gidx) * LANES:(COL_QC + C_GRP * (gidx + 1)) * LANES], dil)
            ks = _to_sub(proj3[:, :, (COL_KC + gidx) * LANES:(COL_KC + gidx + 1) * LANES], dil)
            vs = _to_sub(proj3[:, :, (COL_VC + gidx) * LANES:(COL_VC + gidx + 1) * LANES], dil)
            o, lse = _band_attn(qs, ks, vs, tables["bias_c%d" % gidx], n_kv=1, grp=C_GRP,
                                q_col=0, k_col=0, v_col=0, with_lse=True)
            o, lse = _from_sub(o, dil, B), _from_sub(lse, dil, B)
        o_cs.append(o.reshape(B * T, -1))
        lses.append(lse.reshape(B * T, -1))
    return _combine(gates, o_cmp.reshape(B * T, -1), o_slc.reshape(B * T, -1), o_win.reshape(B * T, -1),
                    o_b.reshape(B * T, -1), o_cs, lses, tm=256)


def _build_tables(rel_bias, T):
    n_cmp = (T - CMP_BLOCK) // CMP_STRIDE + 1
    n_slc = T // SLC_BLOCK
    ncp = T // CMP_STRIDE
    nq = T // QBLK
    tables = {}
    tables["bias_c"] = _bias_tiles(rel_bias, 0, A_Q_HEADS, nq, width=ncp, kstride=CMP_STRIDE,
                                   koff=CMP_BLOCK - 1, ncols=n_cmp).reshape(A_Q_HEADS, T, ncp)
    n_sat = -(-(SAT_DIST + QBLK - 1) // QBLK) + 1
    tables["bias_slc"] = _bias_tiles(rel_bias, 0, A_Q_HEADS, min(n_sat, nq) + 1, koff=QBLK)
    win = NSA_WINDOW - 1
    tables["bias_win"] = _bias_tiles(rel_bias, 0, A_Q_HEADS, -(-win // QBLK) + 2, max_dist=win)
    swa = SWA_WINDOW - 1
    tables["bias_b"] = _bias_tiles(rel_bias, A_Q_HEADS, B_Q_HEADS, -(-swa // QBLK) + 2, max_dist=swa)
    for gidx, (w, dil) in enumerate(DIL_PAIRS):
        md = w // dil
        tables["bias_c%d" % gidx] = _bias_tiles(rel_bias, A_Q_HEADS + B_Q_HEADS + C_GRP * gidx, C_GRP,
                                                -(-md // QBLK) + 2, dscale=dil, max_dist=md)
    c0 = np.arange(ncp)[:, None] * CMP_STRIDE
    s0 = np.arange(LANES)[None, :] * SLC_BLOCK
    ovl = np.clip(np.minimum(c0 + CMP_BLOCK, s0 + SLC_BLOCK) - np.maximum(c0, s0), 0, None) / CMP_BLOCK
    ovl = ovl * (np.arange(ncp)[:, None] < n_cmp) * (np.arange(LANES)[None, :] < n_slc)
    tables["ovl"] = jnp.asarray(ovl, BF16)
    expand = (np.arange(T)[None, :] // SLC_BLOCK) == np.arange(LANES)[:, None]
    tables["expand"] = jnp.asarray(expand, BF16)
    return tables


def _proj_gain(g):
    ones = jnp.ones((HEAD_DIM,), F32)
    spec = [(g[0] * SCALE, 6), (ones, 4), (g[2], 2), (ones, 2), (g[3], 2), (ones, 2),
            (g[4] * SCALE, 4), (g[5], 2), (ones, 2), (g[6] * SCALE, 6), (g[7], 3), (ones, 3)]
    assert sum(n for _, n in spec) == N_MAIN_BLOCKS
    return jnp.concatenate([jnp.tile(v, n) for v, n in spec]).reshape(1, N_MAIN)


def kernel(x, norm_attn, w_in, qk_gain, cmp_pe, cmp_w1, cmp_w2, sinks, rel_bias, w_out, norm_ffn, w_gate, w_up, w_down):
    B, T, D = x.shape
    depth = w_in.shape[0]
    tables = _build_tables(rel_bias, T)
    x2 = x.reshape(B * T, D)
    w_all = _wprep(w_in, tr=256)
    tm = min(512, B * T)
    for l in range(depth):
        proj, gates = _proj(x2, norm_attn[l].reshape(1, D), w_all[l], _proj_gain(qk_gain[l]), tm=tm)
        mix = _mixers(proj.reshape(B, T, N_MAIN), gates,
                      (cmp_pe[l], cmp_w1[l], cmp_w2[l], qk_gain[l][1], sinks[l]), tables)
        x2 = _outproj(x2, mix, w_out[l].astype(BF16), tm=tm)
        x2 = _ffn(x2, norm_ffn[l].reshape(1, D), w_gate[l].astype(BF16), w_up[l].astype(BF16),
                  w_down[l].astype(BF16), tm=tm, tf=512)
    return x2.reshape(B, T, D)
```

```python
import functools
import math

import numpy as np
import jax
import jax.numpy as jnp
from jax import lax
from jax.experimental import pallas as pl
from jax.experimental.pallas import tpu as pltpu

F32 = jnp.float32
BF16 = jnp.bfloat16

HEAD_DIM = 128
LANES = 128
SUBLANES = 8
QBLK = 128
BAND_SUBBLOCKS = 8
CMP_SUBBLOCKS = 4
A_Q_HEADS, A_KV_HEADS = 6, 2
A_GRP = A_Q_HEADS // A_KV_HEADS
B_Q_HEADS, B_KV_HEADS = 4, 2
B_GRP = B_Q_HEADS // B_KV_HEADS
DIL_PAIRS = ((128, 1), (512, 4), (2048, 16))
C_GRP = 2
C_Q_HEADS = C_GRP * len(DIL_PAIRS)
CMP_BLOCK, CMP_STRIDE = 32, 16
SLC_BLOCK, SLC_TOPK = 64, 16
SLC_CHUNK = 4
SLC_QTILES = 1
NSA_WINDOW, SWA_WINDOW = 512, 128
FORCED_SCORE = 1.0e4
REL_BUCKETS, REL_MAX_EXACT, REL_MAX_DIST = 32, 16, 2048
SCALE = HEAD_DIM ** -0.5
LOG2E = math.log2(math.e)
LN2 = math.log(2.0)
EPS = 1e-6
NEG = -1e30
VMEM_LIMIT = 56 * 1024 * 1024

COL_QA, COL_KCA, COL_VCA, COL_KSA, COL_VSA, COL_KWA, COL_VWA = 0, 6, 8, 10, 12, 14, 16
COL_QB, COL_KB, COL_VB, COL_QC, COL_KC, COL_VC = 18, 22, 24, 26, 32, 35
N_MAIN_BLOCKS = 38
N_MAIN = N_MAIN_BLOCKS * LANES
GATE_START = 2304
N_GATES = A_Q_HEADS * 3
PROJ_NORMED = ([True] * 6 + [False] * 4 + [True] * 2 + [False] * 2 + [True] * 2 + [False] * 2
               + [True] * 6 + [False] * 2 + [True] * 9 + [False] * 3)
PROJ_SUBTILE = 256
OUT_SUBTILE = 512


def _bucket_starts():
    d = np.arange(0, 1 << 17)
    out = []
    for dt in (np.float32, np.float64):
        far = np.maximum(d, REL_MAX_EXACT).astype(dt)
        lb = REL_MAX_EXACT + (np.log(far / dt(REL_MAX_EXACT)) / dt(math.log(REL_MAX_DIST / REL_MAX_EXACT))
                              * dt(REL_BUCKETS - REL_MAX_EXACT)).astype(np.int64)
        out.append(np.where(d < REL_MAX_EXACT, d, np.minimum(lb, REL_BUCKETS - 1)))
    assert (out[0] == out[1]).all() and (np.diff(out[0]) >= 0).all()
    return [int(np.argmax(out[0] >= b)) for b in range(REL_BUCKETS)]


BUCKET_START = _bucket_starts()
SAT_DIST = BUCKET_START[REL_BUCKETS - 1]


def _cparams(sem, vmem=VMEM_LIMIT):
    return pltpu.CompilerParams(dimension_semantics=sem, vmem_limit_bytes=vmem)


def _bias_tile_kernel(tab_ref, o_ref, *, head0, width, kstride, koff, dscale, max_dist, ncols):
    h = pl.program_id(0) + head0
    t = pl.program_id(1)
    r = lax.broadcasted_iota(jnp.int32, (QBLK, width), 0)
    c = lax.broadcasted_iota(jnp.int32, (QBLK, width), 1)
    dist = t * QBLK + r - kstride * c - koff
    d = dist * dscale
    val = jnp.full((QBLK, width), tab_ref[REL_BUCKETS - 1, h] * LOG2E, F32)
    for b in range(REL_BUCKETS - 2, -1, -1):
        val = jnp.where(d < BUCKET_START[b + 1], tab_ref[b, h] * LOG2E, val)
    val = jnp.where(dist >= 0, val, NEG)
    if max_dist is not None:
        val = jnp.where(dist <= max_dist, val, NEG)
    if ncols < width:
        val = jnp.where(c < ncols, val, NEG)
    o_ref[0, 0] = val


def _bias_tiles(rel_bias, head0, nheads, ntiles, *, width=LANES, kstride=1, koff=0, dscale=1,
                max_dist=None, ncols=None):
    ncols = width if ncols is None else ncols
    kern = functools.partial(_bias_tile_kernel, head0=head0, width=width, kstride=kstride, koff=koff,
                             dscale=dscale, max_dist=max_dist, ncols=ncols)
    return pl.pallas_call(
        kern,
        grid=(nheads, ntiles),
        in_specs=[pl.BlockSpec(memory_space=pltpu.SMEM)],
        out_specs=pl.BlockSpec((1, 1, QBLK, width), lambda h, t: (h, t, 0, 0)),
        out_shape=jax.ShapeDtypeStruct((nheads, ntiles, QBLK, width), F32),
        compiler_params=_cparams(("parallel", "parallel")),
        name="bias_tiles",
    )(rel_bias)


def _rms(x, w):
    ms = jnp.mean(x * x, axis=-1, keepdims=True)
    return x * lax.rsqrt(ms + EPS) * w


def _wprep_kernel(w_ref, o_ref):
    o_ref[0, :, :GATE_START] = w_ref[0, :, :GATE_START].astype(BF16)
    o_ref[0, :, GATE_START:N_MAIN] = w_ref[0, :, GATE_START + N_GATES:].astype(BF16)
    o_ref[0, :, N_MAIN:] = w_ref[0, :, GATE_START:GATE_START + LANES].astype(BF16)


def _wprep(w_in, *, tr):
    depth, d, n_in = w_in.shape
    return pl.pallas_call(
        _wprep_kernel,
        grid=(depth, d // tr),
        in_specs=[pl.BlockSpec((1, tr, n_in), lambda l, i: (l, i, 0))],
        out_specs=pl.BlockSpec((1, tr, N_MAIN + LANES), lambda l, i: (l, i, 0)),
        out_shape=jax.ShapeDtypeStruct((depth, d, N_MAIN + LANES), BF16),
        compiler_params=_cparams(("parallel", "parallel")),
        name="w_in_prep",
    )(w_in)


def _proj_kernel(x_ref, nw_ref, w_ref, gain_ref, o_ref, gate_ref):
    h = _rms(x_ref[...], nw_ref[...]).astype(BF16)
    for c0 in range(0, N_MAIN, PROJ_SUBTILE):
        width = min(PROJ_SUBTILE, N_MAIN - c0)
        acc = jnp.dot(h, w_ref[:, c0:c0 + width], preferred_element_type=F32)
        for c in range(width // LANES):
            blk = c0 // LANES + c
            sl = slice(blk * LANES, (blk + 1) * LANES)
            y = acc[:, c * LANES:(c + 1) * LANES]
            if PROJ_NORMED[blk]:
                ms = jnp.mean(y * y, axis=-1, keepdims=True)
                y = y * lax.rsqrt(ms + EPS) * gain_ref[:, sl]
            o_ref[:, sl] = y.astype(o_ref.dtype)
    gate_ref[...] = jnp.dot(h, w_ref[:, N_MAIN:], preferred_element_type=F32)


def _proj(x2d, norm_w, w, gain, *, tm):
    m, d = x2d.shape
    n = w.shape[1]
    return pl.pallas_call(
        _proj_kernel,
        grid=(m // tm,),
        in_specs=[
            pl.BlockSpec((tm, d), lambda i: (i, 0)),
            pl.BlockSpec((1, d), lambda i: (0, 0)),
            pl.BlockSpec((d, n), lambda i: (0, 0), pipeline_mode=pl.Buffered(1)),
            pl.BlockSpec((1, N_MAIN), lambda i: (0, 0)),
        ],
        out_specs=(pl.BlockSpec((tm, N_MAIN), lambda i: (i, 0)), pl.BlockSpec((tm, LANES), lambda i: (i, 0))),
        out_shape=(jax.ShapeDtypeStruct((m, N_MAIN), BF16), jax.ShapeDtypeStruct((m, LANES), F32)),
        compiler_params=_cparams(("parallel",)),
        name="in_proj",
    )(x2d, norm_w, w, gain)


def _outproj_kernel(x_ref, a_ref, w_ref, o_ref):
    a = a_ref[...]
    for c0 in range(0, o_ref.shape[1], OUT_SUBTILE):
        sl = slice(c0, c0 + OUT_SUBTILE)
        o_ref[:, sl] = x_ref[:, sl] + jnp.dot(a, w_ref[:, sl], preferred_element_type=F32)


def _outproj(x2d, a, w, *, tm):
    m, d = x2d.shape
    k = a.shape[1]
    return pl.pallas_call(
        _outproj_kernel,
        grid=(m // tm,),
        in_specs=[
            pl.BlockSpec((tm, d), lambda i: (i, 0)),
            pl.BlockSpec((tm, k), lambda i: (i, 0)),
            pl.BlockSpec((k, d), lambda i: (0, 0), pipeline_mode=pl.Buffered(1)),
        ],
        out_specs=pl.BlockSpec((tm, d), lambda i: (i, 0)),
        out_shape=jax.ShapeDtypeStruct((m, d), F32),
        compiler_params=_cparams(("parallel",)),
        name="out_proj",
    )(x2d, a, w)


def _ffn_kernel(x_ref, nw_ref, wg_ref, wu_ref, wd_ref, o_ref, h_ref, acc_ref):
    f = pl.program_id(1)

    @pl.when(f == 0)
    def _():
        h_ref[...] = _rms(x_ref[...], nw_ref[...]).astype(BF16)
        acc_ref[...] = jnp.zeros_like(acc_ref)

    h = h_ref[...]
    g = jnp.dot(h, wg_ref[...], preferred_element_type=F32)
    u = jnp.dot(h, wu_ref[...], preferred_element_type=F32)
    a = (g * (1.0 / (1.0 + jnp.exp(-g))) * u).astype(BF16)
    acc_ref[...] += jnp.dot(a, wd_ref[...], preferred_element_type=F32)

    @pl.when(f == pl.num_programs(1) - 1)
    def _():
        o_ref[...] = x_ref[...] + acc_ref[...]


def _ffn(x2d, norm_w, wg, wu, wd, *, tm, tf):
    m, d = x2d.shape
    dff = wg.shape[1]
    return pl.pallas_call(
        _ffn_kernel,
        grid=(m // tm, dff // tf),
        in_specs=[
            pl.BlockSpec((tm, d), lambda i, f: (i, 0)),
            pl.BlockSpec((1, d), lambda i, f: (0, 0)),
            pl.BlockSpec((d, tf), lambda i, f: (0, f)),
            pl.BlockSpec((d, tf), lambda i, f: (0, f)),
            pl.BlockSpec((tf, d), lambda i, f: (f, 0)),
        ],
        out_specs=pl.BlockSpec((tm, d), lambda i, f: (i, 0)),
        out_shape=jax.ShapeDtypeStruct((m, d), F32),
        scratch_shapes=[pltpu.VMEM((tm, d), BF16), pltpu.VMEM((tm, d), F32)],
        compiler_params=_cparams(("parallel", "arbitrary")),
        name="ffn",
    )(x2d, norm_w, wg, wu, wd)


def _band_kernel(*refs, grp, n_off, nsub, has_sinks, with_lse):
    if has_sinks:
        sink_ref, q_ref, k_ref, v_ref, b_ref = refs[:5]
        outs = refs[5:]
    else:
        q_ref, k_ref, v_ref, b_ref = refs[:4]
        outs = refs[4:]
    o_ref = outs[0]
    g = pl.program_id(1)
    i0 = pl.program_id(2) * nsub
    tiles = {}

    def kv_tile(rel):
        if rel not in tiles:
            start = pl.multiple_of(jnp.maximum(i0 + rel, 0) * QBLK, QBLK)
            tiles[rel] = (k_ref[0, pl.ds(start, QBLK), :], v_ref[0, pl.ds(start, QBLK), :])
        return tiles[rel]

    def head_rows(x, h):
        return x[h * QBLK:(h + 1) * QBLK]

    kvs = [[kv_tile(j - off) for off in range(n_off)] for j in range(nsub)]
    tidxs = [[jnp.where(i0 + j - off >= 0, off, n_off) for off in range(n_off)] for j in range(nsub)]
    scores = []
    for j in range(nsub):
        rs = slice(j * QBLK, (j + 1) * QBLK)
        kcat = jnp.concatenate([t[0] for t in kvs[j]], axis=0)
        q = jnp.concatenate([q_ref[0, rs, h * HEAD_DIM:(h + 1) * HEAD_DIM] for h in range(grp)], axis=0)
        scores.append(lax.dot_general(q, kcat, (((1,), (1,)), ((), ())), preferred_element_type=F32))
    probs, stats = [], []
    for j in range(nsub):
        s = scores[j]
        st = [[head_rows(s, h)[:, off * QBLK:(off + 1) * QBLK] + b_ref[h, tidxs[j][off]] for off in range(n_off)]
              for h in range(grp)]
        m = jnp.max(jnp.concatenate([functools.reduce(jnp.maximum, r) for r in st], axis=0), axis=-1, keepdims=True)
        p = [[jnp.exp2(st[h][off] - head_rows(m, h)) for off in range(n_off)] for h in range(grp)]
        l = jnp.sum(jnp.concatenate([functools.reduce(jnp.add, r) for r in p], axis=0), axis=-1, keepdims=True)
        probs.append(jnp.concatenate([jnp.concatenate([t.astype(BF16) for t in r], axis=1) for r in p], axis=0))
        stats.append((m, l))
    for j in range(nsub):
        rs = slice(j * QBLK, (j + 1) * QBLK)
        m, l = stats[j]
        vcat = jnp.concatenate([t[1] for t in kvs[j]], axis=0)
        o = jnp.dot(probs[j], vcat, preferred_element_type=F32)
        den = l
        if has_sinks:
            sink = jnp.concatenate([jnp.full((QBLK, 1), sink_ref[g * grp + h] * LOG2E, F32) for h in range(grp)], axis=0)
            den = l + jnp.exp2(sink - m)
        o = o / den
        for h in range(grp):
            sl = slice(h * HEAD_DIM, (h + 1) * HEAD_DIM)
            o_ref[0, rs, sl] = head_rows(o, h).astype(o_ref.dtype)
            if with_lse:
                outs[1][0, rs, sl] = jnp.broadcast_to(head_rows(m * LN2 + jnp.log(l), h), (QBLK, HEAD_DIM))


def _band_attn(q_arr, k_arr, v_arr, bias, *, n_kv, grp, q_col, k_col, v_col, sinks=None, with_lse=False):
    n, L, _ = q_arr.shape
    n_off = bias.shape[1] - 1
    gw = grp * HEAD_DIM
    nsub = math.gcd(BAND_SUBBLOCKS, L // QBLK)
    rows = nsub * QBLK
    kern = functools.partial(_band_kernel, grp=grp, n_off=n_off, nsub=nsub, has_sinks=sinks is not None,
                             with_lse=with_lse)
    in_specs = [
        pl.BlockSpec((1, rows, gw), lambda b, g, i: (b, i, q_col + g)),
        pl.BlockSpec((1, L, HEAD_DIM), lambda b, g, i: (b, 0, k_col + g)),
        pl.BlockSpec((1, L, HEAD_DIM), lambda b, g, i: (b, 0, v_col + g)),
        pl.BlockSpec((grp, n_off + 1, QBLK, LANES), lambda b, g, i: (g, 0, 0, 0)),
    ]
    args = [q_arr, k_arr, v_arr, bias]
    if sinks is not None:
        in_specs = [pl.BlockSpec(memory_space=pltpu.SMEM)] + in_specs
        args = [sinks] + args
    o_spec = pl.BlockSpec((1, rows, gw), lambda b, g, i: (b, i, g))
    o_shape = jax.ShapeDtypeStruct((n, L, n_kv * gw), F32)
    return pl.pallas_call(
        kern,
        grid=(n, n_kv, L // rows),
        in_specs=in_specs,
        out_specs=(o_spec, o_spec) if with_lse else o_spec,
        out_shape=(o_shape, o_shape) if with_lse else o_shape,
        compiler_params=_cparams(("parallel", "parallel", "arbitrary")),
        name="band_attn",
    )(*args)


def _compress_kernel(x_ref, pe_ref, w1_ref, w2_ref, gain_ref, o_ref, *, n_cmp):
    half = w1_ref.shape[1] // 2
    x = x_ref[0, 0, 0]
    y0 = jnp.dot(x, w1_ref[0, :half], preferred_element_type=F32)
    y1 = jnp.dot(x, w1_ref[0, half:], preferred_element_type=F32)
    pe = jnp.dot(pe_ref[0], w1_ref[0], preferred_element_type=F32)[0:1]
    rows = x.shape[0]
    c = y0 + pltpu.roll(y1, rows - 1, 0) + pe
    gl = 0.5 * c * (1.0 + jnp.tanh(math.sqrt(2.0 / math.pi) * (c + 0.044715 * (c * c * c))))
    out = jnp.dot(gl.astype(BF16), w2_ref[0], preferred_element_type=F32)
    out = jnp.where(pl.program_id(0) == 0, _rms(out, gain_ref[...]), out)
    valid = lax.broadcasted_iota(jnp.int32, out.shape, 0) < n_cmp
    o_ref[0, 0, 0] = jnp.where(valid, out, 0.0).astype(o_ref.dtype)


def _compress(xr, pe, w1, w2, gain, n_cmp):
    _, B, hk, rows, wide = xr.shape
    return pl.pallas_call(
        functools.partial(_compress_kernel, n_cmp=n_cmp),
        grid=(2, B, hk),
        in_specs=[
            pl.BlockSpec((1, 1, 1, rows, wide), lambda s, b, g: (s, b, g, 0, 0)),
            pl.BlockSpec((1, 8, 2 * wide), lambda s, b, g: (s, 0, 0)),
            pl.BlockSpec((1, 2 * wide, HEAD_DIM), lambda s, b, g: (s, 0, 0)),
            pl.BlockSpec((1, HEAD_DIM, HEAD_DIM), lambda s, b, g: (s, 0, 0)),
            pl.BlockSpec((1, HEAD_DIM), lambda s, b, g: (0, 0)),
        ],
        out_specs=pl.BlockSpec((1, 1, 1, rows, HEAD_DIM), lambda s, b, g: (s, b, g, 0, 0)),
        out_shape=jax.ShapeDtypeStruct((2, B, hk, rows, HEAD_DIM), BF16),
        compiler_params=_cparams(("arbitrary", "arbitrary", "arbitrary")),
        name="nsa_compress",
    )(xr, pe, w1, w2, gain)


def _cmp_select_kernel(q_ref, kc_ref, vc_ref, b_ref, ovl_ref, o_ref, sel_ref, *, n_slc, nsub):
    i0 = pl.program_id(2) * nsub
    rows = nsub * QBLK
    kc = kc_ref[0, 0, 0]
    vc = vc_ref[0, 0, 0]
    q_all = jnp.concatenate([q_ref[0, :, h * HEAD_DIM:(h + 1) * HEAD_DIM] for h in range(A_GRP)], axis=0)
    s_all = lax.dot_general(q_all, kc, (((1,), (1,)), ((), ())), preferred_element_type=F32)
    ps = []
    for h in range(A_GRP):
        s = s_all[h * rows:(h + 1) * rows] + b_ref[h]
        m = jnp.max(s, axis=-1, keepdims=True)
        e = jnp.exp2(s - jnp.where(m > 0.5 * NEG, m, 0.0))
        den = jnp.sum(e, axis=-1, keepdims=True)
        ps.append((e / jnp.where(den > 0, den, 1.0)).astype(BF16))
    r_all = jnp.dot(jnp.concatenate(ps, axis=0), jnp.concatenate([vc, ovl_ref[...]], axis=1),
                    preferred_element_type=F32)
    imp = jnp.zeros((rows, LANES), F32)
    for h in range(A_GRP):
        o_ref[0, :, h * HEAD_DIM:(h + 1) * HEAD_DIM] = r_all[h * rows:(h + 1) * rows, :HEAD_DIM]
        imp = imp + r_all[h * rows:(h + 1) * rows, HEAD_DIM:]
    t = i0 * QBLK + lax.broadcasted_iota(jnp.int32, (rows, LANES), 0)
    blk = lax.broadcasted_iota(jnp.int32, (rows, LANES), 1)
    cur = t // SLC_BLOCK
    imp = jnp.where(blk == 0, FORCED_SCORE, imp)
    imp = jnp.where(blk == cur, FORCED_SCORE, imp)
    imp = jnp.where(blk == cur - 1, FORCED_SCORE, imp)
    imp = jnp.where(blk * SLC_BLOCK <= t, imp, NEG)
    imp = jnp.where(blk < n_slc, imp, 2.0 * NEG)
    imp_t = jnp.concatenate([imp[j * QBLK:(j + 1) * QBLK].T for j in range(nsub)], axis=1)
    ngrp = -(-n_slc // SUBLANES)
    cands = [imp_t[r * SUBLANES:(r + 1) * SUBLANES] for r in range(ngrp)]
    row_id = lax.broadcasted_iota(jnp.int32, (SUBLANES, rows), 0)
    ranks = [jnp.zeros((SUBLANES, rows), F32) for _ in range(ngrp)]
    for j in range(n_slc):
        other = imp_t[j:j + 1, :]
        for r in range(ngrp):
            if r * SUBLANES > j:
                beats = other >= cands[r]
            elif (r + 1) * SUBLANES <= j:
                beats = other > cands[r]
            else:
                beats = jnp.where(row_id > j - r * SUBLANES, jnp.where(other >= cands[r], 1.0, 0.0),
                                  jnp.where(other > cands[r], 1.0, 0.0)) > 0.5
            ranks[r] = ranks[r] + jnp.where(beats, 1.0, 0.0)
    sel_t = jnp.concatenate([jnp.where(r < float(min(SLC_TOPK, n_slc)), 1.0, 0.0) for r in ranks], axis=0)
    if ngrp * SUBLANES < LANES:
        sel_t = jnp.concatenate([sel_t, jnp.zeros((LANES - ngrp * SUBLANES, rows), F32)], axis=0)
    for j in range(nsub):
        sel_ref[0, 0, j * QBLK:(j + 1) * QBLK, :] = sel_t[:, j * QBLK:(j + 1) * QBLK].T.astype(sel_ref.dtype)


def _cmp_select(proj3, kvc, bias_c, ovl, n_slc):
    B, T, _ = proj3.shape
    ncp = kvc.shape[3]
    nsub = math.gcd(CMP_SUBBLOCKS, T // QBLK)
    rows = nsub * QBLK
    return pl.pallas_call(
        functools.partial(_cmp_select_kernel, n_slc=n_slc, nsub=nsub),
        grid=(B, A_KV_HEADS, T // rows),
        in_specs=[
            pl.BlockSpec((1, rows, A_GRP * HEAD_DIM), lambda b, g, i: (b, i, g)),
            pl.BlockSpec((1, 1, 1, ncp, HEAD_DIM), lambda b, g, i: (0, b, g, 0, 0)),
            pl.BlockSpec((1, 1, 1, ncp, HEAD_DIM), lambda b, g, i: (1, b, g, 0, 0)),
            pl.BlockSpec((A_GRP, rows, ncp), lambda b, g, i: (g, i, 0)),
            pl.BlockSpec((ncp, LANES), lambda b, g, i: (0, 0)),
        ],
        out_specs=(
            pl.BlockSpec((1, rows, A_GRP * HEAD_DIM), lambda b, g, i: (b, i, g)),
            pl.BlockSpec((1, 1, rows, LANES), lambda b, g, i: (b, g, i, 0)),
        ),
        out_shape=(
            jax.ShapeDtypeStruct((B, T, A_Q_HEADS * HEAD_DIM), F32),
            jax.ShapeDtypeStruct((B, A_KV_HEADS, T, LANES), BF16),
        ),
        compiler_params=_cparams(("parallel", "parallel", "arbitrary")),
        name="nsa_cmp_select",
    )(proj3, kvc, kvc, bias_c, ovl)


def _slc_kernel(q_ref, k_ref, v_ref, sel_ref, e_ref, b_ref, o_ref, *scratch, n_tiles, nq):
    ig = pl.program_id(1)
    chains = [(g, a) for g in range(A_KV_HEADS) for a in range(nq)]
    m_refs, l_refs, acc_refs = scratch[0::3], scratch[1::3], scratch[2::3]
    qs, sels = [], []
    for ci, (g, a) in enumerate(chains):
        m_refs[ci][...] = jnp.full(m_refs[ci].shape, NEG, F32)
        l_refs[ci][...] = jnp.zeros(l_refs[ci].shape, F32)
        acc_refs[ci][...] = jnp.zeros(acc_refs[ci].shape, F32)
        rs = slice(a * QBLK, (a + 1) * QBLK)
        qs.append(jnp.concatenate([q_ref[0, rs, (g * A_GRP + h) * HEAD_DIM:(g * A_GRP + h + 1) * HEAD_DIM]
                                   for h in range(A_GRP)], axis=0))
        sels.append(sel_ref[0, g, rs, :])
    kw = SLC_CHUNK * QBLK

    def body(c, carry):
        start = pl.multiple_of(c * kw, kw)
        base = ig * nq - c * SLC_CHUNK + 1
        tidx = {d: jnp.clip(base + d, 0, n_tiles - 1) for d in range(-(SLC_CHUNK - 1), nq)}
        kts = [k_ref[0, pl.ds(start, kw), g * HEAD_DIM:(g + 1) * HEAD_DIM] for g in range(A_KV_HEADS)]
        vts = [v_ref[0, pl.ds(start, kw), g * HEAD_DIM:(g + 1) * HEAD_DIM] for g in range(A_KV_HEADS)]
        ech = e_ref[:, pl.ds(start, kw)]
        ss = [lax.dot_general(qs[ci], kts[g], (((1,), (1,)), ((), ())), preferred_element_type=F32)
              for ci, (g, a) in enumerate(chains)]
        chosens = [jnp.dot(sels[ci], ech, preferred_element_type=F32) for ci in range(len(chains))]
        pbs, alphas = [], []
        for ci, (g, a) in enumerate(chains):
            madd = (chosens[ci] - 1.0) * (-NEG)
            s = ss[ci]
            rows = [[s[h * QBLK:(h + 1) * QBLK, j * QBLK:(j + 1) * QBLK] + b_ref[g * A_GRP + h, tidx[a - j]]
                     + madd[:, j * QBLK:(j + 1) * QBLK] for j in range(SLC_CHUNK)] for h in range(A_GRP)]
            tile_max = jnp.concatenate([functools.reduce(jnp.maximum, r) for r in rows], axis=0)
            m_old = m_refs[ci][...]
            m_new = jnp.maximum(m_old, jnp.max(tile_max, axis=-1, keepdims=True))
            alpha = jnp.exp2(m_old - m_new)
            p = [[jnp.exp2(t - m_new[h * QBLK:(h + 1) * QBLK]) for t in r] for h, r in enumerate(rows)]
            tile_sum = jnp.concatenate([functools.reduce(jnp.add, r) for r in p], axis=0)
            l_refs[ci][...] = alpha * l_refs[ci][...] + jnp.sum(tile_sum, axis=-1, keepdims=True)
            pbs.append(jnp.concatenate([jnp.concatenate([t.astype(BF16) for t in r], axis=1) for r in p], axis=0))
            alphas.append(alpha)
            m_refs[ci][...] = m_new
        for ci, (g, a) in enumerate(chains):
            acc_refs[ci][...] = alphas[ci] * acc_refs[ci][...] + jnp.dot(pbs[ci], vts[g], preferred_element_type=F32)
        return carry

    lax.fori_loop(0, (ig * nq + nq - 1) // SLC_CHUNK + 1, body, 0)
    for ci, (g, a) in enumerate(chains):
        out = acc_refs[ci][...] / l_refs[ci][...]
        for h in range(A_GRP):
            col = (g * A_GRP + h) * HEAD_DIM
            o_ref[0, a * QBLK:(a + 1) * QBLK, col:col + HEAD_DIM] = out[h * QBLK:(h + 1) * QBLK]


def _slc_attn(proj3, sel, expand, bias):
    B, T, _ = proj3.shape
    n_tiles = bias.shape[1]
    nq = SLC_QTILES
    qrows = nq * QBLK
    assert SLC_CHUNK % nq == 0 and T % (SLC_CHUNK * QBLK) == 0
    chain_rows = A_GRP * QBLK
    n_chains = A_KV_HEADS * nq
    return pl.pallas_call(
        functools.partial(_slc_kernel, n_tiles=n_tiles, nq=nq),
        grid=(B, T // qrows),
        in_specs=[
            pl.BlockSpec((1, qrows, A_Q_HEADS * HEAD_DIM), lambda b, i: (b, i, COL_QA // A_Q_HEADS)),
            pl.BlockSpec((1, T, A_KV_HEADS * HEAD_DIM), lambda b, i: (b, 0, COL_KSA // A_KV_HEADS)),
            pl.BlockSpec((1, T, A_KV_HEADS * HEAD_DIM), lambda b, i: (b, 0, COL_VSA // A_KV_HEADS)),
            pl.BlockSpec((1, A_KV_HEADS, qrows, LANES), lambda b, i: (b, 0, i, 0)),
            pl.BlockSpec((LANES, T), lambda b, i: (0, 0)),
            pl.BlockSpec((A_Q_HEADS, n_tiles, QBLK, LANES), lambda b, i: (0, 0, 0, 0)),
        ],
        out_specs=pl.BlockSpec((1, qrows, A_Q_HEADS * HEAD_DIM), lambda b, i: (b, i, 0)),
        out_shape=jax.ShapeDtypeStruct((B, T, A_Q_HEADS * HEAD_DIM), F32),
        scratch_shapes=[pltpu.VMEM((chain_rows, 1), F32), pltpu.VMEM((chain_rows, 1), F32),
                        pltpu.VMEM((chain_rows, HEAD_DIM), F32)] * n_chains,
        compiler_params=_cparams(("parallel", "arbitrary")),
        name="nsa_slc_attn",
    )(proj3, proj3, proj3, sel, expand, bias)


def _combine_kernel(gate_ref, ocmp_ref, oslc_ref, owin_ref, ob_ref, *rest):
    npair = len(DIL_PAIRS)
    oc_refs, lse_refs, o_ref = rest[:npair], rest[npair:2 * npair], rest[2 * npair]
    gate = 1.0 / (1.0 + jnp.exp(-gate_ref[...]))
    for h in range(A_Q_HEADS):
        sl = slice(h * HEAD_DIM, (h + 1) * HEAD_DIM)
        o = (gate[:, 3 * h:3 * h + 1] * ocmp_ref[:, sl] + gate[:, 3 * h + 1:3 * h + 2] * oslc_ref[:, sl]
             + gate[:, 3 * h + 2:3 * h + 3] * owin_ref[:, sl])
        o_ref[:, sl] = o.astype(o_ref.dtype)
    base = A_Q_HEADS * HEAD_DIM
    width = B_Q_HEADS * HEAD_DIM
    o_ref[:, base:base + width] = ob_ref[...].astype(o_ref.dtype)
    base += width
    lses = [r[...] for r in lse_refs]
    mx = functools.reduce(jnp.maximum, lses)
    ws = [jnp.exp(x - mx) for x in lses]
    tot = functools.reduce(jnp.add, ws)
    for gidx in range(npair):
        w = C_GRP * HEAD_DIM
        o_ref[:, base + gidx * w:base + (gidx + 1) * w] = (oc_refs[gidx][...] * (ws[gidx] / tot)).astype(o_ref.dtype)


def _combine(gates, o_cmp, o_slc, o_win, o_b, o_cs, lses, *, tm):
    m = gates.shape[0]
    d_out = (A_Q_HEADS + B_Q_HEADS + C_Q_HEADS) * HEAD_DIM
    row = lambda w: pl.BlockSpec((tm, w), lambda i: (i, 0))
    ins = [gates, o_cmp, o_slc, o_win, o_b, *o_cs, *lses]
    return pl.pallas_call(
        _combine_kernel,
        grid=(m // tm,),
        in_specs=[row(a.shape[1]) for a in ins],
        out_specs=row(d_out),
        out_shape=jax.ShapeDtypeStruct((m, d_out), BF16),
        compiler_params=_cparams(("parallel",)),
        name="combine",
    )(*ins)


def _to_sub(a, d):
    B, T, c = a.shape
    return a.reshape(B, T // d, d, c).transpose(0, 2, 1, 3).reshape(B * d, T // d, c)


def _from_sub(a, d, B):
    _, ls, c = a.shape
    return a.reshape(B, d, ls, c).transpose(0, 2, 1, 3).reshape(B, ls * d, c)


def _mixers(proj3, gates, layer_params, tables):
    B, T, _ = proj3.shape
    cmp_pe, cmp_w1, cmp_w2, kc_gain, sinks = layer_params
    n_cmp = (T - CMP_BLOCK) // CMP_STRIDE + 1
    n_slc = T // SLC_BLOCK
    ncp = T // CMP_STRIDE

    def cmp_rows(col):
        a = proj3[:, :, col * LANES:(col + A_KV_HEADS) * LANES].reshape(B, ncp, CMP_STRIDE, A_KV_HEADS, HEAD_DIM)
        return a.transpose(0, 3, 1, 2, 4).reshape(B, A_KV_HEADS, ncp, CMP_STRIDE * HEAD_DIM)

    xr = jnp.stack([cmp_rows(COL_KCA), cmp_rows(COL_VCA)])
    pe = jnp.broadcast_to(cmp_pe.reshape(2, 1, CMP_BLOCK * HEAD_DIM), (2, 8, CMP_BLOCK * HEAD_DIM)).astype(BF16)
    kvc = _compress(xr, pe, cmp_w1.astype(BF16), cmp_w2.astype(BF16), kc_gain.reshape(1, HEAD_DIM), n_cmp)

    o_cmp, sel = _cmp_select(proj3, kvc, tables["bias_c"], tables["ovl"], n_slc)
    o_slc = _slc_attn(proj3, sel, tables["expand"], tables["bias_slc"])
    o_win = _band_attn(proj3, proj3, proj3, tables["bias_win"], n_kv=A_KV_HEADS, grp=A_GRP,
                       q_col=COL_QA // A_GRP, k_col=COL_KWA, v_col=COL_VWA)
    o_b = _band_attn(proj3, proj3, proj3, tables["bias_b"], n_kv=B_KV_HEADS, grp=B_GRP,
                     q_col=COL_QB // B_GRP, k_col=COL_KB, v_col=COL_VB, sinks=sinks)
    o_cs, lses = [], []
    for gidx, (_, dil) in enumerate(DIL_PAIRS):
        if dil == 1:
            o, lse = _band_attn(proj3, proj3, proj3, tables["bias_c%d" % gidx], n_kv=1, grp=C_GRP,
                                q_col=COL_QC // C_GRP + gidx, k_col=COL_KC + gidx, v_col=COL_VC + gidx,
                                with_lse=True)
        else:
            qs = _to_sub(proj3[:, :, (COL_QC + C_GRP * gidx) * LANES:(COL_QC + C_GRP * (gidx + 1)) * LANES], dil)
            ks = _to_sub(proj3[:, :, (COL_KC + gidx) * LANES:(COL_KC + gidx + 1) * LANES], dil)
            vs = _to_sub(proj3[:, :, (COL_VC + gidx) * LANES:(COL_VC + gidx + 1) * LANES], dil)
            o, lse = _band_attn(qs, ks, vs, tables["bias_c%d" % gidx], n_kv=1, grp=C_GRP,
                                q_col=0, k_col=0, v_col=0, with_lse=True)
            o, lse = _from_sub(o, dil, B), _from_sub(lse, dil, B)
        o_cs.append(o.reshape(B * T, -1))
        lses.append(lse.reshape(B * T, -1))
    return _combine(gates, o_cmp.reshape(B * T, -1), o_slc.reshape(B * T, -1), o_win.reshape(B * T, -1),
                    o_b.reshape(B * T, -1), o_cs, lses, tm=256)


def _build_tables(rel_bias, T):
    n_cmp = (T - CMP_BLOCK) // CMP_STRIDE + 1
    n_slc = T // SLC_BLOCK
    ncp = T // CMP_STRIDE
    nq = T // QBLK
    tables = {}
    tables["bias_c"] = _bias_tiles(rel_bias, 0, A_Q_HEADS, nq, width=ncp, kstride=CMP_STRIDE,
                                   koff=CMP_BLOCK - 1, ncols=n_cmp).reshape(A_Q_HEADS, T, ncp)
    n_sat = -(-(SAT_DIST + QBLK - 1) // QBLK) + 1
    tables["bias_slc"] = _bias_tiles(rel_bias, 0, A_Q_HEADS, min(n_sat, nq) + 1, koff=QBLK)
    win = NSA_WINDOW - 1
    tables["bias_win"] = _bias_tiles(rel_bias, 0, A_Q_HEADS, -(-win // QBLK) + 2, max_dist=win)
    swa = SWA_WINDOW - 1
    tables["bias_b"] = _bias_tiles(rel_bias, A_Q_HEADS, B_Q_HEADS, -(-swa // QBLK) + 2, max_dist=swa)
    for gidx, (w, dil) in enumerate(DIL_PAIRS):
        md = w // dil
        tables["bias_c%d" % gidx] = _bias_tiles(rel_bias, A_Q_HEADS + B_Q_HEADS + C_GRP * gidx, C_GRP,
                                                -(-md // QBLK) + 2, dscale=dil, max_dist=md)
    c0 = np.arange(ncp)[:, None] * CMP_STRIDE
    s0 = np.arange(LANES)[None, :] * SLC_BLOCK
    ovl = np.clip(np.minimum(c0 + CMP_BLOCK, s0 + SLC_BLOCK) - np.maximum(c0, s0), 0, None) / CMP_BLOCK
    ovl = ovl * (np.arange(ncp)[:, None] < n_cmp) * (np.arange(LANES)[None, :] < n_slc)
    tables["ovl"] = jnp.asarray(ovl, BF16)
    expand = (np.arange(T)[None, :] // SLC_BLOCK) == np.arange(LANES)[:, None]
    tables["expand"] = jnp.asarray(expand, BF16)
    return tables


def _proj_gain(g):
    ones = jnp.ones((HEAD_DIM,), F32)
    spec = [(g[0] * (SCALE * LOG2E), 6), (ones, 4), (g[2], 2), (ones, 2), (g[3], 2), (ones, 2),
            (g[4] * (SCALE * LOG2E), 4), (g[5], 2), (ones, 2), (g[6] * (SCALE * LOG2E), 6), (g[7], 3), (ones, 3)]
    assert sum(n for _, n in spec) == N_MAIN_BLOCKS
    return jnp.concatenate([jnp.tile(v, n) for v, n in spec]).reshape(1, N_MAIN)


def kernel(x, norm_attn, w_in, qk_gain, cmp_pe, cmp_w1, cmp_w2, sinks, rel_bias, w_out, norm_ffn, w_gate, w_up, w_down):
    B, T, D = x.shape
    depth = w_in.shape[0]
    tables = _build_tables(rel_bias, T)
    x2 = x.reshape(B * T, D)
    w_all = _wprep(w_in, tr=256)
    tm = min(512, B * T)
    for l in range(depth):
        proj, gates = _proj(x2, norm_attn[l].reshape(1, D), w_all[l], _proj_gain(qk_gain[l]), tm=tm)
        mix = _mixers(proj.reshape(B, T, N_MAIN), gates,
                      (cmp_pe[l], cmp_w1[l], cmp_w2[l], qk_gain[l][1], sinks[l]), tables)
        x2 = _outproj(x2, mix, w_out[l].astype(BF16), tm=tm)
        x2 = _ffn(x2, norm_ffn[l].reshape(1, D), w_gate[l].astype(BF16), w_up[l].astype(BF16),
                  w_down[l].astype(BF16), tm=tm, tf=512)
    return x2.reshape(B, T, D)
```

```python
import functools
import math

import numpy as np
import jax
import jax.numpy as jnp
from jax import lax
from jax.experimental import pallas as pl
from jax.experimental.pallas import tpu as pltpu

F32 = jnp.float32
BF16 = jnp.bfloat16

HEAD_DIM = 128
LANES = 128
SUBLANES = 8
QBLK = 128
BAND_SUBBLOCKS = 8
CMP_SUBBLOCKS = 4
A_Q_HEADS, A_KV_HEADS = 6, 2
A_GRP = A_Q_HEADS // A_KV_HEADS
B_Q_HEADS, B_KV_HEADS = 4, 2
B_GRP = B_Q_HEADS // B_KV_HEADS
DIL_PAIRS = ((128, 1), (512, 4), (2048, 16))
C_GRP = 2
C_Q_HEADS = C_GRP * len(DIL_PAIRS)
CMP_BLOCK, CMP_STRIDE = 32, 16
SLC_BLOCK, SLC_TOPK = 64, 16
SLC_CHUNK = 4
SLC_QTILES = 1
NSA_WINDOW, SWA_WINDOW = 512, 128
FORCED_SCORE = 1.0e4
REL_BUCKETS, REL_MAX_EXACT, REL_MAX_DIST = 32, 16, 2048
SCALE = HEAD_DIM ** -0.5
LOG2E = math.log2(math.e)
LN2 = math.log(2.0)
EPS = 1e-6
NEG = -1e30
VMEM_LIMIT = 56 * 1024 * 1024

COL_QA, COL_KCA, COL_VCA, COL_KSA, COL_VSA, COL_KWA, COL_VWA = 0, 6, 8, 10, 12, 14, 16
COL_QB, COL_KB, COL_VB, COL_QC, COL_KC, COL_VC = 18, 22, 24, 26, 32, 35
N_MAIN_BLOCKS = 38
N_MAIN = N_MAIN_BLOCKS * LANES
GATE_START = 2304
N_GATES = A_Q_HEADS * 3
PROJ_NORMED = ([True] * 6 + [False] * 4 + [True] * 2 + [False] * 2 + [True] * 2 + [False] * 2
               + [True] * 6 + [False] * 2 + [True] * 9 + [False] * 3)
DIL_F32_SLOT = {}
for _g, (_, _d) in enumerate(DIL_PAIRS):
    if _d > 1:
        _base = 4 * len([1 for _, _dd in DIL_PAIRS[:_g] if _dd > 1])
        DIL_F32_SLOT.update({COL_QC + C_GRP * _g: _base, COL_QC + C_GRP * _g + 1: _base + 1,
                             COL_KC + _g: _base + 2, COL_VC + _g: _base + 3})
DIL_F32_COLS = LANES * len(DIL_F32_SLOT)
DIL_TILES_PER_STEP = 8
PROJ_SUBTILE = 256
OUT_SUBTILE = 512


def _bucket_starts():
    d = np.arange(0, 1 << 17)
    out = []
    for dt in (np.float32, np.float64):
        far = np.maximum(d, REL_MAX_EXACT).astype(dt)
        lb = REL_MAX_EXACT + (np.log(far / dt(REL_MAX_EXACT)) / dt(math.log(REL_MAX_DIST / REL_MAX_EXACT))
                              * dt(REL_BUCKETS - REL_MAX_EXACT)).astype(np.int64)
        out.append(np.where(d < REL_MAX_EXACT, d, np.minimum(lb, REL_BUCKETS - 1)))
    assert (out[0] == out[1]).all() and (np.diff(out[0]) >= 0).all()
    return [int(np.argmax(out[0] >= b)) for b in range(REL_BUCKETS)]


BUCKET_START = _bucket_starts()
SAT_DIST = BUCKET_START[REL_BUCKETS - 1]


def _cparams(sem, vmem=VMEM_LIMIT):
    return pltpu.CompilerParams(dimension_semantics=sem, vmem_limit_bytes=vmem)


def _bias_tile_kernel(tab_ref, o_ref, *, head0, width, kstride, koff, dscale, max_dist, ncols):
    h = pl.program_id(0) + head0
    t = pl.program_id(1)
    r = lax.broadcasted_iota(jnp.int32, (QBLK, width), 0)
    c = lax.broadcasted_iota(jnp.int32, (QBLK, width), 1)
    dist = t * QBLK + r - kstride * c - koff
    d = dist * dscale
    val = jnp.full((QBLK, width), tab_ref[REL_BUCKETS - 1, h] * LOG2E, F32)
    for b in range(REL_BUCKETS - 2, -1, -1):
        val = jnp.where(d < BUCKET_START[b + 1], tab_ref[b, h] * LOG2E, val)
    val = jnp.where(dist >= 0, val, NEG)
    if max_dist is not None:
        val = jnp.where(dist <= max_dist, val, NEG)
    if ncols < width:
        val = jnp.where(c < ncols, val, NEG)
    o_ref[0, 0] = val


def _bias_tiles(rel_bias, head0, nheads, ntiles, *, width=LANES, kstride=1, koff=0, dscale=1,
                max_dist=None, ncols=None):
    ncols = width if ncols is None else ncols
    kern = functools.partial(_bias_tile_kernel, head0=head0, width=width, kstride=kstride, koff=koff,
                             dscale=dscale, max_dist=max_dist, ncols=ncols)
    return pl.pallas_call(
        kern,
        grid=(nheads, ntiles),
        in_specs=[pl.BlockSpec(memory_space=pltpu.SMEM)],
        out_specs=pl.BlockSpec((1, 1, QBLK, width), lambda h, t: (h, t, 0, 0)),
        out_shape=jax.ShapeDtypeStruct((nheads, ntiles, QBLK, width), F32),
        compiler_params=_cparams(("parallel", "parallel")),
        name="bias_tiles",
    )(rel_bias)


def _rms(x, w):
    ms = jnp.mean(x * x, axis=-1, keepdims=True)
    return x * lax.rsqrt(ms + EPS) * w


def _wprep_kernel(w_ref, o_ref):
    o_ref[0, :, :GATE_START] = w_ref[0, :, :GATE_START].astype(BF16)
    o_ref[0, :, GATE_START:N_MAIN] = w_ref[0, :, GATE_START + N_GATES:].astype(BF16)
    o_ref[0, :, N_MAIN:] = w_ref[0, :, GATE_START:GATE_START + LANES].astype(BF16)


def _wprep(w_in, *, tr):
    depth, d, n_in = w_in.shape
    return pl.pallas_call(
        _wprep_kernel,
        grid=(depth, d // tr),
        in_specs=[pl.BlockSpec((1, tr, n_in), lambda l, i: (l, i, 0))],
        out_specs=pl.BlockSpec((1, tr, N_MAIN + LANES), lambda l, i: (l, i, 0)),
        out_shape=jax.ShapeDtypeStruct((depth, d, N_MAIN + LANES), BF16),
        compiler_params=_cparams(("parallel", "parallel")),
        name="w_in_prep",
    )(w_in)


def _proj_kernel(x_ref, nw_ref, w_ref, gain_ref, o_ref, gate_ref, dil_ref):
    h = _rms(x_ref[...], nw_ref[...]).astype(BF16)
    for c0 in range(0, N_MAIN, PROJ_SUBTILE):
        width = min(PROJ_SUBTILE, N_MAIN - c0)
        acc = jnp.dot(h, w_ref[:, c0:c0 + width], preferred_element_type=F32)
        for c in range(width // LANES):
            blk = c0 // LANES + c
            sl = slice(blk * LANES, (blk + 1) * LANES)
            y = acc[:, c * LANES:(c + 1) * LANES]
            if PROJ_NORMED[blk]:
                ms = jnp.mean(y * y, axis=-1, keepdims=True)
                y = y * lax.rsqrt(ms + EPS) * gain_ref[:, sl]
            o_ref[:, sl] = y.astype(o_ref.dtype)
            if blk in DIL_F32_SLOT:
                dil_ref[:, DIL_F32_SLOT[blk] * LANES:(DIL_F32_SLOT[blk] + 1) * LANES] = y
    gate_ref[...] = jnp.dot(h, w_ref[:, N_MAIN:], preferred_element_type=F32)


def _proj(x2d, norm_w, w_all, layer, gain, *, tm):
    m, d = x2d.shape
    n = w_all.shape[2]
    return pl.pallas_call(
        _proj_kernel,
        grid=(m // tm,),
        in_specs=[
            pl.BlockSpec((tm, d), lambda i: (i, 0)),
            pl.BlockSpec((1, d), lambda i: (0, 0)),
            pl.BlockSpec((None, d, n), lambda i: (layer, 0, 0), pipeline_mode=pl.Buffered(1)),
            pl.BlockSpec((1, N_MAIN), lambda i: (0, 0)),
        ],
        out_specs=(pl.BlockSpec((tm, N_MAIN), lambda i: (i, 0)), pl.BlockSpec((tm, LANES), lambda i: (i, 0)),
                   pl.BlockSpec((tm, DIL_F32_COLS), lambda i: (i, 0))),
        out_shape=(jax.ShapeDtypeStruct((m, N_MAIN), BF16), jax.ShapeDtypeStruct((m, LANES), F32),
                   jax.ShapeDtypeStruct((m, DIL_F32_COLS), F32)),
        compiler_params=_cparams(("parallel",)),
        name="in_proj",
    )(x2d, norm_w, w_all, gain)


def _outproj_kernel(x_ref, a_ref, w_ref, o_ref):
    a = a_ref[...]
    for c0 in range(0, o_ref.shape[1], OUT_SUBTILE):
        sl = slice(c0, c0 + OUT_SUBTILE)
        o_ref[:, sl] = x_ref[:, sl] + jnp.dot(a, w_ref[:, sl], preferred_element_type=F32)


def _outproj(x2d, a, w_all, layer, *, tm):
    m, d = x2d.shape
    k = a.shape[1]
    return pl.pallas_call(
        _outproj_kernel,
        grid=(m // tm,),
        in_specs=[
            pl.BlockSpec((tm, d), lambda i: (i, 0)),
            pl.BlockSpec((tm, k), lambda i: (i, 0)),
            pl.BlockSpec((None, k, d), lambda i: (layer, 0, 0), pipeline_mode=pl.Buffered(1)),
        ],
        out_specs=pl.BlockSpec((tm, d), lambda i: (i, 0)),
        out_shape=jax.ShapeDtypeStruct((m, d), F32),
        compiler_params=_cparams(("parallel",)),
        name="out_proj",
    )(x2d, a, w_all)


def _ffn_kernel(x_ref, nw_ref, wg_ref, wu_ref, wd_ref, o_ref, h_ref, acc_ref):
    f = pl.program_id(1)

    @pl.when(f == 0)
    def _():
        h_ref[...] = _rms(x_ref[...], nw_ref[...]).astype(BF16)
        acc_ref[...] = jnp.zeros_like(acc_ref)

    h = h_ref[...]
    g = jnp.dot(h, wg_ref[...], preferred_element_type=F32)
    u = jnp.dot(h, wu_ref[...], preferred_element_type=F32)
    a = (g * (1.0 / (1.0 + jnp.exp(-g))) * u).astype(BF16)
    acc_ref[...] += jnp.dot(a, wd_ref[...], preferred_element_type=F32)

    @pl.when(f == pl.num_programs(1) - 1)
    def _():
        o_ref[...] = x_ref[...] + acc_ref[...]


def _ffn(x2d, norm_w, wg, wu, wd, layer, *, tm, tf):
    m, d = x2d.shape
    dff = wg.shape[2]
    return pl.pallas_call(
        _ffn_kernel,
        grid=(m // tm, dff // tf),
        in_specs=[
            pl.BlockSpec((tm, d), lambda i, f: (i, 0)),
            pl.BlockSpec((1, d), lambda i, f: (0, 0)),
            pl.BlockSpec((None, d, tf), lambda i, f: (layer, 0, f)),
            pl.BlockSpec((None, d, tf), lambda i, f: (layer, 0, f)),
            pl.BlockSpec((None, tf, d), lambda i, f: (layer, f, 0)),
        ],
        out_specs=pl.BlockSpec((tm, d), lambda i, f: (i, 0)),
        out_shape=jax.ShapeDtypeStruct((m, d), F32),
        scratch_shapes=[pltpu.VMEM((tm, d), BF16), pltpu.VMEM((tm, d), F32)],
        compiler_params=_cparams(("parallel", "arbitrary")),
        name="ffn",
    )(x2d, norm_w, wg, wu, wd)


def _band_kernel(*refs, grp, n_off, nsub, has_sinks, with_lse):
    if has_sinks:
        sink_ref, q_ref, k_ref, v_ref, b_ref = refs[:5]
        outs = refs[5:]
    else:
        q_ref, k_ref, v_ref, b_ref = refs[:4]
        outs = refs[4:]
    o_ref = outs[0]
    g = pl.program_id(1)
    i0 = pl.program_id(2) * nsub
    tiles = {}

    def kv_tile(rel):
        if rel not in tiles:
            start = pl.multiple_of(jnp.maximum(i0 + rel, 0) * QBLK, QBLK)
            tiles[rel] = (k_ref[0, pl.ds(start, QBLK), :], v_ref[0, pl.ds(start, QBLK), :])
        return tiles[rel]

    def head_rows(x, h):
        return x[h * QBLK:(h + 1) * QBLK]

    kvs = [[kv_tile(j - off) for off in range(n_off)] for j in range(nsub)]
    tidxs = [[jnp.where(i0 + j - off >= 0, off, n_off) for off in range(n_off)] for j in range(nsub)]
    scores = []
    for j in range(nsub):
        rs = slice(j * QBLK, (j + 1) * QBLK)
        kcat = jnp.concatenate([t[0] for t in kvs[j]], axis=0)
        q = jnp.concatenate([q_ref[0, rs, h * HEAD_DIM:(h + 1) * HEAD_DIM] for h in range(grp)], axis=0)
        scores.append(lax.dot_general(q, kcat, (((1,), (1,)), ((), ())), preferred_element_type=F32))
    probs, stats = [], []
    for j in range(nsub):
        s = scores[j]
        st = [[head_rows(s, h)[:, off * QBLK:(off + 1) * QBLK] + b_ref[h, tidxs[j][off]] for off in range(n_off)]
              for h in range(grp)]
        m = jnp.max(jnp.concatenate([functools.reduce(jnp.maximum, r) for r in st], axis=0), axis=-1, keepdims=True)
        p = [[jnp.exp2(st[h][off] - head_rows(m, h)) for off in range(n_off)] for h in range(grp)]
        l = jnp.sum(jnp.concatenate([functools.reduce(jnp.add, r) for r in p], axis=0), axis=-1, keepdims=True)
        probs.append(jnp.concatenate([jnp.concatenate([t.astype(BF16) for t in r], axis=1) for r in p], axis=0))
        stats.append((m, l))
    for j in range(nsub):
        rs = slice(j * QBLK, (j + 1) * QBLK)
        m, l = stats[j]
        vcat = jnp.concatenate([t[1] for t in kvs[j]], axis=0)
        o = jnp.dot(probs[j], vcat, preferred_element_type=F32)
        den = l
        if has_sinks:
            sink = jnp.concatenate([jnp.full((QBLK, 1), sink_ref[g * grp + h] * LOG2E, F32) for h in range(grp)], axis=0)
            den = l + jnp.exp2(sink - m)
        o = o / den
        for h in range(grp):
            sl = slice(h * HEAD_DIM, (h + 1) * HEAD_DIM)
            o_ref[0, rs, sl] = head_rows(o, h).astype(o_ref.dtype)
            if with_lse:
                outs[1][0, rs, sl] = jnp.broadcast_to(head_rows(m * LN2 + jnp.log(l), h), (QBLK, HEAD_DIM))


def _band_attn(q_arr, k_arr, v_arr, bias, *, n_kv, grp, q_col, k_col, v_col, sinks=None, with_lse=False):
    n, L, _ = q_arr.shape
    n_off = bias.shape[1] - 1
    gw = grp * HEAD_DIM
    nsub = math.gcd(BAND_SUBBLOCKS, L // QBLK)
    rows = nsub * QBLK
    kern = functools.partial(_band_kernel, grp=grp, n_off=n_off, nsub=nsub, has_sinks=sinks is not None,
                             with_lse=with_lse)
    in_specs = [
        pl.BlockSpec((1, rows, gw), lambda b, g, i: (b, i, q_col + g)),
        pl.BlockSpec((1, L, HEAD_DIM), lambda b, g, i: (b, 0, k_col + g)),
        pl.BlockSpec((1, L, HEAD_DIM), lambda b, g, i: (b, 0, v_col + g)),
        pl.BlockSpec((grp, n_off + 1, QBLK, LANES), lambda b, g, i: (g, 0, 0, 0)),
    ]
    args = [q_arr, k_arr, v_arr, bias]
    if sinks is not None:
        in_specs = [pl.BlockSpec(memory_space=pltpu.SMEM)] + in_specs
        args = [sinks] + args
    o_spec = pl.BlockSpec((1, rows, gw), lambda b, g, i: (b, i, g))
    o_shape = jax.ShapeDtypeStruct((n, L, n_kv * gw), F32)
    return pl.pallas_call(
        kern,
        grid=(n, n_kv, L // rows),
        in_specs=in_specs,
        out_specs=(o_spec, o_spec) if with_lse else o_spec,
        out_shape=(o_shape, o_shape) if with_lse else o_shape,
        compiler_params=_cparams(("parallel", "parallel", "arbitrary")),
        name="band_attn",
    )(*args)


def _dil_kernel(*refs, dil, nres):
    q_refs, (k_ref, v_ref, b_ref) = refs[:C_GRP], refs[C_GRP:C_GRP + 3]
    o_refs, lse_refs = refs[C_GRP + 3:2 * C_GRP + 3], refs[2 * C_GRP + 3:]
    ls = k_ref.shape[1] // dil
    ntile = ls // QBLK
    r0 = pl.program_id(1) * nres
    units = [(rr, j) for rr in range(nres) for j in range(ntile)]

    def rows(rr, j):
        return pl.ds(r0 + rr + dil * QBLK * j, QBLK, stride=dil)

    kv = {u: (k_ref[0, rows(*u), :].astype(BF16), v_ref[0, rows(*u), :].astype(BF16)) for u in units}
    n_off = b_ref.shape[1] - 1
    offs = {(rr, j): [off for off in range(n_off) if j - off >= 0] for rr, j in units}
    scores = []
    for rr, j in units:
        kcat = jnp.concatenate([kv[(rr, j - off)][0] for off in offs[(rr, j)]], axis=0)
        q = jnp.concatenate([q_ref[0, rows(rr, j), :].astype(BF16) for q_ref in q_refs], axis=0)
        scores.append(lax.dot_general(q, kcat, (((1,), (1,)), ((), ())), preferred_element_type=F32))
    probs, stats = [], []
    for u, s in zip(units, scores):
        st = [[s[h * QBLK:(h + 1) * QBLK, n * QBLK:(n + 1) * QBLK] + b_ref[h, off] for n, off in enumerate(offs[u])]
              for h in range(C_GRP)]
        m = jnp.max(jnp.concatenate([functools.reduce(jnp.maximum, r) for r in st], axis=0), axis=-1, keepdims=True)
        p = [[jnp.exp2(t - m[h * QBLK:(h + 1) * QBLK]) for t in r] for h, r in enumerate(st)]
        l = jnp.sum(jnp.concatenate([functools.reduce(jnp.add, r) for r in p], axis=0), axis=-1, keepdims=True)
        probs.append(jnp.concatenate([jnp.concatenate([t.astype(BF16) for t in r], axis=1) for r in p], axis=0))
        stats.append((m, l))
    for (rr, j), pb, (m, l) in zip(units, probs, stats):
        vcat = jnp.concatenate([kv[(rr, j - off)][1] for off in offs[(rr, j)]], axis=0)
        o = jnp.dot(pb, vcat, preferred_element_type=F32) / l
        lse = m * LN2 + jnp.log(l)
        for h in range(C_GRP):
            o_refs[h][0, rows(rr, j), :] = o[h * QBLK:(h + 1) * QBLK]
            lse_refs[h][0, rows(rr, j), :] = jnp.broadcast_to(lse[h * QBLK:(h + 1) * QBLK], (QBLK, HEAD_DIM))


def _dil_attn(x3, bias, *, col, dil):
    B, T, _ = x3.shape
    ntile = T // dil // QBLK
    nres = max(1, min(dil, DIL_TILES_PER_STEP // ntile))
    spec = pl.BlockSpec((1, T, HEAD_DIM), lambda b, r: (b, 0, 0))
    shape = jax.ShapeDtypeStruct((B, T, HEAD_DIM), F32)
    outs = pl.pallas_call(
        functools.partial(_dil_kernel, dil=dil, nres=nres),
        grid=(B, dil // nres),
        in_specs=[pl.BlockSpec((1, T, HEAD_DIM), functools.partial(lambda b, r, c: (b, 0, c), c=col + c))
                  for c in range(C_GRP + 2)]
        + [pl.BlockSpec((C_GRP, bias.shape[1], QBLK, LANES), lambda b, r: (0, 0, 0, 0))],
        out_specs=(spec,) * (2 * C_GRP),
        out_shape=(shape,) * (2 * C_GRP),
        compiler_params=_cparams(("parallel", "arbitrary")),
        name="dilated_attn",
    )(*([x3] * (C_GRP + 2)), bias)
    return outs[:C_GRP], outs[C_GRP:]


def _compress_kernel(x_ref, pe_ref, w1_ref, w2_ref, gain_ref, o_ref, *, n_cmp):
    half = w1_ref.shape[1] // 2
    x = x_ref[0, 0, 0]
    y0 = jnp.dot(x, w1_ref[0, :half], preferred_element_type=F32)
    y1 = jnp.dot(x, w1_ref[0, half:], preferred_element_type=F32)
    pe = jnp.dot(pe_ref[0], w1_ref[0], preferred_element_type=F32)[0:1]
    rows = x.shape[0]
    c = y0 + pltpu.roll(y1, rows - 1, 0) + pe
    gl = 0.5 * c * (1.0 + jnp.tanh(math.sqrt(2.0 / math.pi) * (c + 0.044715 * (c * c * c))))
    out = jnp.dot(gl.astype(BF16), w2_ref[0], preferred_element_type=F32)
    out = jnp.where(pl.program_id(0) == 0, _rms(out, gain_ref[...]), out)
    valid = lax.broadcasted_iota(jnp.int32, out.shape, 0) < n_cmp
    o_ref[0, 0, 0] = jnp.where(valid, out, 0.0).astype(o_ref.dtype)


def _compress(xr, pe, w1, w2, gain, n_cmp):
    _, B, hk, rows, wide = xr.shape
    return pl.pallas_call(
        functools.partial(_compress_kernel, n_cmp=n_cmp),
        grid=(2, B, hk),
        in_specs=[
            pl.BlockSpec((1, 1, 1, rows, wide), lambda s, b, g: (s, b, g, 0, 0)),
            pl.BlockSpec((1, 8, 2 * wide), lambda s, b, g: (s, 0, 0)),
            pl.BlockSpec((1, 2 * wide, HEAD_DIM), lambda s, b, g: (s, 0, 0)),
            pl.BlockSpec((1, HEAD_DIM, HEAD_DIM), lambda s, b, g: (s, 0, 0)),
            pl.BlockSpec((1, HEAD_DIM), lambda s, b, g: (0, 0)),
        ],
        out_specs=pl.BlockSpec((1, 1, 1, rows, HEAD_DIM), lambda s, b, g: (s, b, g, 0, 0)),
        out_shape=jax.ShapeDtypeStruct((2, B, hk, rows, HEAD_DIM), BF16),
        compiler_params=_cparams(("arbitrary", "arbitrary", "arbitrary")),
        name="nsa_compress",
    )(xr, pe, w1, w2, gain)


def _cmp_select_kernel(q_ref, kc_ref, vc_ref, b_ref, ovl_ref, o_ref, sel_ref, *, n_slc, nsub):
    i0 = pl.program_id(2) * nsub
    rows = nsub * QBLK
    kc = kc_ref[0, 0, 0]
    vc = vc_ref[0, 0, 0]
    q_all = jnp.concatenate([q_ref[0, :, h * HEAD_DIM:(h + 1) * HEAD_DIM] for h in range(A_GRP)], axis=0)
    s_all = lax.dot_general(q_all, kc, (((1,), (1,)), ((), ())), preferred_element_type=F32)
    ps = []
    for h in range(A_GRP):
        s = s_all[h * rows:(h + 1) * rows] + b_ref[h]
        m = jnp.max(s, axis=-1, keepdims=True)
        e = jnp.exp2(s - jnp.where(m > 0.5 * NEG, m, 0.0))
        den = jnp.sum(e, axis=-1, keepdims=True)
        ps.append((e / jnp.where(den > 0, den, 1.0)).astype(BF16))
    r_all = jnp.dot(jnp.concatenate(ps, axis=0), jnp.concatenate([vc, ovl_ref[...]], axis=1),
                    preferred_element_type=F32)
    imp = jnp.zeros((rows, LANES), F32)
    for h in range(A_GRP):
        o_ref[0, :, h * HEAD_DIM:(h + 1) * HEAD_DIM] = r_all[h * rows:(h + 1) * rows, :HEAD_DIM]
        imp = imp + r_all[h * rows:(h + 1) * rows, HEAD_DIM:]
    t = i0 * QBLK + lax.broadcasted_iota(jnp.int32, (rows, LANES), 0)
    blk = lax.broadcasted_iota(jnp.int32, (rows, LANES), 1)
    cur = t // SLC_BLOCK
    imp = jnp.where(blk == 0, FORCED_SCORE, imp)
    imp = jnp.where(blk == cur, FORCED_SCORE, imp)
    imp = jnp.where(blk == cur - 1, FORCED_SCORE, imp)
    imp = jnp.where(blk * SLC_BLOCK <= t, imp, NEG)
    imp = jnp.where(blk < n_slc, imp, 2.0 * NEG)
    imp_t = jnp.concatenate([imp[j * QBLK:(j + 1) * QBLK].T for j in range(nsub)], axis=1)
    ngrp = -(-n_slc // SUBLANES)
    cands = [imp_t[r * SUBLANES:(r + 1) * SUBLANES] for r in range(ngrp)]
    row_id = lax.broadcasted_iota(jnp.int32, (SUBLANES, rows), 0)
    ranks = [jnp.zeros((SUBLANES, rows), F32) for _ in range(ngrp)]
    for j in range(n_slc):
        other = imp_t[j:j + 1, :]
        for r in range(ngrp):
            if r * SUBLANES > j:
                beats = other >= cands[r]
            elif (r + 1) * SUBLANES <= j:
                beats = other > cands[r]
            else:
                beats = jnp.where(row_id > j - r * SUBLANES, jnp.where(other >= cands[r], 1.0, 0.0),
                                  jnp.where(other > cands[r], 1.0, 0.0)) > 0.5
            ranks[r] = ranks[r] + jnp.where(beats, 1.0, 0.0)
    sel_t = jnp.concatenate([jnp.where(r < float(min(SLC_TOPK, n_slc)), 1.0, 0.0) for r in ranks], axis=0)
    if ngrp * SUBLANES < LANES:
        sel_t = jnp.concatenate([sel_t, jnp.zeros((LANES - ngrp * SUBLANES, rows), F32)], axis=0)
    for j in range(nsub):
        sel_ref[0, 0, j * QBLK:(j + 1) * QBLK, :] = sel_t[:, j * QBLK:(j + 1) * QBLK].T.astype(sel_ref.dtype)


def _cmp_select(proj3, kvc, bias_c, ovl, n_slc):
    B, T, _ = proj3.shape
    ncp = kvc.shape[3]
    nsub = math.gcd(CMP_SUBBLOCKS, T // QBLK)
    rows = nsub * QBLK
    return pl.pallas_call(
        functools.partial(_cmp_select_kernel, n_slc=n_slc, nsub=nsub),
        grid=(B, A_KV_HEADS, T // rows),
        in_specs=[
            pl.BlockSpec((1, rows, A_GRP * HEAD_DIM), lambda b, g, i: (b, i, g)),
            pl.BlockSpec((1, 1, 1, ncp, HEAD_DIM), lambda b, g, i: (0, b, g, 0, 0)),
            pl.BlockSpec((1, 1, 1, ncp, HEAD_DIM), lambda b, g, i: (1, b, g, 0, 0)),
            pl.BlockSpec((A_GRP, rows, ncp), lambda b, g, i: (g, i, 0)),
            pl.BlockSpec((ncp, LANES), lambda b, g, i: (0, 0)),
        ],
        out_specs=(
            pl.BlockSpec((1, rows, A_GRP * HEAD_DIM), lambda b, g, i: (b, i, g)),
            pl.BlockSpec((1, 1, rows, LANES), lambda b, g, i: (b, g, i, 0)),
        ),
        out_shape=(
            jax.ShapeDtypeStruct((B, T, A_Q_HEADS * HEAD_DIM), F32),
            jax.ShapeDtypeStruct((B, A_KV_HEADS, T, LANES), BF16),
        ),
        compiler_params=_cparams(("parallel", "parallel", "arbitrary")),
        name="nsa_cmp_select",
    )(proj3, kvc, kvc, bias_c, ovl)


def _slc_kernel(q_ref, k_ref, v_ref, sel_ref, e_ref, b_ref, o_ref, *scratch, n_tiles, nq):
    ig = pl.program_id(1)
    chains = [(g, a) for g in range(A_KV_HEADS) for a in range(nq)]
    m_refs, l_refs, acc_refs = scratch[0::3], scratch[1::3], scratch[2::3]
    qs, sels = [], []
    for ci, (g, a) in enumerate(chains):
        m_refs[ci][...] = jnp.full(m_refs[ci].shape, NEG, F32)
        l_refs[ci][...] = jnp.zeros(l_refs[ci].shape, F32)
        acc_refs[ci][...] = jnp.zeros(acc_refs[ci].shape, F32)
        rs = slice(a * QBLK, (a + 1) * QBLK)
        qs.append(jnp.concatenate([q_ref[0, rs, (g * A_GRP + h) * HEAD_DIM:(g * A_GRP + h + 1) * HEAD_DIM]
                                   for h in range(A_GRP)], axis=0))
        sels.append(sel_ref[0, g, rs, :])
    kw = SLC_CHUNK * QBLK

    def body(c, carry):
        start = pl.multiple_of(c * kw, kw)
        base = ig * nq - c * SLC_CHUNK + 1
        tidx = {d: jnp.clip(base + d, 0, n_tiles - 1) for d in range(-(SLC_CHUNK - 1), nq)}
        kts = [k_ref[0, pl.ds(start, kw), g * HEAD_DIM:(g + 1) * HEAD_DIM] for g in range(A_KV_HEADS)]
        vts = [v_ref[0, pl.ds(start, kw), g * HEAD_DIM:(g + 1) * HEAD_DIM] for g in range(A_KV_HEADS)]
        ech = e_ref[:, pl.ds(start, kw)]
        ss = [lax.dot_general(qs[ci], kts[g], (((1,), (1,)), ((), ())), preferred_element_type=F32)
              for ci, (g, a) in enumerate(chains)]
        chosens = [jnp.dot(sels[ci], ech, preferred_element_type=F32) for ci in range(len(chains))]
        pbs, alphas = [], []
        for ci, (g, a) in enumerate(chains):
            madd = (chosens[ci] - 1.0) * (-NEG)
            s = ss[ci]
            rows = [[s[h * QBLK:(h + 1) * QBLK, j * QBLK:(j + 1) * QBLK] + b_ref[g * A_GRP + h, tidx[a - j]]
                     + madd[:, j * QBLK:(j + 1) * QBLK] for j in range(SLC_CHUNK)] for h in range(A_GRP)]
            tile_max = jnp.concatenate([functools.reduce(jnp.maximum, r) for r in rows], axis=0)
            m_old = m_refs[ci][...]
            m_new = jnp.maximum(m_old, jnp.max(tile_max, axis=-1, keepdims=True))
            alpha = jnp.exp2(m_old - m_new)
            p = [[jnp.exp2(t - m_new[h * QBLK:(h + 1) * QBLK]) for t in r] for h, r in enumerate(rows)]
            tile_sum = jnp.concatenate([functools.reduce(jnp.add, r) for r in p], axis=0)
            l_refs[ci][...] = alpha * l_refs[ci][...] + jnp.sum(tile_sum, axis=-1, keepdims=True)
            pbs.append(jnp.concatenate([jnp.concatenate([t.astype(BF16) for t in r], axis=1) for r in p], axis=0))
            alphas.append(alpha)
            m_refs[ci][...] = m_new
        for ci, (g, a) in enumerate(chains):
            acc_refs[ci][...] = alphas[ci] * acc_refs[ci][...] + jnp.dot(pbs[ci], vts[g], preferred_element_type=F32)
        return carry

    lax.fori_loop(0, (ig * nq + nq - 1) // SLC_CHUNK + 1, body, 0)
    for ci, (g, a) in enumerate(chains):
        out = acc_refs[ci][...] / l_refs[ci][...]
        for h in range(A_GRP):
            col = (g * A_GRP + h) * HEAD_DIM
            o_ref[0, a * QBLK:(a + 1) * QBLK, col:col + HEAD_DIM] = out[h * QBLK:(h + 1) * QBLK]


def _slc_attn(proj3, sel, expand, bias):
    B, T, _ = proj3.shape
    n_tiles = bias.shape[1]
    nq = SLC_QTILES
    qrows = nq * QBLK
    assert SLC_CHUNK % nq == 0 and T % (SLC_CHUNK * QBLK) == 0
    chain_rows = A_GRP * QBLK
    n_chains = A_KV_HEADS * nq
    return pl.pallas_call(
        functools.partial(_slc_kernel, n_tiles=n_tiles, nq=nq),
        grid=(B, T // qrows),
        in_specs=[
            pl.BlockSpec((1, qrows, A_Q_HEADS * HEAD_DIM), lambda b, i: (b, i, COL_QA // A_Q_HEADS)),
            pl.BlockSpec((1, T, A_KV_HEADS * HEAD_DIM), lambda b, i: (b, 0, COL_KSA // A_KV_HEADS)),
            pl.BlockSpec((1, T, A_KV_HEADS * HEAD_DIM), lambda b, i: (b, 0, COL_VSA // A_KV_HEADS)),
            pl.BlockSpec((1, A_KV_HEADS, qrows, LANES), lambda b, i: (b, 0, i, 0)),
            pl.BlockSpec((LANES, T), lambda b, i: (0, 0)),
            pl.BlockSpec((A_Q_HEADS, n_tiles, QBLK, LANES), lambda b, i: (0, 0, 0, 0)),
        ],
        out_specs=pl.BlockSpec((1, qrows, A_Q_HEADS * HEAD_DIM), lambda b, i: (b, i, 0)),
        out_shape=jax.ShapeDtypeStruct((B, T, A_Q_HEADS * HEAD_DIM), F32),
        scratch_shapes=[pltpu.VMEM((chain_rows, 1), F32), pltpu.VMEM((chain_rows, 1), F32),
                        pltpu.VMEM((chain_rows, HEAD_DIM), F32)] * n_chains,
        compiler_params=_cparams(("parallel", "arbitrary")),
        name="nsa_slc_attn",
    )(proj3, proj3, proj3, sel, expand, bias)


def _combine_kernel(gate_ref, ocmp_ref, oslc_ref, owin_ref, ob_ref, *rest):
    oc_refs, lse_refs, o_ref = rest[:C_Q_HEADS], rest[C_Q_HEADS:2 * C_Q_HEADS], rest[2 * C_Q_HEADS]
    gate = 1.0 / (1.0 + jnp.exp(-gate_ref[...]))
    for h in range(A_Q_HEADS):
        sl = slice(h * HEAD_DIM, (h + 1) * HEAD_DIM)
        o = (gate[:, 3 * h:3 * h + 1] * ocmp_ref[:, sl] + gate[:, 3 * h + 1:3 * h + 2] * oslc_ref[:, sl]
             + gate[:, 3 * h + 2:3 * h + 3] * owin_ref[:, sl])
        o_ref[:, sl] = o.astype(o_ref.dtype)
    base = A_Q_HEADS * HEAD_DIM
    width = B_Q_HEADS * HEAD_DIM
    o_ref[:, base:base + width] = ob_ref[...].astype(o_ref.dtype)
    base += width
    for hh in range(C_GRP):
        heads = [C_GRP * gidx + hh for gidx in range(len(DIL_PAIRS))]
        lses = [lse_refs[c][...] for c in heads]
        mx = functools.reduce(jnp.maximum, lses)
        ws = [jnp.exp(x - mx) for x in lses]
        tot = functools.reduce(jnp.add, ws)
        for c, w in zip(heads, ws):
            o_ref[:, base + c * HEAD_DIM:base + (c + 1) * HEAD_DIM] = (oc_refs[c][...] * (w / tot)).astype(o_ref.dtype)


def _combine(gates, o_cmp, o_slc, o_win, o_b, o_cs, lses, *, tm):
    m = gates.shape[0]
    d_out = (A_Q_HEADS + B_Q_HEADS + C_Q_HEADS) * HEAD_DIM
    row = lambda w: pl.BlockSpec((tm, w), lambda i: (i, 0))
    head = lambda c: pl.BlockSpec((tm, HEAD_DIM), lambda i: (i, c))
    full = [gates, o_cmp, o_slc, o_win, o_b]
    return pl.pallas_call(
        _combine_kernel,
        grid=(m // tm,),
        in_specs=[row(a.shape[1]) for a in full] + [head(c) for _, c in o_cs + lses],
        out_specs=row(d_out),
        out_shape=jax.ShapeDtypeStruct((m, d_out), BF16),
        compiler_params=_cparams(("parallel",)),
        name="combine",
    )(*full, *[a for a, _ in o_cs + lses])


def _mixers(proj3, gates, dil32, layer_params, tables):
    B, T, _ = proj3.shape
    cmp_pe, cmp_w1, cmp_w2, kc_gain, sinks = layer_params
    n_cmp = (T - CMP_BLOCK) // CMP_STRIDE + 1
    n_slc = T // SLC_BLOCK
    ncp = T // CMP_STRIDE

    def cmp_rows(col):
        a = proj3[:, :, col * LANES:(col + A_KV_HEADS) * LANES].reshape(B, ncp, CMP_STRIDE, A_KV_HEADS, HEAD_DIM)
        return a.transpose(0, 3, 1, 2, 4).reshape(B, A_KV_HEADS, ncp, CMP_STRIDE * HEAD_DIM)

    xr = jnp.stack([cmp_rows(COL_KCA), cmp_rows(COL_VCA)])
    pe = jnp.broadcast_to(cmp_pe.reshape(2, 1, CMP_BLOCK * HEAD_DIM), (2, 8, CMP_BLOCK * HEAD_DIM)).astype(BF16)
    kvc = _compress(xr, pe, cmp_w1.astype(BF16), cmp_w2.astype(BF16), kc_gain.reshape(1, HEAD_DIM), n_cmp)

    o_cmp, sel = _cmp_select(proj3, kvc, tables["bias_c"], tables["ovl"], n_slc)
    o_slc = _slc_attn(proj3, sel, tables["expand"], tables["bias_slc"])
    o_win = _band_attn(proj3, proj3, proj3, tables["bias_win"], n_kv=A_KV_HEADS, grp=A_GRP,
                       q_col=COL_QA // A_GRP, k_col=COL_KWA, v_col=COL_VWA)
    o_b = _band_attn(proj3, proj3, proj3, tables["bias_b"], n_kv=B_KV_HEADS, grp=B_GRP,
                     q_col=COL_QB // B_GRP, k_col=COL_KB, v_col=COL_VB, sinks=sinks)
    o_cs, lses = [], []
    for gidx, (_, dil) in enumerate(DIL_PAIRS):
        if dil == 1:
            o, lse = _band_attn(proj3, proj3, proj3, tables["bias_c%d" % gidx], n_kv=1, grp=C_GRP,
                                q_col=COL_QC // C_GRP + gidx, k_col=COL_KC + gidx, v_col=COL_VC + gidx,
                                with_lse=True)
            o_cs += [(o.reshape(B * T, -1), h) for h in range(C_GRP)]
            lses += [(lse.reshape(B * T, -1), h) for h in range(C_GRP)]
        else:
            os_, ls_ = _dil_attn(dil32, tables["bias_c%d" % gidx], col=DIL_F32_SLOT[COL_QC + C_GRP * gidx], dil=dil)
            o_cs += [(o.reshape(B * T, -1), 0) for o in os_]
            lses += [(lse.reshape(B * T, -1), 0) for lse in ls_]
    return _combine(gates, o_cmp.reshape(B * T, -1), o_slc.reshape(B * T, -1), o_win.reshape(B * T, -1),
                    o_b.reshape(B * T, -1), o_cs, lses, tm=256)


def _build_tables(rel_bias, T):
    n_cmp = (T - CMP_BLOCK) // CMP_STRIDE + 1
    n_slc = T // SLC_BLOCK
    ncp = T // CMP_STRIDE
    nq = T // QBLK
    tables = {}
    tables["bias_c"] = _bias_tiles(rel_bias, 0, A_Q_HEADS, nq, width=ncp, kstride=CMP_STRIDE,
                                   koff=CMP_BLOCK - 1, ncols=n_cmp).reshape(A_Q_HEADS, T, ncp)
    n_sat = -(-(SAT_DIST + QBLK - 1) // QBLK) + 1
    tables["bias_slc"] = _bias_tiles(rel_bias, 0, A_Q_HEADS, min(n_sat, nq) + 1, koff=QBLK)
    win = NSA_WINDOW - 1
    tables["bias_win"] = _bias_tiles(rel_bias, 0, A_Q_HEADS, -(-win // QBLK) + 2, max_dist=win)
    swa = SWA_WINDOW - 1
    tables["bias_b"] = _bias_tiles(rel_bias, A_Q_HEADS, B_Q_HEADS, -(-swa // QBLK) + 2, max_dist=swa)
    for gidx, (w, dil) in enumerate(DIL_PAIRS):
        md = w // dil
        tables["bias_c%d" % gidx] = _bias_tiles(rel_bias, A_Q_HEADS + B_Q_HEADS + C_GRP * gidx, C_GRP,
                                                -(-md // QBLK) + 2, dscale=dil, max_dist=md)
    c0 = np.arange(ncp)[:, None] * CMP_STRIDE
    s0 = np.arange(LANES)[None, :] * SLC_BLOCK
    ovl = np.clip(np.minimum(c0 + CMP_BLOCK, s0 + SLC_BLOCK) - np.maximum(c0, s0), 0, None) / CMP_BLOCK
    ovl = ovl * (np.arange(ncp)[:, None] < n_cmp) * (np.arange(LANES)[None, :] < n_slc)
    tables["ovl"] = jnp.asarray(ovl, BF16)
    expand = (np.arange(T)[None, :] // SLC_BLOCK) == np.arange(LANES)[:, None]
    tables["expand"] = jnp.asarray(expand, BF16)
    return tables


def _proj_gain(g):
    ones = jnp.ones((HEAD_DIM,), F32)
    spec = [(g[0] * (SCALE * LOG2E), 6), (ones, 4), (g[2], 2), (ones, 2), (g[3], 2), (ones, 2),
            (g[4] * (SCALE * LOG2E), 4), (g[5], 2), (ones, 2), (g[6] * (SCALE * LOG2E), 6), (g[7], 3), (ones, 3)]
    assert sum(n for _, n in spec) == N_MAIN_BLOCKS
    return jnp.concatenate([jnp.tile(v, n) for v, n in spec]).reshape(1, N_MAIN)


def kernel(x, norm_attn, w_in, qk_gain, cmp_pe, cmp_w1, cmp_w2, sinks, rel_bias, w_out, norm_ffn, w_gate, w_up, w_down):
    B, T, D = x.shape
    depth = w_in.shape[0]
    tables = _build_tables(rel_bias, T)
    x2 = x.reshape(B * T, D)
    w_all = _wprep(w_in, tr=256)
    w_out_b, w_gate_b, w_up_b, w_down_b = (w.astype(BF16) for w in (w_out, w_gate, w_up, w_down))
    tm = min(512, B * T)
    for l in range(depth):
        proj, gates, dil32 = _proj(x2, norm_attn[l].reshape(1, D), w_all, l, _proj_gain(qk_gain[l]), tm=tm)
        mix = _mixers(proj.reshape(B, T, N_MAIN), gates, dil32.reshape(B, T, DIL_F32_COLS),
                      (cmp_pe[l], cmp_w1[l], cmp_w2[l], qk_gain[l][1], sinks[l]), tables)
        x2 = _outproj(x2, mix, w_out_b, l, tm=tm)
        x2 = _ffn(x2, norm_ffn[l].reshape(1, D), w_gate_b, w_up_b, w_down_b, l, tm=tm, tf=512)
    return x2.reshape(B, T, D)
```

```python
import functools
import math

import numpy as np
import jax
import jax.numpy as jnp
from jax import lax
from jax.experimental import pallas as pl
from jax.experimental.pallas import tpu as pltpu

F32 = jnp.float32
BF16 = jnp.bfloat16

HEAD_DIM = 128
LANES = 128
SUBLANES = 8
QBLK = 128
BAND_SUBBLOCKS = 8
CMP_SUBBLOCKS = 4
A_Q_HEADS, A_KV_HEADS = 6, 2
A_GRP = A_Q_HEADS // A_KV_HEADS
B_Q_HEADS, B_KV_HEADS = 4, 2
B_GRP = B_Q_HEADS // B_KV_HEADS
DIL_PAIRS = ((128, 1), (512, 4), (2048, 16))
C_GRP = 2
C_Q_HEADS = C_GRP * len(DIL_PAIRS)
CMP_BLOCK, CMP_STRIDE = 32, 16
SLC_BLOCK, SLC_TOPK = 64, 16
SLC_CHUNK = 4
SLC_QTILES = 1
NSA_WINDOW, SWA_WINDOW = 512, 128
FORCED_SCORE = 1.0e4
REL_BUCKETS, REL_MAX_EXACT, REL_MAX_DIST = 32, 16, 2048
SCALE = HEAD_DIM ** -0.5
LOG2E = math.log2(math.e)
LN2 = math.log(2.0)
EPS = 1e-6
NEG = -1e30
VMEM_LIMIT = 56 * 1024 * 1024

COL_QA, COL_KCA, COL_VCA, COL_KSA, COL_VSA, COL_KWA, COL_VWA = 0, 6, 8, 10, 12, 14, 16
COL_QB, COL_KB, COL_VB, COL_QC, COL_KC, COL_VC = 18, 22, 24, 26, 32, 35
N_MAIN_BLOCKS = 38
N_MAIN = N_MAIN_BLOCKS * LANES
GATE_START = 2304
N_GATES = A_Q_HEADS * 3
PROJ_NORMED = ([True] * 6 + [False] * 4 + [True] * 2 + [False] * 2 + [True] * 2 + [False] * 2
               + [True] * 6 + [False] * 2 + [True] * 9 + [False] * 3)
DIL_F32_SLOT = {}
for _g, (_, _d) in enumerate(DIL_PAIRS):
    if _d > 1:
        _base = 4 * len([1 for _, _dd in DIL_PAIRS[:_g] if _dd > 1])
        DIL_F32_SLOT.update({COL_QC + C_GRP * _g: _base, COL_QC + C_GRP * _g + 1: _base + 1,
                             COL_KC + _g: _base + 2, COL_VC + _g: _base + 3})
DIL_F32_COLS = LANES * len(DIL_F32_SLOT)
DIL_TILES_PER_STEP = 8
PROJ_SUBTILE = 256
MIX_ROW_SPLIT = 4
OUT_SUBTILE = 512


def _bucket_starts():
    d = np.arange(0, 1 << 17)
    out = []
    for dt in (np.float32, np.float64):
        far = np.maximum(d, REL_MAX_EXACT).astype(dt)
        lb = REL_MAX_EXACT + (np.log(far / dt(REL_MAX_EXACT)) / dt(math.log(REL_MAX_DIST / REL_MAX_EXACT))
                              * dt(REL_BUCKETS - REL_MAX_EXACT)).astype(np.int64)
        out.append(np.where(d < REL_MAX_EXACT, d, np.minimum(lb, REL_BUCKETS - 1)))
    assert (out[0] == out[1]).all() and (np.diff(out[0]) >= 0).all()
    return [int(np.argmax(out[0] >= b)) for b in range(REL_BUCKETS)]


BUCKET_START = _bucket_starts()
SAT_DIST = BUCKET_START[REL_BUCKETS - 1]


def _cparams(sem, vmem=VMEM_LIMIT):
    return pltpu.CompilerParams(dimension_semantics=sem, vmem_limit_bytes=vmem)


def _bias_tile_kernel(tab_ref, o_ref, *, head0, width, kstride, koff, dscale, max_dist, ncols):
    h = pl.program_id(0) + head0
    t = pl.program_id(1)
    r = lax.broadcasted_iota(jnp.int32, (QBLK, width), 0)
    c = lax.broadcasted_iota(jnp.int32, (QBLK, width), 1)
    dist = t * QBLK + r - kstride * c - koff
    d = dist * dscale
    val = jnp.full((QBLK, width), tab_ref[REL_BUCKETS - 1, h] * LOG2E, F32)
    for b in range(REL_BUCKETS - 2, -1, -1):
        val = jnp.where(d < BUCKET_START[b + 1], tab_ref[b, h] * LOG2E, val)
    val = jnp.where(dist >= 0, val, NEG)
    if max_dist is not None:
        val = jnp.where(dist <= max_dist, val, NEG)
    if ncols < width:
        val = jnp.where(c < ncols, val, NEG)
    o_ref[0, 0] = val


def _bias_tiles(rel_bias, head0, nheads, ntiles, *, width=LANES, kstride=1, koff=0, dscale=1,
                max_dist=None, ncols=None):
    ncols = width if ncols is None else ncols
    kern = functools.partial(_bias_tile_kernel, head0=head0, width=width, kstride=kstride, koff=koff,
                             dscale=dscale, max_dist=max_dist, ncols=ncols)
    return pl.pallas_call(
        kern,
        grid=(nheads, ntiles),
        in_specs=[pl.BlockSpec(memory_space=pltpu.SMEM)],
        out_specs=pl.BlockSpec((1, 1, QBLK, width), lambda h, t: (h, t, 0, 0)),
        out_shape=jax.ShapeDtypeStruct((nheads, ntiles, QBLK, width), F32),
        compiler_params=_cparams(("parallel", "parallel")),
        name="bias_tiles",
    )(rel_bias)


def _rms(x, w):
    ms = jnp.mean(x * x, axis=-1, keepdims=True)
    return x * lax.rsqrt(ms + EPS) * w


def _wprep_kernel(w_ref, o_ref):
    o_ref[0, :, :GATE_START] = w_ref[0, :, :GATE_START].astype(BF16)
    o_ref[0, :, GATE_START:N_MAIN] = w_ref[0, :, GATE_START + N_GATES:].astype(BF16)
    o_ref[0, :, N_MAIN:] = w_ref[0, :, GATE_START:GATE_START + LANES].astype(BF16)


def _wprep(w_in, *, tr):
    depth, d, n_in = w_in.shape
    return pl.pallas_call(
        _wprep_kernel,
        grid=(depth, d // tr),
        in_specs=[pl.BlockSpec((1, tr, n_in), lambda l, i: (l, i, 0))],
        out_specs=pl.BlockSpec((1, tr, N_MAIN + LANES), lambda l, i: (l, i, 0)),
        out_shape=jax.ShapeDtypeStruct((depth, d, N_MAIN + LANES), BF16),
        compiler_params=_cparams(("parallel", "parallel")),
        name="w_in_prep",
    )(w_in)


def _proj_kernel(x_ref, nw_ref, w_ref, gain_ref, o_ref, gate_ref, dil_ref):
    h = _rms(x_ref[...], nw_ref[...]).astype(BF16)
    for c0 in range(0, N_MAIN, PROJ_SUBTILE):
        width = min(PROJ_SUBTILE, N_MAIN - c0)
        acc = jnp.dot(h, w_ref[:, c0:c0 + width], preferred_element_type=F32)
        for c in range(width // LANES):
            blk = c0 // LANES + c
            sl = slice(blk * LANES, (blk + 1) * LANES)
            y = acc[:, c * LANES:(c + 1) * LANES]
            if PROJ_NORMED[blk]:
                ms = jnp.mean(y * y, axis=-1, keepdims=True)
                y = y * lax.rsqrt(ms + EPS) * gain_ref[:, sl]
            o_ref[:, sl] = y.astype(o_ref.dtype)
            if blk in DIL_F32_SLOT:
                dil_ref[:, DIL_F32_SLOT[blk] * LANES:(DIL_F32_SLOT[blk] + 1) * LANES] = y
    gate_ref[...] = jnp.dot(h, w_ref[:, N_MAIN:], preferred_element_type=F32)


def _proj(x2d, norm_w, w_all, layer, gain, *, tm):
    m, d = x2d.shape
    n = w_all.shape[2]
    return pl.pallas_call(
        _proj_kernel,
        grid=(m // tm,),
        in_specs=[
            pl.BlockSpec((tm, d), lambda i: (i, 0)),
            pl.BlockSpec((1, d), lambda i: (0, 0)),
            pl.BlockSpec((None, d, n), lambda i: (layer, 0, 0), pipeline_mode=pl.Buffered(1)),
            pl.BlockSpec((1, N_MAIN), lambda i: (0, 0)),
        ],
        out_specs=(pl.BlockSpec((tm, N_MAIN), lambda i: (i, 0)), pl.BlockSpec((tm, LANES), lambda i: (i, 0)),
                   pl.BlockSpec((tm, DIL_F32_COLS), lambda i: (i, 0))),
        out_shape=(jax.ShapeDtypeStruct((m, N_MAIN), BF16), jax.ShapeDtypeStruct((m, LANES), F32),
                   jax.ShapeDtypeStruct((m, DIL_F32_COLS), F32)),
        compiler_params=_cparams(("parallel",)),
        name="in_proj",
    )(x2d, norm_w, w_all, gain)


def _ffn_kernel(x_ref, nw_ref, wg_ref, wu_ref, wd_ref, o_ref, h_ref, acc_ref):
    f = pl.program_id(1)

    @pl.when(f == 0)
    def _():
        h_ref[...] = _rms(x_ref[...], nw_ref[...]).astype(BF16)
        acc_ref[...] = jnp.zeros_like(acc_ref)

    h = h_ref[...]
    g = jnp.dot(h, wg_ref[...], preferred_element_type=F32)
    u = jnp.dot(h, wu_ref[...], preferred_element_type=F32)
    a = (g * (1.0 / (1.0 + jnp.exp(-g))) * u).astype(BF16)
    acc_ref[...] += jnp.dot(a, wd_ref[...], preferred_element_type=F32)

    @pl.when(f == pl.num_programs(1) - 1)
    def _():
        o_ref[...] = x_ref[...] + acc_ref[...]


def _ffn(x2d, norm_w, wg, wu, wd, layer, *, tm, tf):
    m, d = x2d.shape
    dff = wg.shape[2]
    return pl.pallas_call(
        _ffn_kernel,
        grid=(m // tm, dff // tf),
        in_specs=[
            pl.BlockSpec((tm, d), lambda i, f: (i, 0)),
            pl.BlockSpec((1, d), lambda i, f: (0, 0)),
            pl.BlockSpec((None, d, tf), lambda i, f: (layer, 0, f)),
            pl.BlockSpec((None, d, tf), lambda i, f: (layer, 0, f)),
            pl.BlockSpec((None, tf, d), lambda i, f: (layer, f, 0)),
        ],
        out_specs=pl.BlockSpec((tm, d), lambda i, f: (i, 0)),
        out_shape=jax.ShapeDtypeStruct((m, d), F32),
        scratch_shapes=[pltpu.VMEM((tm, d), BF16), pltpu.VMEM((tm, d), F32)],
        compiler_params=_cparams(("parallel", "arbitrary")),
        name="ffn",
    )(x2d, norm_w, wg, wu, wd)


def _band_kernel(*refs, grp, n_off, nsub, has_sinks, with_lse):
    if has_sinks:
        sink_ref, q_ref, k_ref, v_ref, b_ref = refs[:5]
        outs = refs[5:]
    else:
        q_ref, k_ref, v_ref, b_ref = refs[:4]
        outs = refs[4:]
    o_ref = outs[0]
    g = pl.program_id(1)
    i0 = pl.program_id(2) * nsub
    tiles = {}

    def kv_tile(rel):
        if rel not in tiles:
            start = pl.multiple_of(jnp.maximum(i0 + rel, 0) * QBLK, QBLK)
            tiles[rel] = (k_ref[0, pl.ds(start, QBLK), :], v_ref[0, pl.ds(start, QBLK), :])
        return tiles[rel]

    def head_rows(x, h):
        return x[h * QBLK:(h + 1) * QBLK]

    kvs = [[kv_tile(j - off) for off in range(n_off)] for j in range(nsub)]
    tidxs = [[jnp.where(i0 + j - off >= 0, off, n_off) for off in range(n_off)] for j in range(nsub)]
    scores = []
    for j in range(nsub):
        rs = slice(j * QBLK, (j + 1) * QBLK)
        kcat = jnp.concatenate([t[0] for t in kvs[j]], axis=0)
        q = jnp.concatenate([q_ref[0, rs, h * HEAD_DIM:(h + 1) * HEAD_DIM] for h in range(grp)], axis=0)
        scores.append(lax.dot_general(q, kcat, (((1,), (1,)), ((), ())), preferred_element_type=F32))
    probs, stats = [], []
    for j in range(nsub):
        s = scores[j]
        st = [[head_rows(s, h)[:, off * QBLK:(off + 1) * QBLK] + b_ref[h, tidxs[j][off]] for off in range(n_off)]
              for h in range(grp)]
        m = jnp.max(jnp.concatenate([functools.reduce(jnp.maximum, r) for r in st], axis=0), axis=-1, keepdims=True)
        p = [[jnp.exp2(st[h][off] - head_rows(m, h)) for off in range(n_off)] for h in range(grp)]
        l = jnp.sum(jnp.concatenate([functools.reduce(jnp.add, r) for r in p], axis=0), axis=-1, keepdims=True)
        probs.append(jnp.concatenate([jnp.concatenate([t.astype(BF16) for t in r], axis=1) for r in p], axis=0))
        stats.append((m, l))
    for j in range(nsub):
        rs = slice(j * QBLK, (j + 1) * QBLK)
        m, l = stats[j]
        vcat = jnp.concatenate([t[1] for t in kvs[j]], axis=0)
        o = jnp.dot(probs[j], vcat, preferred_element_type=F32)
        den = l
        if has_sinks:
            sink = jnp.concatenate([jnp.full((QBLK, 1), sink_ref[g * grp + h] * LOG2E, F32) for h in range(grp)], axis=0)
            den = l + jnp.exp2(sink - m)
        o = o / den
        for h in range(grp):
            sl = slice(h * HEAD_DIM, (h + 1) * HEAD_DIM)
            o_ref[0, rs, sl] = head_rows(o, h).astype(o_ref.dtype)
            if with_lse:
                outs[1][0, rs, sl] = jnp.broadcast_to(head_rows(m * LN2 + jnp.log(l), h), (QBLK, HEAD_DIM))


def _band_attn(q_arr, k_arr, v_arr, bias, *, n_kv, grp, q_col, k_col, v_col, sinks=None, with_lse=False):
    n, L, _ = q_arr.shape
    n_off = bias.shape[1] - 1
    gw = grp * HEAD_DIM
    nsub = math.gcd(BAND_SUBBLOCKS, L // QBLK)
    rows = nsub * QBLK
    kern = functools.partial(_band_kernel, grp=grp, n_off=n_off, nsub=nsub, has_sinks=sinks is not None,
                             with_lse=with_lse)
    in_specs = [
        pl.BlockSpec((1, rows, gw), lambda b, g, i: (b, i, q_col + g)),
        pl.BlockSpec((1, L, HEAD_DIM), lambda b, g, i: (b, 0, k_col + g)),
        pl.BlockSpec((1, L, HEAD_DIM), lambda b, g, i: (b, 0, v_col + g)),
        pl.BlockSpec((grp, n_off + 1, QBLK, LANES), lambda b, g, i: (g, 0, 0, 0)),
    ]
    args = [q_arr, k_arr, v_arr, bias]
    if sinks is not None:
        in_specs = [pl.BlockSpec(memory_space=pltpu.SMEM)] + in_specs
        args = [sinks] + args
    o_spec = pl.BlockSpec((1, rows, gw), lambda b, g, i: (b, i, g))
    o_shape = jax.ShapeDtypeStruct((n, L, n_kv * gw), F32)
    return pl.pallas_call(
        kern,
        grid=(n, n_kv, L // rows),
        in_specs=in_specs,
        out_specs=(o_spec, o_spec) if with_lse else o_spec,
        out_shape=(o_shape, o_shape) if with_lse else o_shape,
        compiler_params=_cparams(("parallel", "parallel", "arbitrary")),
        name="band_attn",
    )(*args)


def _dil_kernel(*refs, dil, nres):
    q_refs, (k_ref, v_ref, b_ref) = refs[:C_GRP], refs[C_GRP:C_GRP + 3]
    o_refs, lse_refs = refs[C_GRP + 3:2 * C_GRP + 3], refs[2 * C_GRP + 3:]
    ls = k_ref.shape[1] // dil
    ntile = ls // QBLK
    r0 = pl.program_id(1) * nres
    units = [(rr, j) for rr in range(nres) for j in range(ntile)]

    def rows(rr, j):
        return pl.ds(r0 + rr + dil * QBLK * j, QBLK, stride=dil)

    kv = {u: (k_ref[0, rows(*u), :].astype(BF16), v_ref[0, rows(*u), :].astype(BF16)) for u in units}
    n_off = b_ref.shape[1] - 1
    offs = {(rr, j): [off for off in range(n_off) if j - off >= 0] for rr, j in units}
    scores = []
    for rr, j in units:
        kcat = jnp.concatenate([kv[(rr, j - off)][0] for off in offs[(rr, j)]], axis=0)
        q = jnp.concatenate([q_ref[0, rows(rr, j), :].astype(BF16) for q_ref in q_refs], axis=0)
        scores.append(lax.dot_general(q, kcat, (((1,), (1,)), ((), ())), preferred_element_type=F32))
    probs, stats = [], []
    for u, s in zip(units, scores):
        st = [[s[h * QBLK:(h + 1) * QBLK, n * QBLK:(n + 1) * QBLK] + b_ref[h, off] for n, off in enumerate(offs[u])]
              for h in range(C_GRP)]
        m = jnp.max(jnp.concatenate([functools.reduce(jnp.maximum, r) for r in st], axis=0), axis=-1, keepdims=True)
        p = [[jnp.exp2(t - m[h * QBLK:(h + 1) * QBLK]) for t in r] for h, r in enumerate(st)]
        l = jnp.sum(jnp.concatenate([functools.reduce(jnp.add, r) for r in p], axis=0), axis=-1, keepdims=True)
        probs.append(jnp.concatenate([jnp.concatenate([t.astype(BF16) for t in r], axis=1) for r in p], axis=0))
        stats.append((m, l))
    for (rr, j), pb, (m, l) in zip(units, probs, stats):
        vcat = jnp.concatenate([kv[(rr, j - off)][1] for off in offs[(rr, j)]], axis=0)
        o = jnp.dot(pb, vcat, preferred_element_type=F32) / l
        lse = m * LN2 + jnp.log(l)
        for h in range(C_GRP):
            o_refs[h][0, rows(rr, j), :] = o[h * QBLK:(h + 1) * QBLK]
            lse_refs[h][0, rows(rr, j), :] = jnp.broadcast_to(lse[h * QBLK:(h + 1) * QBLK], (QBLK, HEAD_DIM))


def _dil_attn(x3, bias, *, col, dil):
    B, T, _ = x3.shape
    ntile = T // dil // QBLK
    nres = max(1, min(dil, DIL_TILES_PER_STEP // ntile))
    spec = pl.BlockSpec((1, T, HEAD_DIM), lambda b, r: (b, 0, 0))
    shape = jax.ShapeDtypeStruct((B, T, HEAD_DIM), F32)
    outs = pl.pallas_call(
        functools.partial(_dil_kernel, dil=dil, nres=nres),
        grid=(B, dil // nres),
        in_specs=[pl.BlockSpec((1, T, HEAD_DIM), functools.partial(lambda b, r, c: (b, 0, c), c=col + c))
                  for c in range(C_GRP + 2)]
        + [pl.BlockSpec((C_GRP, bias.shape[1], QBLK, LANES), lambda b, r: (0, 0, 0, 0))],
        out_specs=(spec,) * (2 * C_GRP),
        out_shape=(shape,) * (2 * C_GRP),
        compiler_params=_cparams(("parallel", "arbitrary")),
        name="dilated_attn",
    )(*([x3] * (C_GRP + 2)), bias)
    return outs[:C_GRP], outs[C_GRP:]


def _compress_kernel(x_ref, pe_ref, w1_ref, w2_ref, gain_ref, o_ref, *, n_cmp):
    half = w1_ref.shape[1] // 2
    x = x_ref[0, 0, 0]
    y0 = jnp.dot(x, w1_ref[0, :half], preferred_element_type=F32)
    y1 = jnp.dot(x, w1_ref[0, half:], preferred_element_type=F32)
    pe = jnp.dot(pe_ref[0], w1_ref[0], preferred_element_type=F32)[0:1]
    rows = x.shape[0]
    c = y0 + pltpu.roll(y1, rows - 1, 0) + pe
    gl = 0.5 * c * (1.0 + jnp.tanh(math.sqrt(2.0 / math.pi) * (c + 0.044715 * (c * c * c))))
    out = jnp.dot(gl.astype(BF16), w2_ref[0], preferred_element_type=F32)
    out = jnp.where(pl.program_id(0) == 0, _rms(out, gain_ref[...]), out)
    valid = lax.broadcasted_iota(jnp.int32, out.shape, 0) < n_cmp
    o_ref[0, 0, 0] = jnp.where(valid, out, 0.0).astype(o_ref.dtype)


def _compress(xr, pe, w1, w2, gain, n_cmp):
    _, B, hk, rows, wide = xr.shape
    return pl.pallas_call(
        functools.partial(_compress_kernel, n_cmp=n_cmp),
        grid=(2, B, hk),
        in_specs=[
            pl.BlockSpec((1, 1, 1, rows, wide), lambda s, b, g: (s, b, g, 0, 0)),
            pl.BlockSpec((1, 8, 2 * wide), lambda s, b, g: (s, 0, 0)),
            pl.BlockSpec((1, 2 * wide, HEAD_DIM), lambda s, b, g: (s, 0, 0)),
            pl.BlockSpec((1, HEAD_DIM, HEAD_DIM), lambda s, b, g: (s, 0, 0)),
            pl.BlockSpec((1, HEAD_DIM), lambda s, b, g: (0, 0)),
        ],
        out_specs=pl.BlockSpec((1, 1, 1, rows, HEAD_DIM), lambda s, b, g: (s, b, g, 0, 0)),
        out_shape=jax.ShapeDtypeStruct((2, B, hk, rows, HEAD_DIM), BF16),
        compiler_params=_cparams(("arbitrary", "arbitrary", "arbitrary")),
        name="nsa_compress",
    )(xr, pe, w1, w2, gain)


def _cmp_select_kernel(q_ref, kc_ref, vc_ref, b_ref, ovl_ref, o_ref, sel_ref, *, n_slc, nsub):
    i0 = pl.program_id(2) * nsub
    rows = nsub * QBLK
    kc = kc_ref[0, 0, 0]
    vc = vc_ref[0, 0, 0]
    q_all = jnp.concatenate([q_ref[0, :, h * HEAD_DIM:(h + 1) * HEAD_DIM] for h in range(A_GRP)], axis=0)
    s_all = lax.dot_general(q_all, kc, (((1,), (1,)), ((), ())), preferred_element_type=F32)
    ps = []
    for h in range(A_GRP):
        s = s_all[h * rows:(h + 1) * rows] + b_ref[h]
        m = jnp.max(s, axis=-1, keepdims=True)
        e = jnp.exp2(s - jnp.where(m > 0.5 * NEG, m, 0.0))
        den = jnp.sum(e, axis=-1, keepdims=True)
        ps.append((e / jnp.where(den > 0, den, 1.0)).astype(BF16))
    r_all = jnp.dot(jnp.concatenate(ps, axis=0), jnp.concatenate([vc, ovl_ref[...]], axis=1),
                    preferred_element_type=F32)
    imp = jnp.zeros((rows, LANES), F32)
    for h in range(A_GRP):
        o_ref[0, :, h * HEAD_DIM:(h + 1) * HEAD_DIM] = r_all[h * rows:(h + 1) * rows, :HEAD_DIM]
        imp = imp + r_all[h * rows:(h + 1) * rows, HEAD_DIM:]
    t = i0 * QBLK + lax.broadcasted_iota(jnp.int32, (rows, LANES), 0)
    blk = lax.broadcasted_iota(jnp.int32, (rows, LANES), 1)
    cur = t // SLC_BLOCK
    imp = jnp.where(blk == 0, FORCED_SCORE, imp)
    imp = jnp.where(blk == cur, FORCED_SCORE, imp)
    imp = jnp.where(blk == cur - 1, FORCED_SCORE, imp)
    imp = jnp.where(blk * SLC_BLOCK <= t, imp, NEG)
    imp = jnp.where(blk < n_slc, imp, 2.0 * NEG)
    imp_t = jnp.concatenate([imp[j * QBLK:(j + 1) * QBLK].T for j in range(nsub)], axis=1)
    ngrp = -(-n_slc // SUBLANES)
    cands = [imp_t[r * SUBLANES:(r + 1) * SUBLANES] for r in range(ngrp)]
    row_id = lax.broadcasted_iota(jnp.int32, (SUBLANES, rows), 0)
    ranks = [jnp.zeros((SUBLANES, rows), F32) for _ in range(ngrp)]
    for j in range(n_slc):
        other = imp_t[j:j + 1, :]
        for r in range(ngrp):
            if r * SUBLANES > j:
                beats = other >= cands[r]
            elif (r + 1) * SUBLANES <= j:
                beats = other > cands[r]
            else:
                beats = jnp.where(row_id > j - r * SUBLANES, jnp.where(other >= cands[r], 1.0, 0.0),
                                  jnp.where(other > cands[r], 1.0, 0.0)) > 0.5
            ranks[r] = ranks[r] + jnp.where(beats, 1.0, 0.0)
    sel_t = jnp.concatenate([jnp.where(r < float(min(SLC_TOPK, n_slc)), 1.0, 0.0) for r in ranks], axis=0)
    if ngrp * SUBLANES < LANES:
        sel_t = jnp.concatenate([sel_t, jnp.zeros((LANES - ngrp * SUBLANES, rows), F32)], axis=0)
    for j in range(nsub):
        sel_ref[0, 0, j * QBLK:(j + 1) * QBLK, :] = sel_t[:, j * QBLK:(j + 1) * QBLK].T.astype(sel_ref.dtype)


def _cmp_select(proj3, kvc, bias_c, ovl, n_slc):
    B, T, _ = proj3.shape
    ncp = kvc.shape[3]
    nsub = math.gcd(CMP_SUBBLOCKS, T // QBLK)
    rows = nsub * QBLK
    return pl.pallas_call(
        functools.partial(_cmp_select_kernel, n_slc=n_slc, nsub=nsub),
        grid=(B, A_KV_HEADS, T // rows),
        in_specs=[
            pl.BlockSpec((1, rows, A_GRP * HEAD_DIM), lambda b, g, i: (b, i, g)),
            pl.BlockSpec((1, 1, 1, ncp, HEAD_DIM), lambda b, g, i: (0, b, g, 0, 0)),
            pl.BlockSpec((1, 1, 1, ncp, HEAD_DIM), lambda b, g, i: (1, b, g, 0, 0)),
            pl.BlockSpec((A_GRP, rows, ncp), lambda b, g, i: (g, i, 0)),
            pl.BlockSpec((ncp, LANES), lambda b, g, i: (0, 0)),
        ],
        out_specs=(
            pl.BlockSpec((1, rows, A_GRP * HEAD_DIM), lambda b, g, i: (b, i, g)),
            pl.BlockSpec((1, 1, rows, LANES), lambda b, g, i: (b, g, i, 0)),
        ),
        out_shape=(
            jax.ShapeDtypeStruct((B, T, A_Q_HEADS * HEAD_DIM), F32),
            jax.ShapeDtypeStruct((B, A_KV_HEADS, T, LANES), BF16),
        ),
        compiler_params=_cparams(("parallel", "parallel", "arbitrary")),
        name="nsa_cmp_select",
    )(proj3, kvc, kvc, bias_c, ovl)


def _slc_kernel(q_ref, k_ref, v_ref, sel_ref, e_ref, b_ref, o_ref, *scratch, n_tiles, nq):
    ig = pl.program_id(1)
    chains = [(g, a) for g in range(A_KV_HEADS) for a in range(nq)]
    m_refs, l_refs, acc_refs = scratch[0::3], scratch[1::3], scratch[2::3]
    qs, sels = [], []
    for ci, (g, a) in enumerate(chains):
        m_refs[ci][...] = jnp.full(m_refs[ci].shape, NEG, F32)
        l_refs[ci][...] = jnp.zeros(l_refs[ci].shape, F32)
        acc_refs[ci][...] = jnp.zeros(acc_refs[ci].shape, F32)
        rs = slice(a * QBLK, (a + 1) * QBLK)
        qs.append(jnp.concatenate([q_ref[0, rs, (g * A_GRP + h) * HEAD_DIM:(g * A_GRP + h + 1) * HEAD_DIM]
                                   for h in range(A_GRP)], axis=0))
        sels.append(sel_ref[0, g, rs, :])
    kw = SLC_CHUNK * QBLK

    def body(c, carry):
        start = pl.multiple_of(c * kw, kw)
        base = ig * nq - c * SLC_CHUNK + 1
        tidx = {d: jnp.clip(base + d, 0, n_tiles - 1) for d in range(-(SLC_CHUNK - 1), nq)}
        kts = [k_ref[0, pl.ds(start, kw), g * HEAD_DIM:(g + 1) * HEAD_DIM] for g in range(A_KV_HEADS)]
        vts = [v_ref[0, pl.ds(start, kw), g * HEAD_DIM:(g + 1) * HEAD_DIM] for g in range(A_KV_HEADS)]
        ech = e_ref[:, pl.ds(start, kw)]
        ss = [lax.dot_general(qs[ci], kts[g], (((1,), (1,)), ((), ())), preferred_element_type=F32)
              for ci, (g, a) in enumerate(chains)]
        chosens = [jnp.dot(sels[ci], ech, preferred_element_type=F32) for ci in range(len(chains))]
        pbs, alphas = [], []
        for ci, (g, a) in enumerate(chains):
            madd = (chosens[ci] - 1.0) * (-NEG)
            s = ss[ci]
            rows = [[s[h * QBLK:(h + 1) * QBLK, j * QBLK:(j + 1) * QBLK] + b_ref[g * A_GRP + h, tidx[a - j]]
                     + madd[:, j * QBLK:(j + 1) * QBLK] for j in range(SLC_CHUNK)] for h in range(A_GRP)]
            tile_max = jnp.concatenate([functools.reduce(jnp.maximum, r) for r in rows], axis=0)
            m_old = m_refs[ci][...]
            m_new = jnp.maximum(m_old, jnp.max(tile_max, axis=-1, keepdims=True))
            alpha = jnp.exp2(m_old - m_new)
            p = [[jnp.exp2(t - m_new[h * QBLK:(h + 1) * QBLK]) for t in r] for h, r in enumerate(rows)]
            tile_sum = jnp.concatenate([functools.reduce(jnp.add, r) for r in p], axis=0)
            l_refs[ci][...] = alpha * l_refs[ci][...] + jnp.sum(tile_sum, axis=-1, keepdims=True)
            pbs.append(jnp.concatenate([jnp.concatenate([t.astype(BF16) for t in r], axis=1) for r in p], axis=0))
            alphas.append(alpha)
            m_refs[ci][...] = m_new
        for ci, (g, a) in enumerate(chains):
            acc_refs[ci][...] = alphas[ci] * acc_refs[ci][...] + jnp.dot(pbs[ci], vts[g], preferred_element_type=F32)
        return carry

    lax.fori_loop(0, (ig * nq + nq - 1) // SLC_CHUNK + 1, body, 0)
    for ci, (g, a) in enumerate(chains):
        out = acc_refs[ci][...] / l_refs[ci][...]
        for h in range(A_GRP):
            col = (g * A_GRP + h) * HEAD_DIM
            o_ref[0, a * QBLK:(a + 1) * QBLK, col:col + HEAD_DIM] = out[h * QBLK:(h + 1) * QBLK]


def _slc_attn(proj3, sel, expand, bias):
    B, T, _ = proj3.shape
    n_tiles = bias.shape[1]
    nq = SLC_QTILES
    qrows = nq * QBLK
    assert SLC_CHUNK % nq == 0 and T % (SLC_CHUNK * QBLK) == 0
    chain_rows = A_GRP * QBLK
    n_chains = A_KV_HEADS * nq
    return pl.pallas_call(
        functools.partial(_slc_kernel, n_tiles=n_tiles, nq=nq),
        grid=(B, T // qrows),
        in_specs=[
            pl.BlockSpec((1, qrows, A_Q_HEADS * HEAD_DIM), lambda b, i: (b, i, COL_QA // A_Q_HEADS)),
            pl.BlockSpec((1, T, A_KV_HEADS * HEAD_DIM), lambda b, i: (b, 0, COL_KSA // A_KV_HEADS)),
            pl.BlockSpec((1, T, A_KV_HEADS * HEAD_DIM), lambda b, i: (b, 0, COL_VSA // A_KV_HEADS)),
            pl.BlockSpec((1, A_KV_HEADS, qrows, LANES), lambda b, i: (b, 0, i, 0)),
            pl.BlockSpec((LANES, T), lambda b, i: (0, 0)),
            pl.BlockSpec((A_Q_HEADS, n_tiles, QBLK, LANES), lambda b, i: (0, 0, 0, 0)),
        ],
        out_specs=pl.BlockSpec((1, qrows, A_Q_HEADS * HEAD_DIM), lambda b, i: (b, i, 0)),
        out_shape=jax.ShapeDtypeStruct((B, T, A_Q_HEADS * HEAD_DIM), F32),
        scratch_shapes=[pltpu.VMEM((chain_rows, 1), F32), pltpu.VMEM((chain_rows, 1), F32),
                        pltpu.VMEM((chain_rows, HEAD_DIM), F32)] * n_chains,
        compiler_params=_cparams(("parallel", "arbitrary")),
        name="nsa_slc_attn",
    )(proj3, proj3, proj3, sel, expand, bias)


def _mix_outproj_kernel(x_ref, gate_ref, ocmp_ref, oslc_ref, owin_ref, ob_ref, *rest):
    oc_refs, lse_refs = rest[:C_Q_HEADS], rest[C_Q_HEADS:2 * C_Q_HEADS]
    w_ref, o_ref, mix_ref = rest[2 * C_Q_HEADS:]
    tm = x_ref.shape[0]
    nsplit = MIX_ROW_SPLIT
    rows = tm // nsplit
    for part in range(nsplit):
        rs = slice(part * rows, (part + 1) * rows)
        gate = 1.0 / (1.0 + jnp.exp(-gate_ref[rs, :]))
        for h in range(A_Q_HEADS):
            sl = slice(h * HEAD_DIM, (h + 1) * HEAD_DIM)
            o = (gate[:, 3 * h:3 * h + 1] * ocmp_ref[rs, sl] + gate[:, 3 * h + 1:3 * h + 2] * oslc_ref[rs, sl]
                 + gate[:, 3 * h + 2:3 * h + 3] * owin_ref[rs, sl])
            mix_ref[rs, sl] = o.astype(mix_ref.dtype)
        base = A_Q_HEADS * HEAD_DIM
        width = B_Q_HEADS * HEAD_DIM
        mix_ref[rs, base:base + width] = ob_ref[rs, :].astype(mix_ref.dtype)
        base += width
        for hh in range(C_GRP):
            heads = [C_GRP * gidx + hh for gidx in range(len(DIL_PAIRS))]
            lses = [lse_refs[c][rs, :] for c in heads]
            mx = functools.reduce(jnp.maximum, lses)
            ws = [jnp.exp(x - mx) for x in lses]
            tot = functools.reduce(jnp.add, ws)
            for c, w in zip(heads, ws):
                mix_ref[rs, base + c * HEAD_DIM:base + (c + 1) * HEAD_DIM] = (oc_refs[c][rs, :] * (w / tot)).astype(mix_ref.dtype)
    for part in range(nsplit):
        rs = slice(part * rows, (part + 1) * rows)
        a = mix_ref[rs, :]
        for c0 in range(0, o_ref.shape[1], OUT_SUBTILE):
            sl = slice(c0, c0 + OUT_SUBTILE)
            o_ref[rs, sl] = x_ref[rs, sl] + jnp.dot(a, w_ref[:, sl], preferred_element_type=F32)


def _mix_outproj(x2d, gates, o_cmp, o_slc, o_win, o_b, o_cs, lses, w_all, layer, *, tm):
    m, d = x2d.shape
    k = w_all.shape[1]
    row = lambda w: pl.BlockSpec((tm, w), lambda i: (i, 0))
    head = lambda c: pl.BlockSpec((tm, HEAD_DIM), lambda i: (i, c))
    full = [x2d, gates, o_cmp, o_slc, o_win, o_b]
    return pl.pallas_call(
        _mix_outproj_kernel,
        grid=(m // tm,),
        in_specs=[row(a.shape[1]) for a in full] + [head(c) for _, c in o_cs + lses]
        + [pl.BlockSpec((None, k, d), lambda i: (layer, 0, 0), pipeline_mode=pl.Buffered(1))],
        out_specs=row(d),
        out_shape=jax.ShapeDtypeStruct((m, d), F32),
        scratch_shapes=[pltpu.VMEM((tm, k), BF16)],
        compiler_params=_cparams(("parallel",)),
        name="mix_out_proj",
    )(*full, *[a for a, _ in o_cs + lses], w_all)


def _mixers_outproj(x2d, proj3, gates, dil32, layer_params, tables, w_out_all, layer, *, tm):
    B, T, _ = proj3.shape
    cmp_pe, cmp_w1, cmp_w2, kc_gain, sinks = layer_params
    n_cmp = (T - CMP_BLOCK) // CMP_STRIDE + 1
    n_slc = T // SLC_BLOCK
    ncp = T // CMP_STRIDE

    def cmp_rows(col):
        a = proj3[:, :, col * LANES:(col + A_KV_HEADS) * LANES].reshape(B, ncp, CMP_STRIDE, A_KV_HEADS, HEAD_DIM)
        return a.transpose(0, 3, 1, 2, 4).reshape(B, A_KV_HEADS, ncp, CMP_STRIDE * HEAD_DIM)

    xr = jnp.stack([cmp_rows(COL_KCA), cmp_rows(COL_VCA)])
    pe = jnp.broadcast_to(cmp_pe.reshape(2, 1, CMP_BLOCK * HEAD_DIM), (2, 8, CMP_BLOCK * HEAD_DIM)).astype(BF16)
    kvc = _compress(xr, pe, cmp_w1.astype(BF16), cmp_w2.astype(BF16), kc_gain.reshape(1, HEAD_DIM), n_cmp)

    o_cmp, sel = _cmp_select(proj3, kvc, tables["bias_c"], tables["ovl"], n_slc)
    o_slc = _slc_attn(proj3, sel, tables["expand"], tables["bias_slc"])
    o_win = _band_attn(proj3, proj3, proj3, tables["bias_win"], n_kv=A_KV_HEADS, grp=A_GRP,
                       q_col=COL_QA // A_GRP, k_col=COL_KWA, v_col=COL_VWA)
    o_b = _band_attn(proj3, proj3, proj3, tables["bias_b"], n_kv=B_KV_HEADS, grp=B_GRP,
                     q_col=COL_QB // B_GRP, k_col=COL_KB, v_col=COL_VB, sinks=sinks)
    o_cs, lses = [], []
    for gidx, (_, dil) in enumerate(DIL_PAIRS):
        if dil == 1:
            o, lse = _band_attn(proj3, proj3, proj3, tables["bias_c%d" % gidx], n_kv=1, grp=C_GRP,
                                q_col=COL_QC // C_GRP + gidx, k_col=COL_KC + gidx, v_col=COL_VC + gidx,
                                with_lse=True)
            o_cs += [(o.reshape(B * T, -1), h) for h in range(C_GRP)]
            lses += [(lse.reshape(B * T, -1), h) for h in range(C_GRP)]
        else:
            os_, ls_ = _dil_attn(dil32, tables["bias_c%d" % gidx], col=DIL_F32_SLOT[COL_QC + C_GRP * gidx], dil=dil)
            o_cs += [(o.reshape(B * T, -1), 0) for o in os_]
            lses += [(lse.reshape(B * T, -1), 0) for lse in ls_]
    return _mix_outproj(x2d, gates, o_cmp.reshape(B * T, -1), o_slc.reshape(B * T, -1), o_win.reshape(B * T, -1),
                        o_b.reshape(B * T, -1), o_cs, lses, w_out_all, layer, tm=tm)


def _build_tables(rel_bias, T):
    n_cmp = (T - CMP_BLOCK) // CMP_STRIDE + 1
    n_slc = T // SLC_BLOCK
    ncp = T // CMP_STRIDE
    nq = T // QBLK
    tables = {}
    tables["bias_c"] = _bias_tiles(rel_bias, 0, A_Q_HEADS, nq, width=ncp, kstride=CMP_STRIDE,
                                   koff=CMP_BLOCK - 1, ncols=n_cmp).reshape(A_Q_HEADS, T, ncp)
    n_sat = -(-(SAT_DIST + QBLK - 1) // QBLK) + 1
    tables["bias_slc"] = _bias_tiles(rel_bias, 0, A_Q_HEADS, min(n_sat, nq) + 1, koff=QBLK)
    win = NSA_WINDOW - 1
    tables["bias_win"] = _bias_tiles(rel_bias, 0, A_Q_HEADS, -(-win // QBLK) + 2, max_dist=win)
    swa = SWA_WINDOW - 1
    tables["bias_b"] = _bias_tiles(rel_bias, A_Q_HEADS, B_Q_HEADS, -(-swa // QBLK) + 2, max_dist=swa)
    for gidx, (w, dil) in enumerate(DIL_PAIRS):
        md = w // dil
        tables["bias_c%d" % gidx] = _bias_tiles(rel_bias, A_Q_HEADS + B_Q_HEADS + C_GRP * gidx, C_GRP,
                                                -(-md // QBLK) + 2, dscale=dil, max_dist=md)
    c0 = np.arange(ncp)[:, None] * CMP_STRIDE
    s0 = np.arange(LANES)[None, :] * SLC_BLOCK
    ovl = np.clip(np.minimum(c0 + CMP_BLOCK, s0 + SLC_BLOCK) - np.maximum(c0, s0), 0, None) / CMP_BLOCK
    ovl = ovl * (np.arange(ncp)[:, None] < n_cmp) * (np.arange(LANES)[None, :] < n_slc)
    tables["ovl"] = jnp.asarray(ovl, BF16)
    expand = (np.arange(T)[None, :] // SLC_BLOCK) == np.arange(LANES)[:, None]
    tables["expand"] = jnp.asarray(expand, BF16)
    return tables


def _proj_gain(g):
    ones = jnp.ones((HEAD_DIM,), F32)
    spec = [(g[0] * (SCALE * LOG2E), 6), (ones, 4), (g[2], 2), (ones, 2), (g[3], 2), (ones, 2),
            (g[4] * (SCALE * LOG2E), 4), (g[5], 2), (ones, 2), (g[6] * (SCALE * LOG2E), 6), (g[7], 3), (ones, 3)]
    assert sum(n for _, n in spec) == N_MAIN_BLOCKS
    return jnp.concatenate([jnp.tile(v, n) for v, n in spec]).reshape(1, N_MAIN)


def kernel(x, norm_attn, w_in, qk_gain, cmp_pe, cmp_w1, cmp_w2, sinks, rel_bias, w_out, norm_ffn, w_gate, w_up, w_down):
    B, T, D = x.shape
    depth = w_in.shape[0]
    tables = _build_tables(rel_bias, T)
    x2 = x.reshape(B * T, D)
    w_all = _wprep(w_in, tr=256)
    w_out_b, w_gate_b, w_up_b, w_down_b = (w.astype(BF16) for w in (w_out, w_gate, w_up, w_down))
    tm = min(512, B * T)
    for l in range(depth):
        proj, gates, dil32 = _proj(x2, norm_attn[l].reshape(1, D), w_all, l, _proj_gain(qk_gain[l]), tm=tm)
        x2 = _mixers_outproj(x2, proj.reshape(B, T, N_MAIN), gates, dil32.reshape(B, T, DIL_F32_COLS),
                             (cmp_pe[l], cmp_w1[l], cmp_w2[l], qk_gain[l][1], sinks[l]), tables, w_out_b, l, tm=tm)
        x2 = _ffn(x2, norm_ffn[l].reshape(1, D), w_gate_b, w_up_b, w_down_b, l, tm=tm, tf=512)
    return x2.reshape(B, T, D)
```

```python
import functools
import math

import numpy as np
import jax
import jax.numpy as jnp
from jax import lax
from jax.experimental import pallas as pl
from jax.experimental.pallas import tpu as pltpu

F32 = jnp.float32
BF16 = jnp.bfloat16

HEAD_DIM = 128
LANES = 128
SUBLANES = 8
QBLK = 128
BAND_SUBBLOCKS = 8
CMP_SUBBLOCKS = 4
A_Q_HEADS, A_KV_HEADS = 6, 2
A_GRP = A_Q_HEADS // A_KV_HEADS
B_Q_HEADS, B_KV_HEADS = 4, 2
B_GRP = B_Q_HEADS // B_KV_HEADS
DIL_PAIRS = ((128, 1), (512, 4), (2048, 16))
C_GRP = 2
C_Q_HEADS = C_GRP * len(DIL_PAIRS)
CMP_BLOCK, CMP_STRIDE = 32, 16
SLC_BLOCK, SLC_TOPK = 64, 16
SLC_CHUNK = 4
SLC_QTILES = 1
NSA_WINDOW, SWA_WINDOW = 512, 128
FORCED_SCORE = 1.0e4
REL_BUCKETS, REL_MAX_EXACT, REL_MAX_DIST = 32, 16, 2048
SCALE = HEAD_DIM ** -0.5
LOG2E = math.log2(math.e)
LN2 = math.log(2.0)
EPS = 1e-6
NEG = -1e30
VMEM_LIMIT = 56 * 1024 * 1024

COL_QA, COL_KCA, COL_VCA, COL_KSA, COL_VSA, COL_KWA, COL_VWA = 0, 6, 8, 10, 12, 14, 16
COL_QB, COL_KB, COL_VB, COL_QC, COL_KC, COL_VC = 18, 22, 24, 26, 32, 35
N_MAIN_BLOCKS = 38
N_MAIN = N_MAIN_BLOCKS * LANES
GATE_START = 2304
N_GATES = A_Q_HEADS * 3
PROJ_NORMED = ([True] * 6 + [False] * 4 + [True] * 2 + [False] * 2 + [True] * 2 + [False] * 2
               + [True] * 6 + [False] * 2 + [True] * 9 + [False] * 3)
DIL_F32_SLOT = {}
for _g, (_, _d) in enumerate(DIL_PAIRS):
    if _d > 1:
        _base = 4 * len([1 for _, _dd in DIL_PAIRS[:_g] if _dd > 1])
        DIL_F32_SLOT.update({COL_QC + C_GRP * _g: _base, COL_QC + C_GRP * _g + 1: _base + 1,
                             COL_KC + _g: _base + 2, COL_VC + _g: _base + 3})
CMP_F32_SLOT0 = len(DIL_F32_SLOT)
DIL_F32_SLOT.update({COL_KCA + _i: CMP_F32_SLOT0 + _i for _i in range(2 * A_KV_HEADS)})
DIL_F32_COLS = LANES * len(DIL_F32_SLOT)
DIL_TILES_PER_STEP = 8
PROJ_SUBTILE = 256
MIX_ROW_SPLIT = 4
OUT_SUBTILE = 512


def _bucket_starts():
    d = np.arange(0, 1 << 17)
    out = []
    for dt in (np.float32, np.float64):
        far = np.maximum(d, REL_MAX_EXACT).astype(dt)
        lb = REL_MAX_EXACT + (np.log(far / dt(REL_MAX_EXACT)) / dt(math.log(REL_MAX_DIST / REL_MAX_EXACT))
                              * dt(REL_BUCKETS - REL_MAX_EXACT)).astype(np.int64)
        out.append(np.where(d < REL_MAX_EXACT, d, np.minimum(lb, REL_BUCKETS - 1)))
    assert (out[0] == out[1]).all() and (np.diff(out[0]) >= 0).all()
    return [int(np.argmax(out[0] >= b)) for b in range(REL_BUCKETS)]


BUCKET_START = _bucket_starts()
SAT_DIST = BUCKET_START[REL_BUCKETS - 1]


def _cparams(sem, vmem=VMEM_LIMIT):
    return pltpu.CompilerParams(dimension_semantics=sem, vmem_limit_bytes=vmem)


def _bias_tile_kernel(tab_ref, o_ref, *, head0, width, kstride, koff, dscale, max_dist, ncols):
    h = pl.program_id(0) + head0
    t = pl.program_id(1)
    r = lax.broadcasted_iota(jnp.int32, (QBLK, width), 0)
    c = lax.broadcasted_iota(jnp.int32, (QBLK, width), 1)
    dist = t * QBLK + r - kstride * c - koff
    d = dist * dscale
    val = jnp.full((QBLK, width), tab_ref[REL_BUCKETS - 1, h] * LOG2E, F32)
    for b in range(REL_BUCKETS - 2, -1, -1):
        val = jnp.where(d < BUCKET_START[b + 1], tab_ref[b, h] * LOG2E, val)
    val = jnp.where(dist >= 0, val, NEG)
    if max_dist is not None:
        val = jnp.where(dist <= max_dist, val, NEG)
    if ncols < width:
        val = jnp.where(c < ncols, val, NEG)
    o_ref[0, 0] = val


def _bias_tiles(rel_bias, head0, nheads, ntiles, *, width=LANES, kstride=1, koff=0, dscale=1,
                max_dist=None, ncols=None):
    ncols = width if ncols is None else ncols
    kern = functools.partial(_bias_tile_kernel, head0=head0, width=width, kstride=kstride, koff=koff,
                             dscale=dscale, max_dist=max_dist, ncols=ncols)
    return pl.pallas_call(
        kern,
        grid=(nheads, ntiles),
        in_specs=[pl.BlockSpec(memory_space=pltpu.SMEM)],
        out_specs=pl.BlockSpec((1, 1, QBLK, width), lambda h, t: (h, t, 0, 0)),
        out_shape=jax.ShapeDtypeStruct((nheads, ntiles, QBLK, width), F32),
        compiler_params=_cparams(("parallel", "parallel")),
        name="bias_tiles",
    )(rel_bias)


def _rms(x, w):
    ms = jnp.mean(x * x, axis=-1, keepdims=True)
    return x * lax.rsqrt(ms + EPS) * w


def _wprep_kernel(w_ref, o_ref):
    o_ref[0, :, :GATE_START] = w_ref[0, :, :GATE_START].astype(BF16)
    o_ref[0, :, GATE_START:N_MAIN] = w_ref[0, :, GATE_START + N_GATES:].astype(BF16)
    o_ref[0, :, N_MAIN:] = w_ref[0, :, GATE_START:GATE_START + LANES].astype(BF16)


def _wprep(w_in, *, tr):
    depth, d, n_in = w_in.shape
    return pl.pallas_call(
        _wprep_kernel,
        grid=(depth, d // tr),
        in_specs=[pl.BlockSpec((1, tr, n_in), lambda l, i: (l, i, 0))],
        out_specs=pl.BlockSpec((1, tr, N_MAIN + LANES), lambda l, i: (l, i, 0)),
        out_shape=jax.ShapeDtypeStruct((depth, d, N_MAIN + LANES), BF16),
        compiler_params=_cparams(("parallel", "parallel")),
        name="w_in_prep",
    )(w_in)


def _proj_kernel(x_ref, nw_ref, w_ref, gain_ref, o_ref, gate_ref, dil_ref):
    h = _rms(x_ref[...], nw_ref[...]).astype(BF16)
    for c0 in range(0, N_MAIN, PROJ_SUBTILE):
        width = min(PROJ_SUBTILE, N_MAIN - c0)
        acc = jnp.dot(h, w_ref[:, c0:c0 + width], preferred_element_type=F32)
        for c in range(width // LANES):
            blk = c0 // LANES + c
            sl = slice(blk * LANES, (blk + 1) * LANES)
            y = acc[:, c * LANES:(c + 1) * LANES]
            if PROJ_NORMED[blk]:
                ms = jnp.mean(y * y, axis=-1, keepdims=True)
                y = y * lax.rsqrt(ms + EPS) * gain_ref[:, sl]
            o_ref[:, sl] = y.astype(o_ref.dtype)
            if blk in DIL_F32_SLOT:
                dil_ref[:, DIL_F32_SLOT[blk] * LANES:(DIL_F32_SLOT[blk] + 1) * LANES] = y
    gate_ref[...] = jnp.dot(h, w_ref[:, N_MAIN:], preferred_element_type=F32)


def _proj(x2d, norm_w, w_all, layer, gain, *, tm):
    m, d = x2d.shape
    n = w_all.shape[2]
    return pl.pallas_call(
        _proj_kernel,
        grid=(m // tm,),
        in_specs=[
            pl.BlockSpec((tm, d), lambda i: (i, 0)),
            pl.BlockSpec((1, d), lambda i: (0, 0)),
            pl.BlockSpec((None, d, n), lambda i: (layer, 0, 0), pipeline_mode=pl.Buffered(1)),
            pl.BlockSpec((1, N_MAIN), lambda i: (0, 0)),
        ],
        out_specs=(pl.BlockSpec((tm, N_MAIN), lambda i: (i, 0)), pl.BlockSpec((tm, LANES), lambda i: (i, 0)),
                   pl.BlockSpec((tm, DIL_F32_COLS), lambda i: (i, 0))),
        out_shape=(jax.ShapeDtypeStruct((m, N_MAIN), BF16), jax.ShapeDtypeStruct((m, LANES), F32),
                   jax.ShapeDtypeStruct((m, DIL_F32_COLS), F32)),
        compiler_params=_cparams(("parallel",)),
        name="in_proj",
    )(x2d, norm_w, w_all, gain)


def _ffn_kernel(x_ref, nw_ref, wg_ref, wu_ref, wd_ref, o_ref, h_ref, acc_ref):
    f = pl.program_id(1)

    @pl.when(f == 0)
    def _():
        h_ref[...] = _rms(x_ref[...], nw_ref[...]).astype(BF16)
        acc_ref[...] = jnp.zeros_like(acc_ref)

    h = h_ref[...]
    g = jnp.dot(h, wg_ref[...], preferred_element_type=F32)
    u = jnp.dot(h, wu_ref[...], preferred_element_type=F32)
    a = (g * (1.0 / (1.0 + jnp.exp(-g))) * u).astype(BF16)
    acc_ref[...] += jnp.dot(a, wd_ref[...], preferred_element_type=F32)

    @pl.when(f == pl.num_programs(1) - 1)
    def _():
        o_ref[...] = x_ref[...] + acc_ref[...]


def _ffn(x2d, norm_w, wg, wu, wd, layer, *, tm, tf):
    m, d = x2d.shape
    dff = wg.shape[2]
    return pl.pallas_call(
        _ffn_kernel,
        grid=(m // tm, dff // tf),
        in_specs=[
            pl.BlockSpec((tm, d), lambda i, f: (i, 0)),
            pl.BlockSpec((1, d), lambda i, f: (0, 0)),
            pl.BlockSpec((None, d, tf), lambda i, f: (layer, 0, f)),
            pl.BlockSpec((None, d, tf), lambda i, f: (layer, 0, f)),
            pl.BlockSpec((None, tf, d), lambda i, f: (layer, f, 0)),
        ],
        out_specs=pl.BlockSpec((tm, d), lambda i, f: (i, 0)),
        out_shape=jax.ShapeDtypeStruct((m, d), F32),
        scratch_shapes=[pltpu.VMEM((tm, d), BF16), pltpu.VMEM((tm, d), F32)],
        compiler_params=_cparams(("parallel", "arbitrary")),
        name="ffn",
    )(x2d, norm_w, wg, wu, wd)


def _band_kernel(*refs, grp, n_off, nsub, has_sinks, with_lse):
    if has_sinks:
        sink_ref, q_ref, k_ref, v_ref, b_ref = refs[:5]
        outs = refs[5:]
    else:
        q_ref, k_ref, v_ref, b_ref = refs[:4]
        outs = refs[4:]
    o_ref = outs[0]
    g = pl.program_id(1)
    i0 = pl.program_id(2) * nsub
    tiles = {}

    def kv_tile(rel):
        if rel not in tiles:
            start = pl.multiple_of(jnp.maximum(i0 + rel, 0) * QBLK, QBLK)
            tiles[rel] = (k_ref[0, pl.ds(start, QBLK), :], v_ref[0, pl.ds(start, QBLK), :])
        return tiles[rel]

    def head_rows(x, h):
        return x[h * QBLK:(h + 1) * QBLK]

    kvs = [[kv_tile(j - off) for off in range(n_off)] for j in range(nsub)]
    tidxs = [[jnp.where(i0 + j - off >= 0, off, n_off) for off in range(n_off)] for j in range(nsub)]
    scores = []
    for j in range(nsub):
        rs = slice(j * QBLK, (j + 1) * QBLK)
        kcat = jnp.concatenate([t[0] for t in kvs[j]], axis=0)
        q = jnp.concatenate([q_ref[0, rs, h * HEAD_DIM:(h + 1) * HEAD_DIM] for h in range(grp)], axis=0)
        scores.append(lax.dot_general(q, kcat, (((1,), (1,)), ((), ())), preferred_element_type=F32))
    probs, stats = [], []
    for j in range(nsub):
        s = scores[j]
        st = [[head_rows(s, h)[:, off * QBLK:(off + 1) * QBLK] + b_ref[h, tidxs[j][off]] for off in range(n_off)]
              for h in range(grp)]
        m = jnp.max(jnp.concatenate([functools.reduce(jnp.maximum, r) for r in st], axis=0), axis=-1, keepdims=True)
        p = [[jnp.exp2(st[h][off] - head_rows(m, h)) for off in range(n_off)] for h in range(grp)]
        l = jnp.sum(jnp.concatenate([functools.reduce(jnp.add, r) for r in p], axis=0), axis=-1, keepdims=True)
        probs.append(jnp.concatenate([jnp.concatenate([t.astype(BF16) for t in r], axis=1) for r in p], axis=0))
        stats.append((m, l))
    for j in range(nsub):
        rs = slice(j * QBLK, (j + 1) * QBLK)
        m, l = stats[j]
        vcat = jnp.concatenate([t[1] for t in kvs[j]], axis=0)
        o = jnp.dot(probs[j], vcat, preferred_element_type=F32)
        den = l
        if has_sinks:
            sink = jnp.concatenate([jnp.full((QBLK, 1), sink_ref[g * grp + h] * LOG2E, F32) for h in range(grp)], axis=0)
            den = l + jnp.exp2(sink - m)
        o = o / den
        for h in range(grp):
            sl = slice(h * HEAD_DIM, (h + 1) * HEAD_DIM)
            o_ref[0, rs, sl] = head_rows(o, h).astype(o_ref.dtype)
            if with_lse:
                outs[1][0, rs, sl] = jnp.broadcast_to(head_rows(m * LN2 + jnp.log(l), h), (QBLK, HEAD_DIM))


def _band_attn(q_arr, k_arr, v_arr, bias, *, n_kv, grp, q_col, k_col, v_col, sinks=None, with_lse=False):
    n, L, _ = q_arr.shape
    n_off = bias.shape[1] - 1
    gw = grp * HEAD_DIM
    nsub = math.gcd(BAND_SUBBLOCKS, L // QBLK)
    rows = nsub * QBLK
    kern = functools.partial(_band_kernel, grp=grp, n_off=n_off, nsub=nsub, has_sinks=sinks is not None,
                             with_lse=with_lse)
    in_specs = [
        pl.BlockSpec((1, rows, gw), lambda b, g, i: (b, i, q_col + g)),
        pl.BlockSpec((1, L, HEAD_DIM), lambda b, g, i: (b, 0, k_col + g)),
        pl.BlockSpec((1, L, HEAD_DIM), lambda b, g, i: (b, 0, v_col + g)),
        pl.BlockSpec((grp, n_off + 1, QBLK, LANES), lambda b, g, i: (g, 0, 0, 0)),
    ]
    args = [q_arr, k_arr, v_arr, bias]
    if sinks is not None:
        in_specs = [pl.BlockSpec(memory_space=pltpu.SMEM)] + in_specs
        args = [sinks] + args
    o_spec = pl.BlockSpec((1, rows, gw), lambda b, g, i: (b, i, g))
    o_shape = jax.ShapeDtypeStruct((n, L, n_kv * gw), F32)
    return pl.pallas_call(
        kern,
        grid=(n, n_kv, L // rows),
        in_specs=in_specs,
        out_specs=(o_spec, o_spec) if with_lse else o_spec,
        out_shape=(o_shape, o_shape) if with_lse else o_shape,
        compiler_params=_cparams(("parallel", "parallel", "arbitrary")),
        name="band_attn",
    )(*args)


def _dil_kernel(*refs, dil, nres):
    q_refs, (k_ref, v_ref, b_ref) = refs[:C_GRP], refs[C_GRP:C_GRP + 3]
    o_refs, lse_refs = refs[C_GRP + 3:2 * C_GRP + 3], refs[2 * C_GRP + 3:]
    ls = k_ref.shape[1] // dil
    ntile = ls // QBLK
    r0 = pl.program_id(1) * nres
    units = [(rr, j) for rr in range(nres) for j in range(ntile)]

    def rows(rr, j):
        return pl.ds(r0 + rr + dil * QBLK * j, QBLK, stride=dil)

    kv = {u: (k_ref[0, rows(*u), :].astype(BF16), v_ref[0, rows(*u), :].astype(BF16)) for u in units}
    n_off = b_ref.shape[1] - 1
    offs = {(rr, j): [off for off in range(n_off) if j - off >= 0] for rr, j in units}
    scores = []
    for rr, j in units:
        kcat = jnp.concatenate([kv[(rr, j - off)][0] for off in offs[(rr, j)]], axis=0)
        q = jnp.concatenate([q_ref[0, rows(rr, j), :].astype(BF16) for q_ref in q_refs], axis=0)
        scores.append(lax.dot_general(q, kcat, (((1,), (1,)), ((), ())), preferred_element_type=F32))
    probs, stats = [], []
    for u, s in zip(units, scores):
        st = [[s[h * QBLK:(h + 1) * QBLK, n * QBLK:(n + 1) * QBLK] + b_ref[h, off] for n, off in enumerate(offs[u])]
              for h in range(C_GRP)]
        m = jnp.max(jnp.concatenate([functools.reduce(jnp.maximum, r) for r in st], axis=0), axis=-1, keepdims=True)
        p = [[jnp.exp2(t - m[h * QBLK:(h + 1) * QBLK]) for t in r] for h, r in enumerate(st)]
        l = jnp.sum(jnp.concatenate([functools.reduce(jnp.add, r) for r in p], axis=0), axis=-1, keepdims=True)
        probs.append(jnp.concatenate([jnp.concatenate([t.astype(BF16) for t in r], axis=1) for r in p], axis=0))
        stats.append((m, l))
    for (rr, j), pb, (m, l) in zip(units, probs, stats):
        vcat = jnp.concatenate([kv[(rr, j - off)][1] for off in offs[(rr, j)]], axis=0)
        o = jnp.dot(pb, vcat, preferred_element_type=F32) / l
        lse = m * LN2 + jnp.log(l)
        for h in range(C_GRP):
            o_refs[h][0, rows(rr, j), :] = o[h * QBLK:(h + 1) * QBLK]
            lse_refs[h][0, rows(rr, j), :] = jnp.broadcast_to(lse[h * QBLK:(h + 1) * QBLK], (QBLK, HEAD_DIM))


def _dil_attn(x3, bias, *, col, dil):
    B, T, _ = x3.shape
    ntile = T // dil // QBLK
    nres = max(1, min(dil, DIL_TILES_PER_STEP // ntile))
    spec = pl.BlockSpec((1, T, HEAD_DIM), lambda b, r: (b, 0, 0))
    shape = jax.ShapeDtypeStruct((B, T, HEAD_DIM), F32)
    outs = pl.pallas_call(
        functools.partial(_dil_kernel, dil=dil, nres=nres),
        grid=(B, dil // nres),
        in_specs=[pl.BlockSpec((1, T, HEAD_DIM), functools.partial(lambda b, r, c: (b, 0, c), c=col + c))
                  for c in range(C_GRP + 2)]
        + [pl.BlockSpec((C_GRP, bias.shape[1], QBLK, LANES), lambda b, r: (0, 0, 0, 0))],
        out_specs=(spec,) * (2 * C_GRP),
        out_shape=(shape,) * (2 * C_GRP),
        compiler_params=_cparams(("parallel", "arbitrary")),
        name="dilated_attn",
    )(*([x3] * (C_GRP + 2)), bias)
    return outs[:C_GRP], outs[C_GRP:]


def _compress_kernel(x_ref, pe_ref, w1_ref, w2_ref, gain_ref, o_ref, *, n_cmp):
    half = w1_ref.shape[1] // 2
    rows = x_ref.shape[1] // CMP_STRIDE
    xr = jnp.concatenate([x_ref[0, pl.ds(t, rows, stride=CMP_STRIDE), :].astype(BF16) for t in range(CMP_STRIDE)],
                         axis=1)
    y0 = jnp.dot(xr, w1_ref[0, :half], preferred_element_type=F32)
    y1 = jnp.dot(xr, w1_ref[0, half:], preferred_element_type=F32)
    pe = jnp.dot(pe_ref[0], w1_ref[0], preferred_element_type=F32)[0:1]
    c = y0 + pltpu.roll(y1, rows - 1, 0) + pe
    gl = 0.5 * c * (1.0 + jnp.tanh(math.sqrt(2.0 / math.pi) * (c + 0.044715 * (c * c * c))))
    out = jnp.dot(gl.astype(BF16), w2_ref[0], preferred_element_type=F32)
    out = jnp.where(pl.program_id(0) == 0, _rms(out, gain_ref[...]), out)
    valid = lax.broadcasted_iota(jnp.int32, out.shape, 0) < n_cmp
    o_ref[0, 0, 0] = jnp.where(valid, out, 0.0).astype(o_ref.dtype)


def _compress(x3, pe, w1, w2, gain, n_cmp):
    B, T, _ = x3.shape
    rows = T // CMP_STRIDE
    wide = CMP_BLOCK * HEAD_DIM
    return pl.pallas_call(
        functools.partial(_compress_kernel, n_cmp=n_cmp),
        grid=(2, B, A_KV_HEADS),
        in_specs=[
            pl.BlockSpec((1, T, HEAD_DIM), lambda s, b, g: (b, 0, CMP_F32_SLOT0 + s * A_KV_HEADS + g)),
            pl.BlockSpec((1, 8, wide), lambda s, b, g: (s, 0, 0)),
            pl.BlockSpec((1, wide, HEAD_DIM), lambda s, b, g: (s, 0, 0)),
            pl.BlockSpec((1, HEAD_DIM, HEAD_DIM), lambda s, b, g: (s, 0, 0)),
            pl.BlockSpec((1, HEAD_DIM), lambda s, b, g: (0, 0)),
        ],
        out_specs=pl.BlockSpec((1, 1, 1, rows, HEAD_DIM), lambda s, b, g: (s, b, g, 0, 0)),
        out_shape=jax.ShapeDtypeStruct((2, B, A_KV_HEADS, rows, HEAD_DIM), BF16),
        compiler_params=_cparams(("arbitrary", "arbitrary", "arbitrary")),
        name="nsa_compress",
    )(x3, pe, w1, w2, gain)


def _cmp_select_kernel(q_ref, kc_ref, vc_ref, b_ref, ovl_ref, o_ref, sel_ref, *, n_slc, nsub):
    i0 = pl.program_id(2) * nsub
    rows = nsub * QBLK
    kc = kc_ref[0, 0, 0]
    vc = vc_ref[0, 0, 0]
    q_all = jnp.concatenate([q_ref[0, :, h * HEAD_DIM:(h + 1) * HEAD_DIM] for h in range(A_GRP)], axis=0)
    s_all = lax.dot_general(q_all, kc, (((1,), (1,)), ((), ())), preferred_element_type=F32)
    ps = []
    for h in range(A_GRP):
        s = s_all[h * rows:(h + 1) * rows] + b_ref[h]
        m = jnp.max(s, axis=-1, keepdims=True)
        e = jnp.exp2(s - jnp.where(m > 0.5 * NEG, m, 0.0))
        den = jnp.sum(e, axis=-1, keepdims=True)
        ps.append((e / jnp.where(den > 0, den, 1.0)).astype(BF16))
    r_all = jnp.dot(jnp.concatenate(ps, axis=0), jnp.concatenate([vc, ovl_ref[...]], axis=1),
                    preferred_element_type=F32)
    imp = jnp.zeros((rows, LANES), F32)
    for h in range(A_GRP):
        o_ref[0, :, h * HEAD_DIM:(h + 1) * HEAD_DIM] = r_all[h * rows:(h + 1) * rows, :HEAD_DIM]
        imp = imp + r_all[h * rows:(h + 1) * rows, HEAD_DIM:]
    t = i0 * QBLK + lax.broadcasted_iota(jnp.int32, (rows, LANES), 0)
    blk = lax.broadcasted_iota(jnp.int32, (rows, LANES), 1)
    cur = t // SLC_BLOCK
    imp = jnp.where(blk == 0, FORCED_SCORE, imp)
    imp = jnp.where(blk == cur, FORCED_SCORE, imp)
    imp = jnp.where(blk == cur - 1, FORCED_SCORE, imp)
    imp = jnp.where(blk * SLC_BLOCK <= t, imp, NEG)
    imp = jnp.where(blk < n_slc, imp, 2.0 * NEG)
    imp_t = jnp.concatenate([imp[j * QBLK:(j + 1) * QBLK].T for j in range(nsub)], axis=1)
    ngrp = -(-n_slc // SUBLANES)
    cands = [imp_t[r * SUBLANES:(r + 1) * SUBLANES] for r in range(ngrp)]
    row_id = lax.broadcasted_iota(jnp.int32, (SUBLANES, rows), 0)
    ranks = [jnp.zeros((SUBLANES, rows), F32) for _ in range(ngrp)]
    for j in range(n_slc):
        other = imp_t[j:j + 1, :]
        for r in range(ngrp):
            if r * SUBLANES > j:
                beats = other >= cands[r]
            elif (r + 1) * SUBLANES <= j:
                beats = other > cands[r]
            else:
                beats = jnp.where(row_id > j - r * SUBLANES, jnp.where(other >= cands[r], 1.0, 0.0),
                                  jnp.where(other > cands[r], 1.0, 0.0)) > 0.5
            ranks[r] = ranks[r] + jnp.where(beats, 1.0, 0.0)
    sel_t = jnp.concatenate([jnp.where(r < float(min(SLC_TOPK, n_slc)), 1.0, 0.0) for r in ranks], axis=0)
    if ngrp * SUBLANES < LANES:
        sel_t = jnp.concatenate([sel_t, jnp.zeros((LANES - ngrp * SUBLANES, rows), F32)], axis=0)
    for j in range(nsub):
        sel_ref[0, 0, j * QBLK:(j + 1) * QBLK, :] = sel_t[:, j * QBLK:(j + 1) * QBLK].T.astype(sel_ref.dtype)


def _cmp_select(proj3, kvc, bias_c, ovl, n_slc):
    B, T, _ = proj3.shape
    ncp = kvc.shape[3]
    nsub = math.gcd(CMP_SUBBLOCKS, T // QBLK)
    rows = nsub * QBLK
    return pl.pallas_call(
        functools.partial(_cmp_select_kernel, n_slc=n_slc, nsub=nsub),
        grid=(B, A_KV_HEADS, T // rows),
        in_specs=[
            pl.BlockSpec((1, rows, A_GRP * HEAD_DIM), lambda b, g, i: (b, i, g)),
            pl.BlockSpec((1, 1, 1, ncp, HEAD_DIM), lambda b, g, i: (0, b, g, 0, 0)),
            pl.BlockSpec((1, 1, 1, ncp, HEAD_DIM), lambda b, g, i: (1, b, g, 0, 0)),
            pl.BlockSpec((A_GRP, rows, ncp), lambda b, g, i: (g, i, 0)),
            pl.BlockSpec((ncp, LANES), lambda b, g, i: (0, 0)),
        ],
        out_specs=(
            pl.BlockSpec((1, rows, A_GRP * HEAD_DIM), lambda b, g, i: (b, i, g)),
            pl.BlockSpec((1, 1, rows, LANES), lambda b, g, i: (b, g, i, 0)),
        ),
        out_shape=(
            jax.ShapeDtypeStruct((B, T, A_Q_HEADS * HEAD_DIM), F32),
            jax.ShapeDtypeStruct((B, A_KV_HEADS, T, LANES), BF16),
        ),
        compiler_params=_cparams(("parallel", "parallel", "arbitrary")),
        name="nsa_cmp_select",
    )(proj3, kvc, kvc, bias_c, ovl)


def _slc_kernel(q_ref, k_ref, v_ref, sel_ref, e_ref, b_ref, o_ref, *scratch, n_tiles, nq):
    ig = pl.program_id(1)
    chains = [(g, a) for g in range(A_KV_HEADS) for a in range(nq)]
    m_refs, l_refs, acc_refs = scratch[0::3], scratch[1::3], scratch[2::3]
    qs, sels = [], []
    for ci, (g, a) in enumerate(chains):
        m_refs[ci][...] = jnp.full(m_refs[ci].shape, NEG, F32)
        l_refs[ci][...] = jnp.zeros(l_refs[ci].shape, F32)
        acc_refs[ci][...] = jnp.zeros(acc_refs[ci].shape, F32)
        rs = slice(a * QBLK, (a + 1) * QBLK)
        qs.append(jnp.concatenate([q_ref[0, rs, (g * A_GRP + h) * HEAD_DIM:(g * A_GRP + h + 1) * HEAD_DIM]
                                   for h in range(A_GRP)], axis=0))
        sels.append(sel_ref[0, g, rs, :])
    kw = SLC_CHUNK * QBLK

    def body(c, carry):
        start = pl.multiple_of(c * kw, kw)
        base = ig * nq - c * SLC_CHUNK + 1
        tidx = {d: jnp.clip(base + d, 0, n_tiles - 1) for d in range(-(SLC_CHUNK - 1), nq)}
        kts = [k_ref[0, pl.ds(start, kw), g * HEAD_DIM:(g + 1) * HEAD_DIM] for g in range(A_KV_HEADS)]
        vts = [v_ref[0, pl.ds(start, kw), g * HEAD_DIM:(g + 1) * HEAD_DIM] for g in range(A_KV_HEADS)]
        ech = e_ref[:, pl.ds(start, kw)]
        ss = [lax.dot_general(qs[ci], kts[g], (((1,), (1,)), ((), ())), preferred_element_type=F32)
              for ci, (g, a) in enumerate(chains)]
        chosens = [jnp.dot(sels[ci], ech, preferred_element_type=F32) for ci in range(len(chains))]
        pbs, alphas = [], []
        for ci, (g, a) in enumerate(chains):
            madd = (chosens[ci] - 1.0) * (-NEG)
            s = ss[ci]
            rows = [[s[h * QBLK:(h + 1) * QBLK, j * QBLK:(j + 1) * QBLK] + b_ref[g * A_GRP + h, tidx[a - j]]
                     + madd[:, j * QBLK:(j + 1) * QBLK] for j in range(SLC_CHUNK)] for h in range(A_GRP)]
            tile_max = jnp.concatenate([functools.reduce(jnp.maximum, r) for r in rows], axis=0)
            m_old = m_refs[ci][...]
            m_new = jnp.maximum(m_old, jnp.max(tile_max, axis=-1, keepdims=True))
            alpha = jnp.exp2(m_old - m_new)
            p = [[jnp.exp2(t - m_new[h * QBLK:(h + 1) * QBLK]) for t in r] for h, r in enumerate(rows)]
            tile_sum = jnp.concatenate([functools.reduce(jnp.add, r) for r in p], axis=0)
            l_refs[ci][...] = alpha * l_refs[ci][...] + jnp.sum(tile_sum, axis=-1, keepdims=True)
            pbs.append(jnp.concatenate([jnp.concatenate([t.astype(BF16) for t in r], axis=1) for r in p], axis=0))
            alphas.append(alpha)
            m_refs[ci][...] = m_new
        for ci, (g, a) in enumerate(chains):
            acc_refs[ci][...] = alphas[ci] * acc_refs[ci][...] + jnp.dot(pbs[ci], vts[g], preferred_element_type=F32)
        return carry

    lax.fori_loop(0, (ig * nq + nq - 1) // SLC_CHUNK + 1, body, 0)
    for ci, (g, a) in enumerate(chains):
        out = acc_refs[ci][...] / l_refs[ci][...]
        for h in range(A_GRP):
            col = (g * A_GRP + h) * HEAD_DIM
            o_ref[0, a * QBLK:(a + 1) * QBLK, col:col + HEAD_DIM] = out[h * QBLK:(h + 1) * QBLK]


def _slc_attn(proj3, sel, expand, bias):
    B, T, _ = proj3.shape
    n_tiles = bias.shape[1]
    nq = SLC_QTILES
    qrows = nq * QBLK
    assert SLC_CHUNK % nq == 0 and T % (SLC_CHUNK * QBLK) == 0
    chain_rows = A_GRP * QBLK
    n_chains = A_KV_HEADS * nq
    return pl.pallas_call(
        functools.partial(_slc_kernel, n_tiles=n_tiles, nq=nq),
        grid=(B, T // qrows),
        in_specs=[
            pl.BlockSpec((1, qrows, A_Q_HEADS * HEAD_DIM), lambda b, i: (b, i, COL_QA // A_Q_HEADS)),
            pl.BlockSpec((1, T, A_KV_HEADS * HEAD_DIM), lambda b, i: (b, 0, COL_KSA // A_KV_HEADS)),
            pl.BlockSpec((1, T, A_KV_HEADS * HEAD_DIM), lambda b, i: (b, 0, COL_VSA // A_KV_HEADS)),
            pl.BlockSpec((1, A_KV_HEADS, qrows, LANES), lambda b, i: (b, 0, i, 0)),
            pl.BlockSpec((LANES, T), lambda b, i: (0, 0)),
            pl.BlockSpec((A_Q_HEADS, n_tiles, QBLK, LANES), lambda b, i: (0, 0, 0, 0)),
        ],
        out_specs=pl.BlockSpec((1, qrows, A_Q_HEADS * HEAD_DIM), lambda b, i: (b, i, 0)),
        out_shape=jax.ShapeDtypeStruct((B, T, A_Q_HEADS * HEAD_DIM), F32),
        scratch_shapes=[pltpu.VMEM((chain_rows, 1), F32), pltpu.VMEM((chain_rows, 1), F32),
                        pltpu.VMEM((chain_rows, HEAD_DIM), F32)] * n_chains,
        compiler_params=_cparams(("parallel", "arbitrary")),
        name="nsa_slc_attn",
    )(proj3, proj3, proj3, sel, expand, bias)


def _mix_outproj_kernel(x_ref, gate_ref, ocmp_ref, oslc_ref, owin_ref, ob_ref, *rest):
    oc_refs, lse_refs = rest[:C_Q_HEADS], rest[C_Q_HEADS:2 * C_Q_HEADS]
    w_ref, o_ref, mix_ref = rest[2 * C_Q_HEADS:]
    tm = x_ref.shape[0]
    nsplit = MIX_ROW_SPLIT
    rows = tm // nsplit
    for part in range(nsplit):
        rs = slice(part * rows, (part + 1) * rows)
        gate = 1.0 / (1.0 + jnp.exp(-gate_ref[rs, :]))
        for h in range(A_Q_HEADS):
            sl = slice(h * HEAD_DIM, (h + 1) * HEAD_DIM)
            o = (gate[:, 3 * h:3 * h + 1] * ocmp_ref[rs, sl] + gate[:, 3 * h + 1:3 * h + 2] * oslc_ref[rs, sl]
                 + gate[:, 3 * h + 2:3 * h + 3] * owin_ref[rs, sl])
            mix_ref[rs, sl] = o.astype(mix_ref.dtype)
        base = A_Q_HEADS * HEAD_DIM
        width = B_Q_HEADS * HEAD_DIM
        mix_ref[rs, base:base + width] = ob_ref[rs, :].astype(mix_ref.dtype)
        base += width
        for hh in range(C_GRP):
            heads = [C_GRP * gidx + hh for gidx in range(len(DIL_PAIRS))]
            lses = [lse_refs[c][rs, :] for c in heads]
            mx = functools.reduce(jnp.maximum, lses)
            ws = [jnp.exp(x - mx) for x in lses]
            tot = functools.reduce(jnp.add, ws)
            for c, w in zip(heads, ws):
                mix_ref[rs, base + c * HEAD_DIM:base + (c + 1) * HEAD_DIM] = (oc_refs[c][rs, :] * (w / tot)).astype(mix_ref.dtype)
    for part in range(nsplit):
        rs = slice(part * rows, (part + 1) * rows)
        a = mix_ref[rs, :]
        for c0 in range(0, o_ref.shape[1], OUT_SUBTILE):
            sl = slice(c0, c0 + OUT_SUBTILE)
            o_ref[rs, sl] = x_ref[rs, sl] + jnp.dot(a, w_ref[:, sl], preferred_element_type=F32)


def _mix_outproj(x2d, gates, o_cmp, o_slc, o_win, o_b, o_cs, lses, w_all, layer, *, tm):
    m, d = x2d.shape
    k = w_all.shape[1]
    row = lambda w: pl.BlockSpec((tm, w), lambda i: (i, 0))
    head = lambda c: pl.BlockSpec((tm, HEAD_DIM), lambda i: (i, c))
    full = [x2d, gates, o_cmp, o_slc, o_win, o_b]
    return pl.pallas_call(
        _mix_outproj_kernel,
        grid=(m // tm,),
        in_specs=[row(a.shape[1]) for a in full] + [head(c) for _, c in o_cs + lses]
        + [pl.BlockSpec((None, k, d), lambda i: (layer, 0, 0), pipeline_mode=pl.Buffered(1))],
        out_specs=row(d),
        out_shape=jax.ShapeDtypeStruct((m, d), F32),
        scratch_shapes=[pltpu.VMEM((tm, k), BF16)],
        compiler_params=_cparams(("parallel",)),
        name="mix_out_proj",
    )(*full, *[a for a, _ in o_cs + lses], w_all)


def _mixers_outproj(x2d, proj3, gates, dil32, layer_params, tables, w_out_all, layer, *, tm):
    B, T, _ = proj3.shape
    cmp_pe, cmp_w1, cmp_w2, kc_gain, sinks = layer_params
    n_cmp = (T - CMP_BLOCK) // CMP_STRIDE + 1
    n_slc = T // SLC_BLOCK
    ncp = T // CMP_STRIDE

    pe = jnp.broadcast_to(cmp_pe.reshape(2, 1, CMP_BLOCK * HEAD_DIM), (2, 8, CMP_BLOCK * HEAD_DIM)).astype(BF16)
    kvc = _compress(dil32, pe, cmp_w1.astype(BF16), cmp_w2.astype(BF16), kc_gain.reshape(1, HEAD_DIM), n_cmp)

    o_cmp, sel = _cmp_select(proj3, kvc, tables["bias_c"], tables["ovl"], n_slc)
    o_slc = _slc_attn(proj3, sel, tables["expand"], tables["bias_slc"])
    o_win = _band_attn(proj3, proj3, proj3, tables["bias_win"], n_kv=A_KV_HEADS, grp=A_GRP,
                       q_col=COL_QA // A_GRP, k_col=COL_KWA, v_col=COL_VWA)
    o_b = _band_attn(proj3, proj3, proj3, tables["bias_b"], n_kv=B_KV_HEADS, grp=B_GRP,
                     q_col=COL_QB // B_GRP, k_col=COL_KB, v_col=COL_VB, sinks=sinks)
    o_cs, lses = [], []
    for gidx, (_, dil) in enumerate(DIL_PAIRS):
        if dil == 1:
            o, lse = _band_attn(proj3, proj3, proj3, tables["bias_c%d" % gidx], n_kv=1, grp=C_GRP,
                                q_col=COL_QC // C_GRP + gidx, k_col=COL_KC + gidx, v_col=COL_VC + gidx,
                                with_lse=True)
            o_cs += [(o.reshape(B * T, -1), h) for h in range(C_GRP)]
            lses += [(lse.reshape(B * T, -1), h) for h in range(C_GRP)]
        else:
            os_, ls_ = _dil_attn(dil32, tables["bias_c%d" % gidx], col=DIL_F32_SLOT[COL_QC + C_GRP * gidx], dil=dil)
            o_cs += [(o.reshape(B * T, -1), 0) for o in os_]
            lses += [(lse.reshape(B * T, -1), 0) for lse in ls_]
    return _mix_outproj(x2d, gates, o_cmp.reshape(B * T, -1), o_slc.reshape(B * T, -1), o_win.reshape(B * T, -1),
                        o_b.reshape(B * T, -1), o_cs, lses, w_out_all, layer, tm=tm)


def _build_tables(rel_bias, T):
    n_cmp = (T - CMP_BLOCK) // CMP_STRIDE + 1
    n_slc = T // SLC_BLOCK
    ncp = T // CMP_STRIDE
    nq = T // QBLK
    tables = {}
    tables["bias_c"] = _bias_tiles(rel_bias, 0, A_Q_HEADS, nq, width=ncp, kstride=CMP_STRIDE,
                                   koff=CMP_BLOCK - 1, ncols=n_cmp).reshape(A_Q_HEADS, T, ncp)
    n_sat = -(-(SAT_DIST + QBLK - 1) // QBLK) + 1
    tables["bias_slc"] = _bias_tiles(rel_bias, 0, A_Q_HEADS, min(n_sat, nq) + 1, koff=QBLK)
    win = NSA_WINDOW - 1
    tables["bias_win"] = _bias_tiles(rel_bias, 0, A_Q_HEADS, -(-win // QBLK) + 2, max_dist=win)
    swa = SWA_WINDOW - 1
    tables["bias_b"] = _bias_tiles(rel_bias, A_Q_HEADS, B_Q_HEADS, -(-swa // QBLK) + 2, max_dist=swa)
    for gidx, (w, dil) in enumerate(DIL_PAIRS):
        md = w // dil
        tables["bias_c%d" % gidx] = _bias_tiles(rel_bias, A_Q_HEADS + B_Q_HEADS + C_GRP * gidx, C_GRP,
                                                -(-md // QBLK) + 2, dscale=dil, max_dist=md)
    c0 = np.arange(ncp)[:, None] * CMP_STRIDE
    s0 = np.arange(LANES)[None, :] * SLC_BLOCK
    ovl = np.clip(np.minimum(c0 + CMP_BLOCK, s0 + SLC_BLOCK) - np.maximum(c0, s0), 0, None) / CMP_BLOCK
    ovl = ovl * (np.arange(ncp)[:, None] < n_cmp) * (np.arange(LANES)[None, :] < n_slc)
    tables["ovl"] = jnp.asarray(ovl, BF16)
    expand = (np.arange(T)[None, :] // SLC_BLOCK) == np.arange(LANES)[:, None]
    tables["expand"] = jnp.asarray(expand, BF16)
    return tables


def _proj_gain(g):
    ones = jnp.ones((HEAD_DIM,), F32)
    spec = [(g[0] * (SCALE * LOG2E), 6), (ones, 4), (g[2], 2), (ones, 2), (g[3], 2), (ones, 2),
            (g[4] * (SCALE * LOG2E), 4), (g[5], 2), (ones, 2), (g[6] * (SCALE * LOG2E), 6), (g[7], 3), (ones, 3)]
    assert sum(n for _, n in spec) == N_MAIN_BLOCKS
    return jnp.concatenate([jnp.tile(v, n) for v, n in spec]).reshape(1, N_MAIN)


def kernel(x, norm_attn, w_in, qk_gain, cmp_pe, cmp_w1, cmp_w2, sinks, rel_bias, w_out, norm_ffn, w_gate, w_up, w_down):
    B, T, D = x.shape
    depth = w_in.shape[0]
    tables = _build_tables(rel_bias, T)
    x2 = x.reshape(B * T, D)
    w_all = _wprep(w_in, tr=256)
    w_out_b, w_gate_b, w_up_b, w_down_b = (w.astype(BF16) for w in (w_out, w_gate, w_up, w_down))
    tm = min(512, B * T)
    for l in range(depth):
        proj, gates, dil32 = _proj(x2, norm_attn[l].reshape(1, D), w_all, l, _proj_gain(qk_gain[l]), tm=tm)
        x2 = _mixers_outproj(x2, proj.reshape(B, T, N_MAIN), gates, dil32.reshape(B, T, DIL_F32_COLS),
                             (cmp_pe[l], cmp_w1[l], cmp_w2[l], qk_gain[l][1], sinks[l]), tables, w_out_b, l, tm=tm)
        x2 = _ffn(x2, norm_ffn[l].reshape(1, D), w_gate_b, w_up_b, w_down_b, l, tm=tm, tf=512)
    return x2.reshape(B, T, D)
```

```python
import functools
import math

import numpy as np
import jax
import jax.numpy as jnp
from jax import lax
from jax.experimental import pallas as pl
from jax.experimental.pallas import tpu as pltpu

F32 = jnp.float32
BF16 = jnp.bfloat16

HEAD_DIM = 128
LANES = 128
SUBLANES = 8
QBLK = 128
BAND_SUBBLOCKS = 8
CMP_SUBBLOCKS = 4
A_Q_HEADS, A_KV_HEADS = 6, 2
A_GRP = A_Q_HEADS // A_KV_HEADS
B_Q_HEADS, B_KV_HEADS = 4, 2
B_GRP = B_Q_HEADS // B_KV_HEADS
DIL_PAIRS = ((128, 1), (512, 4), (2048, 16))
C_GRP = 2
C_Q_HEADS = C_GRP * len(DIL_PAIRS)
CMP_BLOCK, CMP_STRIDE = 32, 16
SLC_BLOCK, SLC_TOPK = 64, 16
SLC_CHUNK = 4
SLC_QTILES = 1
NSA_WINDOW, SWA_WINDOW = 512, 128
FORCED_SCORE = 1.0e4
REL_BUCKETS, REL_MAX_EXACT, REL_MAX_DIST = 32, 16, 2048
SCALE = HEAD_DIM ** -0.5
LOG2E = math.log2(math.e)
LN2 = math.log(2.0)
MASK_WEIGHT = -2.0 ** 100
EPS = 1e-6
NEG = -1e30
VMEM_LIMIT = 56 * 1024 * 1024

COL_QA, COL_KCA, COL_VCA, COL_KSA, COL_VSA, COL_KWA, COL_VWA = 0, 6, 8, 10, 12, 14, 16
COL_QB, COL_KB, COL_VB, COL_QC, COL_KC, COL_VC = 18, 22, 24, 26, 32, 35
N_MAIN_BLOCKS = 38
N_MAIN = N_MAIN_BLOCKS * LANES
GATE_START = 2304
N_GATES = A_Q_HEADS * 3
PROJ_NORMED = ([True] * 6 + [False] * 4 + [True] * 2 + [False] * 2 + [True] * 2 + [False] * 2
               + [True] * 6 + [False] * 2 + [True] * 9 + [False] * 3)
DIL_F32_SLOT = {}
for _g, (_, _d) in enumerate(DIL_PAIRS):
    if _d > 1:
        _base = 4 * len([1 for _, _dd in DIL_PAIRS[:_g] if _dd > 1])
        DIL_F32_SLOT.update({COL_QC + C_GRP * _g: _base, COL_QC + C_GRP * _g + 1: _base + 1,
                             COL_KC + _g: _base + 2, COL_VC + _g: _base + 3})
CMP_F32_SLOT0 = len(DIL_F32_SLOT)
DIL_F32_SLOT.update({COL_KCA + _i: CMP_F32_SLOT0 + _i for _i in range(2 * A_KV_HEADS)})
DIL_F32_COLS = LANES * len(DIL_F32_SLOT)
DIL_TILES_PER_STEP = 8
ROW_TILE = 512
FFN_TILE = 512
WPREP_ROWS = 256
PROJ_SUBTILE = 256
MIX_ROW_SPLIT = 4
OUT_SUBTILE = 512


def _bucket_starts():
    d = np.arange(0, 1 << 17)
    out = []
    for dt in (np.float32, np.float64):
        far = np.maximum(d, REL_MAX_EXACT).astype(dt)
        lb = REL_MAX_EXACT + (np.log(far / dt(REL_MAX_EXACT)) / dt(math.log(REL_MAX_DIST / REL_MAX_EXACT))
                              * dt(REL_BUCKETS - REL_MAX_EXACT)).astype(np.int64)
        out.append(np.where(d < REL_MAX_EXACT, d, np.minimum(lb, REL_BUCKETS - 1)))
    assert (out[0] == out[1]).all() and (np.diff(out[0]) >= 0).all()
    return [int(np.argmax(out[0] >= b)) for b in range(REL_BUCKETS)]


BUCKET_START = _bucket_starts()
SAT_DIST = BUCKET_START[REL_BUCKETS - 1]


def _cparams(sem, vmem=VMEM_LIMIT):
    return pltpu.CompilerParams(dimension_semantics=sem, vmem_limit_bytes=vmem)


def _bias_tile_kernel(tab_ref, o_ref, *, head0, koff, dscale, max_dist):
    h = pl.program_id(0) + head0
    t = pl.program_id(1)
    r = lax.broadcasted_iota(jnp.int32, (QBLK, LANES), 0)
    c = lax.broadcasted_iota(jnp.int32, (QBLK, LANES), 1)
    dist = t * QBLK + r - c - koff
    d = dist * dscale
    val = jnp.full((QBLK, LANES), tab_ref[REL_BUCKETS - 1, h] * LOG2E, F32)
    for b in range(REL_BUCKETS - 2, -1, -1):
        val = jnp.where(d < BUCKET_START[b + 1], tab_ref[b, h] * LOG2E, val)
    val = jnp.where(dist >= 0, val, NEG)
    if max_dist is not None:
        val = jnp.where(dist <= max_dist, val, NEG)
    o_ref[0, 0] = val


def _bias_tiles(rel_bias, head0, nheads, ntiles, *, koff=0, dscale=1, max_dist=None):
    kern = functools.partial(_bias_tile_kernel, head0=head0, koff=koff, dscale=dscale, max_dist=max_dist)
    return pl.pallas_call(
        kern,
        grid=(nheads, ntiles),
        in_specs=[pl.BlockSpec(memory_space=pltpu.SMEM)],
        out_specs=pl.BlockSpec((1, 1, QBLK, LANES), lambda h, t: (h, t, 0, 0)),
        out_shape=jax.ShapeDtypeStruct((nheads, ntiles, QBLK, LANES), F32),
        compiler_params=_cparams(("parallel", "parallel")),
        name="bias_tiles",
    )(rel_bias)


def _cmp_bias_kernel(tab_ref, o_ref, *, n_cmp):
    h = pl.program_id(0)
    ntile, _, width = o_ref.shape[1:]
    r = lax.broadcasted_iota(jnp.int32, (QBLK, 2 * width), 0)
    c = lax.broadcasted_iota(jnp.int32, (QBLK, 2 * width), 1)
    dist = r - CMP_STRIDE * (c - width) - (CMP_BLOCK - 1)
    base = jnp.full((QBLK, 2 * width), tab_ref[REL_BUCKETS - 1, h] * LOG2E, F32)
    for b in range(REL_BUCKETS - 2, -1, -1):
        base = jnp.where(dist < BUCKET_START[b + 1], tab_ref[b, h] * LOG2E, base)
    base = jnp.where(dist >= 0, base, NEG)
    col = lax.broadcasted_iota(jnp.int32, (QBLK, width), 1)
    per_tile = QBLK // CMP_STRIDE
    for i in range(ntile):
        tile = pltpu.roll(base, (width + per_tile * i) % (2 * width), 1)[:, :width]
        o_ref[0, i] = jnp.where(col < n_cmp, tile, NEG)


def _cmp_bias(rel_bias, nheads, ntiles, width, n_cmp):
    return pl.pallas_call(
        functools.partial(_cmp_bias_kernel, n_cmp=n_cmp),
        grid=(nheads,),
        in_specs=[pl.BlockSpec(memory_space=pltpu.SMEM)],
        out_specs=pl.BlockSpec((1, ntiles, QBLK, width), lambda h: (h, 0, 0, 0)),
        out_shape=jax.ShapeDtypeStruct((nheads, ntiles, QBLK, width), F32),
        compiler_params=_cparams(("parallel",)),
        name="cmp_bias",
    )(rel_bias)


def _rms(x, w):
    ms = jnp.mean(x * x, axis=-1, keepdims=True)
    return x * lax.rsqrt(ms + EPS) * w


def _wprep_kernel(w_ref, o_ref):
    o_ref[0, :, :GATE_START] = w_ref[0, :, :GATE_START].astype(BF16)
    o_ref[0, :, GATE_START:N_MAIN] = w_ref[0, :, GATE_START + N_GATES:].astype(BF16)
    o_ref[0, :, N_MAIN:] = w_ref[0, :, GATE_START:GATE_START + LANES].astype(BF16)


def _wprep(w_in, *, tr):
    depth, d, n_in = w_in.shape
    return pl.pallas_call(
        _wprep_kernel,
        grid=(depth, d // tr),
        in_specs=[pl.BlockSpec((1, tr, n_in), lambda l, i: (l, i, 0))],
        out_specs=pl.BlockSpec((1, tr, N_MAIN + LANES), lambda l, i: (l, i, 0)),
        out_shape=jax.ShapeDtypeStruct((depth, d, N_MAIN + LANES), BF16),
        compiler_params=_cparams(("parallel", "parallel")),
        name="w_in_prep",
    )(w_in)


def _proj_kernel(x_ref, nw_ref, w_ref, gain_ref, o_ref, gate_ref, dil_ref):
    h = _rms(x_ref[...], nw_ref[...]).astype(BF16)
    for c0 in range(0, N_MAIN, PROJ_SUBTILE):
        width = min(PROJ_SUBTILE, N_MAIN - c0)
        acc = jnp.dot(h, w_ref[:, c0:c0 + width], preferred_element_type=F32)
        for c in range(width // LANES):
            blk = c0 // LANES + c
            sl = slice(blk * LANES, (blk + 1) * LANES)
            y = acc[:, c * LANES:(c + 1) * LANES]
            if PROJ_NORMED[blk]:
                ms = jnp.mean(y * y, axis=-1, keepdims=True)
                y = y * lax.rsqrt(ms + EPS) * gain_ref[:, sl]
            o_ref[:, sl] = y.astype(o_ref.dtype)
            if blk in DIL_F32_SLOT:
                dil_ref[:, DIL_F32_SLOT[blk] * LANES:(DIL_F32_SLOT[blk] + 1) * LANES] = y
    gate_ref[...] = jnp.dot(h, w_ref[:, N_MAIN:], preferred_element_type=F32)


def _proj(x2d, norm_w, w_all, layer, gain, *, tm):
    m, d = x2d.shape
    n = w_all.shape[2]
    return pl.pallas_call(
        _proj_kernel,
        grid=(m // tm,),
        in_specs=[
            pl.BlockSpec((tm, d), lambda i: (i, 0)),
            pl.BlockSpec((1, d), lambda i: (0, 0)),
            pl.BlockSpec((None, d, n), lambda i: (layer, 0, 0), pipeline_mode=pl.Buffered(1)),
            pl.BlockSpec((1, N_MAIN), lambda i: (0, 0)),
        ],
        out_specs=(pl.BlockSpec((tm, N_MAIN), lambda i: (i, 0)), pl.BlockSpec((tm, LANES), lambda i: (i, 0)),
                   pl.BlockSpec((tm, DIL_F32_COLS), lambda i: (i, 0))),
        out_shape=(jax.ShapeDtypeStruct((m, N_MAIN), BF16), jax.ShapeDtypeStruct((m, LANES), F32),
                   jax.ShapeDtypeStruct((m, DIL_F32_COLS), F32)),
        compiler_params=_cparams(("parallel",)),
        name="in_proj",
    )(x2d, norm_w, w_all, gain)


def _ffn_kernel(x_ref, nw_ref, wg_ref, wu_ref, wd_ref, o_ref, h_ref, acc_ref):
    f = pl.program_id(1)

    @pl.when(f == 0)
    def _():
        h_ref[...] = _rms(x_ref[...], nw_ref[...]).astype(BF16)
        acc_ref[...] = jnp.zeros_like(acc_ref)

    h = h_ref[...]
    g = jnp.dot(h, wg_ref[...], preferred_element_type=F32)
    u = jnp.dot(h, wu_ref[...], preferred_element_type=F32)
    a = (g * (1.0 / (1.0 + jnp.exp(-g))) * u).astype(BF16)
    acc_ref[...] += jnp.dot(a, wd_ref[...], preferred_element_type=F32)

    @pl.when(f == pl.num_programs(1) - 1)
    def _():
        o_ref[...] = x_ref[...] + acc_ref[...]


def _ffn(x2d, norm_w, wg, wu, wd, layer, *, tm, tf):
    m, d = x2d.shape
    dff = wg.shape[2]
    return pl.pallas_call(
        _ffn_kernel,
        grid=(m // tm, dff // tf),
        in_specs=[
            pl.BlockSpec((tm, d), lambda i, f: (i, 0)),
            pl.BlockSpec((1, d), lambda i, f: (0, 0)),
            pl.BlockSpec((None, d, tf), lambda i, f: (layer, 0, f)),
            pl.BlockSpec((None, d, tf), lambda i, f: (layer, 0, f)),
            pl.BlockSpec((None, tf, d), lambda i, f: (layer, f, 0)),
        ],
        out_specs=pl.BlockSpec((tm, d), lambda i, f: (i, 0)),
        out_shape=jax.ShapeDtypeStruct((m, d), F32),
        scratch_shapes=[pltpu.VMEM((tm, d), BF16), pltpu.VMEM((tm, d), F32)],
        compiler_params=_cparams(("parallel", "arbitrary")),
        name="ffn",
    )(x2d, norm_w, wg, wu, wd)


def _band_kernel(*refs, grp, n_off, nsub, has_sinks, with_lse):
    if has_sinks:
        sink_ref, q_ref, k_ref, v_ref, b_ref = refs[:5]
        outs = refs[5:]
    else:
        q_ref, k_ref, v_ref, b_ref = refs[:4]
        outs = refs[4:]
    o_ref = outs[0]
    g = pl.program_id(1)
    i0 = pl.program_id(2) * nsub
    tiles = {}

    def kv_tile(rel):
        if rel not in tiles:
            start = pl.multiple_of(jnp.maximum(i0 + rel, 0) * QBLK, QBLK)
            tiles[rel] = (k_ref[0, pl.ds(start, QBLK), :], v_ref[0, pl.ds(start, QBLK), :])
        return tiles[rel]

    def head_rows(x, h):
        return x[h * QBLK:(h + 1) * QBLK]

    kvs = [[kv_tile(j - off) for off in range(n_off)] for j in range(nsub)]
    tidxs = [[jnp.where(i0 + j - off >= 0, off, n_off) for off in range(n_off)] for j in range(nsub)]
    scores = []
    for j in range(nsub):
        rs = slice(j * QBLK, (j + 1) * QBLK)
        kcat = jnp.concatenate([t[0] for t in kvs[j]], axis=0)
        q = jnp.concatenate([q_ref[0, rs, h * HEAD_DIM:(h + 1) * HEAD_DIM] for h in range(grp)], axis=0)
        scores.append(lax.dot_general(q, kcat, (((1,), (1,)), ((), ())), preferred_element_type=F32))
    probs, stats = [], []
    for j in range(nsub):
        s = scores[j]
        st = [[head_rows(s, h)[:, off * QBLK:(off + 1) * QBLK] + b_ref[h, tidxs[j][off]] for off in range(n_off)]
              for h in range(grp)]
        m = jnp.max(jnp.concatenate([functools.reduce(jnp.maximum, r) for r in st], axis=0), axis=-1, keepdims=True)
        p = [[jnp.exp2(st[h][off] - head_rows(m, h)) for off in range(n_off)] for h in range(grp)]
        l = jnp.sum(jnp.concatenate([functools.reduce(jnp.add, r) for r in p], axis=0), axis=-1, keepdims=True)
        probs.append(jnp.concatenate([jnp.concatenate([t.astype(BF16) for t in r], axis=1) for r in p], axis=0))
        stats.append((m, l))
    for j in range(nsub):
        rs = slice(j * QBLK, (j + 1) * QBLK)
        m, l = stats[j]
        vcat = jnp.concatenate([t[1] for t in kvs[j]], axis=0)
        o = jnp.dot(probs[j], vcat, preferred_element_type=F32)
        den = l
        if has_sinks:
            sink = jnp.concatenate([jnp.full((QBLK, 1), sink_ref[g * grp + h] * LOG2E, F32) for h in range(grp)], axis=0)
            den = l + jnp.exp2(sink - m)
        o = o / den
        for h in range(grp):
            sl = slice(h * HEAD_DIM, (h + 1) * HEAD_DIM)
            o_ref[0, rs, sl] = head_rows(o, h).astype(o_ref.dtype)
            if with_lse:
                outs[1][0, rs, sl] = jnp.broadcast_to(head_rows(m * LN2 + jnp.log(l), h), (QBLK, HEAD_DIM))


def _band_attn(q_arr, k_arr, v_arr, bias, *, n_kv, grp, q_col, k_col, v_col, sinks=None, with_lse=False):
    n, L, _ = q_arr.shape
    n_off = bias.shape[1] - 1
    gw = grp * HEAD_DIM
    nsub = math.gcd(BAND_SUBBLOCKS, L // QBLK)
    rows = nsub * QBLK
    kern = functools.partial(_band_kernel, grp=grp, n_off=n_off, nsub=nsub, has_sinks=sinks is not None,
                             with_lse=with_lse)
    in_specs = [
        pl.BlockSpec((1, rows, gw), lambda b, g, i: (b, i, q_col + g)),
        pl.BlockSpec((1, L, HEAD_DIM), lambda b, g, i: (b, 0, k_col + g)),
        pl.BlockSpec((1, L, HEAD_DIM), lambda b, g, i: (b, 0, v_col + g)),
        pl.BlockSpec((grp, n_off + 1, QBLK, LANES), lambda b, g, i: (g, 0, 0, 0)),
    ]
    args = [q_arr, k_arr, v_arr, bias]
    if sinks is not None:
        in_specs = [pl.BlockSpec(memory_space=pltpu.SMEM)] + in_specs
        args = [sinks] + args
    o_spec = pl.BlockSpec((1, rows, gw), lambda b, g, i: (b, i, g))
    o_shape = jax.ShapeDtypeStruct((n, L, n_kv * gw), F32)
    return pl.pallas_call(
        kern,
        grid=(n, n_kv, L // rows),
        in_specs=in_specs,
        out_specs=(o_spec, o_spec) if with_lse else o_spec,
        out_shape=(o_shape, o_shape) if with_lse else o_shape,
        compiler_params=_cparams(("parallel", "parallel", "arbitrary")),
        name="band_attn",
    )(*args)


def _dil_kernel(*refs, dil, nres):
    q_refs, (k_ref, v_ref, b_ref) = refs[:C_GRP], refs[C_GRP:C_GRP + 3]
    o_refs, lse_refs = refs[C_GRP + 3:2 * C_GRP + 3], refs[2 * C_GRP + 3:]
    ls = k_ref.shape[1] // dil
    ntile = ls // QBLK
    r0 = pl.program_id(1) * nres
    units = [(rr, j) for rr in range(nres) for j in range(ntile)]

    def rows(rr, j):
        return pl.ds(r0 + rr + dil * QBLK * j, QBLK, stride=dil)

    kv = {u: (k_ref[0, rows(*u), :].astype(BF16), v_ref[0, rows(*u), :].astype(BF16)) for u in units}
    n_off = b_ref.shape[1] - 1
    offs = {(rr, j): [off for off in range(n_off) if j - off >= 0] for rr, j in units}
    scores = []
    for rr, j in units:
        kcat = jnp.concatenate([kv[(rr, j - off)][0] for off in offs[(rr, j)]], axis=0)
        q = jnp.concatenate([q_ref[0, rows(rr, j), :].astype(BF16) for q_ref in q_refs], axis=0)
        scores.append(lax.dot_general(q, kcat, (((1,), (1,)), ((), ())), preferred_element_type=F32))
    probs, stats = [], []
    for u, s in zip(units, scores):
        st = [[s[h * QBLK:(h + 1) * QBLK, n * QBLK:(n + 1) * QBLK] + b_ref[h, off] for n, off in enumerate(offs[u])]
              for h in range(C_GRP)]
        m = jnp.max(jnp.concatenate([functools.reduce(jnp.maximum, r) for r in st], axis=0), axis=-1, keepdims=True)
        p = [[jnp.exp2(t - m[h * QBLK:(h + 1) * QBLK]) for t in r] for h, r in enumerate(st)]
        l = jnp.sum(jnp.concatenate([functools.reduce(jnp.add, r) for r in p], axis=0), axis=-1, keepdims=True)
        probs.append(jnp.concatenate([jnp.concatenate([t.astype(BF16) for t in r], axis=1) for r in p], axis=0))
        stats.append((m, l))
    for (rr, j), pb, (m, l) in zip(units, probs, stats):
        vcat = jnp.concatenate([kv[(rr, j - off)][1] for off in offs[(rr, j)]], axis=0)
        o = jnp.dot(pb, vcat, preferred_element_type=F32) / l
        lse = m * LN2 + jnp.log(l)
        for h in range(C_GRP):
            o_refs[h][0, rows(rr, j), :] = o[h * QBLK:(h + 1) * QBLK]
            lse_refs[h][0, rows(rr, j), :] = jnp.broadcast_to(lse[h * QBLK:(h + 1) * QBLK], (QBLK, HEAD_DIM))


def _dil_attn(x3, bias, *, col, dil):
    B, T, _ = x3.shape
    ntile = T // dil // QBLK
    nres = max(1, min(dil, DIL_TILES_PER_STEP // ntile))
    spec = pl.BlockSpec((1, T, HEAD_DIM), lambda b, r: (b, 0, 0))
    shape = jax.ShapeDtypeStruct((B, T, HEAD_DIM), F32)
    outs = pl.pallas_call(
        functools.partial(_dil_kernel, dil=dil, nres=nres),
        grid=(B, dil // nres),
        in_specs=[pl.BlockSpec((1, T, HEAD_DIM), functools.partial(lambda b, r, c: (b, 0, c), c=col + c))
                  for c in range(C_GRP + 2)]
        + [pl.BlockSpec((C_GRP, bias.shape[1], QBLK, LANES), lambda b, r: (0, 0, 0, 0))],
        out_specs=(spec,) * (2 * C_GRP),
        out_shape=(shape,) * (2 * C_GRP),
        compiler_params=_cparams(("parallel", "arbitrary")),
        name="dilated_attn",
    )(*([x3] * (C_GRP + 2)), bias)
    return outs[:C_GRP], outs[C_GRP:]


def _compress_kernel(x_ref, pe_ref, w1_ref, w2_ref, gain_ref, o_ref, *, n_cmp):
    half = w1_ref.shape[1] // 2
    rows = x_ref.shape[1] // CMP_STRIDE
    xr = jnp.concatenate([x_ref[0, pl.ds(t, rows, stride=CMP_STRIDE), :].astype(BF16) for t in range(CMP_STRIDE)],
                         axis=1)
    y0 = jnp.dot(xr, w1_ref[0, :half], preferred_element_type=F32)
    y1 = jnp.dot(xr, w1_ref[0, half:], preferred_element_type=F32)
    pe = jnp.dot(pe_ref[0], w1_ref[0], preferred_element_type=F32)[0:1]
    c = y0 + pltpu.roll(y1, rows - 1, 0) + pe
    gl = 0.5 * c * (1.0 + jnp.tanh(math.sqrt(2.0 / math.pi) * (c + 0.044715 * (c * c * c))))
    out = jnp.dot(gl.astype(BF16), w2_ref[0], preferred_element_type=F32)
    out = jnp.where(pl.program_id(0) == 0, _rms(out, gain_ref[...]), out)
    valid = lax.broadcasted_iota(jnp.int32, out.shape, 0) < n_cmp
    o_ref[0, 0, 0] = jnp.where(valid, out, 0.0).astype(o_ref.dtype)


def _compress(x3, pe, w1, w2, gain, n_cmp):
    B, T, _ = x3.shape
    rows = T // CMP_STRIDE
    wide = CMP_BLOCK * HEAD_DIM
    return pl.pallas_call(
        functools.partial(_compress_kernel, n_cmp=n_cmp),
        grid=(2, B, A_KV_HEADS),
        in_specs=[
            pl.BlockSpec((1, T, HEAD_DIM), lambda s, b, g: (b, 0, CMP_F32_SLOT0 + s * A_KV_HEADS + g)),
            pl.BlockSpec((1, 8, wide), lambda s, b, g: (s, 0, 0)),
            pl.BlockSpec((1, wide, HEAD_DIM), lambda s, b, g: (s, 0, 0)),
            pl.BlockSpec((1, HEAD_DIM, HEAD_DIM), lambda s, b, g: (s, 0, 0)),
            pl.BlockSpec((1, HEAD_DIM), lambda s, b, g: (0, 0)),
        ],
        out_specs=pl.BlockSpec((1, 1, 1, rows, HEAD_DIM), lambda s, b, g: (s, b, g, 0, 0)),
        out_shape=jax.ShapeDtypeStruct((2, B, A_KV_HEADS, rows, HEAD_DIM), BF16),
        compiler_params=_cparams(("arbitrary", "arbitrary", "arbitrary")),
        name="nsa_compress",
    )(x3, pe, w1, w2, gain)


def _cmp_select_kernel(q_ref, kc_ref, vc_ref, b_ref, ovl_ref, o_ref, sel_ref, *, n_slc, nsub):
    i0 = pl.program_id(2) * nsub
    rows = nsub * QBLK
    kc = kc_ref[0, 0, 0]
    vc = vc_ref[0, 0, 0]
    q_all = jnp.concatenate([q_ref[0, :, h * HEAD_DIM:(h + 1) * HEAD_DIM] for h in range(A_GRP)], axis=0)
    s_all = lax.dot_general(q_all, kc, (((1,), (1,)), ((), ())), preferred_element_type=F32)
    ps = []
    for h in range(A_GRP):
        s = s_all[h * rows:(h + 1) * rows] + b_ref[h]
        m = jnp.max(s, axis=-1, keepdims=True)
        e = jnp.exp2(s - jnp.where(m > 0.5 * NEG, m, 0.0))
        den = jnp.sum(e, axis=-1, keepdims=True)
        ps.append((e / jnp.where(den > 0, den, 1.0)).astype(BF16))
    r_all = jnp.dot(jnp.concatenate(ps, axis=0), jnp.concatenate([vc, ovl_ref[...]], axis=1),
                    preferred_element_type=F32)
    imp = jnp.zeros((rows, LANES), F32)
    for h in range(A_GRP):
        o_ref[0, :, h * HEAD_DIM:(h + 1) * HEAD_DIM] = r_all[h * rows:(h + 1) * rows, :HEAD_DIM]
        imp = imp + r_all[h * rows:(h + 1) * rows, HEAD_DIM:]
    t = i0 * QBLK + lax.broadcasted_iota(jnp.int32, (rows, LANES), 0)
    blk = lax.broadcasted_iota(jnp.int32, (rows, LANES), 1)
    cur = t // SLC_BLOCK
    imp = jnp.where(blk == 0, FORCED_SCORE, imp)
    imp = jnp.where(blk == cur, FORCED_SCORE, imp)
    imp = jnp.where(blk == cur - 1, FORCED_SCORE, imp)
    imp = jnp.where(blk * SLC_BLOCK <= t, imp, NEG)
    imp = jnp.where(blk < n_slc, imp, 2.0 * NEG)
    imp_t = jnp.concatenate([imp[j * QBLK:(j + 1) * QBLK].T for j in range(nsub)], axis=1)
    ngrp = -(-n_slc // SUBLANES)
    cands = [imp_t[r * SUBLANES:(r + 1) * SUBLANES] for r in range(ngrp)]
    row_id = lax.broadcasted_iota(jnp.int32, (SUBLANES, rows), 0)
    ranks = [jnp.zeros((SUBLANES, rows), F32) for _ in range(ngrp)]
    for j in range(n_slc):
        other = imp_t[j:j + 1, :]
        for r in range(ngrp):
            if r * SUBLANES > j:
                beats = other >= cands[r]
            elif (r + 1) * SUBLANES <= j:
                beats = other > cands[r]
            else:
                beats = jnp.where(row_id > j - r * SUBLANES, jnp.where(other >= cands[r], 1.0, 0.0),
                                  jnp.where(other > cands[r], 1.0, 0.0)) > 0.5
            ranks[r] = ranks[r] + jnp.where(beats, 1.0, 0.0)
    unsel = [jnp.where(r < float(min(SLC_TOPK, n_slc)), 0.0, 1.0) for r in ranks]
    if ngrp * SUBLANES > n_slc:
        unsel[-1] = jnp.where(row_id < n_slc - (ngrp - 1) * SUBLANES, unsel[-1], 0.0)
    sel_t = jnp.concatenate(unsel, axis=0)
    if ngrp * SUBLANES < LANES:
        sel_t = jnp.concatenate([sel_t, jnp.zeros((LANES - ngrp * SUBLANES, rows), F32)], axis=0)
    for j in range(nsub):
        sel_ref[0, 0, j * QBLK:(j + 1) * QBLK, :] = sel_t[:, j * QBLK:(j + 1) * QBLK].T.astype(sel_ref.dtype)


def _cmp_select(proj3, kvc, bias_c, ovl, n_slc):
    B, T, _ = proj3.shape
    ncp = kvc.shape[3]
    nsub = math.gcd(CMP_SUBBLOCKS, T // QBLK)
    rows = nsub * QBLK
    return pl.pallas_call(
        functools.partial(_cmp_select_kernel, n_slc=n_slc, nsub=nsub),
        grid=(B, A_KV_HEADS, T // rows),
        in_specs=[
            pl.BlockSpec((1, rows, A_GRP * HEAD_DIM), lambda b, g, i: (b, i, g)),
            pl.BlockSpec((1, 1, 1, ncp, HEAD_DIM), lambda b, g, i: (0, b, g, 0, 0)),
            pl.BlockSpec((1, 1, 1, ncp, HEAD_DIM), lambda b, g, i: (1, b, g, 0, 0)),
            pl.BlockSpec((A_GRP, rows, ncp), lambda b, g, i: (g, i, 0)),
            pl.BlockSpec((ncp, LANES), lambda b, g, i: (0, 0)),
        ],
        out_specs=(
            pl.BlockSpec((1, rows, A_GRP * HEAD_DIM), lambda b, g, i: (b, i, g)),
            pl.BlockSpec((1, 1, rows, LANES), lambda b, g, i: (b, g, i, 0)),
        ),
        out_shape=(
            jax.ShapeDtypeStruct((B, T, A_Q_HEADS * HEAD_DIM), F32),
            jax.ShapeDtypeStruct((B, A_KV_HEADS, T, LANES), BF16),
        ),
        compiler_params=_cparams(("parallel", "parallel", "arbitrary")),
        name="nsa_cmp_select",
    )(proj3, kvc, kvc, bias_c, ovl)


def _slc_kernel(q_ref, k_ref, v_ref, sel_ref, e_ref, b_ref, o_ref, *scratch, n_tiles, nq):
    ig = pl.program_id(1)
    chains = [(g, a) for g in range(A_KV_HEADS) for a in range(nq)]
    m_refs, l_refs, acc_refs = scratch[0::3], scratch[1::3], scratch[2::3]
    qs, sels = [], []
    for ci, (g, a) in enumerate(chains):
        m_refs[ci][...] = jnp.full(m_refs[ci].shape, NEG, F32)
        l_refs[ci][...] = jnp.zeros(l_refs[ci].shape, F32)
        acc_refs[ci][...] = jnp.zeros(acc_refs[ci].shape, F32)
        rs = slice(a * QBLK, (a + 1) * QBLK)
        qs.append(jnp.concatenate([q_ref[0, rs, (g * A_GRP + h) * HEAD_DIM:(g * A_GRP + h + 1) * HEAD_DIM]
                                   for h in range(A_GRP)], axis=0))
        sels.append(sel_ref[0, g, rs, :])
    kw = SLC_CHUNK * QBLK

    def body(c, carry):
        start = pl.multiple_of(c * kw, kw)
        base = ig * nq - c * SLC_CHUNK + 1
        tidx = {d: jnp.clip(base + d, 0, n_tiles - 1) for d in range(-(SLC_CHUNK - 1), nq)}
        kts = [k_ref[0, pl.ds(start, kw), g * HEAD_DIM:(g + 1) * HEAD_DIM] for g in range(A_KV_HEADS)]
        vts = [v_ref[0, pl.ds(start, kw), g * HEAD_DIM:(g + 1) * HEAD_DIM] for g in range(A_KV_HEADS)]
        ech = e_ref[:, pl.ds(start, kw)]
        ss = [lax.dot_general(qs[ci], kts[g], (((1,), (1,)), ((), ())), preferred_element_type=F32)
              for ci, (g, a) in enumerate(chains)]
        madds = [jnp.dot(sels[ci], ech, preferred_element_type=F32) for ci in range(len(chains))]
        pbs, alphas = [], []
        for ci, (g, a) in enumerate(chains):
            madd = madds[ci]
            s = ss[ci]
            rows = [[s[h * QBLK:(h + 1) * QBLK, j * QBLK:(j + 1) * QBLK] + b_ref[g * A_GRP + h, tidx[a - j]]
                     + madd[:, j * QBLK:(j + 1) * QBLK] for j in range(SLC_CHUNK)] for h in range(A_GRP)]
            tile_max = jnp.concatenate([functools.reduce(jnp.maximum, r) for r in rows], axis=0)
            m_old = m_refs[ci][...]
            m_new = jnp.maximum(m_old, jnp.max(tile_max, axis=-1, keepdims=True))
            alpha = jnp.exp2(m_old - m_new)
            p = [[jnp.exp2(t - m_new[h * QBLK:(h + 1) * QBLK]) for t in r] for h, r in enumerate(rows)]
            tile_sum = jnp.concatenate([functools.reduce(jnp.add, r) for r in p], axis=0)
            l_refs[ci][...] = alpha * l_refs[ci][...] + jnp.sum(tile_sum, axis=-1, keepdims=True)
            pbs.append(jnp.concatenate([jnp.concatenate([t.astype(BF16) for t in r], axis=1) for r in p], axis=0))
            alphas.append(alpha)
            m_refs[ci][...] = m_new
        for ci, (g, a) in enumerate(chains):
            acc_refs[ci][...] = alphas[ci] * acc_refs[ci][...] + jnp.dot(pbs[ci], vts[g], preferred_element_type=F32)
        return carry

    lax.fori_loop(0, (ig * nq + nq - 1) // SLC_CHUNK + 1, body, 0)
    for ci, (g, a) in enumerate(chains):
        out = acc_refs[ci][...] / l_refs[ci][...]
        for h in range(A_GRP):
            col = (g * A_GRP + h) * HEAD_DIM
            o_ref[0, a * QBLK:(a + 1) * QBLK, col:col + HEAD_DIM] = out[h * QBLK:(h + 1) * QBLK]


def _slc_attn(proj3, sel, expand, bias):
    B, T, _ = proj3.shape
    n_tiles = bias.shape[1]
    nq = SLC_QTILES
    qrows = nq * QBLK
    assert SLC_CHUNK % nq == 0 and T % (SLC_CHUNK * QBLK) == 0
    chain_rows = A_GRP * QBLK
    n_chains = A_KV_HEADS * nq
    return pl.pallas_call(
        functools.partial(_slc_kernel, n_tiles=n_tiles, nq=nq),
        grid=(B, T // qrows),
        in_specs=[
            pl.BlockSpec((1, qrows, A_Q_HEADS * HEAD_DIM), lambda b, i: (b, i, COL_QA // A_Q_HEADS)),
            pl.BlockSpec((1, T, A_KV_HEADS * HEAD_DIM), lambda b, i: (b, 0, COL_KSA // A_KV_HEADS)),
            pl.BlockSpec((1, T, A_KV_HEADS * HEAD_DIM), lambda b, i: (b, 0, COL_VSA // A_KV_HEADS)),
            pl.BlockSpec((1, A_KV_HEADS, qrows, LANES), lambda b, i: (b, 0, i, 0)),
            pl.BlockSpec((LANES, T), lambda b, i: (0, 0)),
            pl.BlockSpec((A_Q_HEADS, n_tiles, QBLK, LANES), lambda b, i: (0, 0, 0, 0)),
        ],
        out_specs=pl.BlockSpec((1, qrows, A_Q_HEADS * HEAD_DIM), lambda b, i: (b, i, 0)),
        out_shape=jax.ShapeDtypeStruct((B, T, A_Q_HEADS * HEAD_DIM), F32),
        scratch_shapes=[pltpu.VMEM((chain_rows, 1), F32), pltpu.VMEM((chain_rows, 1), F32),
                        pltpu.VMEM((chain_rows, HEAD_DIM), F32)] * n_chains,
        compiler_params=_cparams(("parallel", "arbitrary")),
        name="nsa_slc_attn",
    )(proj3, proj3, proj3, sel, expand, bias)


def _mix_outproj_kernel(x_ref, gate_ref, ocmp_ref, oslc_ref, owin_ref, ob_ref, *rest):
    oc_refs, lse_refs = rest[:C_Q_HEADS], rest[C_Q_HEADS:2 * C_Q_HEADS]
    w_ref, o_ref, mix_ref = rest[2 * C_Q_HEADS:]
    tm = x_ref.shape[0]
    nsplit = MIX_ROW_SPLIT
    rows = tm // nsplit
    for part in range(nsplit):
        rs = slice(part * rows, (part + 1) * rows)
        gate = 1.0 / (1.0 + jnp.exp(-gate_ref[rs, :]))
        for h in range(A_Q_HEADS):
            sl = slice(h * HEAD_DIM, (h + 1) * HEAD_DIM)
            o = (gate[:, 3 * h:3 * h + 1] * ocmp_ref[rs, sl] + gate[:, 3 * h + 1:3 * h + 2] * oslc_ref[rs, sl]
                 + gate[:, 3 * h + 2:3 * h + 3] * owin_ref[rs, sl])
            mix_ref[rs, sl] = o.astype(mix_ref.dtype)
        base = A_Q_HEADS * HEAD_DIM
        width = B_Q_HEADS * HEAD_DIM
        mix_ref[rs, base:base + width] = ob_ref[rs, :].astype(mix_ref.dtype)
        base += width
        for hh in range(C_GRP):
            heads = [C_GRP * gidx + hh for gidx in range(len(DIL_PAIRS))]
            lses = [lse_refs[c][rs, :] for c in heads]
            mx = functools.reduce(jnp.maximum, lses)
            ws = [jnp.exp(x - mx) for x in lses]
            tot = functools.reduce(jnp.add, ws)
            for c, w in zip(heads, ws):
                mix_ref[rs, base + c * HEAD_DIM:base + (c + 1) * HEAD_DIM] = (oc_refs[c][rs, :] * (w / tot)).astype(mix_ref.dtype)
    for part in range(nsplit):
        rs = slice(part * rows, (part + 1) * rows)
        a = mix_ref[rs, :]
        for c0 in range(0, o_ref.shape[1], OUT_SUBTILE):
            sl = slice(c0, c0 + OUT_SUBTILE)
            o_ref[rs, sl] = x_ref[rs, sl] + jnp.dot(a, w_ref[:, sl], preferred_element_type=F32)


def _mix_outproj(x2d, gates, o_cmp, o_slc, o_win, o_b, o_cs, lses, w_all, layer, *, tm):
    m, d = x2d.shape
    k = w_all.shape[1]
    row = lambda w: pl.BlockSpec((tm, w), lambda i: (i, 0))
    head = lambda c: pl.BlockSpec((tm, HEAD_DIM), lambda i: (i, c))
    full = [x2d, gates, o_cmp, o_slc, o_win, o_b]
    return pl.pallas_call(
        _mix_outproj_kernel,
        grid=(m // tm,),
        in_specs=[row(a.shape[1]) for a in full] + [head(c) for _, c in o_cs + lses]
        + [pl.BlockSpec((None, k, d), lambda i: (layer, 0, 0), pipeline_mode=pl.Buffered(1))],
        out_specs=row(d),
        out_shape=jax.ShapeDtypeStruct((m, d), F32),
        scratch_shapes=[pltpu.VMEM((tm, k), BF16)],
        compiler_params=_cparams(("parallel",)),
        name="mix_out_proj",
    )(*full, *[a for a, _ in o_cs + lses], w_all)


def _mixers_outproj(x2d, proj3, gates, dil32, layer_params, tables, w_out_all, layer, *, tm):
    B, T, _ = proj3.shape
    cmp_pe, cmp_w1, cmp_w2, kc_gain, sinks = layer_params
    n_cmp = (T - CMP_BLOCK) // CMP_STRIDE + 1
    n_slc = T // SLC_BLOCK
    pe = jnp.broadcast_to(cmp_pe.reshape(2, 1, CMP_BLOCK * HEAD_DIM), (2, 8, CMP_BLOCK * HEAD_DIM)).astype(BF16)
    kvc = _compress(dil32, pe, cmp_w1.astype(BF16), cmp_w2.astype(BF16), kc_gain.reshape(1, HEAD_DIM), n_cmp)

    o_cmp, sel = _cmp_select(proj3, kvc, tables["bias_c"], tables["ovl"], n_slc)
    o_slc = _slc_attn(proj3, sel, tables["expand"], tables["bias_slc"])
    o_win = _band_attn(proj3, proj3, proj3, tables["bias_win"], n_kv=A_KV_HEADS, grp=A_GRP,
                       q_col=COL_QA // A_GRP, k_col=COL_KWA, v_col=COL_VWA)
    o_b = _band_attn(proj3, proj3, proj3, tables["bias_b"], n_kv=B_KV_HEADS, grp=B_GRP,
                     q_col=COL_QB // B_GRP, k_col=COL_KB, v_col=COL_VB, sinks=sinks)
    o_cs, lses = [], []
    for gidx, (_, dil) in enumerate(DIL_PAIRS):
        if dil == 1:
            o, lse = _band_attn(proj3, proj3, proj3, tables["bias_c%d" % gidx], n_kv=1, grp=C_GRP,
                                q_col=COL_QC // C_GRP + gidx, k_col=COL_KC + gidx, v_col=COL_VC + gidx,
                                with_lse=True)
            o_cs += [(o.reshape(B * T, -1), h) for h in range(C_GRP)]
            lses += [(lse.reshape(B * T, -1), h) for h in range(C_GRP)]
        else:
            os_, ls_ = _dil_attn(dil32, tables["bias_c%d" % gidx], col=DIL_F32_SLOT[COL_QC + C_GRP * gidx], dil=dil)
            o_cs += [(o.reshape(B * T, -1), 0) for o in os_]
            lses += [(lse.reshape(B * T, -1), 0) for lse in ls_]
    return _mix_outproj(x2d, gates, o_cmp.reshape(B * T, -1), o_slc.reshape(B * T, -1), o_win.reshape(B * T, -1),
                        o_b.reshape(B * T, -1), o_cs, lses, w_out_all, layer, tm=tm)


def _build_tables(rel_bias, T):
    n_cmp = (T - CMP_BLOCK) // CMP_STRIDE + 1
    n_slc = T // SLC_BLOCK
    ncp = T // CMP_STRIDE
    nq = T // QBLK
    tables = {}
    tables["bias_c"] = _cmp_bias(rel_bias, A_Q_HEADS, nq, ncp, n_cmp).reshape(A_Q_HEADS, T, ncp)
    n_sat = -(-(SAT_DIST + QBLK - 1) // QBLK) + 1
    tables["bias_slc"] = _bias_tiles(rel_bias, 0, A_Q_HEADS, min(n_sat, nq) + 1, koff=QBLK)
    win = NSA_WINDOW - 1
    tables["bias_win"] = _bias_tiles(rel_bias, 0, A_Q_HEADS, -(-win // QBLK) + 2, max_dist=win)
    swa = SWA_WINDOW - 1
    tables["bias_b"] = _bias_tiles(rel_bias, A_Q_HEADS, B_Q_HEADS, -(-swa // QBLK) + 2, max_dist=swa)
    for gidx, (w, dil) in enumerate(DIL_PAIRS):
        md = w // dil
        tables["bias_c%d" % gidx] = _bias_tiles(rel_bias, A_Q_HEADS + B_Q_HEADS + C_GRP * gidx, C_GRP,
                                                -(-md // QBLK) + 2, dscale=dil, max_dist=md)
    c0 = np.arange(ncp)[:, None] * CMP_STRIDE
    s0 = np.arange(LANES)[None, :] * SLC_BLOCK
    ovl = np.clip(np.minimum(c0 + CMP_BLOCK, s0 + SLC_BLOCK) - np.maximum(c0, s0), 0, None) / CMP_BLOCK
    ovl = ovl * (np.arange(ncp)[:, None] < n_cmp) * (np.arange(LANES)[None, :] < n_slc)
    tables["ovl"] = jnp.asarray(ovl, BF16)
    member = (np.arange(T)[None, :] // SLC_BLOCK) == np.arange(LANES)[:, None]
    tables["expand"] = jnp.asarray(member * MASK_WEIGHT, BF16)
    return tables


def _proj_gain(g):
    ones = jnp.ones((HEAD_DIM,), F32)
    spec = [(g[0] * (SCALE * LOG2E), 6), (ones, 4), (g[2], 2), (ones, 2), (g[3], 2), (ones, 2),
            (g[4] * (SCALE * LOG2E), 4), (g[5], 2), (ones, 2), (g[6] * (SCALE * LOG2E), 6), (g[7], 3), (ones, 3)]
    assert sum(n for _, n in spec) == N_MAIN_BLOCKS
    return jnp.concatenate([jnp.tile(v, n) for v, n in spec]).reshape(1, N_MAIN)


def kernel(x, norm_attn, w_in, qk_gain, cmp_pe, cmp_w1, cmp_w2, sinks, rel_bias, w_out, norm_ffn, w_gate, w_up, w_down):
    B, T, D = x.shape
    depth = w_in.shape[0]
    tables = _build_tables(rel_bias, T)
    x2 = x.reshape(B * T, D)
    w_all = _wprep(w_in, tr=WPREP_ROWS)
    w_out_b, w_gate_b, w_up_b, w_down_b = (w.astype(BF16) for w in (w_out, w_gate, w_up, w_down))
    tm = min(ROW_TILE, B * T)
    for l in range(depth):
        proj, gates, dil32 = _proj(x2, norm_attn[l].reshape(1, D), w_all, l, _proj_gain(qk_gain[l]), tm=tm)
        x2 = _mixers_outproj(x2, proj.reshape(B, T, N_MAIN), gates, dil32.reshape(B, T, DIL_F32_COLS),
                             (cmp_pe[l], cmp_w1[l], cmp_w2[l], qk_gain[l][1], sinks[l]), tables, w_out_b, l, tm=tm)
        x2 = _ffn(x2, norm_ffn[l].reshape(1, D), w_gate_b, w_up_b, w_down_b, l, tm=tm, tf=FFN_TILE)
    return x2.reshape(B, T, D)
```

```python
import functools
import math

import numpy as np
import jax
import jax.numpy as jnp
from jax import lax
from jax.experimental import pallas as pl
from jax.experimental.pallas import tpu as pltpu

F32 = jnp.float32
BF16 = jnp.bfloat16

HEAD_DIM = 128
LANES = 128
SUBLANES = 8
QBLK = 128
BAND_SUBBLOCKS = 8
CMP_SUBBLOCKS = 4
A_Q_HEADS, A_KV_HEADS = 6, 2
A_GRP = A_Q_HEADS // A_KV_HEADS
B_Q_HEADS, B_KV_HEADS = 4, 2
B_GRP = B_Q_HEADS // B_KV_HEADS
DIL_PAIRS = ((128, 1), (512, 4), (2048, 16))
C_GRP = 2
C_Q_HEADS = C_GRP * len(DIL_PAIRS)
CMP_BLOCK, CMP_STRIDE = 32, 16
SLC_BLOCK, SLC_TOPK = 64, 16
SLC_CHUNK = 4
SLC_QTILES = 1
NSA_WINDOW, SWA_WINDOW = 512, 128
FORCED_SCORE = 1.0e4
REL_BUCKETS, REL_MAX_EXACT, REL_MAX_DIST = 32, 16, 2048
SCALE = HEAD_DIM ** -0.5
LOG2E = math.log2(math.e)
LN2 = math.log(2.0)
MASK_WEIGHT = -2.0 ** 100
EPS = 1e-6
NEG = -1e30
VMEM_LIMIT = 56 * 1024 * 1024

COL_QA, COL_KCA, COL_VCA, COL_KSA, COL_VSA, COL_KWA, COL_VWA = 0, 6, 8, 10, 12, 14, 16
COL_QB, COL_KB, COL_VB, COL_QC, COL_KC, COL_VC = 18, 22, 24, 26, 32, 35
N_MAIN_BLOCKS = 38
N_MAIN = N_MAIN_BLOCKS * LANES
GATE_START = 2304
N_GATES = A_Q_HEADS * 3
PROJ_NORMED = ([True] * 6 + [False] * 4 + [True] * 2 + [False] * 2 + [True] * 2 + [False] * 2
               + [True] * 6 + [False] * 2 + [True] * 9 + [False] * 3)
DIL_F32_SLOT = {}
for _g, (_, _d) in enumerate(DIL_PAIRS):
    if _d > 1:
        _base = 4 * len([1 for _, _dd in DIL_PAIRS[:_g] if _dd > 1])
        DIL_F32_SLOT.update({COL_QC + C_GRP * _g: _base, COL_QC + C_GRP * _g + 1: _base + 1,
                             COL_KC + _g: _base + 2, COL_VC + _g: _base + 3})
CMP_F32_SLOT0 = len(DIL_F32_SLOT)
DIL_F32_SLOT.update({COL_KCA + _i: CMP_F32_SLOT0 + _i for _i in range(2 * A_KV_HEADS)})
DIL_F32_COLS = LANES * len(DIL_F32_SLOT)
DIL_TILES_PER_STEP = 8
ROW_TILE = 512
FFN_TILE = 512
PROJ_SUBTILE = 256
MIX_ROW_SPLIT = 4
OUT_SUBTILE = 512


def _bucket_starts():
    d = np.arange(0, 1 << 17)
    out = []
    for dt in (np.float32, np.float64):
        far = np.maximum(d, REL_MAX_EXACT).astype(dt)
        lb = REL_MAX_EXACT + (np.log(far / dt(REL_MAX_EXACT)) / dt(math.log(REL_MAX_DIST / REL_MAX_EXACT))
                              * dt(REL_BUCKETS - REL_MAX_EXACT)).astype(np.int64)
        out.append(np.where(d < REL_MAX_EXACT, d, np.minimum(lb, REL_BUCKETS - 1)))
    assert (out[0] == out[1]).all() and (np.diff(out[0]) >= 0).all()
    return [int(np.argmax(out[0] >= b)) for b in range(REL_BUCKETS)]


BUCKET_START = _bucket_starts()
SAT_DIST = BUCKET_START[REL_BUCKETS - 1]


def _cparams(sem, vmem=VMEM_LIMIT):
    return pltpu.CompilerParams(dimension_semantics=sem, vmem_limit_bytes=vmem)


def _bias_tile_kernel(tab_ref, o_ref, *, head0, koff, dscale, max_dist):
    h = pl.program_id(0) + head0
    t = pl.program_id(1)
    r = lax.broadcasted_iota(jnp.int32, (QBLK, LANES), 0)
    c = lax.broadcasted_iota(jnp.int32, (QBLK, LANES), 1)
    dist = t * QBLK + r - c - koff
    d = dist * dscale
    val = jnp.full((QBLK, LANES), tab_ref[REL_BUCKETS - 1, h] * LOG2E, F32)
    for b in range(REL_BUCKETS - 2, -1, -1):
        val = jnp.where(d < BUCKET_START[b + 1], tab_ref[b, h] * LOG2E, val)
    val = jnp.where(dist >= 0, val, NEG)
    if max_dist is not None:
        val = jnp.where(dist <= max_dist, val, NEG)
    o_ref[0, 0] = val


def _bias_tiles(rel_bias, head0, nheads, ntiles, *, koff=0, dscale=1, max_dist=None):
    kern = functools.partial(_bias_tile_kernel, head0=head0, koff=koff, dscale=dscale, max_dist=max_dist)
    return pl.pallas_call(
        kern,
        grid=(nheads, ntiles),
        in_specs=[pl.BlockSpec(memory_space=pltpu.SMEM)],
        out_specs=pl.BlockSpec((1, 1, QBLK, LANES), lambda h, t: (h, t, 0, 0)),
        out_shape=jax.ShapeDtypeStruct((nheads, ntiles, QBLK, LANES), F32),
        compiler_params=_cparams(("parallel", "parallel")),
        name="bias_tiles",
    )(rel_bias)


def _cmp_bias_kernel(tab_ref, o_ref, *, n_cmp):
    h = pl.program_id(0)
    ntile, _, width = o_ref.shape[1:]
    r = lax.broadcasted_iota(jnp.int32, (QBLK, 2 * width), 0)
    c = lax.broadcasted_iota(jnp.int32, (QBLK, 2 * width), 1)
    dist = r - CMP_STRIDE * (c - width) - (CMP_BLOCK - 1)
    base = jnp.full((QBLK, 2 * width), tab_ref[REL_BUCKETS - 1, h] * LOG2E, F32)
    for b in range(REL_BUCKETS - 2, -1, -1):
        base = jnp.where(dist < BUCKET_START[b + 1], tab_ref[b, h] * LOG2E, base)
    base = jnp.where(dist >= 0, base, NEG)
    col = lax.broadcasted_iota(jnp.int32, (QBLK, width), 1)
    per_tile = QBLK // CMP_STRIDE
    for i in range(ntile):
        tile = pltpu.roll(base, (width + per_tile * i) % (2 * width), 1)[:, :width]
        o_ref[0, i] = jnp.where(col < n_cmp, tile, NEG)


def _cmp_bias(rel_bias, nheads, ntiles, width, n_cmp):
    return pl.pallas_call(
        functools.partial(_cmp_bias_kernel, n_cmp=n_cmp),
        grid=(nheads,),
        in_specs=[pl.BlockSpec(memory_space=pltpu.SMEM)],
        out_specs=pl.BlockSpec((1, ntiles, QBLK, width), lambda h: (h, 0, 0, 0)),
        out_shape=jax.ShapeDtypeStruct((nheads, ntiles, QBLK, width), F32),
        compiler_params=_cparams(("parallel",)),
        name="cmp_bias",
    )(rel_bias)


def _rms(x, w):
    ms = jnp.mean(x * x, axis=-1, keepdims=True)
    return x * lax.rsqrt(ms + EPS) * w


def _wprep_kernel(w_ref, nxt_ref, o_ref):
    j = pl.program_id(1)
    gate_blk = GATE_START // LANES

    @pl.when((j < gate_blk) | (j == N_MAIN_BLOCKS))
    def _():
        o_ref[0] = w_ref[0]

    @pl.when((j >= gate_blk) & (j < N_MAIN_BLOCKS))
    def _():
        o_ref[0, :LANES - N_GATES] = w_ref[0, N_GATES:]
        o_ref[0, LANES - N_GATES:] = nxt_ref[0, :N_GATES]


def _wprep(wt):
    depth, n_in, d = wt.shape
    gate_blk = GATE_START // LANES
    last = n_in // LANES
    return pl.pallas_call(
        _wprep_kernel,
        grid=(depth, N_MAIN_BLOCKS + 1),
        in_specs=[pl.BlockSpec((1, LANES, d), lambda l, j: (l, jnp.where(j < N_MAIN_BLOCKS, j, gate_blk), 0)),
                  pl.BlockSpec((1, LANES, d), lambda l, j: (l, jnp.minimum(j + 1, last), 0))],
        out_specs=pl.BlockSpec((1, LANES, d), lambda l, j: (l, j, 0)),
        out_shape=jax.ShapeDtypeStruct((depth, N_MAIN + LANES, d), BF16),
        compiler_params=_cparams(("parallel", "parallel")),
        name="w_in_prep",
    )(wt, wt)


def _proj_kernel(x_ref, nw_ref, w_ref, gain_ref, o_ref, gate_ref, dil_ref):
    h = _rms(x_ref[...], nw_ref[...]).astype(BF16)
    for c0 in range(0, N_MAIN, PROJ_SUBTILE):
        width = min(PROJ_SUBTILE, N_MAIN - c0)
        acc = lax.dot_general(h, w_ref[c0:c0 + width, :], (((1,), (1,)), ((), ())), preferred_element_type=F32)
        for c in range(width // LANES):
            blk = c0 // LANES + c
            sl = slice(blk * LANES, (blk + 1) * LANES)
            y = acc[:, c * LANES:(c + 1) * LANES]
            if PROJ_NORMED[blk]:
                ms = jnp.mean(y * y, axis=-1, keepdims=True)
                y = y * lax.rsqrt(ms + EPS) * gain_ref[:, sl]
            o_ref[:, sl] = y.astype(o_ref.dtype)
            if blk in DIL_F32_SLOT:
                dil_ref[:, DIL_F32_SLOT[blk] * LANES:(DIL_F32_SLOT[blk] + 1) * LANES] = y
    gate_ref[...] = lax.dot_general(h, w_ref[N_MAIN:, :], (((1,), (1,)), ((), ())), preferred_element_type=F32)


def _proj(x2d, norm_w, w_all, layer, gain, *, tm):
    m, d = x2d.shape
    n = w_all.shape[1]
    return pl.pallas_call(
        _proj_kernel,
        grid=(m // tm,),
        in_specs=[
            pl.BlockSpec((tm, d), lambda i: (i, 0)),
            pl.BlockSpec((1, d), lambda i: (0, 0)),
            pl.BlockSpec((None, n, d), lambda i: (layer, 0, 0), pipeline_mode=pl.Buffered(1)),
            pl.BlockSpec((1, N_MAIN), lambda i: (0, 0)),
        ],
        out_specs=(pl.BlockSpec((tm, N_MAIN), lambda i: (i, 0)), pl.BlockSpec((tm, LANES), lambda i: (i, 0)),
                   pl.BlockSpec((tm, DIL_F32_COLS), lambda i: (i, 0))),
        out_shape=(jax.ShapeDtypeStruct((m, N_MAIN), BF16), jax.ShapeDtypeStruct((m, LANES), F32),
                   jax.ShapeDtypeStruct((m, DIL_F32_COLS), F32)),
        compiler_params=_cparams(("parallel",)),
        name="in_proj",
    )(x2d, norm_w, w_all, gain)


def _ffn_kernel(x_ref, nw_ref, wg_ref, wu_ref, wd_ref, o_ref, h_ref, acc_ref):
    f = pl.program_id(1)

    @pl.when(f == 0)
    def _():
        h_ref[...] = _rms(x_ref[...], nw_ref[...]).astype(BF16)
        acc_ref[...] = jnp.zeros_like(acc_ref)

    h = h_ref[...]
    g = jnp.dot(h, wg_ref[...], preferred_element_type=F32)
    u = jnp.dot(h, wu_ref[...], preferred_element_type=F32)
    a = (g * (1.0 / (1.0 + jnp.exp(-g))) * u).astype(BF16)
    acc_ref[...] += jnp.dot(a, wd_ref[...], preferred_element_type=F32)

    @pl.when(f == pl.num_programs(1) - 1)
    def _():
        o_ref[...] = x_ref[...] + acc_ref[...]


def _ffn(x2d, norm_w, wg, wu, wd, layer, *, tm, tf):
    m, d = x2d.shape
    dff = wg.shape[2]
    return pl.pallas_call(
        _ffn_kernel,
        grid=(m // tm, dff // tf),
        in_specs=[
            pl.BlockSpec((tm, d), lambda i, f: (i, 0)),
            pl.BlockSpec((1, d), lambda i, f: (0, 0)),
            pl.BlockSpec((None, d, tf), lambda i, f: (layer, 0, f)),
            pl.BlockSpec((None, d, tf), lambda i, f: (layer, 0, f)),
            pl.BlockSpec((None, tf, d), lambda i, f: (layer, f, 0)),
        ],
        out_specs=pl.BlockSpec((tm, d), lambda i, f: (i, 0)),
        out_shape=jax.ShapeDtypeStruct((m, d), F32),
        scratch_shapes=[pltpu.VMEM((tm, d), BF16), pltpu.VMEM((tm, d), F32)],
        compiler_params=_cparams(("parallel", "arbitrary")),
        name="ffn",
    )(x2d, norm_w, wg, wu, wd)


def _band_kernel(*refs, grp, n_off, nsub, has_sinks, with_lse):
    if has_sinks:
        sink_ref, q_ref, k_ref, v_ref, b_ref = refs[:5]
        outs = refs[5:]
    else:
        q_ref, k_ref, v_ref, b_ref = refs[:4]
        outs = refs[4:]
    o_ref = outs[0]
    g = pl.program_id(1)
    i0 = pl.program_id(2) * nsub
    tiles = {}

    def kv_tile(rel):
        if rel not in tiles:
            start = pl.multiple_of(jnp.maximum(i0 + rel, 0) * QBLK, QBLK)
            tiles[rel] = (k_ref[0, pl.ds(start, QBLK), :], v_ref[0, pl.ds(start, QBLK), :])
        return tiles[rel]

    def head_rows(x, h):
        return x[h * QBLK:(h + 1) * QBLK]

    kvs = [[kv_tile(j - off) for off in range(n_off)] for j in range(nsub)]
    tidxs = [[jnp.where(i0 + j - off >= 0, off, n_off) for off in range(n_off)] for j in range(nsub)]
    scores = []
    for j in range(nsub):
        rs = slice(j * QBLK, (j + 1) * QBLK)
        kcat = jnp.concatenate([t[0] for t in kvs[j]], axis=0)
        q = jnp.concatenate([q_ref[0, rs, h * HEAD_DIM:(h + 1) * HEAD_DIM] for h in range(grp)], axis=0)
        scores.append(lax.dot_general(q, kcat, (((1,), (1,)), ((), ())), preferred_element_type=F32))
    probs, stats = [], []
    for j in range(nsub):
        s = scores[j]
        st = [[head_rows(s, h)[:, off * QBLK:(off + 1) * QBLK] + b_ref[h, tidxs[j][off]] for off in range(n_off)]
              for h in range(grp)]
        m = jnp.max(jnp.concatenate([functools.reduce(jnp.maximum, r) for r in st], axis=0), axis=-1, keepdims=True)
        p = [[jnp.exp2(st[h][off] - head_rows(m, h)) for off in range(n_off)] for h in range(grp)]
        l = jnp.sum(jnp.concatenate([functools.reduce(jnp.add, r) for r in p], axis=0), axis=-1, keepdims=True)
        probs.append(jnp.concatenate([jnp.concatenate([t.astype(BF16) for t in r], axis=1) for r in p], axis=0))
        stats.append((m, l))
    for j in range(nsub):
        rs = slice(j * QBLK, (j + 1) * QBLK)
        m, l = stats[j]
        vcat = jnp.concatenate([t[1] for t in kvs[j]], axis=0)
        o = jnp.dot(probs[j], vcat, preferred_element_type=F32)
        den = l
        if has_sinks:
            sink = jnp.concatenate([jnp.full((QBLK, 1), sink_ref[g * grp + h] * LOG2E, F32) for h in range(grp)], axis=0)
            den = l + jnp.exp2(sink - m)
        o = o / den
        for h in range(grp):
            sl = slice(h * HEAD_DIM, (h + 1) * HEAD_DIM)
            o_ref[0, rs, sl] = head_rows(o, h).astype(o_ref.dtype)
            if with_lse:
                outs[1][0, rs, sl] = jnp.broadcast_to(head_rows(m * LN2 + jnp.log(l), h), (QBLK, HEAD_DIM))


def _band_attn(q_arr, k_arr, v_arr, bias, *, n_kv, grp, q_col, k_col, v_col, sinks=None, with_lse=False):
    n, L, _ = q_arr.shape
    n_off = bias.shape[1] - 1
    gw = grp * HEAD_DIM
    nsub = math.gcd(BAND_SUBBLOCKS, L // QBLK)
    rows = nsub * QBLK
    kern = functools.partial(_band_kernel, grp=grp, n_off=n_off, nsub=nsub, has_sinks=sinks is not None,
                             with_lse=with_lse)
    in_specs = [
        pl.BlockSpec((1, rows, gw), lambda b, g, i: (b, i, q_col + g)),
        pl.BlockSpec((1, L, HEAD_DIM), lambda b, g, i: (b, 0, k_col + g)),
        pl.BlockSpec((1, L, HEAD_DIM), lambda b, g, i: (b, 0, v_col + g)),
        pl.BlockSpec((grp, n_off + 1, QBLK, LANES), lambda b, g, i: (g, 0, 0, 0)),
    ]
    args = [q_arr, k_arr, v_arr, bias]
    if sinks is not None:
        in_specs = [pl.BlockSpec(memory_space=pltpu.SMEM)] + in_specs
        args = [sinks] + args
    o_spec = pl.BlockSpec((1, rows, gw), lambda b, g, i: (b, i, g))
    o_shape = jax.ShapeDtypeStruct((n, L, n_kv * gw), F32)
    return pl.pallas_call(
        kern,
        grid=(n, n_kv, L // rows),
        in_specs=in_specs,
        out_specs=(o_spec, o_spec) if with_lse else o_spec,
        out_shape=(o_shape, o_shape) if with_lse else o_shape,
        compiler_params=_cparams(("parallel", "parallel", "arbitrary")),
        name="band_attn",
    )(*args)


def _dil_kernel(*refs, dil, nres):
    q_refs, (k_ref, v_ref, b_ref) = refs[:C_GRP], refs[C_GRP:C_GRP + 3]
    o_refs, lse_refs = refs[C_GRP + 3:2 * C_GRP + 3], refs[2 * C_GRP + 3:]
    ls = k_ref.shape[1] // dil
    ntile = ls // QBLK
    r0 = pl.program_id(1) * nres
    units = [(rr, j) for rr in range(nres) for j in range(ntile)]

    def rows(rr, j):
        return pl.ds(r0 + rr + dil * QBLK * j, QBLK, stride=dil)

    kv = {u: (k_ref[0, rows(*u), :].astype(BF16), v_ref[0, rows(*u), :].astype(BF16)) for u in units}
    n_off = b_ref.shape[1] - 1
    offs = {(rr, j): [off for off in range(n_off) if j - off >= 0] for rr, j in units}
    scores = []
    for rr, j in units:
        kcat = jnp.concatenate([kv[(rr, j - off)][0] for off in offs[(rr, j)]], axis=0)
        q = jnp.concatenate([q_ref[0, rows(rr, j), :].astype(BF16) for q_ref in q_refs], axis=0)
        scores.append(lax.dot_general(q, kcat, (((1,), (1,)), ((), ())), preferred_element_type=F32))
    probs, stats = [], []
    for u, s in zip(units, scores):
        st = [[s[h * QBLK:(h + 1) * QBLK, n * QBLK:(n + 1) * QBLK] + b_ref[h, off] for n, off in enumerate(offs[u])]
              for h in range(C_GRP)]
        m = jnp.max(jnp.concatenate([functools.reduce(jnp.maximum, r) for r in st], axis=0), axis=-1, keepdims=True)
        p = [[jnp.exp2(t - m[h * QBLK:(h + 1) * QBLK]) for t in r] for h, r in enumerate(st)]
        l = jnp.sum(jnp.concatenate([functools.reduce(jnp.add, r) for r in p], axis=0), axis=-1, keepdims=True)
        probs.append(jnp.concatenate([jnp.concatenate([t.astype(BF16) for t in r], axis=1) for r in p], axis=0))
        stats.append((m, l))
    for (rr, j), pb, (m, l) in zip(units, probs, stats):
        vcat = jnp.concatenate([kv[(rr, j - off)][1] for off in offs[(rr, j)]], axis=0)
        o = jnp.dot(pb, vcat, preferred_element_type=F32) / l
        lse = m * LN2 + jnp.log(l)
        for h in range(C_GRP):
            o_refs[h][0, rows(rr, j), :] = o[h * QBLK:(h + 1) * QBLK]
            lse_refs[h][0, rows(rr, j), :] = jnp.broadcast_to(lse[h * QBLK:(h + 1) * QBLK], (QBLK, HEAD_DIM))


def _dil_attn(x3, bias, *, col, dil):
    B, T, _ = x3.shape
    ntile = T // dil // QBLK
    nres = max(1, min(dil, DIL_TILES_PER_STEP // ntile))
    spec = pl.BlockSpec((1, T, HEAD_DIM), lambda b, r: (b, 0, 0))
    shape = jax.ShapeDtypeStruct((B, T, HEAD_DIM), F32)
    outs = pl.pallas_call(
        functools.partial(_dil_kernel, dil=dil, nres=nres),
        grid=(B, dil // nres),
        in_specs=[pl.BlockSpec((1, T, HEAD_DIM), functools.partial(lambda b, r, c: (b, 0, c), c=col + c))
                  for c in range(C_GRP + 2)]
        + [pl.BlockSpec((C_GRP, bias.shape[1], QBLK, LANES), lambda b, r: (0, 0, 0, 0))],
        out_specs=(spec,) * (2 * C_GRP),
        out_shape=(shape,) * (2 * C_GRP),
        compiler_params=_cparams(("parallel", "arbitrary")),
        name="dilated_attn",
    )(*([x3] * (C_GRP + 2)), bias)
    return outs[:C_GRP], outs[C_GRP:]


def _compress_kernel(x_ref, pe_ref, w1_ref, w2_ref, gain_ref, o_ref, *, n_cmp):
    half = w1_ref.shape[1] // 2
    rows = x_ref.shape[1] // CMP_STRIDE
    xr = jnp.concatenate([x_ref[0, pl.ds(t, rows, stride=CMP_STRIDE), :].astype(BF16) for t in range(CMP_STRIDE)],
                         axis=1)
    y0 = jnp.dot(xr, w1_ref[0, :half], preferred_element_type=F32)
    y1 = jnp.dot(xr, w1_ref[0, half:], preferred_element_type=F32)
    pe = jnp.dot(pe_ref[0], w1_ref[0], preferred_element_type=F32)[0:1]
    c = y0 + pltpu.roll(y1, rows - 1, 0) + pe
    gl = 0.5 * c * (1.0 + jnp.tanh(math.sqrt(2.0 / math.pi) * (c + 0.044715 * (c * c * c))))
    out = jnp.dot(gl.astype(BF16), w2_ref[0], preferred_element_type=F32)
    out = jnp.where(pl.program_id(0) == 0, _rms(out, gain_ref[...]), out)
    valid = lax.broadcasted_iota(jnp.int32, out.shape, 0) < n_cmp
    o_ref[0, 0, 0] = jnp.where(valid, out, 0.0).astype(o_ref.dtype)


def _compress(x3, pe, w1, w2, gain, n_cmp):
    B, T, _ = x3.shape
    rows = T // CMP_STRIDE
    wide = CMP_BLOCK * HEAD_DIM
    return pl.pallas_call(
        functools.partial(_compress_kernel, n_cmp=n_cmp),
        grid=(2, B, A_KV_HEADS),
        in_specs=[
            pl.BlockSpec((1, T, HEAD_DIM), lambda s, b, g: (b, 0, CMP_F32_SLOT0 + s * A_KV_HEADS + g)),
            pl.BlockSpec((1, 8, wide), lambda s, b, g: (s, 0, 0)),
            pl.BlockSpec((1, wide, HEAD_DIM), lambda s, b, g: (s, 0, 0)),
            pl.BlockSpec((1, HEAD_DIM, HEAD_DIM), lambda s, b, g: (s, 0, 0)),
            pl.BlockSpec((1, HEAD_DIM), lambda s, b, g: (0, 0)),
        ],
        out_specs=pl.BlockSpec((1, 1, 1, rows, HEAD_DIM), lambda s, b, g: (s, b, g, 0, 0)),
        out_shape=jax.ShapeDtypeStruct((2, B, A_KV_HEADS, rows, HEAD_DIM), BF16),
        compiler_params=_cparams(("arbitrary", "arbitrary", "arbitrary")),
        name="nsa_compress",
    )(x3, pe, w1, w2, gain)


def _cmp_select_kernel(q_ref, kc_ref, vc_ref, b_ref, ovl_ref, o_ref, sel_ref, *, n_slc, nsub):
    i0 = pl.program_id(2) * nsub
    rows = nsub * QBLK
    kc = kc_ref[0, 0, 0]
    vc = vc_ref[0, 0, 0]
    q_all = jnp.concatenate([q_ref[0, :, h * HEAD_DIM:(h + 1) * HEAD_DIM] for h in range(A_GRP)], axis=0)
    s_all = lax.dot_general(q_all, kc, (((1,), (1,)), ((), ())), preferred_element_type=F32)
    ps = []
    for h in range(A_GRP):
        s = s_all[h * rows:(h + 1) * rows] + b_ref[h]
        m = jnp.max(s, axis=-1, keepdims=True)
        e = jnp.exp2(s - jnp.where(m > 0.5 * NEG, m, 0.0))
        den = jnp.sum(e, axis=-1, keepdims=True)
        ps.append((e / jnp.where(den > 0, den, 1.0)).astype(BF16))
    r_all = jnp.dot(jnp.concatenate(ps, axis=0), jnp.concatenate([vc, ovl_ref[...]], axis=1),
                    preferred_element_type=F32)
    imp = jnp.zeros((rows, LANES), F32)
    for h in range(A_GRP):
        o_ref[0, :, h * HEAD_DIM:(h + 1) * HEAD_DIM] = r_all[h * rows:(h + 1) * rows, :HEAD_DIM]
        imp = imp + r_all[h * rows:(h + 1) * rows, HEAD_DIM:]
    t = i0 * QBLK + lax.broadcasted_iota(jnp.int32, (rows, LANES), 0)
    blk = lax.broadcasted_iota(jnp.int32, (rows, LANES), 1)
    cur = t // SLC_BLOCK
    imp = jnp.where(blk == 0, FORCED_SCORE, imp)
    imp = jnp.where(blk == cur, FORCED_SCORE, imp)
    imp = jnp.where(blk == cur - 1, FORCED_SCORE, imp)
    imp = jnp.where(blk * SLC_BLOCK <= t, imp, NEG)
    imp = jnp.where(blk < n_slc, imp, 2.0 * NEG)
    imp_t = jnp.concatenate([imp[j * QBLK:(j + 1) * QBLK].T for j in range(nsub)], axis=1)
    ngrp = -(-n_slc // SUBLANES)
    cands = [imp_t[r * SUBLANES:(r + 1) * SUBLANES] for r in range(ngrp)]
    row_id = lax.broadcasted_iota(jnp.int32, (SUBLANES, rows), 0)
    ranks = [jnp.zeros((SUBLANES, rows), F32) for _ in range(ngrp)]
    for j in range(n_slc):
        other = imp_t[j:j + 1, :]
        for r in range(ngrp):
            if r * SUBLANES > j:
                beats = other >= cands[r]
            elif (r + 1) * SUBLANES <= j:
                beats = other > cands[r]
            else:
                beats = jnp.where(row_id > j - r * SUBLANES, jnp.where(other >= cands[r], 1.0, 0.0),
                                  jnp.where(other > cands[r], 1.0, 0.0)) > 0.5
            ranks[r] = ranks[r] + jnp.where(beats, 1.0, 0.0)
    unsel = [jnp.where(r < float(min(SLC_TOPK, n_slc)), 0.0, 1.0) for r in ranks]
    if ngrp * SUBLANES > n_slc:
        unsel[-1] = jnp.where(row_id < n_slc - (ngrp - 1) * SUBLANES, unsel[-1], 0.0)
    sel_t = jnp.concatenate(unsel, axis=0)
    if ngrp * SUBLANES < LANES:
        sel_t = jnp.concatenate([sel_t, jnp.zeros((LANES - ngrp * SUBLANES, rows), F32)], axis=0)
    for j in range(nsub):
        sel_ref[0, 0, j * QBLK:(j + 1) * QBLK, :] = sel_t[:, j * QBLK:(j + 1) * QBLK].T.astype(sel_ref.dtype)


def _cmp_select(proj3, kvc, bias_c, ovl, n_slc):
    B, T, _ = proj3.shape
    ncp = kvc.shape[3]
    nsub = math.gcd(CMP_SUBBLOCKS, T // QBLK)
    rows = nsub * QBLK
    return pl.pallas_call(
        functools.partial(_cmp_select_kernel, n_slc=n_slc, nsub=nsub),
        grid=(B, A_KV_HEADS, T // rows),
        in_specs=[
            pl.BlockSpec((1, rows, A_GRP * HEAD_DIM), lambda b, g, i: (b, i, g)),
            pl.BlockSpec((1, 1, 1, ncp, HEAD_DIM), lambda b, g, i: (0, b, g, 0, 0)),
            pl.BlockSpec((1, 1, 1, ncp, HEAD_DIM), lambda b, g, i: (1, b, g, 0, 0)),
            pl.BlockSpec((A_GRP, rows, ncp), lambda b, g, i: (g, i, 0)),
            pl.BlockSpec((ncp, LANES), lambda b, g, i: (0, 0)),
        ],
        out_specs=(
            pl.BlockSpec((1, rows, A_GRP * HEAD_DIM), lambda b, g, i: (b, i, g)),
            pl.BlockSpec((1, 1, rows, LANES), lambda b, g, i: (b, g, i, 0)),
        ),
        out_shape=(
            jax.ShapeDtypeStruct((B, T, A_Q_HEADS * HEAD_DIM), F32),
            jax.ShapeDtypeStruct((B, A_KV_HEADS, T, LANES), BF16),
        ),
        compiler_params=_cparams(("parallel", "parallel", "arbitrary")),
        name="nsa_cmp_select",
    )(proj3, kvc, kvc, bias_c, ovl)


def _slc_kernel(q_ref, k_ref, v_ref, sel_ref, e_ref, b_ref, o_ref, *scratch, n_tiles, nq):
    ig = pl.program_id(1)
    chains = [(g, a) for g in range(A_KV_HEADS) for a in range(nq)]
    m_refs, l_refs, acc_refs = scratch[0::3], scratch[1::3], scratch[2::3]
    qs, sels = [], []
    for ci, (g, a) in enumerate(chains):
        m_refs[ci][...] = jnp.full(m_refs[ci].shape, NEG, F32)
        l_refs[ci][...] = jnp.zeros(l_refs[ci].shape, F32)
        acc_refs[ci][...] = jnp.zeros(acc_refs[ci].shape, F32)
        rs = slice(a * QBLK, (a + 1) * QBLK)
        qs.append(jnp.concatenate([q_ref[0, rs, (g * A_GRP + h) * HEAD_DIM:(g * A_GRP + h + 1) * HEAD_DIM]
                                   for h in range(A_GRP)], axis=0))
        sels.append(sel_ref[0, g, rs, :])
    kw = SLC_CHUNK * QBLK

    def body(c, carry):
        start = pl.multiple_of(c * kw, kw)
        base = ig * nq - c * SLC_CHUNK + 1
        tidx = {d: jnp.clip(base + d, 0, n_tiles - 1) for d in range(-(SLC_CHUNK - 1), nq)}
        kts = [k_ref[0, pl.ds(start, kw), g * HEAD_DIM:(g + 1) * HEAD_DIM] for g in range(A_KV_HEADS)]
        vts = [v_ref[0, pl.ds(start, kw), g * HEAD_DIM:(g + 1) * HEAD_DIM] for g in range(A_KV_HEADS)]
        ech = e_ref[:, pl.ds(start, kw)]
        ss = [lax.dot_general(qs[ci], kts[g], (((1,), (1,)), ((), ())), preferred_element_type=F32)
              for ci, (g, a) in enumerate(chains)]
        madds = [jnp.dot(sels[ci], ech, preferred_element_type=F32) for ci in range(len(chains))]
        pbs, alphas = [], []
        for ci, (g, a) in enumerate(chains):
            madd = madds[ci]
            s = ss[ci]
            rows = [[s[h * QBLK:(h + 1) * QBLK, j * QBLK:(j + 1) * QBLK] + b_ref[g * A_GRP + h, tidx[a - j]]
                     + madd[:, j * QBLK:(j + 1) * QBLK] for j in range(SLC_CHUNK)] for h in range(A_GRP)]
            tile_max = jnp.concatenate([functools.reduce(jnp.maximum, r) for r in rows], axis=0)
            m_old = m_refs[ci][...]
            m_new = jnp.maximum(m_old, jnp.max(tile_max, axis=-1, keepdims=True))
            alpha = jnp.exp2(m_old - m_new)
            p = [[jnp.exp2(t - m_new[h * QBLK:(h + 1) * QBLK]) for t in r] for h, r in enumerate(rows)]
            tile_sum = jnp.concatenate([functools.reduce(jnp.add, r) for r in p], axis=0)
            l_refs[ci][...] = alpha * l_refs[ci][...] + jnp.sum(tile_sum, axis=-1, keepdims=True)
            pbs.append(jnp.concatenate([jnp.concatenate([t.astype(BF16) for t in r], axis=1) for r in p], axis=0))
            alphas.append(alpha)
            m_refs[ci][...] = m_new
        for ci, (g, a) in enumerate(chains):
            acc_refs[ci][...] = alphas[ci] * acc_refs[ci][...] + jnp.dot(pbs[ci], vts[g], preferred_element_type=F32)
        return carry

    lax.fori_loop(0, (ig * nq + nq - 1) // SLC_CHUNK + 1, body, 0)
    for ci, (g, a) in enumerate(chains):
        out = acc_refs[ci][...] / l_refs[ci][...]
        for h in range(A_GRP):
            col = (g * A_GRP + h) * HEAD_DIM
            o_ref[0, a * QBLK:(a + 1) * QBLK, col:col + HEAD_DIM] = out[h * QBLK:(h + 1) * QBLK]


def _slc_attn(proj3, sel, expand, bias):
    B, T, _ = proj3.shape
    n_tiles = bias.shape[1]
    nq = SLC_QTILES
    qrows = nq * QBLK
    assert SLC_CHUNK % nq == 0 and T % (SLC_CHUNK * QBLK) == 0
    chain_rows = A_GRP * QBLK
    n_chains = A_KV_HEADS * nq
    return pl.pallas_call(
        functools.partial(_slc_kernel, n_tiles=n_tiles, nq=nq),
        grid=(B, T // qrows),
        in_specs=[
            pl.BlockSpec((1, qrows, A_Q_HEADS * HEAD_DIM), lambda b, i: (b, i, COL_QA // A_Q_HEADS)),
            pl.BlockSpec((1, T, A_KV_HEADS * HEAD_DIM), lambda b, i: (b, 0, COL_KSA // A_KV_HEADS)),
            pl.BlockSpec((1, T, A_KV_HEADS * HEAD_DIM), lambda b, i: (b, 0, COL_VSA // A_KV_HEADS)),
            pl.BlockSpec((1, A_KV_HEADS, qrows, LANES), lambda b, i: (b, 0, i, 0)),
            pl.BlockSpec((LANES, T), lambda b, i: (0, 0)),
            pl.BlockSpec((A_Q_HEADS, n_tiles, QBLK, LANES), lambda b, i: (0, 0, 0, 0)),
        ],
        out_specs=pl.BlockSpec((1, qrows, A_Q_HEADS * HEAD_DIM), lambda b, i: (b, i, 0)),
        out_shape=jax.ShapeDtypeStruct((B, T, A_Q_HEADS * HEAD_DIM), F32),
        scratch_shapes=[pltpu.VMEM((chain_rows, 1), F32), pltpu.VMEM((chain_rows, 1), F32),
                        pltpu.VMEM((chain_rows, HEAD_DIM), F32)] * n_chains,
        compiler_params=_cparams(("parallel", "arbitrary")),
        name="nsa_slc_attn",
    )(proj3, proj3, proj3, sel, expand, bias)


def _mix_outproj_kernel(x_ref, gate_ref, ocmp_ref, oslc_ref, owin_ref, ob_ref, *rest):
    oc_refs, lse_refs = rest[:C_Q_HEADS], rest[C_Q_HEADS:2 * C_Q_HEADS]
    w_ref, o_ref, mix_ref = rest[2 * C_Q_HEADS:]
    tm = x_ref.shape[0]
    nsplit = MIX_ROW_SPLIT
    rows = tm // nsplit
    for part in range(nsplit):
        rs = slice(part * rows, (part + 1) * rows)
        gate = 1.0 / (1.0 + jnp.exp(-gate_ref[rs, :]))
        for h in range(A_Q_HEADS):
            sl = slice(h * HEAD_DIM, (h + 1) * HEAD_DIM)
            o = (gate[:, 3 * h:3 * h + 1] * ocmp_ref[rs, sl] + gate[:, 3 * h + 1:3 * h + 2] * oslc_ref[rs, sl]
                 + gate[:, 3 * h + 2:3 * h + 3] * owin_ref[rs, sl])
            mix_ref[rs, sl] = o.astype(mix_ref.dtype)
        base = A_Q_HEADS * HEAD_DIM
        width = B_Q_HEADS * HEAD_DIM
        mix_ref[rs, base:base + width] = ob_ref[rs, :].astype(mix_ref.dtype)
        base += width
        for hh in range(C_GRP):
            heads = [C_GRP * gidx + hh for gidx in range(len(DIL_PAIRS))]
            lses = [lse_refs[c][rs, :] for c in heads]
            mx = functools.reduce(jnp.maximum, lses)
            ws = [jnp.exp(x - mx) for x in lses]
            tot = functools.reduce(jnp.add, ws)
            for c, w in zip(heads, ws):
                mix_ref[rs, base + c * HEAD_DIM:base + (c + 1) * HEAD_DIM] = (oc_refs[c][rs, :] * (w / tot)).astype(mix_ref.dtype)
    for part in range(nsplit):
        rs = slice(part * rows, (part + 1) * rows)
        a = mix_ref[rs, :]
        for c0 in range(0, o_ref.shape[1], OUT_SUBTILE):
            sl = slice(c0, c0 + OUT_SUBTILE)
            o_ref[rs, sl] = x_ref[rs, sl] + jnp.dot(a, w_ref[:, sl], preferred_element_type=F32)


def _mix_outproj(x2d, gates, o_cmp, o_slc, o_win, o_b, o_cs, lses, w_all, layer, *, tm):
    m, d = x2d.shape
    k = w_all.shape[1]
    row = lambda w: pl.BlockSpec((tm, w), lambda i: (i, 0))
    head = lambda c: pl.BlockSpec((tm, HEAD_DIM), lambda i: (i, c))
    full = [x2d, gates, o_cmp, o_slc, o_win, o_b]
    return pl.pallas_call(
        _mix_outproj_kernel,
        grid=(m // tm,),
        in_specs=[row(a.shape[1]) for a in full] + [head(c) for _, c in o_cs + lses]
        + [pl.BlockSpec((None, k, d), lambda i: (layer, 0, 0), pipeline_mode=pl.Buffered(1))],
        out_specs=row(d),
        out_shape=jax.ShapeDtypeStruct((m, d), F32),
        scratch_shapes=[pltpu.VMEM((tm, k), BF16)],
        compiler_params=_cparams(("parallel",)),
        name="mix_out_proj",
    )(*full, *[a for a, _ in o_cs + lses], w_all)


def _mixers_outproj(x2d, proj3, gates, dil32, layer_params, tables, w_out_all, layer, *, tm):
    B, T, _ = proj3.shape
    cmp_pe, cmp_w1, cmp_w2, kc_gain, sinks = layer_params
    n_cmp = (T - CMP_BLOCK) // CMP_STRIDE + 1
    n_slc = T // SLC_BLOCK
    pe = jnp.broadcast_to(cmp_pe.reshape(2, 1, CMP_BLOCK * HEAD_DIM), (2, 8, CMP_BLOCK * HEAD_DIM)).astype(BF16)
    kvc = _compress(dil32, pe, cmp_w1.astype(BF16), cmp_w2.astype(BF16), kc_gain.reshape(1, HEAD_DIM), n_cmp)

    o_cmp, sel = _cmp_select(proj3, kvc, tables["bias_c"], tables["ovl"], n_slc)
    o_slc = _slc_attn(proj3, sel, tables["expand"], tables["bias_slc"])
    o_win = _band_attn(proj3, proj3, proj3, tables["bias_win"], n_kv=A_KV_HEADS, grp=A_GRP,
                       q_col=COL_QA // A_GRP, k_col=COL_KWA, v_col=COL_VWA)
    o_b = _band_attn(proj3, proj3, proj3, tables["bias_b"], n_kv=B_KV_HEADS, grp=B_GRP,
                     q_col=COL_QB // B_GRP, k_col=COL_KB, v_col=COL_VB, sinks=sinks)
    o_cs, lses = [], []
    for gidx, (_, dil) in enumerate(DIL_PAIRS):
        if dil == 1:
            o, lse = _band_attn(proj3, proj3, proj3, tables["bias_c%d" % gidx], n_kv=1, grp=C_GRP,
                                q_col=COL_QC // C_GRP + gidx, k_col=COL_KC + gidx, v_col=COL_VC + gidx,
                                with_lse=True)
            o_cs += [(o.reshape(B * T, -1), h) for h in range(C_GRP)]
            lses += [(lse.reshape(B * T, -1), h) for h in range(C_GRP)]
        else:
            os_, ls_ = _dil_attn(dil32, tables["bias_c%d" % gidx], col=DIL_F32_SLOT[COL_QC + C_GRP * gidx], dil=dil)
            o_cs += [(o.reshape(B * T, -1), 0) for o in os_]
            lses += [(lse.reshape(B * T, -1), 0) for lse in ls_]
    return _mix_outproj(x2d, gates, o_cmp.reshape(B * T, -1), o_slc.reshape(B * T, -1), o_win.reshape(B * T, -1),
                        o_b.reshape(B * T, -1), o_cs, lses, w_out_all, layer, tm=tm)


def _build_tables(rel_bias, T):
    n_cmp = (T - CMP_BLOCK) // CMP_STRIDE + 1
    n_slc = T // SLC_BLOCK
    ncp = T // CMP_STRIDE
    nq = T // QBLK
    tables = {}
    tables["bias_c"] = _cmp_bias(rel_bias, A_Q_HEADS, nq, ncp, n_cmp).reshape(A_Q_HEADS, T, ncp)
    n_sat = -(-(SAT_DIST + QBLK - 1) // QBLK) + 1
    tables["bias_slc"] = _bias_tiles(rel_bias, 0, A_Q_HEADS, min(n_sat, nq) + 1, koff=QBLK)
    win = NSA_WINDOW - 1
    tables["bias_win"] = _bias_tiles(rel_bias, 0, A_Q_HEADS, -(-win // QBLK) + 2, max_dist=win)
    swa = SWA_WINDOW - 1
    tables["bias_b"] = _bias_tiles(rel_bias, A_Q_HEADS, B_Q_HEADS, -(-swa // QBLK) + 2, max_dist=swa)
    for gidx, (w, dil) in enumerate(DIL_PAIRS):
        md = w // dil
        tables["bias_c%d" % gidx] = _bias_tiles(rel_bias, A_Q_HEADS + B_Q_HEADS + C_GRP * gidx, C_GRP,
                                                -(-md // QBLK) + 2, dscale=dil, max_dist=md)
    c0 = np.arange(ncp)[:, None] * CMP_STRIDE
    s0 = np.arange(LANES)[None, :] * SLC_BLOCK
    ovl = np.clip(np.minimum(c0 + CMP_BLOCK, s0 + SLC_BLOCK) - np.maximum(c0, s0), 0, None) / CMP_BLOCK
    ovl = ovl * (np.arange(ncp)[:, None] < n_cmp) * (np.arange(LANES)[None, :] < n_slc)
    tables["ovl"] = jnp.asarray(ovl, BF16)
    member = (np.arange(T)[None, :] // SLC_BLOCK) == np.arange(LANES)[:, None]
    tables["expand"] = jnp.asarray(member * MASK_WEIGHT, BF16)
    return tables


def _proj_gain(g):
    ones = jnp.ones((HEAD_DIM,), F32)
    spec = [(g[0] * (SCALE * LOG2E), 6), (ones, 4), (g[2], 2), (ones, 2), (g[3], 2), (ones, 2),
            (g[4] * (SCALE * LOG2E), 4), (g[5], 2), (ones, 2), (g[6] * (SCALE * LOG2E), 6), (g[7], 3), (ones, 3)]
    assert sum(n for _, n in spec) == N_MAIN_BLOCKS
    return jnp.concatenate([jnp.tile(v, n) for v, n in spec]).reshape(1, N_MAIN)


def kernel(x, norm_attn, w_in, qk_gain, cmp_pe, cmp_w1, cmp_w2, sinks, rel_bias, w_out, norm_ffn, w_gate, w_up, w_down):
    B, T, D = x.shape
    depth = w_in.shape[0]
    tables = _build_tables(rel_bias, T)
    x2 = x.reshape(B * T, D)
    w_all = _wprep(jnp.transpose(w_in, (0, 2, 1)).astype(BF16))
    w_out_b, w_gate_b, w_up_b, w_down_b = (w.astype(BF16) for w in (w_out, w_gate, w_up, w_down))
    tm = min(ROW_TILE, B * T)
    for l in range(depth):
        proj, gates, dil32 = _proj(x2, norm_attn[l].reshape(1, D), w_all, l, _proj_gain(qk_gain[l]), tm=tm)
        x2 = _mixers_outproj(x2, proj.reshape(B, T, N_MAIN), gates, dil32.reshape(B, T, DIL_F32_COLS),
                             (cmp_pe[l], cmp_w1[l], cmp_w2[l], qk_gain[l][1], sinks[l]), tables, w_out_b, l, tm=tm)
        x2 = _ffn(x2, norm_ffn[l].reshape(1, D), w_gate_b, w_up_b, w_down_b, l, tm=tm, tf=FFN_TILE)
    return x2.reshape(B, T, D)
```

```python
import functools
import math

import numpy as np
import jax
import jax.numpy as jnp
from jax import lax
from jax.experimental import pallas as pl
from jax.experimental.pallas import tpu as pltpu

F32 = jnp.float32
BF16 = jnp.bfloat16

HEAD_DIM = 128
MIXER_OUT = jnp.bfloat16
LANES = 128
SUBLANES = 8
QBLK = 128
BAND_SUBBLOCKS = 8
CMP_SUBBLOCKS = 4
A_Q_HEADS, A_KV_HEADS = 6, 2
A_GRP = A_Q_HEADS // A_KV_HEADS
B_Q_HEADS, B_KV_HEADS = 4, 2
B_GRP = B_Q_HEADS // B_KV_HEADS
DIL_PAIRS = ((128, 1), (512, 4), (2048, 16))
C_GRP = 2
C_Q_HEADS = C_GRP * len(DIL_PAIRS)
CMP_BLOCK, CMP_STRIDE = 32, 16
SLC_BLOCK, SLC_TOPK = 64, 16
SLC_CHUNK = 4
SLC_QTILES = 1
NSA_WINDOW, SWA_WINDOW = 512, 128
FORCED_SCORE = 1.0e4
REL_BUCKETS, REL_MAX_EXACT, REL_MAX_DIST = 32, 16, 2048
SCALE = HEAD_DIM ** -0.5
LOG2E = math.log2(math.e)
LN2 = math.log(2.0)
MASK_WEIGHT = -2.0 ** 100
EPS = 1e-6
NEG = -1e30
VMEM_LIMIT = 56 * 1024 * 1024

COL_QA, COL_KCA, COL_VCA, COL_KSA, COL_VSA, COL_KWA, COL_VWA = 0, 6, 8, 10, 12, 14, 16
COL_QB, COL_KB, COL_VB, COL_QC, COL_KC, COL_VC = 18, 22, 24, 26, 32, 35
N_MAIN_BLOCKS = 38
N_MAIN = N_MAIN_BLOCKS * LANES
GATE_START = 2304
N_GATES = A_Q_HEADS * 3
PROJ_NORMED = ([True] * 6 + [False] * 4 + [True] * 2 + [False] * 2 + [True] * 2 + [False] * 2
               + [True] * 6 + [False] * 2 + [True] * 9 + [False] * 3)
DIL_F32_SLOT = {}
for _g, (_, _d) in enumerate(DIL_PAIRS):
    if _d > 1:
        _base = 4 * len([1 for _, _dd in DIL_PAIRS[:_g] if _dd > 1])
        DIL_F32_SLOT.update({COL_QC + C_GRP * _g: _base, COL_QC + C_GRP * _g + 1: _base + 1,
                             COL_KC + _g: _base + 2, COL_VC + _g: _base + 3})
CMP_F32_SLOT0 = len(DIL_F32_SLOT)
DIL_F32_SLOT.update({COL_KCA + _i: CMP_F32_SLOT0 + _i for _i in range(2 * A_KV_HEADS)})
DIL_F32_COLS = LANES * len(DIL_F32_SLOT)
DIL_TILES_PER_STEP = 8
ROW_TILE = 512
FFN_TILE = 512
PROJ_SUBTILE = 256
MIX_ROW_SPLIT = 4
OUT_SUBTILE = 512


def _bucket_starts():
    d = np.arange(0, 1 << 17)
    out = []
    for dt in (np.float32, np.float64):
        far = np.maximum(d, REL_MAX_EXACT).astype(dt)
        lb = REL_MAX_EXACT + (np.log(far / dt(REL_MAX_EXACT)) / dt(math.log(REL_MAX_DIST / REL_MAX_EXACT))
                              * dt(REL_BUCKETS - REL_MAX_EXACT)).astype(np.int64)
        out.append(np.where(d < REL_MAX_EXACT, d, np.minimum(lb, REL_BUCKETS - 1)))
    assert (out[0] == out[1]).all() and (np.diff(out[0]) >= 0).all()
    return [int(np.argmax(out[0] >= b)) for b in range(REL_BUCKETS)]


BUCKET_START = _bucket_starts()
SAT_DIST = BUCKET_START[REL_BUCKETS - 1]


def _cparams(sem, vmem=VMEM_LIMIT):
    return pltpu.CompilerParams(dimension_semantics=sem, vmem_limit_bytes=vmem)


def _bias_tile_kernel(tab_ref, o_ref, *, head0, koff, dscale, max_dist):
    h = pl.program_id(0) + head0
    t = pl.program_id(1)
    r = lax.broadcasted_iota(jnp.int32, (QBLK, LANES), 0)
    c = lax.broadcasted_iota(jnp.int32, (QBLK, LANES), 1)
    dist = t * QBLK + r - c - koff
    d = dist * dscale
    val = jnp.full((QBLK, LANES), tab_ref[REL_BUCKETS - 1, h] * LOG2E, F32)
    for b in range(REL_BUCKETS - 2, -1, -1):
        val = jnp.where(d < BUCKET_START[b + 1], tab_ref[b, h] * LOG2E, val)
    val = jnp.where(dist >= 0, val, NEG)
    if max_dist is not None:
        val = jnp.where(dist <= max_dist, val, NEG)
    o_ref[0, 0] = val


def _bias_tiles(rel_bias, head0, nheads, ntiles, *, koff=0, dscale=1, max_dist=None):
    kern = functools.partial(_bias_tile_kernel, head0=head0, koff=koff, dscale=dscale, max_dist=max_dist)
    return pl.pallas_call(
        kern,
        grid=(nheads, ntiles),
        in_specs=[pl.BlockSpec(memory_space=pltpu.SMEM)],
        out_specs=pl.BlockSpec((1, 1, QBLK, LANES), lambda h, t: (h, t, 0, 0)),
        out_shape=jax.ShapeDtypeStruct((nheads, ntiles, QBLK, LANES), F32),
        compiler_params=_cparams(("parallel", "parallel")),
        name="bias_tiles",
    )(rel_bias)


def _cmp_bias_kernel(tab_ref, o_ref, *, n_cmp):
    h = pl.program_id(0)
    ntile, _, width = o_ref.shape[1:]
    r = lax.broadcasted_iota(jnp.int32, (QBLK, 2 * width), 0)
    c = lax.broadcasted_iota(jnp.int32, (QBLK, 2 * width), 1)
    dist = r - CMP_STRIDE * (c - width) - (CMP_BLOCK - 1)
    base = jnp.full((QBLK, 2 * width), tab_ref[REL_BUCKETS - 1, h] * LOG2E, F32)
    for b in range(REL_BUCKETS - 2, -1, -1):
        base = jnp.where(dist < BUCKET_START[b + 1], tab_ref[b, h] * LOG2E, base)
    base = jnp.where(dist >= 0, base, NEG)
    col = lax.broadcasted_iota(jnp.int32, (QBLK, width), 1)
    per_tile = QBLK // CMP_STRIDE
    for i in range(ntile):
        tile = pltpu.roll(base, (width + per_tile * i) % (2 * width), 1)[:, :width]
        o_ref[0, i] = jnp.where(col < n_cmp, tile, NEG)


def _cmp_bias(rel_bias, nheads, ntiles, width, n_cmp):
    return pl.pallas_call(
        functools.partial(_cmp_bias_kernel, n_cmp=n_cmp),
        grid=(nheads,),
        in_specs=[pl.BlockSpec(memory_space=pltpu.SMEM)],
        out_specs=pl.BlockSpec((1, ntiles, QBLK, width), lambda h: (h, 0, 0, 0)),
        out_shape=jax.ShapeDtypeStruct((nheads, ntiles, QBLK, width), F32),
        compiler_params=_cparams(("parallel",)),
        name="cmp_bias",
    )(rel_bias)


def _rms(x, w):
    ms = jnp.mean(x * x, axis=-1, keepdims=True)
    return x * lax.rsqrt(ms + EPS) * w


def _wprep_kernel(w_ref, nxt_ref, o_ref):
    j = pl.program_id(1)
    gate_blk = GATE_START // LANES

    @pl.when((j < gate_blk) | (j == N_MAIN_BLOCKS))
    def _():
        o_ref[0] = w_ref[0]

    @pl.when((j >= gate_blk) & (j < N_MAIN_BLOCKS))
    def _():
        o_ref[0, :LANES - N_GATES] = w_ref[0, N_GATES:]
        o_ref[0, LANES - N_GATES:] = nxt_ref[0, :N_GATES]


def _wprep(wt):
    depth, n_in, d = wt.shape
    gate_blk = GATE_START // LANES
    last = n_in // LANES
    return pl.pallas_call(
        _wprep_kernel,
        grid=(depth, N_MAIN_BLOCKS + 1),
        in_specs=[pl.BlockSpec((1, LANES, d), lambda l, j: (l, jnp.where(j < N_MAIN_BLOCKS, j, gate_blk), 0)),
                  pl.BlockSpec((1, LANES, d), lambda l, j: (l, jnp.minimum(j + 1, last), 0))],
        out_specs=pl.BlockSpec((1, LANES, d), lambda l, j: (l, j, 0)),
        out_shape=jax.ShapeDtypeStruct((depth, N_MAIN + LANES, d), BF16),
        compiler_params=_cparams(("parallel", "parallel")),
        name="w_in_prep",
    )(wt, wt)


def _proj_kernel(x_ref, nw_ref, w_ref, gain_ref, o_ref, gate_ref, dil_ref):
    h = _rms(x_ref[...], nw_ref[...]).astype(BF16)
    for c0 in range(0, N_MAIN, PROJ_SUBTILE):
        width = min(PROJ_SUBTILE, N_MAIN - c0)
        acc = lax.dot_general(h, w_ref[c0:c0 + width, :], (((1,), (1,)), ((), ())), preferred_element_type=F32)
        for c in range(width // LANES):
            blk = c0 // LANES + c
            sl = slice(blk * LANES, (blk + 1) * LANES)
            y = acc[:, c * LANES:(c + 1) * LANES]
            if PROJ_NORMED[blk]:
                ms = jnp.mean(y * y, axis=-1, keepdims=True)
                y = y * lax.rsqrt(ms + EPS) * gain_ref[:, sl]
            o_ref[:, sl] = y.astype(o_ref.dtype)
            if blk in DIL_F32_SLOT:
                dil_ref[:, DIL_F32_SLOT[blk] * LANES:(DIL_F32_SLOT[blk] + 1) * LANES] = y
    gate_ref[...] = lax.dot_general(h, w_ref[N_MAIN:, :], (((1,), (1,)), ((), ())), preferred_element_type=F32)


def _proj(x2d, norm_w, w_all, layer, gain, *, tm):
    m, d = x2d.shape
    n = w_all.shape[1]
    return pl.pallas_call(
        _proj_kernel,
        grid=(m // tm,),
        in_specs=[
            pl.BlockSpec((tm, d), lambda i: (i, 0)),
            pl.BlockSpec((1, d), lambda i: (0, 0)),
            pl.BlockSpec((None, n, d), lambda i: (layer, 0, 0), pipeline_mode=pl.Buffered(1)),
            pl.BlockSpec((1, N_MAIN), lambda i: (0, 0)),
        ],
        out_specs=(pl.BlockSpec((tm, N_MAIN), lambda i: (i, 0)), pl.BlockSpec((tm, LANES), lambda i: (i, 0)),
                   pl.BlockSpec((tm, DIL_F32_COLS), lambda i: (i, 0))),
        out_shape=(jax.ShapeDtypeStruct((m, N_MAIN), BF16), jax.ShapeDtypeStruct((m, LANES), F32),
                   jax.ShapeDtypeStruct((m, DIL_F32_COLS), F32)),
        compiler_params=_cparams(("parallel",)),
        name="in_proj",
    )(x2d, norm_w, w_all, gain)


def _ffn_kernel(x_ref, nw_ref, wg_ref, wu_ref, wd_ref, o_ref, h_ref, acc_ref):
    f = pl.program_id(1)

    @pl.when(f == 0)
    def _():
        h_ref[...] = _rms(x_ref[...], nw_ref[...]).astype(BF16)
        acc_ref[...] = jnp.zeros_like(acc_ref)

    h = h_ref[...]
    g = jnp.dot(h, wg_ref[...], preferred_element_type=F32)
    u = jnp.dot(h, wu_ref[...], preferred_element_type=F32)
    a = (g * (1.0 / (1.0 + jnp.exp(-g))) * u).astype(BF16)
    acc_ref[...] += jnp.dot(a, wd_ref[...], preferred_element_type=F32)

    @pl.when(f == pl.num_programs(1) - 1)
    def _():
        o_ref[...] = x_ref[...] + acc_ref[...]


def _ffn(x2d, norm_w, wg, wu, wd, layer, *, tm, tf):
    m, d = x2d.shape
    dff = wg.shape[2]
    return pl.pallas_call(
        _ffn_kernel,
        grid=(m // tm, dff // tf),
        in_specs=[
            pl.BlockSpec((tm, d), lambda i, f: (i, 0)),
            pl.BlockSpec((1, d), lambda i, f: (0, 0)),
            pl.BlockSpec((None, d, tf), lambda i, f: (layer, 0, f)),
            pl.BlockSpec((None, d, tf), lambda i, f: (layer, 0, f)),
            pl.BlockSpec((None, tf, d), lambda i, f: (layer, f, 0)),
        ],
        out_specs=pl.BlockSpec((tm, d), lambda i, f: (i, 0)),
        out_shape=jax.ShapeDtypeStruct((m, d), F32),
        scratch_shapes=[pltpu.VMEM((tm, d), BF16), pltpu.VMEM((tm, d), F32)],
        compiler_params=_cparams(("parallel", "arbitrary")),
        name="ffn",
    )(x2d, norm_w, wg, wu, wd)


def _band_kernel(*refs, grp, n_off, nsub, has_sinks, with_lse):
    if has_sinks:
        sink_ref, q_ref, k_ref, v_ref, b_ref = refs[:5]
        outs = refs[5:]
    else:
        q_ref, k_ref, v_ref, b_ref = refs[:4]
        outs = refs[4:]
    o_ref = outs[0]
    g = pl.program_id(1)
    i0 = pl.program_id(2) * nsub
    tiles = {}

    def kv_tile(rel):
        if rel not in tiles:
            start = pl.multiple_of(jnp.maximum(i0 + rel, 0) * QBLK, QBLK)
            tiles[rel] = (k_ref[0, pl.ds(start, QBLK), :], v_ref[0, pl.ds(start, QBLK), :])
        return tiles[rel]

    def head_rows(x, h):
        return x[h * QBLK:(h + 1) * QBLK]

    kvs = [[kv_tile(j - off) for off in range(n_off)] for j in range(nsub)]
    tidxs = [[jnp.where(i0 + j - off >= 0, off, n_off) for off in range(n_off)] for j in range(nsub)]
    scores = []
    for j in range(nsub):
        rs = slice(j * QBLK, (j + 1) * QBLK)
        kcat = jnp.concatenate([t[0] for t in kvs[j]], axis=0)
        q = jnp.concatenate([q_ref[0, rs, h * HEAD_DIM:(h + 1) * HEAD_DIM] for h in range(grp)], axis=0)
        scores.append(lax.dot_general(q, kcat, (((1,), (1,)), ((), ())), preferred_element_type=F32))
    probs, stats = [], []
    for j in range(nsub):
        s = scores[j]
        st = [[head_rows(s, h)[:, off * QBLK:(off + 1) * QBLK] + b_ref[h, tidxs[j][off]] for off in range(n_off)]
              for h in range(grp)]
        m = jnp.max(jnp.concatenate([functools.reduce(jnp.maximum, r) for r in st], axis=0), axis=-1, keepdims=True)
        p = [[jnp.exp2(st[h][off] - head_rows(m, h)) for off in range(n_off)] for h in range(grp)]
        l = jnp.sum(jnp.concatenate([functools.reduce(jnp.add, r) for r in p], axis=0), axis=-1, keepdims=True)
        probs.append(jnp.concatenate([jnp.concatenate([t.astype(BF16) for t in r], axis=1) for r in p], axis=0))
        stats.append((m, l))
    for j in range(nsub):
        rs = slice(j * QBLK, (j + 1) * QBLK)
        m, l = stats[j]
        vcat = jnp.concatenate([t[1] for t in kvs[j]], axis=0)
        o = jnp.dot(probs[j], vcat, preferred_element_type=F32)
        den = l
        if has_sinks:
            sink = jnp.concatenate([jnp.full((QBLK, 1), sink_ref[g * grp + h] * LOG2E, F32) for h in range(grp)], axis=0)
            den = l + jnp.exp2(sink - m)
        o = o / den
        for h in range(grp):
            sl = slice(h * HEAD_DIM, (h + 1) * HEAD_DIM)
            o_ref[0, rs, sl] = head_rows(o, h).astype(o_ref.dtype)
            if with_lse:
                outs[1][0, rs, sl] = jnp.broadcast_to(head_rows(m * LN2 + jnp.log(l), h), (QBLK, HEAD_DIM))


def _band_attn(q_arr, k_arr, v_arr, bias, *, n_kv, grp, q_col, k_col, v_col, sinks=None, with_lse=False):
    n, L, _ = q_arr.shape
    n_off = bias.shape[1] - 1
    gw = grp * HEAD_DIM
    nsub = math.gcd(BAND_SUBBLOCKS, L // QBLK)
    rows = nsub * QBLK
    kern = functools.partial(_band_kernel, grp=grp, n_off=n_off, nsub=nsub, has_sinks=sinks is not None,
                             with_lse=with_lse)
    in_specs = [
        pl.BlockSpec((1, rows, gw), lambda b, g, i: (b, i, q_col + g)),
        pl.BlockSpec((1, L, HEAD_DIM), lambda b, g, i: (b, 0, k_col + g)),
        pl.BlockSpec((1, L, HEAD_DIM), lambda b, g, i: (b, 0, v_col + g)),
        pl.BlockSpec((grp, n_off + 1, QBLK, LANES), lambda b, g, i: (g, 0, 0, 0)),
    ]
    args = [q_arr, k_arr, v_arr, bias]
    if sinks is not None:
        in_specs = [pl.BlockSpec(memory_space=pltpu.SMEM)] + in_specs
        args = [sinks] + args
    o_spec = pl.BlockSpec((1, rows, gw), lambda b, g, i: (b, i, g))
    o_shape = jax.ShapeDtypeStruct((n, L, n_kv * gw), MIXER_OUT)
    lse_shape = jax.ShapeDtypeStruct((n, L, n_kv * gw), F32)
    return pl.pallas_call(
        kern,
        grid=(n, n_kv, L // rows),
        in_specs=in_specs,
        out_specs=(o_spec, o_spec) if with_lse else o_spec,
        out_shape=(o_shape, lse_shape) if with_lse else o_shape,
        compiler_params=_cparams(("parallel", "parallel", "arbitrary")),
        name="band_attn",
    )(*args)


def _dil_kernel(*refs, dil, nres):
    q_refs, (k_ref, v_ref, b_ref) = refs[:C_GRP], refs[C_GRP:C_GRP + 3]
    o_refs, lse_refs = refs[C_GRP + 3:2 * C_GRP + 3], refs[2 * C_GRP + 3:]
    ls = k_ref.shape[1] // dil
    ntile = ls // QBLK
    r0 = pl.program_id(1) * nres
    units = [(rr, j) for rr in range(nres) for j in range(ntile)]

    def rows(rr, j):
        return pl.ds(r0 + rr + dil * QBLK * j, QBLK, stride=dil)

    kv = {u: (k_ref[0, rows(*u), :].astype(BF16), v_ref[0, rows(*u), :].astype(BF16)) for u in units}
    n_off = b_ref.shape[1] - 1
    offs = {(rr, j): [off for off in range(n_off) if j - off >= 0] for rr, j in units}
    scores = []
    for rr, j in units:
        kcat = jnp.concatenate([kv[(rr, j - off)][0] for off in offs[(rr, j)]], axis=0)
        q = jnp.concatenate([q_ref[0, rows(rr, j), :].astype(BF16) for q_ref in q_refs], axis=0)
        scores.append(lax.dot_general(q, kcat, (((1,), (1,)), ((), ())), preferred_element_type=F32))
    probs, stats = [], []
    for u, s in zip(units, scores):
        st = [[s[h * QBLK:(h + 1) * QBLK, n * QBLK:(n + 1) * QBLK] + b_ref[h, off] for n, off in enumerate(offs[u])]
              for h in range(C_GRP)]
        m = jnp.max(jnp.concatenate([functools.reduce(jnp.maximum, r) for r in st], axis=0), axis=-1, keepdims=True)
        p = [[jnp.exp2(t - m[h * QBLK:(h + 1) * QBLK]) for t in r] for h, r in enumerate(st)]
        l = jnp.sum(jnp.concatenate([functools.reduce(jnp.add, r) for r in p], axis=0), axis=-1, keepdims=True)
        probs.append(jnp.concatenate([jnp.concatenate([t.astype(BF16) for t in r], axis=1) for r in p], axis=0))
        stats.append((m, l))
    for (rr, j), pb, (m, l) in zip(units, probs, stats):
        vcat = jnp.concatenate([kv[(rr, j - off)][1] for off in offs[(rr, j)]], axis=0)
        o = jnp.dot(pb, vcat, preferred_element_type=F32) / l
        lse = m * LN2 + jnp.log(l)
        for h in range(C_GRP):
            o_refs[h][0, rows(rr, j), :] = o[h * QBLK:(h + 1) * QBLK]
            lse_refs[h][0, rows(rr, j), :] = jnp.broadcast_to(lse[h * QBLK:(h + 1) * QBLK], (QBLK, HEAD_DIM))


def _dil_attn(x3, bias, *, col, dil):
    B, T, _ = x3.shape
    ntile = T // dil // QBLK
    nres = max(1, min(dil, DIL_TILES_PER_STEP // ntile))
    spec = pl.BlockSpec((1, T, HEAD_DIM), lambda b, r: (b, 0, 0))
    shape = jax.ShapeDtypeStruct((B, T, HEAD_DIM), F32)
    outs = pl.pallas_call(
        functools.partial(_dil_kernel, dil=dil, nres=nres),
        grid=(B, dil // nres),
        in_specs=[pl.BlockSpec((1, T, HEAD_DIM), functools.partial(lambda b, r, c: (b, 0, c), c=col + c))
                  for c in range(C_GRP + 2)]
        + [pl.BlockSpec((C_GRP, bias.shape[1], QBLK, LANES), lambda b, r: (0, 0, 0, 0))],
        out_specs=(spec,) * (2 * C_GRP),
        out_shape=(shape,) * (2 * C_GRP),
        compiler_params=_cparams(("parallel", "arbitrary")),
        name="dilated_attn",
    )(*([x3] * (C_GRP + 2)), bias)
    return outs[:C_GRP], outs[C_GRP:]


def _compress_kernel(x_ref, pe_ref, w1_ref, w2_ref, gain_ref, o_ref, *, n_cmp):
    half = w1_ref.shape[1] // 2
    rows = x_ref.shape[1] // CMP_STRIDE
    xr = jnp.concatenate([x_ref[0, pl.ds(t, rows, stride=CMP_STRIDE), :].astype(BF16) for t in range(CMP_STRIDE)],
                         axis=1)
    y0 = jnp.dot(xr, w1_ref[0, :half], preferred_element_type=F32)
    y1 = jnp.dot(xr, w1_ref[0, half:], preferred_element_type=F32)
    pe = jnp.dot(pe_ref[0], w1_ref[0], preferred_element_type=F32)[0:1]
    c = y0 + pltpu.roll(y1, rows - 1, 0) + pe
    gl = 0.5 * c * (1.0 + jnp.tanh(math.sqrt(2.0 / math.pi) * (c + 0.044715 * (c * c * c))))
    out = jnp.dot(gl.astype(BF16), w2_ref[0], preferred_element_type=F32)
    out = jnp.where(pl.program_id(0) == 0, _rms(out, gain_ref[...]), out)
    valid = lax.broadcasted_iota(jnp.int32, out.shape, 0) < n_cmp
    o_ref[0, 0, 0] = jnp.where(valid, out, 0.0).astype(o_ref.dtype)


def _compress(x3, pe, w1, w2, gain, n_cmp):
    B, T, _ = x3.shape
    rows = T // CMP_STRIDE
    wide = CMP_BLOCK * HEAD_DIM
    return pl.pallas_call(
        functools.partial(_compress_kernel, n_cmp=n_cmp),
        grid=(2, B, A_KV_HEADS),
        in_specs=[
            pl.BlockSpec((1, T, HEAD_DIM), lambda s, b, g: (b, 0, CMP_F32_SLOT0 + s * A_KV_HEADS + g)),
            pl.BlockSpec((1, 8, wide), lambda s, b, g: (s, 0, 0)),
            pl.BlockSpec((1, wide, HEAD_DIM), lambda s, b, g: (s, 0, 0)),
            pl.BlockSpec((1, HEAD_DIM, HEAD_DIM), lambda s, b, g: (s, 0, 0)),
            pl.BlockSpec((1, HEAD_DIM), lambda s, b, g: (0, 0)),
        ],
        out_specs=pl.BlockSpec((1, 1, 1, rows, HEAD_DIM), lambda s, b, g: (s, b, g, 0, 0)),
        out_shape=jax.ShapeDtypeStruct((2, B, A_KV_HEADS, rows, HEAD_DIM), BF16),
        compiler_params=_cparams(("arbitrary", "arbitrary", "arbitrary")),
        name="nsa_compress",
    )(x3, pe, w1, w2, gain)


def _cmp_select_kernel(q_ref, kc_ref, vc_ref, b_ref, ovl_ref, o_ref, sel_ref, *, n_slc, nsub):
    i0 = pl.program_id(2) * nsub
    rows = nsub * QBLK
    kc = kc_ref[0, 0, 0]
    vc = vc_ref[0, 0, 0]
    q_all = jnp.concatenate([q_ref[0, :, h * HEAD_DIM:(h + 1) * HEAD_DIM] for h in range(A_GRP)], axis=0)
    s_all = lax.dot_general(q_all, kc, (((1,), (1,)), ((), ())), preferred_element_type=F32)
    ps = []
    for h in range(A_GRP):
        s = s_all[h * rows:(h + 1) * rows] + b_ref[h]
        m = jnp.max(s, axis=-1, keepdims=True)
        e = jnp.exp2(s - jnp.where(m > 0.5 * NEG, m, 0.0))
        den = jnp.sum(e, axis=-1, keepdims=True)
        ps.append((e / jnp.where(den > 0, den, 1.0)).astype(BF16))
    r_all = jnp.dot(jnp.concatenate(ps, axis=0), jnp.concatenate([vc, ovl_ref[...]], axis=1),
                    preferred_element_type=F32)
    imp = jnp.zeros((rows, LANES), F32)
    for h in range(A_GRP):
        o_ref[0, :, h * HEAD_DIM:(h + 1) * HEAD_DIM] = r_all[h * rows:(h + 1) * rows, :HEAD_DIM].astype(o_ref.dtype)
        imp = imp + r_all[h * rows:(h + 1) * rows, HEAD_DIM:]
    t = i0 * QBLK + lax.broadcasted_iota(jnp.int32, (rows, LANES), 0)
    blk = lax.broadcasted_iota(jnp.int32, (rows, LANES), 1)
    cur = t // SLC_BLOCK
    imp = jnp.where(blk == 0, FORCED_SCORE, imp)
    imp = jnp.where(blk == cur, FORCED_SCORE, imp)
    imp = jnp.where(blk == cur - 1, FORCED_SCORE, imp)
    imp = jnp.where(blk * SLC_BLOCK <= t, imp, NEG)
    imp = jnp.where(blk < n_slc, imp, 2.0 * NEG)
    imp_t = jnp.concatenate([imp[j * QBLK:(j + 1) * QBLK].T for j in range(nsub)], axis=1)
    ngrp = -(-n_slc // SUBLANES)
    cands = [imp_t[r * SUBLANES:(r + 1) * SUBLANES] for r in range(ngrp)]
    row_id = lax.broadcasted_iota(jnp.int32, (SUBLANES, rows), 0)
    ranks = [jnp.zeros((SUBLANES, rows), F32) for _ in range(ngrp)]
    for j in range(n_slc):
        other = imp_t[j:j + 1, :]
        for r in range(ngrp):
            if r * SUBLANES > j:
                beats = other >= cands[r]
            elif (r + 1) * SUBLANES <= j:
                beats = other > cands[r]
            else:
                beats = jnp.where(row_id > j - r * SUBLANES, jnp.where(other >= cands[r], 1.0, 0.0),
                                  jnp.where(other > cands[r], 1.0, 0.0)) > 0.5
            ranks[r] = ranks[r] + jnp.where(beats, 1.0, 0.0)
    unsel = [jnp.where(r < float(min(SLC_TOPK, n_slc)), 0.0, 1.0) for r in ranks]
    if ngrp * SUBLANES > n_slc:
        unsel[-1] = jnp.where(row_id < n_slc - (ngrp - 1) * SUBLANES, unsel[-1], 0.0)
    sel_t = jnp.concatenate(unsel, axis=0)
    if ngrp * SUBLANES < LANES:
        sel_t = jnp.concatenate([sel_t, jnp.zeros((LANES - ngrp * SUBLANES, rows), F32)], axis=0)
    for j in range(nsub):
        sel_ref[0, 0, j * QBLK:(j + 1) * QBLK, :] = sel_t[:, j * QBLK:(j + 1) * QBLK].T.astype(sel_ref.dtype)


def _cmp_select(proj3, kvc, bias_c, ovl, n_slc):
    B, T, _ = proj3.shape
    ncp = kvc.shape[3]
    nsub = math.gcd(CMP_SUBBLOCKS, T // QBLK)
    rows = nsub * QBLK
    return pl.pallas_call(
        functools.partial(_cmp_select_kernel, n_slc=n_slc, nsub=nsub),
        grid=(B, A_KV_HEADS, T // rows),
        in_specs=[
            pl.BlockSpec((1, rows, A_GRP * HEAD_DIM), lambda b, g, i: (b, i, g)),
            pl.BlockSpec((1, 1, 1, ncp, HEAD_DIM), lambda b, g, i: (0, b, g, 0, 0)),
            pl.BlockSpec((1, 1, 1, ncp, HEAD_DIM), lambda b, g, i: (1, b, g, 0, 0)),
            pl.BlockSpec((A_GRP, rows, ncp), lambda b, g, i: (g, i, 0)),
            pl.BlockSpec((ncp, LANES), lambda b, g, i: (0, 0)),
        ],
        out_specs=(
            pl.BlockSpec((1, rows, A_GRP * HEAD_DIM), lambda b, g, i: (b, i, g)),
            pl.BlockSpec((1, 1, rows, LANES), lambda b, g, i: (b, g, i, 0)),
        ),
        out_shape=(
            jax.ShapeDtypeStruct((B, T, A_Q_HEADS * HEAD_DIM), MIXER_OUT),
            jax.ShapeDtypeStruct((B, A_KV_HEADS, T, LANES), BF16),
        ),
        compiler_params=_cparams(("parallel", "parallel", "arbitrary")),
        name="nsa_cmp_select",
    )(proj3, kvc, kvc, bias_c, ovl)


def _slc_kernel(q_ref, k_ref, v_ref, sel_ref, e_ref, b_ref, o_ref, *scratch, n_tiles, nq):
    ig = pl.program_id(1)
    chains = [(g, a) for g in range(A_KV_HEADS) for a in range(nq)]
    m_refs, l_refs, acc_refs = scratch[0::3], scratch[1::3], scratch[2::3]
    qs, sels = [], []
    for ci, (g, a) in enumerate(chains):
        m_refs[ci][...] = jnp.full(m_refs[ci].shape, NEG, F32)
        l_refs[ci][...] = jnp.zeros(l_refs[ci].shape, F32)
        acc_refs[ci][...] = jnp.zeros(acc_refs[ci].shape, F32)
        rs = slice(a * QBLK, (a + 1) * QBLK)
        qs.append(jnp.concatenate([q_ref[0, rs, (g * A_GRP + h) * HEAD_DIM:(g * A_GRP + h + 1) * HEAD_DIM]
                                   for h in range(A_GRP)], axis=0))
        sels.append(sel_ref[0, g, rs, :])
    kw = SLC_CHUNK * QBLK

    def body(c, carry):
        start = pl.multiple_of(c * kw, kw)
        base = ig * nq - c * SLC_CHUNK + 1
        tidx = {d: jnp.clip(base + d, 0, n_tiles - 1) for d in range(-(SLC_CHUNK - 1), nq)}
        kts = [k_ref[0, pl.ds(start, kw), g * HEAD_DIM:(g + 1) * HEAD_DIM] for g in range(A_KV_HEADS)]
        vts = [v_ref[0, pl.ds(start, kw), g * HEAD_DIM:(g + 1) * HEAD_DIM] for g in range(A_KV_HEADS)]
        ech = e_ref[:, pl.ds(start, kw)]
        ss = [lax.dot_general(qs[ci], kts[g], (((1,), (1,)), ((), ())), preferred_element_type=F32)
              for ci, (g, a) in enumerate(chains)]
        madds = [jnp.dot(sels[ci], ech, preferred_element_type=F32) for ci in range(len(chains))]
        pbs, alphas = [], []
        for ci, (g, a) in enumerate(chains):
            madd = madds[ci]
            s = ss[ci]
            rows = [[s[h * QBLK:(h + 1) * QBLK, j * QBLK:(j + 1) * QBLK] + b_ref[g * A_GRP + h, tidx[a - j]]
                     + madd[:, j * QBLK:(j + 1) * QBLK] for j in range(SLC_CHUNK)] for h in range(A_GRP)]
            tile_max = jnp.concatenate([functools.reduce(jnp.maximum, r) for r in rows], axis=0)
            m_old = m_refs[ci][...]
            m_new = jnp.maximum(m_old, jnp.max(tile_max, axis=-1, keepdims=True))
            alpha = jnp.exp2(m_old - m_new)
            p = [[jnp.exp2(t - m_new[h * QBLK:(h + 1) * QBLK]) for t in r] for h, r in enumerate(rows)]
            tile_sum = jnp.concatenate([functools.reduce(jnp.add, r) for r in p], axis=0)
            l_refs[ci][...] = alpha * l_refs[ci][...] + jnp.sum(tile_sum, axis=-1, keepdims=True)
            pbs.append(jnp.concatenate([jnp.concatenate([t.astype(BF16) for t in r], axis=1) for r in p], axis=0))
            alphas.append(alpha)
            m_refs[ci][...] = m_new
        for ci, (g, a) in enumerate(chains):
            acc_refs[ci][...] = alphas[ci] * acc_refs[ci][...] + jnp.dot(pbs[ci], vts[g], preferred_element_type=F32)
        return carry

    lax.fori_loop(0, (ig * nq + nq - 1) // SLC_CHUNK + 1, body, 0)
    for ci, (g, a) in enumerate(chains):
        out = acc_refs[ci][...] / l_refs[ci][...]
        for h in range(A_GRP):
            col = (g * A_GRP + h) * HEAD_DIM
            o_ref[0, a * QBLK:(a + 1) * QBLK, col:col + HEAD_DIM] = out[h * QBLK:(h + 1) * QBLK].astype(o_ref.dtype)


def _slc_attn(proj3, sel, expand, bias):
    B, T, _ = proj3.shape
    n_tiles = bias.shape[1]
    nq = SLC_QTILES
    qrows = nq * QBLK
    assert SLC_CHUNK % nq == 0 and T % (SLC_CHUNK * QBLK) == 0
    chain_rows = A_GRP * QBLK
    n_chains = A_KV_HEADS * nq
    return pl.pallas_call(
        functools.partial(_slc_kernel, n_tiles=n_tiles, nq=nq),
        grid=(B, T // qrows),
        in_specs=[
            pl.BlockSpec((1, qrows, A_Q_HEADS * HEAD_DIM), lambda b, i: (b, i, COL_QA // A_Q_HEADS)),
            pl.BlockSpec((1, T, A_KV_HEADS * HEAD_DIM), lambda b, i: (b, 0, COL_KSA // A_KV_HEADS)),
            pl.BlockSpec((1, T, A_KV_HEADS * HEAD_DIM), lambda b, i: (b, 0, COL_VSA // A_KV_HEADS)),
            pl.BlockSpec((1, A_KV_HEADS, qrows, LANES), lambda b, i: (b, 0, i, 0)),
            pl.BlockSpec((LANES, T), lambda b, i: (0, 0)),
            pl.BlockSpec((A_Q_HEADS, n_tiles, QBLK, LANES), lambda b, i: (0, 0, 0, 0)),
        ],
        out_specs=pl.BlockSpec((1, qrows, A_Q_HEADS * HEAD_DIM), lambda b, i: (b, i, 0)),
        out_shape=jax.ShapeDtypeStruct((B, T, A_Q_HEADS * HEAD_DIM), MIXER_OUT),
        scratch_shapes=[pltpu.VMEM((chain_rows, 1), F32), pltpu.VMEM((chain_rows, 1), F32),
                        pltpu.VMEM((chain_rows, HEAD_DIM), F32)] * n_chains,
        compiler_params=_cparams(("parallel", "arbitrary")),
        name="nsa_slc_attn",
    )(proj3, proj3, proj3, sel, expand, bias)


def _mix_outproj_kernel(x_ref, gate_ref, ocmp_ref, oslc_ref, owin_ref, ob_ref, *rest):
    oc_refs, lse_refs = rest[:C_Q_HEADS], rest[C_Q_HEADS:2 * C_Q_HEADS]
    w_ref, o_ref, mix_ref = rest[2 * C_Q_HEADS:]
    tm = x_ref.shape[0]
    nsplit = MIX_ROW_SPLIT
    rows = tm // nsplit
    for part in range(nsplit):
        rs = slice(part * rows, (part + 1) * rows)
        gate = 1.0 / (1.0 + jnp.exp(-gate_ref[rs, :]))
        for h in range(A_Q_HEADS):
            sl = slice(h * HEAD_DIM, (h + 1) * HEAD_DIM)
            o = (gate[:, 3 * h:3 * h + 1] * ocmp_ref[rs, sl] + gate[:, 3 * h + 1:3 * h + 2] * oslc_ref[rs, sl]
                 + gate[:, 3 * h + 2:3 * h + 3] * owin_ref[rs, sl])
            mix_ref[rs, sl] = o.astype(mix_ref.dtype)
        base = A_Q_HEADS * HEAD_DIM
        width = B_Q_HEADS * HEAD_DIM
        mix_ref[rs, base:base + width] = ob_ref[rs, :].astype(mix_ref.dtype)
        base += width
        for hh in range(C_GRP):
            heads = [C_GRP * gidx + hh for gidx in range(len(DIL_PAIRS))]
            lses = [lse_refs[c][rs, :] for c in heads]
            mx = functools.reduce(jnp.maximum, lses)
            ws = [jnp.exp(x - mx) for x in lses]
            tot = functools.reduce(jnp.add, ws)
            for c, w in zip(heads, ws):
                mix_ref[rs, base + c * HEAD_DIM:base + (c + 1) * HEAD_DIM] = (oc_refs[c][rs, :] * (w / tot)).astype(mix_ref.dtype)
    for part in range(nsplit):
        rs = slice(part * rows, (part + 1) * rows)
        a = mix_ref[rs, :]
        for c0 in range(0, o_ref.shape[1], OUT_SUBTILE):
            sl = slice(c0, c0 + OUT_SUBTILE)
            o_ref[rs, sl] = x_ref[rs, sl] + jnp.dot(a, w_ref[:, sl], preferred_element_type=F32)


def _mix_outproj(x2d, gates, o_cmp, o_slc, o_win, o_b, o_cs, lses, w_all, layer, *, tm):
    m, d = x2d.shape
    k = w_all.shape[1]
    row = lambda w: pl.BlockSpec((tm, w), lambda i: (i, 0))
    head = lambda c: pl.BlockSpec((tm, HEAD_DIM), lambda i: (i, c))
    full = [x2d, gates, o_cmp, o_slc, o_win, o_b]
    return pl.pallas_call(
        _mix_outproj_kernel,
        grid=(m // tm,),
        in_specs=[row(a.shape[1]) for a in full] + [head(c) for _, c in o_cs + lses]
        + [pl.BlockSpec((None, k, d), lambda i: (layer, 0, 0), pipeline_mode=pl.Buffered(1))],
        out_specs=row(d),
        out_shape=jax.ShapeDtypeStruct((m, d), F32),
        scratch_shapes=[pltpu.VMEM((tm, k), BF16)],
        compiler_params=_cparams(("parallel",)),
        name="mix_out_proj",
    )(*full, *[a for a, _ in o_cs + lses], w_all)


def _mixers_outproj(x2d, proj3, gates, dil32, layer_params, tables, w_out_all, layer, *, tm):
    B, T, _ = proj3.shape
    cmp_pe, cmp_w1, cmp_w2, kc_gain, sinks = layer_params
    n_cmp = (T - CMP_BLOCK) // CMP_STRIDE + 1
    n_slc = T // SLC_BLOCK
    pe = jnp.broadcast_to(cmp_pe.reshape(2, 1, CMP_BLOCK * HEAD_DIM), (2, 8, CMP_BLOCK * HEAD_DIM)).astype(BF16)
    kvc = _compress(dil32, pe, cmp_w1.astype(BF16), cmp_w2.astype(BF16), kc_gain.reshape(1, HEAD_DIM), n_cmp)

    o_cmp, sel = _cmp_select(proj3, kvc, tables["bias_c"], tables["ovl"], n_slc)
    o_slc = _slc_attn(proj3, sel, tables["expand"], tables["bias_slc"])
    o_win = _band_attn(proj3, proj3, proj3, tables["bias_win"], n_kv=A_KV_HEADS, grp=A_GRP,
                       q_col=COL_QA // A_GRP, k_col=COL_KWA, v_col=COL_VWA)
    o_b = _band_attn(proj3, proj3, proj3, tables["bias_b"], n_kv=B_KV_HEADS, grp=B_GRP,
                     q_col=COL_QB // B_GRP, k_col=COL_KB, v_col=COL_VB, sinks=sinks)
    o_cs, lses = [], []
    for gidx, (_, dil) in enumerate(DIL_PAIRS):
        if dil == 1:
            o, lse = _band_attn(proj3, proj3, proj3, tables["bias_c%d" % gidx], n_kv=1, grp=C_GRP,
                                q_col=COL_QC // C_GRP + gidx, k_col=COL_KC + gidx, v_col=COL_VC + gidx,
                                with_lse=True)
            o_cs += [(o.reshape(B * T, -1), h) for h in range(C_GRP)]
            lses += [(lse.reshape(B * T, -1), h) for h in range(C_GRP)]
        else:
            os_, ls_ = _dil_attn(dil32, tables["bias_c%d" % gidx], col=DIL_F32_SLOT[COL_QC + C_GRP * gidx], dil=dil)
            o_cs += [(o.reshape(B * T, -1), 0) for o in os_]
            lses += [(lse.reshape(B * T, -1), 0) for lse in ls_]
    return _mix_outproj(x2d, gates, o_cmp.reshape(B * T, -1), o_slc.reshape(B * T, -1), o_win.reshape(B * T, -1),
                        o_b.reshape(B * T, -1), o_cs, lses, w_out_all, layer, tm=tm)


def _build_tables(rel_bias, T):
    n_cmp = (T - CMP_BLOCK) // CMP_STRIDE + 1
    n_slc = T // SLC_BLOCK
    ncp = T // CMP_STRIDE
    nq = T // QBLK
    tables = {}
    tables["bias_c"] = _cmp_bias(rel_bias, A_Q_HEADS, nq, ncp, n_cmp).reshape(A_Q_HEADS, T, ncp)
    n_sat = -(-(SAT_DIST + QBLK - 1) // QBLK) + 1
    tables["bias_slc"] = _bias_tiles(rel_bias, 0, A_Q_HEADS, min(n_sat, nq) + 1, koff=QBLK)
    win = NSA_WINDOW - 1
    tables["bias_win"] = _bias_tiles(rel_bias, 0, A_Q_HEADS, -(-win // QBLK) + 2, max_dist=win)
    swa = SWA_WINDOW - 1
    tables["bias_b"] = _bias_tiles(rel_bias, A_Q_HEADS, B_Q_HEADS, -(-swa // QBLK) + 2, max_dist=swa)
    for gidx, (w, dil) in enumerate(DIL_PAIRS):
        md = w // dil
        tables["bias_c%d" % gidx] = _bias_tiles(rel_bias, A_Q_HEADS + B_Q_HEADS + C_GRP * gidx, C_GRP,
                                                -(-md // QBLK) + 2, dscale=dil, max_dist=md)
    c0 = np.arange(ncp)[:, None] * CMP_STRIDE
    s0 = np.arange(LANES)[None, :] * SLC_BLOCK
    ovl = np.clip(np.minimum(c0 + CMP_BLOCK, s0 + SLC_BLOCK) - np.maximum(c0, s0), 0, None) / CMP_BLOCK
    ovl = ovl * (np.arange(ncp)[:, None] < n_cmp) * (np.arange(LANES)[None, :] < n_slc)
    tables["ovl"] = jnp.asarray(ovl, BF16)
    member = (np.arange(T)[None, :] // SLC_BLOCK) == np.arange(LANES)[:, None]
    tables["expand"] = jnp.asarray(member * MASK_WEIGHT, BF16)
    return tables


def _proj_gain(g):
    ones = jnp.ones((HEAD_DIM,), F32)
    spec = [(g[0] * (SCALE * LOG2E), 6), (ones, 4), (g[2], 2), (ones, 2), (g[3], 2), (ones, 2),
            (g[4] * (SCALE * LOG2E), 4), (g[5], 2), (ones, 2), (g[6] * (SCALE * LOG2E), 6), (g[7], 3), (ones, 3)]
    assert sum(n for _, n in spec) == N_MAIN_BLOCKS
    return jnp.concatenate([jnp.tile(v, n) for v, n in spec]).reshape(1, N_MAIN)


def kernel(x, norm_attn, w_in, qk_gain, cmp_pe, cmp_w1, cmp_w2, sinks, rel_bias, w_out, norm_ffn, w_gate, w_up, w_down):
    B, T, D = x.shape
    depth = w_in.shape[0]
    tables = _build_tables(rel_bias, T)
    x2 = x.reshape(B * T, D)
    w_all = _wprep(jnp.transpose(w_in, (0, 2, 1)).astype(BF16))
    w_out_b, w_gate_b, w_up_b, w_down_b = (w.astype(BF16) for w in (w_out, w_gate, w_up, w_down))
    tm = min(ROW_TILE, B * T)
    for l in range(depth):
        proj, gates, dil32 = _proj(x2, norm_attn[l].reshape(1, D), w_all, l, _proj_gain(qk_gain[l]), tm=tm)
        x2 = _mixers_outproj(x2, proj.reshape(B, T, N_MAIN), gates, dil32.reshape(B, T, DIL_F32_COLS),
                             (cmp_pe[l], cmp_w1[l], cmp_w2[l], qk_gain[l][1], sinks[l]), tables, w_out_b, l, tm=tm)
        x2 = _ffn(x2, norm_ffn[l].reshape(1, D), w_gate_b, w_up_b, w_down_b, l, tm=tm, tf=FFN_TILE)
    return x2.reshape(B, T, D)
```

```python
import functools
import math

import numpy as np
import jax
import jax.numpy as jnp
from jax import lax
from jax.experimental import pallas as pl
from jax.experimental.pallas import tpu as pltpu

F32 = jnp.float32
BF16 = jnp.bfloat16

HEAD_DIM = 128
MIXER_OUT = jnp.bfloat16
LANES = 128
SUBLANES = 8
QBLK = 128
BAND_SUBBLOCKS = 8
CMP_SUBBLOCKS = 4
A_Q_HEADS, A_KV_HEADS = 6, 2
A_GRP = A_Q_HEADS // A_KV_HEADS
B_Q_HEADS, B_KV_HEADS = 4, 2
B_GRP = B_Q_HEADS // B_KV_HEADS
DIL_PAIRS = ((128, 1), (512, 4), (2048, 16))
C_GRP = 2
C_Q_HEADS = C_GRP * len(DIL_PAIRS)
CMP_BLOCK, CMP_STRIDE = 32, 16
SLC_BLOCK, SLC_TOPK = 64, 16
SLC_CHUNK = 4
SLC_QTILES = 4
NSA_WINDOW, SWA_WINDOW = 512, 128
FORCED_SCORE = 1.0e4
REL_BUCKETS, REL_MAX_EXACT, REL_MAX_DIST = 32, 16, 2048
SCALE = HEAD_DIM ** -0.5
LOG2E = math.log2(math.e)
LN2 = math.log(2.0)
MASK_WEIGHT = -2.0 ** 100
EPS = 1e-6
NEG = -1e30
VMEM_LIMIT = 56 * 1024 * 1024

COL_QA, COL_KCA, COL_VCA, COL_KSA, COL_VSA, COL_KWA, COL_VWA = 0, 6, 8, 10, 12, 14, 16
COL_QB, COL_KB, COL_VB, COL_QC, COL_KC, COL_VC = 18, 22, 24, 26, 32, 35
N_MAIN_BLOCKS = 38
N_MAIN = N_MAIN_BLOCKS * LANES
GATE_START = 2304
N_GATES = A_Q_HEADS * 3
PROJ_NORMED = ([True] * 6 + [False] * 4 + [True] * 2 + [False] * 2 + [True] * 2 + [False] * 2
               + [True] * 6 + [False] * 2 + [True] * 9 + [False] * 3)
DIL_F32_SLOT = {}
for _g, (_, _d) in enumerate(DIL_PAIRS):
    if _d > 1:
        _base = 4 * len([1 for _, _dd in DIL_PAIRS[:_g] if _dd > 1])
        DIL_F32_SLOT.update({COL_QC + C_GRP * _g: _base, COL_QC + C_GRP * _g + 1: _base + 1,
                             COL_KC + _g: _base + 2, COL_VC + _g: _base + 3})
CMP_F32_SLOT0 = len(DIL_F32_SLOT)
DIL_F32_SLOT.update({COL_KCA + _i: CMP_F32_SLOT0 + _i for _i in range(2 * A_KV_HEADS)})
DIL_F32_COLS = LANES * len(DIL_F32_SLOT)
DIL_TILES_PER_STEP = 8
ROW_TILE = 512
FFN_TILE = 512
PROJ_SUBTILE = 256
MIX_ROW_SPLIT = 4
OUT_SUBTILE = 512


def _bucket_starts():
    d = np.arange(0, 1 << 17)
    out = []
    for dt in (np.float32, np.float64):
        far = np.maximum(d, REL_MAX_EXACT).astype(dt)
        lb = REL_MAX_EXACT + (np.log(far / dt(REL_MAX_EXACT)) / dt(math.log(REL_MAX_DIST / REL_MAX_EXACT))
                              * dt(REL_BUCKETS - REL_MAX_EXACT)).astype(np.int64)
        out.append(np.where(d < REL_MAX_EXACT, d, np.minimum(lb, REL_BUCKETS - 1)))
    assert (out[0] == out[1]).all() and (np.diff(out[0]) >= 0).all()
    return [int(np.argmax(out[0] >= b)) for b in range(REL_BUCKETS)]


BUCKET_START = _bucket_starts()
SAT_DIST = BUCKET_START[REL_BUCKETS - 1]


def _cparams(sem, vmem=VMEM_LIMIT):
    return pltpu.CompilerParams(dimension_semantics=sem, vmem_limit_bytes=vmem)


def _bias_tile_kernel(tab_ref, o_ref, *, head0, koff, dscale, max_dist):
    h = pl.program_id(0) + head0
    t = pl.program_id(1)
    r = lax.broadcasted_iota(jnp.int32, (QBLK, LANES), 0)
    c = lax.broadcasted_iota(jnp.int32, (QBLK, LANES), 1)
    dist = t * QBLK + r - c - koff
    d = dist * dscale
    val = jnp.full((QBLK, LANES), tab_ref[REL_BUCKETS - 1, h] * LOG2E, F32)
    for b in range(REL_BUCKETS - 2, -1, -1):
        val = jnp.where(d < BUCKET_START[b + 1], tab_ref[b, h] * LOG2E, val)
    val = jnp.where(dist >= 0, val, NEG)
    if max_dist is not None:
        val = jnp.where(dist <= max_dist, val, NEG)
    o_ref[0, 0] = val


def _bias_tiles(rel_bias, head0, nheads, ntiles, *, koff=0, dscale=1, max_dist=None):
    kern = functools.partial(_bias_tile_kernel, head0=head0, koff=koff, dscale=dscale, max_dist=max_dist)
    return pl.pallas_call(
        kern,
        grid=(nheads, ntiles),
        in_specs=[pl.BlockSpec(memory_space=pltpu.SMEM)],
        out_specs=pl.BlockSpec((1, 1, QBLK, LANES), lambda h, t: (h, t, 0, 0)),
        out_shape=jax.ShapeDtypeStruct((nheads, ntiles, QBLK, LANES), F32),
        compiler_params=_cparams(("parallel", "parallel")),
        name="bias_tiles",
    )(rel_bias)


def _cmp_bias_kernel(tab_ref, o_ref, *, n_cmp):
    h = pl.program_id(0)
    ntile, _, width = o_ref.shape[1:]
    r = lax.broadcasted_iota(jnp.int32, (QBLK, 2 * width), 0)
    c = lax.broadcasted_iota(jnp.int32, (QBLK, 2 * width), 1)
    dist = r - CMP_STRIDE * (c - width) - (CMP_BLOCK - 1)
    base = jnp.full((QBLK, 2 * width), tab_ref[REL_BUCKETS - 1, h] * LOG2E, F32)
    for b in range(REL_BUCKETS - 2, -1, -1):
        base = jnp.where(dist < BUCKET_START[b + 1], tab_ref[b, h] * LOG2E, base)
    base = jnp.where(dist >= 0, base, NEG)
    col = lax.broadcasted_iota(jnp.int32, (QBLK, width), 1)
    per_tile = QBLK // CMP_STRIDE
    for i in range(ntile):
        tile = pltpu.roll(base, (width + per_tile * i) % (2 * width), 1)[:, :width]
        o_ref[0, i] = jnp.where(col < n_cmp, tile, NEG)


def _cmp_bias(rel_bias, nheads, ntiles, width, n_cmp):
    return pl.pallas_call(
        functools.partial(_cmp_bias_kernel, n_cmp=n_cmp),
        grid=(nheads,),
        in_specs=[pl.BlockSpec(memory_space=pltpu.SMEM)],
        out_specs=pl.BlockSpec((1, ntiles, QBLK, width), lambda h: (h, 0, 0, 0)),
        out_shape=jax.ShapeDtypeStruct((nheads, ntiles, QBLK, width), F32),
        compiler_params=_cparams(("parallel",)),
        name="cmp_bias",
    )(rel_bias)


def _rms(x, w):
    ms = jnp.mean(x * x, axis=-1, keepdims=True)
    return x * lax.rsqrt(ms + EPS) * w


def _wprep_kernel(w_ref, nxt_ref, o_ref):
    j = pl.program_id(1)
    gate_blk = GATE_START // LANES

    @pl.when((j < gate_blk) | (j == N_MAIN_BLOCKS))
    def _():
        o_ref[0] = w_ref[0]

    @pl.when((j >= gate_blk) & (j < N_MAIN_BLOCKS))
    def _():
        o_ref[0, :LANES - N_GATES] = w_ref[0, N_GATES:]
        o_ref[0, LANES - N_GATES:] = nxt_ref[0, :N_GATES]


def _wprep(wt):
    depth, n_in, d = wt.shape
    gate_blk = GATE_START // LANES
    last = n_in // LANES
    return pl.pallas_call(
        _wprep_kernel,
        grid=(depth, N_MAIN_BLOCKS + 1),
        in_specs=[pl.BlockSpec((1, LANES, d), lambda l, j: (l, jnp.where(j < N_MAIN_BLOCKS, j, gate_blk), 0)),
                  pl.BlockSpec((1, LANES, d), lambda l, j: (l, jnp.minimum(j + 1, last), 0))],
        out_specs=pl.BlockSpec((1, LANES, d), lambda l, j: (l, j, 0)),
        out_shape=jax.ShapeDtypeStruct((depth, N_MAIN + LANES, d), BF16),
        compiler_params=_cparams(("parallel", "parallel")),
        name="w_in_prep",
    )(wt, wt)


def _proj_kernel(x_ref, nw_ref, w_ref, gain_ref, o_ref, gate_ref, dil_ref):
    h = _rms(x_ref[...], nw_ref[...]).astype(BF16)
    for c0 in range(0, N_MAIN, PROJ_SUBTILE):
        width = min(PROJ_SUBTILE, N_MAIN - c0)
        acc = lax.dot_general(h, w_ref[c0:c0 + width, :], (((1,), (1,)), ((), ())), preferred_element_type=F32)
        for c in range(width // LANES):
            blk = c0 // LANES + c
            sl = slice(blk * LANES, (blk + 1) * LANES)
            y = acc[:, c * LANES:(c + 1) * LANES]
            if PROJ_NORMED[blk]:
                ms = jnp.mean(y * y, axis=-1, keepdims=True)
                y = y * lax.rsqrt(ms + EPS) * gain_ref[:, sl]
            o_ref[:, sl] = y.astype(o_ref.dtype)
            if blk in DIL_F32_SLOT:
                dil_ref[:, DIL_F32_SLOT[blk] * LANES:(DIL_F32_SLOT[blk] + 1) * LANES] = y
    gate_ref[...] = lax.dot_general(h, w_ref[N_MAIN:, :], (((1,), (1,)), ((), ())), preferred_element_type=F32)


def _proj(x2d, norm_w, w_all, layer, gain, *, tm):
    m, d = x2d.shape
    n = w_all.shape[1]
    return pl.pallas_call(
        _proj_kernel,
        grid=(m // tm,),
        in_specs=[
            pl.BlockSpec((tm, d), lambda i: (i, 0)),
            pl.BlockSpec((1, d), lambda i: (0, 0)),
            pl.BlockSpec((None, n, d), lambda i: (layer, 0, 0), pipeline_mode=pl.Buffered(1)),
            pl.BlockSpec((1, N_MAIN), lambda i: (0, 0)),
        ],
        out_specs=(pl.BlockSpec((tm, N_MAIN), lambda i: (i, 0)), pl.BlockSpec((tm, LANES), lambda i: (i, 0)),
                   pl.BlockSpec((tm, DIL_F32_COLS), lambda i: (i, 0))),
        out_shape=(jax.ShapeDtypeStruct((m, N_MAIN), BF16), jax.ShapeDtypeStruct((m, LANES), F32),
                   jax.ShapeDtypeStruct((m, DIL_F32_COLS), F32)),
        compiler_params=_cparams(("parallel",)),
        name="in_proj",
    )(x2d, norm_w, w_all, gain)


def _ffn_kernel(x_ref, nw_ref, wg_ref, wu_ref, wd_ref, o_ref, h_ref):
    f = pl.program_id(1)

    @pl.when(f == 0)
    def _():
        x = x_ref[...]
        h_ref[...] = _rms(x, nw_ref[...]).astype(BF16)
        o_ref[...] = x

    h = h_ref[...]
    g = jnp.dot(h, wg_ref[...], preferred_element_type=F32)
    u = jnp.dot(h, wu_ref[...], preferred_element_type=F32)
    a = (g * (1.0 / (1.0 + jnp.exp(-g))) * u).astype(BF16)
    o_ref[...] += jnp.dot(a, wd_ref[...], preferred_element_type=F32)


def _ffn(x2d, norm_w, wg, wu, wd, layer, *, tm, tf):
    m, d = x2d.shape
    dff = wg.shape[2]
    return pl.pallas_call(
        _ffn_kernel,
        grid=(m // tm, dff // tf),
        in_specs=[
            pl.BlockSpec((tm, d), lambda i, f: (i, 0)),
            pl.BlockSpec((1, d), lambda i, f: (0, 0)),
            pl.BlockSpec((None, d, tf), lambda i, f: (layer, 0, f)),
            pl.BlockSpec((None, d, tf), lambda i, f: (layer, 0, f)),
            pl.BlockSpec((None, tf, d), lambda i, f: (layer, f, 0)),
        ],
        out_specs=pl.BlockSpec((tm, d), lambda i, f: (i, 0)),
        out_shape=jax.ShapeDtypeStruct((m, d), F32),
        scratch_shapes=[pltpu.VMEM((tm, d), BF16)],
        compiler_params=_cparams(("parallel", "arbitrary")),
        name="ffn",
    )(x2d, norm_w, wg, wu, wd)


def _band_kernel(*refs, grp, n_off, nsub, has_sinks, with_lse):
    if has_sinks:
        sink_ref, q_ref, k_ref, v_ref, b_ref = refs[:5]
        outs = refs[5:]
    else:
        q_ref, k_ref, v_ref, b_ref = refs[:4]
        outs = refs[4:]
    o_ref = outs[0]
    g = pl.program_id(1)
    i0 = pl.program_id(2) * nsub
    tiles = {}

    def kv_tile(rel):
        if rel not in tiles:
            start = pl.multiple_of(jnp.maximum(i0 + rel, 0) * QBLK, QBLK)
            tiles[rel] = (k_ref[0, pl.ds(start, QBLK), :], v_ref[0, pl.ds(start, QBLK), :])
        return tiles[rel]

    def head_rows(x, h):
        return x[h * QBLK:(h + 1) * QBLK]

    kvs = [[kv_tile(j - off) for off in range(n_off)] for j in range(nsub)]
    tidxs = [[jnp.where(i0 + j - off >= 0, off, n_off) for off in range(n_off)] for j in range(nsub)]
    scores = []
    for j in range(nsub):
        rs = slice(j * QBLK, (j + 1) * QBLK)
        kcat = jnp.concatenate([t[0] for t in kvs[j]], axis=0)
        q = jnp.concatenate([q_ref[0, rs, h * HEAD_DIM:(h + 1) * HEAD_DIM] for h in range(grp)], axis=0)
        scores.append(lax.dot_general(q, kcat, (((1,), (1,)), ((), ())), preferred_element_type=F32))
    probs, stats = [], []
    for j in range(nsub):
        s = scores[j]
        st = [[head_rows(s, h)[:, off * QBLK:(off + 1) * QBLK] + b_ref[h, tidxs[j][off]] for off in range(n_off)]
              for h in range(grp)]
        m = jnp.max(jnp.concatenate([functools.reduce(jnp.maximum, r) for r in st], axis=0), axis=-1, keepdims=True)
        p = [[jnp.exp2(st[h][off] - head_rows(m, h)) for off in range(n_off)] for h in range(grp)]
        l = jnp.sum(jnp.concatenate([functools.reduce(jnp.add, r) for r in p], axis=0), axis=-1, keepdims=True)
        probs.append(jnp.concatenate([jnp.concatenate([t.astype(BF16) for t in r], axis=1) for r in p], axis=0))
        stats.append((m, l))
    for j in range(nsub):
        rs = slice(j * QBLK, (j + 1) * QBLK)
        m, l = stats[j]
        vcat = jnp.concatenate([t[1] for t in kvs[j]], axis=0)
        o = jnp.dot(probs[j], vcat, preferred_element_type=F32)
        den = l
        if has_sinks:
            sink = jnp.concatenate([jnp.full((QBLK, 1), sink_ref[g * grp + h] * LOG2E, F32) for h in range(grp)], axis=0)
            den = l + jnp.exp2(sink - m)
        o = o / den
        for h in range(grp):
            sl = slice(h * HEAD_DIM, (h + 1) * HEAD_DIM)
            o_ref[0, rs, sl] = head_rows(o, h).astype(o_ref.dtype)
            if with_lse:
                outs[1][0, rs, sl] = jnp.broadcast_to(head_rows(m * LN2 + jnp.log(l), h), (QBLK, HEAD_DIM))


def _band_attn(q_arr, k_arr, v_arr, bias, *, n_kv, grp, q_col, k_col, v_col, sinks=None, with_lse=False):
    n, L, _ = q_arr.shape
    n_off = bias.shape[1] - 1
    gw = grp * HEAD_DIM
    nsub = math.gcd(BAND_SUBBLOCKS, L // QBLK)
    rows = nsub * QBLK
    kern = functools.partial(_band_kernel, grp=grp, n_off=n_off, nsub=nsub, has_sinks=sinks is not None,
                             with_lse=with_lse)
    in_specs = [
        pl.BlockSpec((1, rows, gw), lambda b, g, i: (b, i, q_col + g)),
        pl.BlockSpec((1, L, HEAD_DIM), lambda b, g, i: (b, 0, k_col + g)),
        pl.BlockSpec((1, L, HEAD_DIM), lambda b, g, i: (b, 0, v_col + g)),
        pl.BlockSpec((grp, n_off + 1, QBLK, LANES), lambda b, g, i: (g, 0, 0, 0)),
    ]
    args = [q_arr, k_arr, v_arr, bias]
    if sinks is not None:
        in_specs = [pl.BlockSpec(memory_space=pltpu.SMEM)] + in_specs
        args = [sinks] + args
    o_spec = pl.BlockSpec((1, rows, gw), lambda b, g, i: (b, i, g))
    o_shape = jax.ShapeDtypeStruct((n, L, n_kv * gw), MIXER_OUT)
    lse_shape = jax.ShapeDtypeStruct((n, L, n_kv * gw), F32)
    return pl.pallas_call(
        kern,
        grid=(n, n_kv, L // rows),
        in_specs=in_specs,
        out_specs=(o_spec, o_spec) if with_lse else o_spec,
        out_shape=(o_shape, lse_shape) if with_lse else o_shape,
        compiler_params=_cparams(("parallel", "parallel", "arbitrary")),
        name="band_attn",
    )(*args)


def _dil_kernel(*refs, dil, nres):
    q_refs, (k_ref, v_ref, b_ref) = refs[:C_GRP], refs[C_GRP:C_GRP + 3]
    o_refs, lse_refs = refs[C_GRP + 3:2 * C_GRP + 3], refs[2 * C_GRP + 3:]
    ls = k_ref.shape[1] // dil
    ntile = ls // QBLK
    r0 = pl.program_id(1) * nres
    units = [(rr, j) for rr in range(nres) for j in range(ntile)]

    def rows(rr, j):
        return pl.ds(r0 + rr + dil * QBLK * j, QBLK, stride=dil)

    kv = {u: (k_ref[0, rows(*u), :].astype(BF16), v_ref[0, rows(*u), :].astype(BF16)) for u in units}
    n_off = b_ref.shape[1] - 1
    offs = {(rr, j): [off for off in range(n_off) if j - off >= 0] for rr, j in units}
    scores = []
    for rr, j in units:
        kcat = jnp.concatenate([kv[(rr, j - off)][0] for off in offs[(rr, j)]], axis=0)
        q = jnp.concatenate([q_ref[0, rows(rr, j), :].astype(BF16) for q_ref in q_refs], axis=0)
        scores.append(lax.dot_general(q, kcat, (((1,), (1,)), ((), ())), preferred_element_type=F32))
    probs, stats = [], []
    for u, s in zip(units, scores):
        st = [[s[h * QBLK:(h + 1) * QBLK, n * QBLK:(n + 1) * QBLK] + b_ref[h, off] for n, off in enumerate(offs[u])]
              for h in range(C_GRP)]
        m = jnp.max(jnp.concatenate([functools.reduce(jnp.maximum, r) for r in st], axis=0), axis=-1, keepdims=True)
        p = [[jnp.exp2(t - m[h * QBLK:(h + 1) * QBLK]) for t in r] for h, r in enumerate(st)]
        l = jnp.sum(jnp.concatenate([functools.reduce(jnp.add, r) for r in p], axis=0), axis=-1, keepdims=True)
        probs.append(jnp.concatenate([jnp.concatenate([t.astype(BF16) for t in r], axis=1) for r in p], axis=0))
        stats.append((m, l))
    for (rr, j), pb, (m, l) in zip(units, probs, stats):
        vcat = jnp.concatenate([kv[(rr, j - off)][1] for off in offs[(rr, j)]], axis=0)
        o = jnp.dot(pb, vcat, preferred_element_type=F32) / l
        lse = m * LN2 + jnp.log(l)
        for h in range(C_GRP):
            o_refs[h][0, rows(rr, j), :] = o[h * QBLK:(h + 1) * QBLK]
            lse_refs[h][0, rows(rr, j), :] = jnp.broadcast_to(lse[h * QBLK:(h + 1) * QBLK], (QBLK, HEAD_DIM))


def _dil_attn(x3, bias, *, col, dil):
    B, T, _ = x3.shape
    ntile = T // dil // QBLK
    nres = max(1, min(dil, DIL_TILES_PER_STEP // ntile))
    spec = pl.BlockSpec((1, T, HEAD_DIM), lambda b, r: (b, 0, 0))
    shape = jax.ShapeDtypeStruct((B, T, HEAD_DIM), F32)
    outs = pl.pallas_call(
        functools.partial(_dil_kernel, dil=dil, nres=nres),
        grid=(B, dil // nres),
        in_specs=[pl.BlockSpec((1, T, HEAD_DIM), functools.partial(lambda b, r, c: (b, 0, c), c=col + c))
                  for c in range(C_GRP + 2)]
        + [pl.BlockSpec((C_GRP, bias.shape[1], QBLK, LANES), lambda b, r: (0, 0, 0, 0))],
        out_specs=(spec,) * (2 * C_GRP),
        out_shape=(shape,) * (2 * C_GRP),
        compiler_params=_cparams(("parallel", "arbitrary")),
        name="dilated_attn",
    )(*([x3] * (C_GRP + 2)), bias)
    return outs[:C_GRP], outs[C_GRP:]


def _compress_kernel(x_ref, pe_ref, w1_ref, w2_ref, gain_ref, o_ref, *, n_cmp):
    half = w1_ref.shape[1] // 2
    rows = x_ref.shape[1] // CMP_STRIDE
    xr = jnp.concatenate([x_ref[0, pl.ds(t, rows, stride=CMP_STRIDE), :].astype(BF16) for t in range(CMP_STRIDE)],
                         axis=1)
    y0 = jnp.dot(xr, w1_ref[0, :half], preferred_element_type=F32)
    y1 = jnp.dot(xr, w1_ref[0, half:], preferred_element_type=F32)
    pe = jnp.dot(pe_ref[0], w1_ref[0], preferred_element_type=F32)[0:1]
    c = y0 + pltpu.roll(y1, rows - 1, 0) + pe
    gl = 0.5 * c * (1.0 + jnp.tanh(math.sqrt(2.0 / math.pi) * (c + 0.044715 * (c * c * c))))
    out = jnp.dot(gl.astype(BF16), w2_ref[0], preferred_element_type=F32)
    out = jnp.where(pl.program_id(0) == 0, _rms(out, gain_ref[...]), out)
    valid = lax.broadcasted_iota(jnp.int32, out.shape, 0) < n_cmp
    o_ref[0, 0, 0] = jnp.where(valid, out, 0.0).astype(o_ref.dtype)


def _compress(x3, pe, w1, w2, gain, n_cmp):
    B, T, _ = x3.shape
    rows = T // CMP_STRIDE
    wide = CMP_BLOCK * HEAD_DIM
    return pl.pallas_call(
        functools.partial(_compress_kernel, n_cmp=n_cmp),
        grid=(2, B, A_KV_HEADS),
        in_specs=[
            pl.BlockSpec((1, T, HEAD_DIM), lambda s, b, g: (b, 0, CMP_F32_SLOT0 + s * A_KV_HEADS + g)),
            pl.BlockSpec((1, 8, wide), lambda s, b, g: (s, 0, 0)),
            pl.BlockSpec((1, wide, HEAD_DIM), lambda s, b, g: (s, 0, 0)),
            pl.BlockSpec((1, HEAD_DIM, HEAD_DIM), lambda s, b, g: (s, 0, 0)),
            pl.BlockSpec((1, HEAD_DIM), lambda s, b, g: (0, 0)),
        ],
        out_specs=pl.BlockSpec((1, 1, 1, rows, HEAD_DIM), lambda s, b, g: (s, b, g, 0, 0)),
        out_shape=jax.ShapeDtypeStruct((2, B, A_KV_HEADS, rows, HEAD_DIM), BF16),
        compiler_params=_cparams(("arbitrary", "arbitrary", "arbitrary")),
        name="nsa_compress",
    )(x3, pe, w1, w2, gain)


def _cmp_select_kernel(q_ref, kc_ref, vc_ref, b_ref, ovl_ref, o_ref, sel_ref, *, n_slc, nsub):
    i0 = pl.program_id(2) * nsub
    rows = nsub * QBLK
    kc = kc_ref[0, 0, 0]
    vc = vc_ref[0, 0, 0]
    q_all = jnp.concatenate([q_ref[0, :, h * HEAD_DIM:(h + 1) * HEAD_DIM] for h in range(A_GRP)], axis=0)
    s_all = lax.dot_general(q_all, kc, (((1,), (1,)), ((), ())), preferred_element_type=F32)
    ps = []
    for h in range(A_GRP):
        s = s_all[h * rows:(h + 1) * rows] + b_ref[h]
        m = jnp.max(s, axis=-1, keepdims=True)
        e = jnp.exp2(s - jnp.where(m > 0.5 * NEG, m, 0.0))
        den = jnp.sum(e, axis=-1, keepdims=True)
        ps.append((e / jnp.where(den > 0, den, 1.0)).astype(BF16))
    r_all = jnp.dot(jnp.concatenate(ps, axis=0), jnp.concatenate([vc, ovl_ref[...]], axis=1),
                    preferred_element_type=F32)
    imp = jnp.zeros((rows, LANES), F32)
    for h in range(A_GRP):
        o_ref[0, :, h * HEAD_DIM:(h + 1) * HEAD_DIM] = r_all[h * rows:(h + 1) * rows, :HEAD_DIM].astype(o_ref.dtype)
        imp = imp + r_all[h * rows:(h + 1) * rows, HEAD_DIM:]
    t = i0 * QBLK + lax.broadcasted_iota(jnp.int32, (rows, LANES), 0)
    blk = lax.broadcasted_iota(jnp.int32, (rows, LANES), 1)
    cur = t // SLC_BLOCK
    imp = jnp.where(blk == 0, FORCED_SCORE, imp)
    imp = jnp.where(blk == cur, FORCED_SCORE, imp)
    imp = jnp.where(blk == cur - 1, FORCED_SCORE, imp)
    imp = jnp.where(blk * SLC_BLOCK <= t, imp, NEG)
    imp = jnp.where(blk < n_slc, imp, 2.0 * NEG)
    imp_t = jnp.concatenate([imp[j * QBLK:(j + 1) * QBLK].T for j in range(nsub)], axis=1)
    ngrp = -(-n_slc // SUBLANES)
    cands = [imp_t[r * SUBLANES:(r + 1) * SUBLANES] for r in range(ngrp)]
    row_id = lax.broadcasted_iota(jnp.int32, (SUBLANES, rows), 0)
    ranks = [jnp.zeros((SUBLANES, rows), F32) for _ in range(ngrp)]
    for j in range(n_slc):
        other = imp_t[j:j + 1, :]
        for r in range(ngrp):
            if r * SUBLANES > j:
                beats = other >= cands[r]
            elif (r + 1) * SUBLANES <= j:
                beats = other > cands[r]
            else:
                beats = jnp.where(row_id > j - r * SUBLANES, jnp.where(other >= cands[r], 1.0, 0.0),
                                  jnp.where(other > cands[r], 1.0, 0.0)) > 0.5
            ranks[r] = ranks[r] + jnp.where(beats, 1.0, 0.0)
    unsel = [jnp.where(r < float(min(SLC_TOPK, n_slc)), 0.0, 1.0) for r in ranks]
    if ngrp * SUBLANES > n_slc:
        unsel[-1] = jnp.where(row_id < n_slc - (ngrp - 1) * SUBLANES, unsel[-1], 0.0)
    sel_t = jnp.concatenate(unsel, axis=0)
    if ngrp * SUBLANES < LANES:
        sel_t = jnp.concatenate([sel_t, jnp.zeros((LANES - ngrp * SUBLANES, rows), F32)], axis=0)
    for j in range(nsub):
        sel_ref[0, 0, j * QBLK:(j + 1) * QBLK, :] = sel_t[:, j * QBLK:(j + 1) * QBLK].T.astype(sel_ref.dtype)


def _cmp_select(proj3, kvc, bias_c, ovl, n_slc):
    B, T, _ = proj3.shape
    ncp = kvc.shape[3]
    nsub = math.gcd(CMP_SUBBLOCKS, T // QBLK)
    rows = nsub * QBLK
    return pl.pallas_call(
        functools.partial(_cmp_select_kernel, n_slc=n_slc, nsub=nsub),
        grid=(B, A_KV_HEADS, T // rows),
        in_specs=[
            pl.BlockSpec((1, rows, A_GRP * HEAD_DIM), lambda b, g, i: (b, i, g)),
            pl.BlockSpec((1, 1, 1, ncp, HEAD_DIM), lambda b, g, i: (0, b, g, 0, 0)),
            pl.BlockSpec((1, 1, 1, ncp, HEAD_DIM), lambda b, g, i: (1, b, g, 0, 0)),
            pl.BlockSpec((A_GRP, rows, ncp), lambda b, g, i: (g, i, 0)),
            pl.BlockSpec((ncp, LANES), lambda b, g, i: (0, 0)),
        ],
        out_specs=(
            pl.BlockSpec((1, rows, A_GRP * HEAD_DIM), lambda b, g, i: (b, i, g)),
            pl.BlockSpec((1, 1, rows, LANES), lambda b, g, i: (b, g, i, 0)),
        ),
        out_shape=(
            jax.ShapeDtypeStruct((B, T, A_Q_HEADS * HEAD_DIM), MIXER_OUT),
            jax.ShapeDtypeStruct((B, A_KV_HEADS, T, LANES), BF16),
        ),
        compiler_params=_cparams(("parallel", "parallel", "arbitrary")),
        name="nsa_cmp_select",
    )(proj3, kvc, kvc, bias_c, ovl)


def _slc_kernel(q_ref, k_ref, v_ref, sel_ref, e_ref, b_ref, o_ref, *scratch, n_tiles, nq):
    m_refs, l_refs, acc_refs = scratch[0::3], scratch[1::3], scratch[2::3]
    kw = SLC_CHUNK * QBLK
    for a in range(nq):
        it = pl.program_id(1) * nq + a
        rs = slice(a * QBLK, (a + 1) * QBLK)
        qs, sels = [], []
        for g in range(A_KV_HEADS):
            m_refs[g][...] = jnp.full(m_refs[g].shape, NEG, F32)
            l_refs[g][...] = jnp.zeros(l_refs[g].shape, F32)
            acc_refs[g][...] = jnp.zeros(acc_refs[g].shape, F32)
            qs.append(jnp.concatenate([q_ref[0, rs, (g * A_GRP + h) * HEAD_DIM:(g * A_GRP + h + 1) * HEAD_DIM]
                                       for h in range(A_GRP)], axis=0))
            sels.append(sel_ref[0, g, rs, :])

        def body(c, carry, it=it, qs=qs, sels=sels):
            start = pl.multiple_of(c * kw, kw)
            tidx = [jnp.clip(it - (c * SLC_CHUNK + j) + 1, 0, n_tiles - 1) for j in range(SLC_CHUNK)]
            kts = [k_ref[0, pl.ds(start, kw), g * HEAD_DIM:(g + 1) * HEAD_DIM] for g in range(A_KV_HEADS)]
            vts = [v_ref[0, pl.ds(start, kw), g * HEAD_DIM:(g + 1) * HEAD_DIM] for g in range(A_KV_HEADS)]
            ech = e_ref[:, pl.ds(start, kw)]
            ss = [lax.dot_general(qs[g], kts[g], (((1,), (1,)), ((), ())), preferred_element_type=F32)
                  for g in range(A_KV_HEADS)]
            madds = [jnp.dot(sels[g], ech, preferred_element_type=F32) for g in range(A_KV_HEADS)]
            pbs, alphas = [], []
            for g in range(A_KV_HEADS):
                rows = [[ss[g][h * QBLK:(h + 1) * QBLK, j * QBLK:(j + 1) * QBLK] + b_ref[g * A_GRP + h, tidx[j]]
                         + madds[g][:, j * QBLK:(j + 1) * QBLK] for j in range(SLC_CHUNK)] for h in range(A_GRP)]
                tile_max = jnp.concatenate([functools.reduce(jnp.maximum, r) for r in rows], axis=0)
                m_old = m_refs[g][...]
                m_new = jnp.maximum(m_old, jnp.max(tile_max, axis=-1, keepdims=True))
                alpha = jnp.exp2(m_old - m_new)
                p = [[jnp.exp2(t - m_new[h * QBLK:(h + 1) * QBLK]) for t in r] for h, r in enumerate(rows)]
                tile_sum = jnp.concatenate([functools.reduce(jnp.add, r) for r in p], axis=0)
                l_refs[g][...] = alpha * l_refs[g][...] + jnp.sum(tile_sum, axis=-1, keepdims=True)
                pbs.append(jnp.concatenate([jnp.concatenate([t.astype(BF16) for t in r], axis=1) for r in p], axis=0))
                alphas.append(alpha)
                m_refs[g][...] = m_new
            for g in range(A_KV_HEADS):
                acc_refs[g][...] = alphas[g] * acc_refs[g][...] + jnp.dot(pbs[g], vts[g], preferred_element_type=F32)
            return carry

        lax.fori_loop(0, it // SLC_CHUNK + 1, body, 0)
        for g in range(A_KV_HEADS):
            out = acc_refs[g][...] / l_refs[g][...]
            for h in range(A_GRP):
                col = (g * A_GRP + h) * HEAD_DIM
                o_ref[0, rs, col:col + HEAD_DIM] = out[h * QBLK:(h + 1) * QBLK].astype(o_ref.dtype)


def _slc_attn(proj3, sel, expand, bias):
    B, T, _ = proj3.shape
    n_tiles = bias.shape[1]
    nq = SLC_QTILES
    qrows = nq * QBLK
    assert T % (nq * QBLK) == 0 and T % (SLC_CHUNK * QBLK) == 0
    chain_rows = A_GRP * QBLK
    n_chains = A_KV_HEADS
    return pl.pallas_call(
        functools.partial(_slc_kernel, n_tiles=n_tiles, nq=nq),
        grid=(B, T // qrows),
        in_specs=[
            pl.BlockSpec((1, qrows, A_Q_HEADS * HEAD_DIM), lambda b, i: (b, i, COL_QA // A_Q_HEADS)),
            pl.BlockSpec((1, T, A_KV_HEADS * HEAD_DIM), lambda b, i: (b, 0, COL_KSA // A_KV_HEADS)),
            pl.BlockSpec((1, T, A_KV_HEADS * HEAD_DIM), lambda b, i: (b, 0, COL_VSA // A_KV_HEADS)),
            pl.BlockSpec((1, A_KV_HEADS, qrows, LANES), lambda b, i: (b, 0, i, 0)),
            pl.BlockSpec((LANES, T), lambda b, i: (0, 0)),
            pl.BlockSpec((A_Q_HEADS, n_tiles, QBLK, LANES), lambda b, i: (0, 0, 0, 0)),
        ],
        out_specs=pl.BlockSpec((1, qrows, A_Q_HEADS * HEAD_DIM), lambda b, i: (b, i, 0)),
        out_shape=jax.ShapeDtypeStruct((B, T, A_Q_HEADS * HEAD_DIM), MIXER_OUT),
        scratch_shapes=[pltpu.VMEM((chain_rows, 1), F32), pltpu.VMEM((chain_rows, 1), F32),
                        pltpu.VMEM((chain_rows, HEAD_DIM), F32)] * n_chains,
        compiler_params=_cparams(("parallel", "arbitrary")),
        name="nsa_slc_attn",
    )(proj3, proj3, proj3, sel, expand, bias)


def _mix_outproj_kernel(x_ref, gate_ref, ocmp_ref, oslc_ref, owin_ref, ob_ref, *rest):
    oc_refs, lse_refs = rest[:C_Q_HEADS], rest[C_Q_HEADS:2 * C_Q_HEADS]
    w_ref, o_ref, mix_ref = rest[2 * C_Q_HEADS:]
    tm = x_ref.shape[0]
    nsplit = MIX_ROW_SPLIT
    rows = tm // nsplit
    for part in range(nsplit):
        rs = slice(part * rows, (part + 1) * rows)
        gate = 1.0 / (1.0 + jnp.exp(-gate_ref[rs, :]))
        for h in range(A_Q_HEADS):
            sl = slice(h * HEAD_DIM, (h + 1) * HEAD_DIM)
            o = (gate[:, 3 * h:3 * h + 1] * ocmp_ref[rs, sl] + gate[:, 3 * h + 1:3 * h + 2] * oslc_ref[rs, sl]
                 + gate[:, 3 * h + 2:3 * h + 3] * owin_ref[rs, sl])
            mix_ref[rs, sl] = o.astype(mix_ref.dtype)
        base = A_Q_HEADS * HEAD_DIM
        width = B_Q_HEADS * HEAD_DIM
        mix_ref[rs, base:base + width] = ob_ref[rs, :].astype(mix_ref.dtype)
        base += width
        for hh in range(C_GRP):
            heads = [C_GRP * gidx + hh for gidx in range(len(DIL_PAIRS))]
            lses = [lse_refs[c][rs, :] for c in heads]
            mx = functools.reduce(jnp.maximum, lses)
            ws = [jnp.exp(x - mx) for x in lses]
            tot = functools.reduce(jnp.add, ws)
            for c, w in zip(heads, ws):
                mix_ref[rs, base + c * HEAD_DIM:base + (c + 1) * HEAD_DIM] = (oc_refs[c][rs, :] * (w / tot)).astype(mix_ref.dtype)
    for part in range(nsplit):
        rs = slice(part * rows, (part + 1) * rows)
        a = mix_ref[rs, :]
        for c0 in range(0, o_ref.shape[1], OUT_SUBTILE):
            sl = slice(c0, c0 + OUT_SUBTILE)
            o_ref[rs, sl] = x_ref[rs, sl] + jnp.dot(a, w_ref[:, sl], preferred_element_type=F32)


def _mix_outproj(x2d, gates, o_cmp, o_slc, o_win, o_b, o_cs, lses, w_all, layer, *, tm):
    m, d = x2d.shape
    k = w_all.shape[1]
    row = lambda w: pl.BlockSpec((tm, w), lambda i: (i, 0))
    head = lambda c: pl.BlockSpec((tm, HEAD_DIM), lambda i: (i, c))
    full = [x2d, gates, o_cmp, o_slc, o_win, o_b]
    return pl.pallas_call(
        _mix_outproj_kernel,
        grid=(m // tm,),
        in_specs=[row(a.shape[1]) for a in full] + [head(c) for _, c in o_cs + lses]
        + [pl.BlockSpec((None, k, d), lambda i: (layer, 0, 0), pipeline_mode=pl.Buffered(1))],
        out_specs=row(d),
        out_shape=jax.ShapeDtypeStruct((m, d), F32),
        scratch_shapes=[pltpu.VMEM((tm, k), BF16)],
        compiler_params=_cparams(("parallel",)),
        name="mix_out_proj",
    )(*full, *[a for a, _ in o_cs + lses], w_all)


def _mixers_outproj(x2d, proj3, gates, dil32, layer_params, tables, w_out_all, layer, *, tm):
    B, T, _ = proj3.shape
    cmp_pe, cmp_w1, cmp_w2, kc_gain, sinks = layer_params
    n_cmp = (T - CMP_BLOCK) // CMP_STRIDE + 1
    n_slc = T // SLC_BLOCK
    pe = jnp.broadcast_to(cmp_pe.reshape(2, 1, CMP_BLOCK * HEAD_DIM), (2, 8, CMP_BLOCK * HEAD_DIM)).astype(BF16)
    kvc = _compress(dil32, pe, cmp_w1.astype(BF16), cmp_w2.astype(BF16), kc_gain.reshape(1, HEAD_DIM), n_cmp)

    o_cmp, sel = _cmp_select(proj3, kvc, tables["bias_c"], tables["ovl"], n_slc)
    o_slc = _slc_attn(proj3, sel, tables["expand"], tables["bias_slc"])
    o_win = _band_attn(proj3, proj3, proj3, tables["bias_win"], n_kv=A_KV_HEADS, grp=A_GRP,
                       q_col=COL_QA // A_GRP, k_col=COL_KWA, v_col=COL_VWA)
    o_b = _band_attn(proj3, proj3, proj3, tables["bias_b"], n_kv=B_KV_HEADS, grp=B_GRP,
                     q_col=COL_QB // B_GRP, k_col=COL_KB, v_col=COL_VB, sinks=sinks)
    o_cs, lses = [], []
    for gidx, (_, dil) in enumerate(DIL_PAIRS):
        if dil == 1:
            o, lse = _band_attn(proj3, proj3, proj3, tables["bias_c%d" % gidx], n_kv=1, grp=C_GRP,
                                q_col=COL_QC // C_GRP + gidx, k_col=COL_KC + gidx, v_col=COL_VC + gidx,
                                with_lse=True)
            o_cs += [(o.reshape(B * T, -1), h) for h in range(C_GRP)]
            lses += [(lse.reshape(B * T, -1), h) for h in range(C_GRP)]
        else:
            os_, ls_ = _dil_attn(dil32, tables["bias_c%d" % gidx], col=DIL_F32_SLOT[COL_QC + C_GRP * gidx], dil=dil)
            o_cs += [(o.reshape(B * T, -1), 0) for o in os_]
            lses += [(lse.reshape(B * T, -1), 0) for lse in ls_]
    return _mix_outproj(x2d, gates, o_cmp.reshape(B * T, -1), o_slc.reshape(B * T, -1), o_win.reshape(B * T, -1),
                        o_b.reshape(B * T, -1), o_cs, lses, w_out_all, layer, tm=tm)


def _build_tables(rel_bias, T):
    n_cmp = (T - CMP_BLOCK) // CMP_STRIDE + 1
    n_slc = T // SLC_BLOCK
    ncp = T // CMP_STRIDE
    nq = T // QBLK
    tables = {}
    tables["bias_c"] = _cmp_bias(rel_bias, A_Q_HEADS, nq, ncp, n_cmp).reshape(A_Q_HEADS, T, ncp)
    n_sat = -(-(SAT_DIST + QBLK - 1) // QBLK) + 1
    tables["bias_slc"] = _bias_tiles(rel_bias, 0, A_Q_HEADS, min(n_sat, nq) + 1, koff=QBLK)
    win = NSA_WINDOW - 1
    tables["bias_win"] = _bias_tiles(rel_bias, 0, A_Q_HEADS, -(-win // QBLK) + 2, max_dist=win)
    swa = SWA_WINDOW - 1
    tables["bias_b"] = _bias_tiles(rel_bias, A_Q_HEADS, B_Q_HEADS, -(-swa // QBLK) + 2, max_dist=swa)
    for gidx, (w, dil) in enumerate(DIL_PAIRS):
        md = w // dil
        tables["bias_c%d" % gidx] = _bias_tiles(rel_bias, A_Q_HEADS + B_Q_HEADS + C_GRP * gidx, C_GRP,
                                                -(-md // QBLK) + 2, dscale=dil, max_dist=md)
    c0 = np.arange(ncp)[:, None] * CMP_STRIDE
    s0 = np.arange(LANES)[None, :] * SLC_BLOCK
    ovl = np.clip(np.minimum(c0 + CMP_BLOCK, s0 + SLC_BLOCK) - np.maximum(c0, s0), 0, None) / CMP_BLOCK
    ovl = ovl * (np.arange(ncp)[:, None] < n_cmp) * (np.arange(LANES)[None, :] < n_slc)
    tables["ovl"] = jnp.asarray(ovl, BF16)
    member = (np.arange(T)[None, :] // SLC_BLOCK) == np.arange(LANES)[:, None]
    tables["expand"] = jnp.asarray(member * MASK_WEIGHT, BF16)
    return tables


def _proj_gain(g):
    ones = jnp.ones((HEAD_DIM,), F32)
    spec = [(g[0] * (SCALE * LOG2E), 6), (ones, 4), (g[2], 2), (ones, 2), (g[3], 2), (ones, 2),
            (g[4] * (SCALE * LOG2E), 4), (g[5], 2), (ones, 2), (g[6] * (SCALE * LOG2E), 6), (g[7], 3), (ones, 3)]
    assert sum(n for _, n in spec) == N_MAIN_BLOCKS
    return jnp.concatenate([jnp.tile(v, n) for v, n in spec]).reshape(1, N_MAIN)


def kernel(x, norm_attn, w_in, qk_gain, cmp_pe, cmp_w1, cmp_w2, sinks, rel_bias, w_out, norm_ffn, w_gate, w_up, w_down):
    B, T, D = x.shape
    depth = w_in.shape[0]
    tables = _build_tables(rel_bias, T)
    x2 = x.reshape(B * T, D)
    w_all = _wprep(jnp.transpose(w_in, (0, 2, 1)).astype(BF16))
    w_out_b, w_gate_b, w_up_b, w_down_b = (w.astype(BF16) for w in (w_out, w_gate, w_up, w_down))
    tm = min(ROW_TILE, B * T)
    for l in range(depth):
        proj, gates, dil32 = _proj(x2, norm_attn[l].reshape(1, D), w_all, l, _proj_gain(qk_gain[l]), tm=tm)
        x2 = _mixers_outproj(x2, proj.reshape(B, T, N_MAIN), gates, dil32.reshape(B, T, DIL_F32_COLS),
                             (cmp_pe[l], cmp_w1[l], cmp_w2[l], qk_gain[l][1], sinks[l]), tables, w_out_b, l, tm=tm)
        x2 = _ffn(x2, norm_ffn[l].reshape(1, D), w_gate_b, w_up_b, w_down_b, l, tm=tm, tf=FFN_TILE)
    return x2.reshape(B, T, D)
```

```python
import functools
import math

import numpy as np
import jax
import jax.numpy as jnp
from jax import lax
from jax.experimental import pallas as pl
from jax.experimental.pallas import tpu as pltpu

F32 = jnp.float32
BF16 = jnp.bfloat16

HEAD_DIM = 128
MIXER_OUT = jnp.bfloat16
LANES = 128
SUBLANES = 8
QBLK = 128
BAND_SUBBLOCKS = 8
CMP_SUBBLOCKS = 4
A_Q_HEADS, A_KV_HEADS = 6, 2
A_GRP = A_Q_HEADS // A_KV_HEADS
B_Q_HEADS, B_KV_HEADS = 4, 2
B_GRP = B_Q_HEADS // B_KV_HEADS
DIL_PAIRS = ((128, 1), (512, 4), (2048, 16))
C_GRP = 2
C_Q_HEADS = C_GRP * len(DIL_PAIRS)
CMP_BLOCK, CMP_STRIDE = 32, 16
SLC_BLOCK, SLC_TOPK = 64, 16
SLC_CHUNK = 4
SLC_QTILES = 4
NSA_WINDOW, SWA_WINDOW = 512, 128
FORCED_SCORE = 1.0e4
REL_BUCKETS, REL_MAX_EXACT, REL_MAX_DIST = 32, 16, 2048
SCALE = HEAD_DIM ** -0.5
LOG2E = math.log2(math.e)
LN2 = math.log(2.0)
MASK_WEIGHT = -2.0 ** 100
EPS = 1e-6
NEG = -1e30
VMEM_LIMIT = 56 * 1024 * 1024

COL_QA, COL_KCA, COL_VCA, COL_KSA, COL_VSA, COL_KWA, COL_VWA = 0, 6, 8, 10, 12, 14, 16
COL_QB, COL_KB, COL_VB, COL_QC, COL_KC, COL_VC = 18, 22, 24, 26, 32, 35
N_MAIN_BLOCKS = 38
N_MAIN = N_MAIN_BLOCKS * LANES
GATE_START = 2304
N_GATES = A_Q_HEADS * 3
PROJ_NORMED = ([True] * 6 + [False] * 4 + [True] * 2 + [False] * 2 + [True] * 2 + [False] * 2
               + [True] * 6 + [False] * 2 + [True] * 9 + [False] * 3)
DIL_F32_SLOT = {}
for _g, (_, _d) in enumerate(DIL_PAIRS):
    if _d > 1:
        _base = 4 * len([1 for _, _dd in DIL_PAIRS[:_g] if _dd > 1])
        DIL_F32_SLOT.update({COL_QC + C_GRP * _g: _base, COL_QC + C_GRP * _g + 1: _base + 1,
                             COL_KC + _g: _base + 2, COL_VC + _g: _base + 3})
CMP_F32_SLOT0 = len(DIL_F32_SLOT)
DIL_F32_SLOT.update({COL_KCA + _i: CMP_F32_SLOT0 + _i for _i in range(2 * A_KV_HEADS)})
DIL_F32_COLS = LANES * len(DIL_F32_SLOT)
DIL_TILES_PER_STEP = 8
ROW_TILE = 512
FFN_TILE = 512
WPREP_ROWS = 384
PROJ_SUBTILE = 256
MIX_ROW_SPLIT = 4
OUT_SUBTILE = 512


def _bucket_starts():
    d = np.arange(0, 1 << 17)
    out = []
    for dt in (np.float32, np.float64):
        far = np.maximum(d, REL_MAX_EXACT).astype(dt)
        lb = REL_MAX_EXACT + (np.log(far / dt(REL_MAX_EXACT)) / dt(math.log(REL_MAX_DIST / REL_MAX_EXACT))
                              * dt(REL_BUCKETS - REL_MAX_EXACT)).astype(np.int64)
        out.append(np.where(d < REL_MAX_EXACT, d, np.minimum(lb, REL_BUCKETS - 1)))
    assert (out[0] == out[1]).all() and (np.diff(out[0]) >= 0).all()
    return [int(np.argmax(out[0] >= b)) for b in range(REL_BUCKETS)]


BUCKET_START = _bucket_starts()
SAT_DIST = BUCKET_START[REL_BUCKETS - 1]


def _cparams(sem, vmem=VMEM_LIMIT):
    return pltpu.CompilerParams(dimension_semantics=sem, vmem_limit_bytes=vmem)


def _bias_tile_kernel(tab_ref, o_ref, *, head0, koff, dscale, max_dist):
    h = pl.program_id(0) + head0
    t = pl.program_id(1)
    r = lax.broadcasted_iota(jnp.int32, (QBLK, LANES), 0)
    c = lax.broadcasted_iota(jnp.int32, (QBLK, LANES), 1)
    dist = t * QBLK + r - c - koff
    d = dist * dscale
    val = jnp.full((QBLK, LANES), tab_ref[REL_BUCKETS - 1, h] * LOG2E, F32)
    for b in range(REL_BUCKETS - 2, -1, -1):
        val = jnp.where(d < BUCKET_START[b + 1], tab_ref[b, h] * LOG2E, val)
    val = jnp.where(dist >= 0, val, NEG)
    if max_dist is not None:
        val = jnp.where(dist <= max_dist, val, NEG)
    o_ref[0, 0] = val


def _bias_tiles(rel_bias, head0, nheads, ntiles, *, koff=0, dscale=1, max_dist=None):
    kern = functools.partial(_bias_tile_kernel, head0=head0, koff=koff, dscale=dscale, max_dist=max_dist)
    return pl.pallas_call(
        kern,
        grid=(nheads, ntiles),
        in_specs=[pl.BlockSpec(memory_space=pltpu.SMEM)],
        out_specs=pl.BlockSpec((1, 1, QBLK, LANES), lambda h, t: (h, t, 0, 0)),
        out_shape=jax.ShapeDtypeStruct((nheads, ntiles, QBLK, LANES), F32),
        compiler_params=_cparams(("parallel", "parallel")),
        name="bias_tiles",
    )(rel_bias)


def _cmp_bias_kernel(tab_ref, o_ref, *, n_cmp):
    h = pl.program_id(0)
    ntile, _, width = o_ref.shape[1:]
    r = lax.broadcasted_iota(jnp.int32, (QBLK, 2 * width), 0)
    c = lax.broadcasted_iota(jnp.int32, (QBLK, 2 * width), 1)
    dist = r - CMP_STRIDE * (c - width) - (CMP_BLOCK - 1)
    base = jnp.full((QBLK, 2 * width), tab_ref[REL_BUCKETS - 1, h] * LOG2E, F32)
    for b in range(REL_BUCKETS - 2, -1, -1):
        base = jnp.where(dist < BUCKET_START[b + 1], tab_ref[b, h] * LOG2E, base)
    base = jnp.where(dist >= 0, base, NEG)
    col = lax.broadcasted_iota(jnp.int32, (QBLK, width), 1)
    per_tile = QBLK // CMP_STRIDE
    for i in range(ntile):
        tile = pltpu.roll(base, (width + per_tile * i) % (2 * width), 1)[:, :width]
        o_ref[0, i] = jnp.where(col < n_cmp, tile, NEG)


def _cmp_bias(rel_bias, nheads, ntiles, width, n_cmp):
    return pl.pallas_call(
        functools.partial(_cmp_bias_kernel, n_cmp=n_cmp),
        grid=(nheads,),
        in_specs=[pl.BlockSpec(memory_space=pltpu.SMEM)],
        out_specs=pl.BlockSpec((1, ntiles, QBLK, width), lambda h: (h, 0, 0, 0)),
        out_shape=jax.ShapeDtypeStruct((nheads, ntiles, QBLK, width), F32),
        compiler_params=_cparams(("parallel",)),
        name="cmp_bias",
    )(rel_bias)


def _rms(x, w):
    ms = jnp.mean(x * x, axis=-1, keepdims=True)
    return x * lax.rsqrt(ms + EPS) * w


def _wprep_kernel(w_ref, nxt_ref, gate_ref, o_ref):
    j = pl.program_id(1)
    rows = o_ref.shape[1]
    gate_step = GATE_START // rows
    last = pl.num_programs(1) - 1

    @pl.when(j < gate_step)
    def _():
        o_ref[0] = w_ref[0]

    @pl.when((j >= gate_step) & (j < last))
    def _():
        o_ref[0, :rows - N_GATES] = w_ref[0, N_GATES:]
        o_ref[0, rows - N_GATES:] = nxt_ref[0, :N_GATES]

    @pl.when(j == last)
    def _():
        o_ref[0, :rows - LANES] = w_ref[0, N_GATES:N_GATES + rows - LANES]
        o_ref[0, rows - LANES:] = gate_ref[0, :LANES]


def _wprep(wt):
    depth, n_in, d = wt.shape
    rows = WPREP_ROWS
    assert GATE_START % rows == 0 and (N_MAIN + LANES) % rows == 0 and N_MAIN % rows == rows - LANES
    steps = (N_MAIN + LANES) // rows
    return pl.pallas_call(
        _wprep_kernel,
        grid=(depth, steps),
        in_specs=[pl.BlockSpec((1, rows, d), lambda l, j: (l, j, 0)),
                  pl.BlockSpec((1, rows, d), lambda l, j: (l, jnp.minimum(j + 1, steps - 1), 0)),
                  pl.BlockSpec((1, rows, d), lambda l, j: (l, GATE_START // rows, 0))],
        out_specs=pl.BlockSpec((1, rows, d), lambda l, j: (l, j, 0)),
        out_shape=jax.ShapeDtypeStruct((depth, N_MAIN + LANES, d), BF16),
        compiler_params=_cparams(("parallel", "parallel")),
        name="w_in_prep",
    )(wt, wt, wt)


def _proj_kernel(x_ref, nw_ref, w_ref, gain_ref, o_ref, gate_ref, dil_ref):
    h = _rms(x_ref[...], nw_ref[...]).astype(BF16)
    for c0 in range(0, N_MAIN, PROJ_SUBTILE):
        width = min(PROJ_SUBTILE, N_MAIN - c0)
        acc = lax.dot_general(h, w_ref[c0:c0 + width, :], (((1,), (1,)), ((), ())), preferred_element_type=F32)
        for c in range(width // LANES):
            blk = c0 // LANES + c
            sl = slice(blk * LANES, (blk + 1) * LANES)
            y = acc[:, c * LANES:(c + 1) * LANES]
            if PROJ_NORMED[blk]:
                ms = jnp.mean(y * y, axis=-1, keepdims=True)
                y = y * lax.rsqrt(ms + EPS) * gain_ref[:, sl]
            o_ref[:, sl] = y.astype(o_ref.dtype)
            if blk in DIL_F32_SLOT:
                dil_ref[:, DIL_F32_SLOT[blk] * LANES:(DIL_F32_SLOT[blk] + 1) * LANES] = y
    gate_ref[...] = lax.dot_general(h, w_ref[N_MAIN:, :], (((1,), (1,)), ((), ())), preferred_element_type=F32)


def _proj(x2d, norm_w, w_all, layer, gain, *, tm):
    m, d = x2d.shape
    n = w_all.shape[1]
    return pl.pallas_call(
        _proj_kernel,
        grid=(m // tm,),
        in_specs=[
            pl.BlockSpec((tm, d), lambda i: (i, 0)),
            pl.BlockSpec((1, d), lambda i: (0, 0)),
            pl.BlockSpec((None, n, d), lambda i: (layer, 0, 0), pipeline_mode=pl.Buffered(1)),
            pl.BlockSpec((1, N_MAIN), lambda i: (0, 0)),
        ],
        out_specs=(pl.BlockSpec((tm, N_MAIN), lambda i: (i, 0)), pl.BlockSpec((tm, LANES), lambda i: (i, 0)),
                   pl.BlockSpec((tm, DIL_F32_COLS), lambda i: (i, 0))),
        out_shape=(jax.ShapeDtypeStruct((m, N_MAIN), BF16), jax.ShapeDtypeStruct((m, LANES), F32),
                   jax.ShapeDtypeStruct((m, DIL_F32_COLS), F32)),
        compiler_params=_cparams(("parallel",)),
        name="in_proj",
    )(x2d, norm_w, w_all, gain)


def _ffn_kernel(x_ref, nw_ref, wg_ref, wu_ref, wd_ref, o_ref, h_ref):
    f = pl.program_id(1)

    @pl.when(f == 0)
    def _():
        x = x_ref[...]
        h_ref[...] = _rms(x, nw_ref[...]).astype(BF16)
        o_ref[...] = x

    h = h_ref[...]
    g = jnp.dot(h, wg_ref[...], preferred_element_type=F32)
    u = jnp.dot(h, wu_ref[...], preferred_element_type=F32)
    a = (g * (1.0 / (1.0 + jnp.exp(-g))) * u).astype(BF16)
    o_ref[...] += jnp.dot(a, wd_ref[...], preferred_element_type=F32)


def _ffn(x2d, norm_w, wg, wu, wd, layer, *, tm, tf):
    m, d = x2d.shape
    dff = wg.shape[2]
    return pl.pallas_call(
        _ffn_kernel,
        grid=(m // tm, dff // tf),
        in_specs=[
            pl.BlockSpec((tm, d), lambda i, f: (i, 0)),
            pl.BlockSpec((1, d), lambda i, f: (0, 0)),
            pl.BlockSpec((None, d, tf), lambda i, f: (layer, 0, f)),
            pl.BlockSpec((None, d, tf), lambda i, f: (layer, 0, f)),
            pl.BlockSpec((None, tf, d), lambda i, f: (layer, f, 0)),
        ],
        out_specs=pl.BlockSpec((tm, d), lambda i, f: (i, 0)),
        out_shape=jax.ShapeDtypeStruct((m, d), F32),
        scratch_shapes=[pltpu.VMEM((tm, d), BF16)],
        compiler_params=_cparams(("parallel", "arbitrary")),
        name="ffn",
    )(x2d, norm_w, wg, wu, wd)


def _band_kernel(*refs, grp, n_off, nsub, has_sinks, with_lse):
    if has_sinks:
        sink_ref, q_ref, k_ref, v_ref, b_ref = refs[:5]
        outs = refs[5:]
    else:
        q_ref, k_ref, v_ref, b_ref = refs[:4]
        outs = refs[4:]
    o_ref = outs[0]
    g = pl.program_id(1)
    i0 = pl.program_id(2) * nsub
    tiles = {}

    def kv_tile(rel):
        if rel not in tiles:
            start = pl.multiple_of(jnp.maximum(i0 + rel, 0) * QBLK, QBLK)
            tiles[rel] = (k_ref[0, pl.ds(start, QBLK), :], v_ref[0, pl.ds(start, QBLK), :])
        return tiles[rel]

    def head_rows(x, h):
        return x[h * QBLK:(h + 1) * QBLK]

    kvs = [[kv_tile(j - off) for off in range(n_off)] for j in range(nsub)]
    tidxs = [[jnp.where(i0 + j - off >= 0, off, n_off) for off in range(n_off)] for j in range(nsub)]
    scores = []
    for j in range(nsub):
        rs = slice(j * QBLK, (j + 1) * QBLK)
        kcat = jnp.concatenate([t[0] for t in kvs[j]], axis=0)
        q = jnp.concatenate([q_ref[0, rs, h * HEAD_DIM:(h + 1) * HEAD_DIM] for h in range(grp)], axis=0)
        scores.append(lax.dot_general(q, kcat, (((1,), (1,)), ((), ())), preferred_element_type=F32))
    probs, stats = [], []
    for j in range(nsub):
        s = scores[j]
        st = [[head_rows(s, h)[:, off * QBLK:(off + 1) * QBLK] + b_ref[h, tidxs[j][off]] for off in range(n_off)]
              for h in range(grp)]
        m = jnp.max(jnp.concatenate([functools.reduce(jnp.maximum, r) for r in st], axis=0), axis=-1, keepdims=True)
        p = [[jnp.exp2(st[h][off] - head_rows(m, h)) for off in range(n_off)] for h in range(grp)]
        l = jnp.sum(jnp.concatenate([functools.reduce(jnp.add, r) for r in p], axis=0), axis=-1, keepdims=True)
        probs.append(jnp.concatenate([jnp.concatenate([t.astype(BF16) for t in r], axis=1) for r in p], axis=0))
        stats.append((m, l))
    for j in range(nsub):
        rs = slice(j * QBLK, (j + 1) * QBLK)
        m, l = stats[j]
        vcat = jnp.concatenate([t[1] for t in kvs[j]], axis=0)
        o = jnp.dot(probs[j], vcat, preferred_element_type=F32)
        den = l
        if has_sinks:
            sink = jnp.concatenate([jnp.full((QBLK, 1), sink_ref[g * grp + h] * LOG2E, F32) for h in range(grp)], axis=0)
            den = l + jnp.exp2(sink - m)
        o = o / den
        for h in range(grp):
            sl = slice(h * HEAD_DIM, (h + 1) * HEAD_DIM)
            o_ref[0, rs, sl] = head_rows(o, h).astype(o_ref.dtype)
            if with_lse:
                outs[1][0, rs, sl] = jnp.broadcast_to(head_rows(m * LN2 + jnp.log(l), h), (QBLK, HEAD_DIM))


def _band_attn(q_arr, k_arr, v_arr, bias, *, n_kv, grp, q_col, k_col, v_col, sinks=None, with_lse=False):
    n, L, _ = q_arr.shape
    n_off = bias.shape[1] - 1
    gw = grp * HEAD_DIM
    nsub = math.gcd(BAND_SUBBLOCKS, L // QBLK)
    rows = nsub * QBLK
    kern = functools.partial(_band_kernel, grp=grp, n_off=n_off, nsub=nsub, has_sinks=sinks is not None,
                             with_lse=with_lse)
    in_specs = [
        pl.BlockSpec((1, rows, gw), lambda b, g, i: (b, i, q_col + g)),
        pl.BlockSpec((1, L, HEAD_DIM), lambda b, g, i: (b, 0, k_col + g)),
        pl.BlockSpec((1, L, HEAD_DIM), lambda b, g, i: (b, 0, v_col + g)),
        pl.BlockSpec((grp, n_off + 1, QBLK, LANES), lambda b, g, i: (g, 0, 0, 0)),
    ]
    args = [q_arr, k_arr, v_arr, bias]
    if sinks is not None:
        in_specs = [pl.BlockSpec(memory_space=pltpu.SMEM)] + in_specs
        args = [sinks] + args
    o_spec = pl.BlockSpec((1, rows, gw), lambda b, g, i: (b, i, g))
    o_shape = jax.ShapeDtypeStruct((n, L, n_kv * gw), MIXER_OUT)
    lse_shape = jax.ShapeDtypeStruct((n, L, n_kv * gw), F32)
    return pl.pallas_call(
        kern,
        grid=(n, n_kv, L // rows),
        in_specs=in_specs,
        out_specs=(o_spec, o_spec) if with_lse else o_spec,
        out_shape=(o_shape, lse_shape) if with_lse else o_shape,
        compiler_params=_cparams(("parallel", "parallel", "arbitrary")),
        name="band_attn",
    )(*args)


def _dil_kernel(*refs, dil, nres):
    q_refs, (k_ref, v_ref, b_ref) = refs[:C_GRP], refs[C_GRP:C_GRP + 3]
    o_refs, lse_refs = refs[C_GRP + 3:2 * C_GRP + 3], refs[2 * C_GRP + 3:]
    ls = k_ref.shape[1] // dil
    ntile = ls // QBLK
    r0 = pl.program_id(1) * nres
    units = [(rr, j) for rr in range(nres) for j in range(ntile)]

    def rows(rr, j):
        return pl.ds(r0 + rr + dil * QBLK * j, QBLK, stride=dil)

    kv = {u: (k_ref[0, rows(*u), :].astype(BF16), v_ref[0, rows(*u), :].astype(BF16)) for u in units}
    n_off = b_ref.shape[1] - 1
    offs = {(rr, j): [off for off in range(n_off) if j - off >= 0] for rr, j in units}
    scores = []
    for rr, j in units:
        kcat = jnp.concatenate([kv[(rr, j - off)][0] for off in offs[(rr, j)]], axis=0)
        q = jnp.concatenate([q_ref[0, rows(rr, j), :].astype(BF16) for q_ref in q_refs], axis=0)
        scores.append(lax.dot_general(q, kcat, (((1,), (1,)), ((), ())), preferred_element_type=F32))
    probs, stats = [], []
    for u, s in zip(units, scores):
        st = [[s[h * QBLK:(h + 1) * QBLK, n * QBLK:(n + 1) * QBLK] + b_ref[h, off] for n, off in enumerate(offs[u])]
              for h in range(C_GRP)]
        m = jnp.max(jnp.concatenate([functools.reduce(jnp.maximum, r) for r in st], axis=0), axis=-1, keepdims=True)
        p = [[jnp.exp2(t - m[h * QBLK:(h + 1) * QBLK]) for t in r] for h, r in enumerate(st)]
        l = jnp.sum(jnp.concatenate([functools.reduce(jnp.add, r) for r in p], axis=0), axis=-1, keepdims=True)
        probs.append(jnp.concatenate([jnp.concatenate([t.astype(BF16) for t in r], axis=1) for r in p], axis=0))
        stats.append((m, l))
    for (rr, j), pb, (m, l) in zip(units, probs, stats):
        vcat = jnp.concatenate([kv[(rr, j - off)][1] for off in offs[(rr, j)]], axis=0)
        o = jnp.dot(pb, vcat, preferred_element_type=F32) / l
        lse = m * LN2 + jnp.log(l)
        for h in range(C_GRP):
            o_refs[h][0, rows(rr, j), :] = o[h * QBLK:(h + 1) * QBLK]
            lse_refs[h][0, rows(rr, j), :] = jnp.broadcast_to(lse[h * QBLK:(h + 1) * QBLK], (QBLK, HEAD_DIM))


def _dil_attn(x3, bias, *, col, dil):
    B, T, _ = x3.shape
    ntile = T // dil // QBLK
    nres = max(1, min(dil, DIL_TILES_PER_STEP // ntile))
    spec = pl.BlockSpec((1, T, HEAD_DIM), lambda b, r: (b, 0, 0))
    shape = jax.ShapeDtypeStruct((B, T, HEAD_DIM), F32)
    outs = pl.pallas_call(
        functools.partial(_dil_kernel, dil=dil, nres=nres),
        grid=(B, dil // nres),
        in_specs=[pl.BlockSpec((1, T, HEAD_DIM), functools.partial(lambda b, r, c: (b, 0, c), c=col + c))
                  for c in range(C_GRP + 2)]
        + [pl.BlockSpec((C_GRP, bias.shape[1], QBLK, LANES), lambda b, r: (0, 0, 0, 0))],
        out_specs=(spec,) * (2 * C_GRP),
        out_shape=(shape,) * (2 * C_GRP),
        compiler_params=_cparams(("parallel", "arbitrary")),
        name="dilated_attn",
    )(*([x3] * (C_GRP + 2)), bias)
    return outs[:C_GRP], outs[C_GRP:]


def _compress_kernel(x_ref, pe_ref, w1_ref, w2_ref, gain_ref, o_ref, *, n_cmp):
    half = w1_ref.shape[1] // 2
    rows = x_ref.shape[1] // CMP_STRIDE
    xr = jnp.concatenate([x_ref[0, pl.ds(t, rows, stride=CMP_STRIDE), :].astype(BF16) for t in range(CMP_STRIDE)],
                         axis=1)
    y0 = jnp.dot(xr, w1_ref[0, :half], preferred_element_type=F32)
    y1 = jnp.dot(xr, w1_ref[0, half:], preferred_element_type=F32)
    pe = jnp.dot(pe_ref[0], w1_ref[0], preferred_element_type=F32)[0:1]
    c = y0 + pltpu.roll(y1, rows - 1, 0) + pe
    gl = 0.5 * c * (1.0 + jnp.tanh(math.sqrt(2.0 / math.pi) * (c + 0.044715 * (c * c * c))))
    out = jnp.dot(gl.astype(BF16), w2_ref[0], preferred_element_type=F32)
    out = jnp.where(pl.program_id(0) == 0, _rms(out, gain_ref[...]), out)
    valid = lax.broadcasted_iota(jnp.int32, out.shape, 0) < n_cmp
    o_ref[0, 0, 0] = jnp.where(valid, out, 0.0).astype(o_ref.dtype)


def _compress(x3, pe, w1, w2, gain, n_cmp):
    B, T, _ = x3.shape
    rows = T // CMP_STRIDE
    wide = CMP_BLOCK * HEAD_DIM
    return pl.pallas_call(
        functools.partial(_compress_kernel, n_cmp=n_cmp),
        grid=(2, B, A_KV_HEADS),
        in_specs=[
            pl.BlockSpec((1, T, HEAD_DIM), lambda s, b, g: (b, 0, CMP_F32_SLOT0 + s * A_KV_HEADS + g)),
            pl.BlockSpec((1, 8, wide), lambda s, b, g: (s, 0, 0)),
            pl.BlockSpec((1, wide, HEAD_DIM), lambda s, b, g: (s, 0, 0)),
            pl.BlockSpec((1, HEAD_DIM, HEAD_DIM), lambda s, b, g: (s, 0, 0)),
            pl.BlockSpec((1, HEAD_DIM), lambda s, b, g: (0, 0)),
        ],
        out_specs=pl.BlockSpec((1, 1, 1, rows, HEAD_DIM), lambda s, b, g: (s, b, g, 0, 0)),
        out_shape=jax.ShapeDtypeStruct((2, B, A_KV_HEADS, rows, HEAD_DIM), BF16),
        compiler_params=_cparams(("arbitrary", "arbitrary", "arbitrary")),
        name="nsa_compress",
    )(x3, pe, w1, w2, gain)


def _cmp_select_kernel(q_ref, kc_ref, vc_ref, b_ref, ovl_ref, o_ref, sel_ref, rank_ref, *, n_slc, nsub):
    i0 = pl.program_id(2) * nsub
    rows = nsub * QBLK
    kc = kc_ref[0, 0, 0]
    vc = vc_ref[0, 0, 0]
    q_all = jnp.concatenate([q_ref[0, :, h * HEAD_DIM:(h + 1) * HEAD_DIM] for h in range(A_GRP)], axis=0)
    s_all = lax.dot_general(q_all, kc, (((1,), (1,)), ((), ())), preferred_element_type=F32)
    ps = []
    for h in range(A_GRP):
        s = s_all[h * rows:(h + 1) * rows] + b_ref[h]
        m = jnp.max(s, axis=-1, keepdims=True)
        e = jnp.exp2(s - jnp.where(m > 0.5 * NEG, m, 0.0))
        den = jnp.sum(e, axis=-1, keepdims=True)
        ps.append((e / jnp.where(den > 0, den, 1.0)).astype(BF16))
    r_all = jnp.dot(jnp.concatenate(ps, axis=0), jnp.concatenate([vc, ovl_ref[...]], axis=1),
                    preferred_element_type=F32)
    imp = jnp.zeros((rows, LANES), F32)
    for h in range(A_GRP):
        o_ref[0, :, h * HEAD_DIM:(h + 1) * HEAD_DIM] = r_all[h * rows:(h + 1) * rows, :HEAD_DIM].astype(o_ref.dtype)
        imp = imp + r_all[h * rows:(h + 1) * rows, HEAD_DIM:]
    t = i0 * QBLK + lax.broadcasted_iota(jnp.int32, (rows, LANES), 0)
    blk = lax.broadcasted_iota(jnp.int32, (rows, LANES), 1)
    cur = t // SLC_BLOCK
    imp = jnp.where(blk == 0, FORCED_SCORE, imp)
    imp = jnp.where(blk == cur, FORCED_SCORE, imp)
    imp = jnp.where(blk == cur - 1, FORCED_SCORE, imp)
    imp = jnp.where(blk * SLC_BLOCK <= t, imp, NEG)
    imp = jnp.where(blk < n_slc, imp, 2.0 * NEG)
    imp_t = jnp.concatenate([imp[j * QBLK:(j + 1) * QBLK].T for j in range(nsub)], axis=1)
    ngrp = -(-n_slc // SUBLANES)
    cands = [imp_t[r * SUBLANES:(r + 1) * SUBLANES] for r in range(ngrp)]
    row_id = lax.broadcasted_iota(jnp.int32, (SUBLANES, rows), 0)
    rank_ref[...] = jnp.zeros(rank_ref.shape, F32)
    last_blk = ((i0 + nsub) * QBLK - 1) // SLC_BLOCK
    for jg in range(ngrp):
        @pl.when(jg * SUBLANES <= last_blk)
        def _(jg=jg):
            for j in range(jg * SUBLANES, min((jg + 1) * SUBLANES, n_slc)):
                other = imp_t[j:j + 1, :]
                for r in range(ngrp):
                    if r * SUBLANES > j:
                        beats = other >= cands[r]
                    elif (r + 1) * SUBLANES <= j:
                        beats = other > cands[r]
                    else:
                        beats = jnp.where(row_id > j - r * SUBLANES, jnp.where(other >= cands[r], 1.0, 0.0),
                                          jnp.where(other > cands[r], 1.0, 0.0)) > 0.5
                    rs = slice(r * SUBLANES, (r + 1) * SUBLANES)
                    rank_ref[rs, :] = rank_ref[rs, :] + jnp.where(beats, 1.0, 0.0)
    ranks = [rank_ref[r * SUBLANES:(r + 1) * SUBLANES, :] for r in range(ngrp)]
    unsel = [jnp.where(r < float(min(SLC_TOPK, n_slc)), 0.0, 1.0) for r in ranks]
    if ngrp * SUBLANES > n_slc:
        unsel[-1] = jnp.where(row_id < n_slc - (ngrp - 1) * SUBLANES, unsel[-1], 0.0)
    sel_t = jnp.concatenate(unsel, axis=0)
    if ngrp * SUBLANES < LANES:
        sel_t = jnp.concatenate([sel_t, jnp.zeros((LANES - ngrp * SUBLANES, rows), F32)], axis=0)
    for j in range(nsub):
        sel_ref[0, 0, j * QBLK:(j + 1) * QBLK, :] = sel_t[:, j * QBLK:(j + 1) * QBLK].T.astype(sel_ref.dtype)


def _cmp_select(proj3, kvc, bias_c, ovl, n_slc):
    B, T, _ = proj3.shape
    ncp = kvc.shape[3]
    nsub = math.gcd(CMP_SUBBLOCKS, T // QBLK)
    rows = nsub * QBLK
    return pl.pallas_call(
        functools.partial(_cmp_select_kernel, n_slc=n_slc, nsub=nsub),
        grid=(B, A_KV_HEADS, T // rows),
        in_specs=[
            pl.BlockSpec((1, rows, A_GRP * HEAD_DIM), lambda b, g, i: (b, i, g)),
            pl.BlockSpec((1, 1, 1, ncp, HEAD_DIM), lambda b, g, i: (0, b, g, 0, 0)),
            pl.BlockSpec((1, 1, 1, ncp, HEAD_DIM), lambda b, g, i: (1, b, g, 0, 0)),
            pl.BlockSpec((A_GRP, rows, ncp), lambda b, g, i: (g, i, 0)),
            pl.BlockSpec((ncp, LANES), lambda b, g, i: (0, 0)),
        ],
        out_specs=(
            pl.BlockSpec((1, rows, A_GRP * HEAD_DIM), lambda b, g, i: (b, i, g)),
            pl.BlockSpec((1, 1, rows, LANES), lambda b, g, i: (b, g, i, 0)),
        ),
        out_shape=(
            jax.ShapeDtypeStruct((B, T, A_Q_HEADS * HEAD_DIM), MIXER_OUT),
            jax.ShapeDtypeStruct((B, A_KV_HEADS, T, LANES), BF16),
        ),
        scratch_shapes=[pltpu.VMEM((-(-n_slc // SUBLANES) * SUBLANES, rows), F32)],
        compiler_params=_cparams(("parallel", "parallel", "arbitrary")),
        name="nsa_cmp_select",
    )(proj3, kvc, kvc, bias_c, ovl)


def _slc_kernel(q_ref, k_ref, v_ref, sel_ref, e_ref, b_ref, o_ref, *scratch, n_tiles, nq):
    m_refs, l_refs, acc_refs = scratch[0::3], scratch[1::3], scratch[2::3]
    kw = SLC_CHUNK * QBLK
    for a in range(nq):
        it = pl.program_id(1) * nq + a
        rs = slice(a * QBLK, (a + 1) * QBLK)
        qs, sels = [], []
        for g in range(A_KV_HEADS):
            m_refs[g][...] = jnp.full(m_refs[g].shape, NEG, F32)
            l_refs[g][...] = jnp.zeros(l_refs[g].shape, F32)
            acc_refs[g][...] = jnp.zeros(acc_refs[g].shape, F32)
            qs.append(jnp.concatenate([q_ref[0, rs, (g * A_GRP + h) * HEAD_DIM:(g * A_GRP + h + 1) * HEAD_DIM]
                                       for h in range(A_GRP)], axis=0))
            sels.append(sel_ref[0, g, rs, :])

        def body(c, carry, it=it, qs=qs, sels=sels):
            start = pl.multiple_of(c * kw, kw)
            tidx = [jnp.clip(it - (c * SLC_CHUNK + j) + 1, 0, n_tiles - 1) for j in range(SLC_CHUNK)]
            kts = [k_ref[0, pl.ds(start, kw), g * HEAD_DIM:(g + 1) * HEAD_DIM] for g in range(A_KV_HEADS)]
            vts = [v_ref[0, pl.ds(start, kw), g * HEAD_DIM:(g + 1) * HEAD_DIM] for g in range(A_KV_HEADS)]
            ech = e_ref[:, pl.ds(start, kw)]
            ss = [lax.dot_general(qs[g], kts[g], (((1,), (1,)), ((), ())), preferred_element_type=F32)
                  for g in range(A_KV_HEADS)]
            madds = [jnp.dot(sels[g], ech, preferred_element_type=F32) for g in range(A_KV_HEADS)]
            pbs, alphas = [], []
            for g in range(A_KV_HEADS):
                rows = [[ss[g][h * QBLK:(h + 1) * QBLK, j * QBLK:(j + 1) * QBLK] + b_ref[g * A_GRP + h, tidx[j]]
                         + madds[g][:, j * QBLK:(j + 1) * QBLK] for j in range(SLC_CHUNK)] for h in range(A_GRP)]
                tile_max = jnp.concatenate([functools.reduce(jnp.maximum, r) for r in rows], axis=0)
                m_old = m_refs[g][...]
                m_new = jnp.maximum(m_old, jnp.max(tile_max, axis=-1, keepdims=True))
                alpha = jnp.exp2(m_old - m_new)
                p = [[jnp.exp2(t - m_new[h * QBLK:(h + 1) * QBLK]) for t in r] for h, r in enumerate(rows)]
                tile_sum = jnp.concatenate([functools.reduce(jnp.add, r) for r in p], axis=0)
                l_refs[g][...] = alpha * l_refs[g][...] + jnp.sum(tile_sum, axis=-1, keepdims=True)
                pbs.append(jnp.concatenate([jnp.concatenate([t.astype(BF16) for t in r], axis=1) for r in p], axis=0))
                alphas.append(alpha)
                m_refs[g][...] = m_new
            for g in range(A_KV_HEADS):
                acc_refs[g][...] = alphas[g] * acc_refs[g][...] + jnp.dot(pbs[g], vts[g], preferred_element_type=F32)
            return carry

        lax.fori_loop(0, it // SLC_CHUNK + 1, body, 0)
        for g in range(A_KV_HEADS):
            out = acc_refs[g][...] / l_refs[g][...]
            for h in range(A_GRP):
                col = (g * A_GRP + h) * HEAD_DIM
                o_ref[0, rs, col:col + HEAD_DIM] = out[h * QBLK:(h + 1) * QBLK].astype(o_ref.dtype)


def _slc_attn(proj3, sel, expand, bias):
    B, T, _ = proj3.shape
    n_tiles = bias.shape[1]
    nq = SLC_QTILES
    qrows = nq * QBLK
    assert T % (nq * QBLK) == 0 and T % (SLC_CHUNK * QBLK) == 0
    chain_rows = A_GRP * QBLK
    n_chains = A_KV_HEADS
    return pl.pallas_call(
        functools.partial(_slc_kernel, n_tiles=n_tiles, nq=nq),
        grid=(B, T // qrows),
        in_specs=[
            pl.BlockSpec((1, qrows, A_Q_HEADS * HEAD_DIM), lambda b, i: (b, i, COL_QA // A_Q_HEADS)),
            pl.BlockSpec((1, T, A_KV_HEADS * HEAD_DIM), lambda b, i: (b, 0, COL_KSA // A_KV_HEADS)),
            pl.BlockSpec((1, T, A_KV_HEADS * HEAD_DIM), lambda b, i: (b, 0, COL_VSA // A_KV_HEADS)),
            pl.BlockSpec((1, A_KV_HEADS, qrows, LANES), lambda b, i: (b, 0, i, 0)),
            pl.BlockSpec((LANES, T), lambda b, i: (0, 0)),
            pl.BlockSpec((A_Q_HEADS, n_tiles, QBLK, LANES), lambda b, i: (0, 0, 0, 0)),
        ],
        out_specs=pl.BlockSpec((1, qrows, A_Q_HEADS * HEAD_DIM), lambda b, i: (b, i, 0)),
        out_shape=jax.ShapeDtypeStruct((B, T, A_Q_HEADS * HEAD_DIM), MIXER_OUT),
        scratch_shapes=[pltpu.VMEM((chain_rows, 1), F32), pltpu.VMEM((chain_rows, 1), F32),
                        pltpu.VMEM((chain_rows, HEAD_DIM), F32)] * n_chains,
        compiler_params=_cparams(("parallel", "arbitrary")),
        name="nsa_slc_attn",
    )(proj3, proj3, proj3, sel, expand, bias)


def _mix_outproj_kernel(x_ref, gate_ref, ocmp_ref, oslc_ref, owin_ref, ob_ref, *rest):
    oc_refs, lse_refs = rest[:C_Q_HEADS], rest[C_Q_HEADS:2 * C_Q_HEADS]
    w_ref, o_ref, mix_ref = rest[2 * C_Q_HEADS:]
    tm = x_ref.shape[0]
    nsplit = MIX_ROW_SPLIT
    rows = tm // nsplit
    for part in range(nsplit):
        rs = slice(part * rows, (part + 1) * rows)
        gate = 1.0 / (1.0 + jnp.exp(-gate_ref[rs, :]))
        for h in range(A_Q_HEADS):
            sl = slice(h * HEAD_DIM, (h + 1) * HEAD_DIM)
            o = (gate[:, 3 * h:3 * h + 1] * ocmp_ref[rs, sl] + gate[:, 3 * h + 1:3 * h + 2] * oslc_ref[rs, sl]
                 + gate[:, 3 * h + 2:3 * h + 3] * owin_ref[rs, sl])
            mix_ref[rs, sl] = o.astype(mix_ref.dtype)
        base = A_Q_HEADS * HEAD_DIM
        width = B_Q_HEADS * HEAD_DIM
        mix_ref[rs, base:base + width] = ob_ref[rs, :].astype(mix_ref.dtype)
        base += width
        for hh in range(C_GRP):
            heads = [C_GRP * gidx + hh for gidx in range(len(DIL_PAIRS))]
            lses = [lse_refs[c][rs, :] for c in heads]
            mx = functools.reduce(jnp.maximum, lses)
            ws = [jnp.exp(x - mx) for x in lses]
            tot = functools.reduce(jnp.add, ws)
            for c, w in zip(heads, ws):
                mix_ref[rs, base + c * HEAD_DIM:base + (c + 1) * HEAD_DIM] = (oc_refs[c][rs, :] * (w / tot)).astype(mix_ref.dtype)
    for part in range(nsplit):
        rs = slice(part * rows, (part + 1) * rows)
        a = mix_ref[rs, :]
        for c0 in range(0, o_ref.shape[1], OUT_SUBTILE):
            sl = slice(c0, c0 + OUT_SUBTILE)
            o_ref[rs, sl] = x_ref[rs, sl] + jnp.dot(a, w_ref[:, sl], preferred_element_type=F32)


def _mix_outproj(x2d, gates, o_cmp, o_slc, o_win, o_b, o_cs, lses, w_all, layer, *, tm):
    m, d = x2d.shape
    k = w_all.shape[1]
    row = lambda w: pl.BlockSpec((tm, w), lambda i: (i, 0))
    head = lambda c: pl.BlockSpec((tm, HEAD_DIM), lambda i: (i, c))
    full = [x2d, gates, o_cmp, o_slc, o_win, o_b]
    return pl.pallas_call(
        _mix_outproj_kernel,
        grid=(m // tm,),
        in_specs=[row(a.shape[1]) for a in full] + [head(c) for _, c in o_cs + lses]
        + [pl.BlockSpec((None, k, d), lambda i: (layer, 0, 0), pipeline_mode=pl.Buffered(1))],
        out_specs=row(d),
        out_shape=jax.ShapeDtypeStruct((m, d), F32),
        scratch_shapes=[pltpu.VMEM((tm, k), BF16)],
        compiler_params=_cparams(("parallel",)),
        name="mix_out_proj",
    )(*full, *[a for a, _ in o_cs + lses], w_all)


def _mixers_outproj(x2d, proj3, gates, dil32, layer_params, tables, w_out_all, layer, *, tm):
    B, T, _ = proj3.shape
    cmp_pe, cmp_w1, cmp_w2, kc_gain, sinks = layer_params
    n_cmp = (T - CMP_BLOCK) // CMP_STRIDE + 1
    n_slc = T // SLC_BLOCK
    pe = jnp.broadcast_to(cmp_pe.reshape(2, 1, CMP_BLOCK * HEAD_DIM), (2, 8, CMP_BLOCK * HEAD_DIM)).astype(BF16)
    kvc = _compress(dil32, pe, cmp_w1.astype(BF16), cmp_w2.astype(BF16), kc_gain.reshape(1, HEAD_DIM), n_cmp)

    o_cmp, sel = _cmp_select(proj3, kvc, tables["bias_c"], tables["ovl"], n_slc)
    o_slc = _slc_attn(proj3, sel, tables["expand"], tables["bias_slc"])
    o_win = _band_attn(proj3, proj3, proj3, tables["bias_win"], n_kv=A_KV_HEADS, grp=A_GRP,
                       q_col=COL_QA // A_GRP, k_col=COL_KWA, v_col=COL_VWA)
    o_b = _band_attn(proj3, proj3, proj3, tables["bias_b"], n_kv=B_KV_HEADS, grp=B_GRP,
                     q_col=COL_QB // B_GRP, k_col=COL_KB, v_col=COL_VB, sinks=sinks)
    o_cs, lses = [], []
    for gidx, (_, dil) in enumerate(DIL_PAIRS):
        if dil == 1:
            o, lse = _band_attn(proj3, proj3, proj3, tables["bias_c%d" % gidx], n_kv=1, grp=C_GRP,
                                q_col=COL_QC // C_GRP + gidx, k_col=COL_KC + gidx, v_col=COL_VC + gidx,
                                with_lse=True)
            o_cs += [(o.reshape(B * T, -1), h) for h in range(C_GRP)]
            lses += [(lse.reshape(B * T, -1), h) for h in range(C_GRP)]
        else:
            os_, ls_ = _dil_attn(dil32, tables["bias_c%d" % gidx], col=DIL_F32_SLOT[COL_QC + C_GRP * gidx], dil=dil)
            o_cs += [(o.reshape(B * T, -1), 0) for o in os_]
            lses += [(lse.reshape(B * T, -1), 0) for lse in ls_]
    return _mix_outproj(x2d, gates, o_cmp.reshape(B * T, -1), o_slc.reshape(B * T, -1), o_win.reshape(B * T, -1),
                        o_b.reshape(B * T, -1), o_cs, lses, w_out_all, layer, tm=tm)


def _build_tables(rel_bias, T):
    n_cmp = (T - CMP_BLOCK) // CMP_STRIDE + 1
    n_slc = T // SLC_BLOCK
    ncp = T // CMP_STRIDE
    nq = T // QBLK
    tables = {}
    tables["bias_c"] = _cmp_bias(rel_bias, A_Q_HEADS, nq, ncp, n_cmp).reshape(A_Q_HEADS, T, ncp)
    n_sat = -(-(SAT_DIST + QBLK - 1) // QBLK) + 1
    tables["bias_slc"] = _bias_tiles(rel_bias, 0, A_Q_HEADS, min(n_sat, nq) + 1, koff=QBLK)
    win = NSA_WINDOW - 1
    tables["bias_win"] = _bias_tiles(rel_bias, 0, A_Q_HEADS, -(-win // QBLK) + 2, max_dist=win)
    swa = SWA_WINDOW - 1
    tables["bias_b"] = _bias_tiles(rel_bias, A_Q_HEADS, B_Q_HEADS, -(-swa // QBLK) + 2, max_dist=swa)
    for gidx, (w, dil) in enumerate(DIL_PAIRS):
        md = w // dil
        tables["bias_c%d" % gidx] = _bias_tiles(rel_bias, A_Q_HEADS + B_Q_HEADS + C_GRP * gidx, C_GRP,
                                                -(-md // QBLK) + 2, dscale=dil, max_dist=md)
    c0 = np.arange(ncp)[:, None] * CMP_STRIDE
    s0 = np.arange(LANES)[None, :] * SLC_BLOCK
    ovl = np.clip(np.minimum(c0 + CMP_BLOCK, s0 + SLC_BLOCK) - np.maximum(c0, s0), 0, None) / CMP_BLOCK
    ovl = ovl * (np.arange(ncp)[:, None] < n_cmp) * (np.arange(LANES)[None, :] < n_slc)
    tables["ovl"] = jnp.asarray(ovl, BF16)
    member = (np.arange(T)[None, :] // SLC_BLOCK) == np.arange(LANES)[:, None]
    tables["expand"] = jnp.asarray(member * MASK_WEIGHT, BF16)
    return tables


def _proj_gain(g):
    ones = jnp.ones((HEAD_DIM,), F32)
    spec = [(g[0] * (SCALE * LOG2E), 6), (ones, 4), (g[2], 2), (ones, 2), (g[3], 2), (ones, 2),
            (g[4] * (SCALE * LOG2E), 4), (g[5], 2), (ones, 2), (g[6] * (SCALE * LOG2E), 6), (g[7], 3), (ones, 3)]
    assert sum(n for _, n in spec) == N_MAIN_BLOCKS
    return jnp.concatenate([jnp.tile(v, n) for v, n in spec]).reshape(1, N_MAIN)


def kernel(x, norm_attn, w_in, qk_gain, cmp_pe, cmp_w1, cmp_w2, sinks, rel_bias, w_out, norm_ffn, w_gate, w_up, w_down):
    B, T, D = x.shape
    depth = w_in.shape[0]
    tables = _build_tables(rel_bias, T)
    x2 = x.reshape(B * T, D)
    w_all = _wprep(jnp.transpose(w_in, (0, 2, 1)).astype(BF16))
    w_out_b, w_gate_b, w_up_b, w_down_b = (w.astype(BF16) for w in (w_out, w_gate, w_up, w_down))
    tm = min(ROW_TILE, B * T)
    for l in range(depth):
        proj, gates, dil32 = _proj(x2, norm_attn[l].reshape(1, D), w_all, l, _proj_gain(qk_gain[l]), tm=tm)
        x2 = _mixers_outproj(x2, proj.reshape(B, T, N_MAIN), gates, dil32.reshape(B, T, DIL_F32_COLS),
                             (cmp_pe[l], cmp_w1[l], cmp_w2[l], qk_gain[l][1], sinks[l]), tables, w_out_b, l, tm=tm)
        x2 = _ffn(x2, norm_ffn[l].reshape(1, D), w_gate_b, w_up_b, w_down_b, l, tm=tm, tf=FFN_TILE)
    return x2.reshape(B, T, D)
```

```python
import functools
import math

import numpy as np
import jax
import jax.numpy as jnp
from jax import lax
from jax.experimental import pallas as pl
from jax.experimental.pallas import tpu as pltpu

F32 = jnp.float32
BF16 = jnp.bfloat16

HEAD_DIM = 128
MIXER_OUT = jnp.bfloat16
LANES = 128
SUBLANES = 8
QBLK = 128
BAND_SUBBLOCKS = 8
CMP_SUBBLOCKS = 4
A_Q_HEADS, A_KV_HEADS = 6, 2
A_GRP = A_Q_HEADS // A_KV_HEADS
B_Q_HEADS, B_KV_HEADS = 4, 2
B_GRP = B_Q_HEADS // B_KV_HEADS
DIL_PAIRS = ((128, 1), (512, 4), (2048, 16))
C_GRP = 2
C_Q_HEADS = C_GRP * len(DIL_PAIRS)
CMP_BLOCK, CMP_STRIDE = 32, 16
SLC_BLOCK, SLC_TOPK = 64, 16
SLC_CHUNK = 4
SLC_QTILES = 4
NSA_WINDOW, SWA_WINDOW = 512, 128
FORCED_SCORE = 1.0e4
REL_BUCKETS, REL_MAX_EXACT, REL_MAX_DIST = 32, 16, 2048
SCALE = HEAD_DIM ** -0.5
LOG2E = math.log2(math.e)
LN2 = math.log(2.0)
MASK_WEIGHT = -2.0 ** 100
EPS = 1e-6
NEG = -1e30
VMEM_LIMIT = 56 * 1024 * 1024

COL_QA, COL_KCA, COL_VCA, COL_KSA, COL_VSA, COL_KWA, COL_VWA = 0, 6, 8, 10, 12, 14, 16
COL_QB, COL_KB, COL_VB, COL_QC, COL_KC, COL_VC = 18, 22, 24, 26, 32, 35
N_MAIN_BLOCKS = 38
N_MAIN = N_MAIN_BLOCKS * LANES
GATE_START = 2304
N_GATES = A_Q_HEADS * 3
PROJ_NORMED = ([True] * 6 + [False] * 4 + [True] * 2 + [False] * 2 + [True] * 2 + [False] * 2
               + [True] * 6 + [False] * 2 + [True] * 9 + [False] * 3)
DIL_F32_SLOT = {}
for _g, (_, _d) in enumerate(DIL_PAIRS):
    if _d > 1:
        _base = 4 * len([1 for _, _dd in DIL_PAIRS[:_g] if _dd > 1])
        DIL_F32_SLOT.update({COL_QC + C_GRP * _g: _base, COL_QC + C_GRP * _g + 1: _base + 1,
                             COL_KC + _g: _base + 2, COL_VC + _g: _base + 3})
CMP_F32_SLOT0 = len(DIL_F32_SLOT)
DIL_F32_SLOT.update({COL_KCA + _i: CMP_F32_SLOT0 + _i for _i in range(2 * A_KV_HEADS)})
DIL_F32_COLS = LANES * len(DIL_F32_SLOT)
DIL_TILES_PER_STEP = 8
ROW_TILE = 512
FFN_TILE = 512
WPREP_ROWS = 384
PROJ_SUBTILE = 256
MIX_ROW_SPLIT = 4
OUT_SUBTILE = 512


def _bucket_starts():
    d = np.arange(0, 1 << 17)
    out = []
    for dt in (np.float32, np.float64):
        far = np.maximum(d, REL_MAX_EXACT).astype(dt)
        lb = REL_MAX_EXACT + (np.log(far / dt(REL_MAX_EXACT)) / dt(math.log(REL_MAX_DIST / REL_MAX_EXACT))
                              * dt(REL_BUCKETS - REL_MAX_EXACT)).astype(np.int64)
        out.append(np.where(d < REL_MAX_EXACT, d, np.minimum(lb, REL_BUCKETS - 1)))
    assert (out[0] == out[1]).all() and (np.diff(out[0]) >= 0).all()
    return [int(np.argmax(out[0] >= b)) for b in range(REL_BUCKETS)]


BUCKET_START = _bucket_starts()
SAT_DIST = BUCKET_START[REL_BUCKETS - 1]


def _cparams(sem, vmem=VMEM_LIMIT):
    return pltpu.CompilerParams(dimension_semantics=sem, vmem_limit_bytes=vmem)


def _bias_tile_kernel(tab_ref, o_ref, *, head0, koff, dscale, max_dist):
    h = pl.program_id(0) + head0
    r = lax.broadcasted_iota(jnp.int32, (QBLK, LANES), 0)
    c = lax.broadcasted_iota(jnp.int32, (QBLK, LANES), 1)
    for t in range(o_ref.shape[1]):
        dist = t * QBLK + r - c - koff
        d = dist * dscale
        val = jnp.full((QBLK, LANES), tab_ref[REL_BUCKETS - 1, h] * LOG2E, F32)
        for b in range(REL_BUCKETS - 2, -1, -1):
            val = jnp.where(d < BUCKET_START[b + 1], tab_ref[b, h] * LOG2E, val)
        val = jnp.where(dist >= 0, val, NEG)
        if max_dist is not None:
            val = jnp.where(dist <= max_dist, val, NEG)
        o_ref[0, t] = val


def _bias_tiles(rel_bias, head0, nheads, ntiles, *, koff=0, dscale=1, max_dist=None):
    kern = functools.partial(_bias_tile_kernel, head0=head0, koff=koff, dscale=dscale, max_dist=max_dist)
    return pl.pallas_call(
        kern,
        grid=(nheads,),
        in_specs=[pl.BlockSpec(memory_space=pltpu.SMEM)],
        out_specs=pl.BlockSpec((1, ntiles, QBLK, LANES), lambda h: (h, 0, 0, 0)),
        out_shape=jax.ShapeDtypeStruct((nheads, ntiles, QBLK, LANES), F32),
        compiler_params=_cparams(("parallel",)),
        name="bias_tiles",
    )(rel_bias)


def _cmp_bias_kernel(tab_ref, o_ref, *, n_cmp):
    h = pl.program_id(0)
    ntile, _, width = o_ref.shape[1:]
    r = lax.broadcasted_iota(jnp.int32, (QBLK, 2 * width), 0)
    c = lax.broadcasted_iota(jnp.int32, (QBLK, 2 * width), 1)
    dist = r - CMP_STRIDE * (c - width) - (CMP_BLOCK - 1)
    base = jnp.full((QBLK, 2 * width), tab_ref[REL_BUCKETS - 1, h] * LOG2E, F32)
    for b in range(REL_BUCKETS - 2, -1, -1):
        base = jnp.where(dist < BUCKET_START[b + 1], tab_ref[b, h] * LOG2E, base)
    base = jnp.where(dist >= 0, base, NEG)
    col = lax.broadcasted_iota(jnp.int32, (QBLK, width), 1)
    per_tile = QBLK // CMP_STRIDE
    for i in range(ntile):
        tile = pltpu.roll(base, (width + per_tile * i) % (2 * width), 1)[:, :width]
        o_ref[0, i] = jnp.where(col < n_cmp, tile, NEG)


def _cmp_bias(rel_bias, nheads, ntiles, width, n_cmp):
    return pl.pallas_call(
        functools.partial(_cmp_bias_kernel, n_cmp=n_cmp),
        grid=(nheads,),
        in_specs=[pl.BlockSpec(memory_space=pltpu.SMEM)],
        out_specs=pl.BlockSpec((1, ntiles, QBLK, width), lambda h: (h, 0, 0, 0)),
        out_shape=jax.ShapeDtypeStruct((nheads, ntiles, QBLK, width), F32),
        compiler_params=_cparams(("parallel",)),
        name="cmp_bias",
    )(rel_bias)


def _rms(x, w):
    ms = jnp.mean(x * x, axis=-1, keepdims=True)
    return x * lax.rsqrt(ms + EPS) * w


def _wprep_kernel(w_ref, nxt_ref, gate_ref, o_ref):
    j = pl.program_id(1)
    rows = o_ref.shape[1]
    gate_step = GATE_START // rows
    last = pl.num_programs(1) - 1

    @pl.when(j < gate_step)
    def _():
        o_ref[0] = w_ref[0]

    @pl.when((j >= gate_step) & (j < last))
    def _():
        o_ref[0, :rows - N_GATES] = w_ref[0, N_GATES:]
        o_ref[0, rows - N_GATES:] = nxt_ref[0, :N_GATES]

    @pl.when(j == last)
    def _():
        o_ref[0, :rows - LANES] = w_ref[0, N_GATES:N_GATES + rows - LANES]
        o_ref[0, rows - LANES:] = gate_ref[0, :LANES]


def _wprep(wt):
    depth, n_in, d = wt.shape
    rows = WPREP_ROWS
    assert GATE_START % rows == 0 and (N_MAIN + LANES) % rows == 0 and N_MAIN % rows == rows - LANES
    steps = (N_MAIN + LANES) // rows
    return pl.pallas_call(
        _wprep_kernel,
        grid=(depth, steps),
        in_specs=[pl.BlockSpec((1, rows, d), lambda l, j: (l, j, 0)),
                  pl.BlockSpec((1, rows, d), lambda l, j: (l, jnp.minimum(j + 1, steps - 1), 0)),
                  pl.BlockSpec((1, rows, d), lambda l, j: (l, GATE_START // rows, 0))],
        out_specs=pl.BlockSpec((1, rows, d), lambda l, j: (l, j, 0)),
        out_shape=jax.ShapeDtypeStruct((depth, N_MAIN + LANES, d), BF16),
        compiler_params=_cparams(("parallel", "parallel")),
        name="w_in_prep",
    )(wt, wt, wt)


def _proj_kernel(x_ref, nw_ref, w_ref, gain_ref, o_ref, gate_ref, dil_ref):
    h = _rms(x_ref[...], nw_ref[...]).astype(BF16)
    for c0 in range(0, N_MAIN, PROJ_SUBTILE):
        width = min(PROJ_SUBTILE, N_MAIN - c0)
        acc = lax.dot_general(h, w_ref[c0:c0 + width, :], (((1,), (1,)), ((), ())), preferred_element_type=F32)
        for c in range(width // LANES):
            blk = c0 // LANES + c
            sl = slice(blk * LANES, (blk + 1) * LANES)
            y = acc[:, c * LANES:(c + 1) * LANES]
            if PROJ_NORMED[blk]:
                ms = jnp.mean(y * y, axis=-1, keepdims=True)
                y = y * lax.rsqrt(ms + EPS) * gain_ref[:, sl]
            o_ref[:, sl] = y.astype(o_ref.dtype)
            if blk in DIL_F32_SLOT:
                dil_ref[:, DIL_F32_SLOT[blk] * LANES:(DIL_F32_SLOT[blk] + 1) * LANES] = y
    gate_ref[...] = lax.dot_general(h, w_ref[N_MAIN:, :], (((1,), (1,)), ((), ())), preferred_element_type=F32)


def _proj(x2d, norm_w, w_all, layer, gain, *, tm):
    m, d = x2d.shape
    n = w_all.shape[1]
    return pl.pallas_call(
        _proj_kernel,
        grid=(m // tm,),
        in_specs=[
            pl.BlockSpec((tm, d), lambda i: (i, 0)),
            pl.BlockSpec((1, d), lambda i: (0, 0)),
            pl.BlockSpec((None, n, d), lambda i: (layer, 0, 0), pipeline_mode=pl.Buffered(1)),
            pl.BlockSpec((1, N_MAIN), lambda i: (0, 0)),
        ],
        out_specs=(pl.BlockSpec((tm, N_MAIN), lambda i: (i, 0)), pl.BlockSpec((tm, LANES), lambda i: (i, 0)),
                   pl.BlockSpec((tm, DIL_F32_COLS), lambda i: (i, 0))),
        out_shape=(jax.ShapeDtypeStruct((m, N_MAIN), BF16), jax.ShapeDtypeStruct((m, LANES), F32),
                   jax.ShapeDtypeStruct((m, DIL_F32_COLS), F32)),
        compiler_params=_cparams(("parallel",)),
        name="in_proj",
    )(x2d, norm_w, w_all, gain)


def _ffn_kernel(x_ref, nw_ref, wg_ref, wu_ref, wd_ref, o_ref, h_ref):
    f = pl.program_id(1)

    @pl.when(f == 0)
    def _():
        x = x_ref[...]
        h_ref[...] = _rms(x, nw_ref[...]).astype(BF16)
        o_ref[...] = x

    h = h_ref[...]
    g = jnp.dot(h, wg_ref[...], preferred_element_type=F32)
    u = jnp.dot(h, wu_ref[...], preferred_element_type=F32)
    a = (g * (1.0 / (1.0 + jnp.exp(-g))) * u).astype(BF16)
    o_ref[...] += jnp.dot(a, wd_ref[...], preferred_element_type=F32)


def _ffn(x2d, norm_w, wg, wu, wd, layer, *, tm, tf):
    m, d = x2d.shape
    dff = wg.shape[2]
    return pl.pallas_call(
        _ffn_kernel,
        grid=(m // tm, dff // tf),
        in_specs=[
            pl.BlockSpec((tm, d), lambda i, f: (i, 0)),
            pl.BlockSpec((1, d), lambda i, f: (0, 0)),
            pl.BlockSpec((None, d, tf), lambda i, f: (layer, 0, f)),
            pl.BlockSpec((None, d, tf), lambda i, f: (layer, 0, f)),
            pl.BlockSpec((None, tf, d), lambda i, f: (layer, f, 0)),
        ],
        out_specs=pl.BlockSpec((tm, d), lambda i, f: (i, 0)),
        out_shape=jax.ShapeDtypeStruct((m, d), F32),
        scratch_shapes=[pltpu.VMEM((tm, d), BF16)],
        compiler_params=_cparams(("parallel", "arbitrary")),
        name="ffn",
    )(x2d, norm_w, wg, wu, wd)


def _band_kernel(*refs, grp, n_off, nsub, has_sinks, with_lse):
    if has_sinks:
        sink_ref, q_ref, k_ref, v_ref, b_ref = refs[:5]
        outs = refs[5:]
    else:
        q_ref, k_ref, v_ref, b_ref = refs[:4]
        outs = refs[4:]
    o_ref = outs[0]
    g = pl.program_id(1)
    i0 = pl.program_id(2) * nsub
    tiles = {}

    def kv_tile(rel):
        if rel not in tiles:
            start = pl.multiple_of(jnp.maximum(i0 + rel, 0) * QBLK, QBLK)
            tiles[rel] = (k_ref[0, pl.ds(start, QBLK), :], v_ref[0, pl.ds(start, QBLK), :])
        return tiles[rel]

    def head_rows(x, h):
        return x[h * QBLK:(h + 1) * QBLK]

    kvs = [[kv_tile(j - off) for off in range(n_off)] for j in range(nsub)]
    tidxs = [[jnp.where(i0 + j - off >= 0, off, n_off) for off in range(n_off)] for j in range(nsub)]
    scores = []
    for j in range(nsub):
        rs = slice(j * QBLK, (j + 1) * QBLK)
        kcat = jnp.concatenate([t[0] for t in kvs[j]], axis=0)
        q = jnp.concatenate([q_ref[0, rs, h * HEAD_DIM:(h + 1) * HEAD_DIM] for h in range(grp)], axis=0)
        scores.append(lax.dot_general(q, kcat, (((1,), (1,)), ((), ())), preferred_element_type=F32))
    probs, stats = [], []
    for j in range(nsub):
        s = scores[j]
        st = [[head_rows(s, h)[:, off * QBLK:(off + 1) * QBLK] + b_ref[h, tidxs[j][off]] for off in range(n_off)]
              for h in range(grp)]
        m = jnp.max(jnp.concatenate([functools.reduce(jnp.maximum, r) for r in st], axis=0), axis=-1, keepdims=True)
        p = [[jnp.exp2(st[h][off] - head_rows(m, h)) for off in range(n_off)] for h in range(grp)]
        l = jnp.sum(jnp.concatenate([functools.reduce(jnp.add, r) for r in p], axis=0), axis=-1, keepdims=True)
        probs.append(jnp.concatenate([jnp.concatenate([t.astype(BF16) for t in r], axis=1) for r in p], axis=0))
        stats.append((m, l))
    for j in range(nsub):
        rs = slice(j * QBLK, (j + 1) * QBLK)
        m, l = stats[j]
        vcat = jnp.concatenate([t[1] for t in kvs[j]], axis=0)
        o = jnp.dot(probs[j], vcat, preferred_element_type=F32)
        den = l
        if has_sinks:
            sink = jnp.concatenate([jnp.full((QBLK, 1), sink_ref[g * grp + h] * LOG2E, F32) for h in range(grp)], axis=0)
            den = l + jnp.exp2(sink - m)
        o = o / den
        for h in range(grp):
            sl = slice(h * HEAD_DIM, (h + 1) * HEAD_DIM)
            o_ref[0, rs, sl] = head_rows(o, h).astype(o_ref.dtype)
            if with_lse:
                outs[1][0, rs, sl] = jnp.broadcast_to(head_rows(m * LN2 + jnp.log(l), h), (QBLK, HEAD_DIM))


def _band_attn(q_arr, k_arr, v_arr, bias, *, n_kv, grp, q_col, k_col, v_col, sinks=None, with_lse=False):
    n, L, _ = q_arr.shape
    n_off = bias.shape[1] - 1
    gw = grp * HEAD_DIM
    nsub = math.gcd(BAND_SUBBLOCKS, L // QBLK)
    rows = nsub * QBLK
    kern = functools.partial(_band_kernel, grp=grp, n_off=n_off, nsub=nsub, has_sinks=sinks is not None,
                             with_lse=with_lse)
    in_specs = [
        pl.BlockSpec((1, rows, gw), lambda b, g, i: (b, i, q_col + g)),
        pl.BlockSpec((1, L, HEAD_DIM), lambda b, g, i: (b, 0, k_col + g)),
        pl.BlockSpec((1, L, HEAD_DIM), lambda b, g, i: (b, 0, v_col + g)),
        pl.BlockSpec((grp, n_off + 1, QBLK, LANES), lambda b, g, i: (g, 0, 0, 0)),
    ]
    args = [q_arr, k_arr, v_arr, bias]
    if sinks is not None:
        in_specs = [pl.BlockSpec(memory_space=pltpu.SMEM)] + in_specs
        args = [sinks] + args
    o_spec = pl.BlockSpec((1, rows, gw), lambda b, g, i: (b, i, g))
    o_shape = jax.ShapeDtypeStruct((n, L, n_kv * gw), MIXER_OUT)
    lse_shape = jax.ShapeDtypeStruct((n, L, n_kv * gw), F32)
    return pl.pallas_call(
        kern,
        grid=(n, n_kv, L // rows),
        in_specs=in_specs,
        out_specs=(o_spec, o_spec) if with_lse else o_spec,
        out_shape=(o_shape, lse_shape) if with_lse else o_shape,
        compiler_params=_cparams(("parallel", "parallel", "arbitrary")),
        name="band_attn",
    )(*args)


def _dil_kernel(*refs, dil, nres):
    q_refs, (k_ref, v_ref, b_ref) = refs[:C_GRP], refs[C_GRP:C_GRP + 3]
    o_refs, lse_refs = refs[C_GRP + 3:2 * C_GRP + 3], refs[2 * C_GRP + 3:]
    ls = k_ref.shape[1] // dil
    ntile = ls // QBLK
    r0 = pl.program_id(1) * nres
    units = [(rr, j) for rr in range(nres) for j in range(ntile)]

    def rows(rr, j):
        return pl.ds(r0 + rr + dil * QBLK * j, QBLK, stride=dil)

    kv = {u: (k_ref[0, rows(*u), :].astype(BF16), v_ref[0, rows(*u), :].astype(BF16)) for u in units}
    n_off = b_ref.shape[1] - 1
    offs = {(rr, j): [off for off in range(n_off) if j - off >= 0] for rr, j in units}
    scores = []
    for rr, j in units:
        kcat = jnp.concatenate([kv[(rr, j - off)][0] for off in offs[(rr, j)]], axis=0)
        q = jnp.concatenate([q_ref[0, rows(rr, j), :].astype(BF16) for q_ref in q_refs], axis=0)
        scores.append(lax.dot_general(q, kcat, (((1,), (1,)), ((), ())), preferred_element_type=F32))
    probs, stats = [], []
    for u, s in zip(units, scores):
        st = [[s[h * QBLK:(h + 1) * QBLK, n * QBLK:(n + 1) * QBLK] + b_ref[h, off] for n, off in enumerate(offs[u])]
              for h in range(C_GRP)]
        m = jnp.max(jnp.concatenate([functools.reduce(jnp.maximum, r) for r in st], axis=0), axis=-1, keepdims=True)
        p = [[jnp.exp2(t - m[h * QBLK:(h + 1) * QBLK]) for t in r] for h, r in enumerate(st)]
        l = jnp.sum(jnp.concatenate([functools.reduce(jnp.add, r) for r in p], axis=0), axis=-1, keepdims=True)
        probs.append(jnp.concatenate([jnp.concatenate([t.astype(BF16) for t in r], axis=1) for r in p], axis=0))
        stats.append((m, l))
    for (rr, j), pb, (m, l) in zip(units, probs, stats):
        vcat = jnp.concatenate([kv[(rr, j - off)][1] for off in offs[(rr, j)]], axis=0)
        o = jnp.dot(pb, vcat, preferred_element_type=F32) / l
        lse = m * LN2 + jnp.log(l)
        for h in range(C_GRP):
            o_refs[h][0, rows(rr, j), :] = o[h * QBLK:(h + 1) * QBLK]
            lse_refs[h][0, rows(rr, j), :] = jnp.broadcast_to(lse[h * QBLK:(h + 1) * QBLK], (QBLK, HEAD_DIM))


def _dil_attn(x3, bias, *, col, dil):
    B, T, _ = x3.shape
    ntile = T // dil // QBLK
    nres = max(1, min(dil, DIL_TILES_PER_STEP // ntile))
    spec = pl.BlockSpec((1, T, HEAD_DIM), lambda b, r: (b, 0, 0))
    shape = jax.ShapeDtypeStruct((B, T, HEAD_DIM), F32)
    outs = pl.pallas_call(
        functools.partial(_dil_kernel, dil=dil, nres=nres),
        grid=(B, dil // nres),
        in_specs=[pl.BlockSpec((1, T, HEAD_DIM), functools.partial(lambda b, r, c: (b, 0, c), c=col + c))
                  for c in range(C_GRP + 2)]
        + [pl.BlockSpec((C_GRP, bias.shape[1], QBLK, LANES), lambda b, r: (0, 0, 0, 0))],
        out_specs=(spec,) * (2 * C_GRP),
        out_shape=(shape,) * (2 * C_GRP),
        compiler_params=_cparams(("parallel", "arbitrary")),
        name="dilated_attn",
    )(*([x3] * (C_GRP + 2)), bias)
    return outs[:C_GRP], outs[C_GRP:]


def _compress_kernel(x_ref, pe_ref, w1_ref, w2_ref, gain_ref, o_ref, *, n_cmp):
    half = w1_ref.shape[1] // 2
    rows = x_ref.shape[1] // CMP_STRIDE
    xr = jnp.concatenate([x_ref[0, pl.ds(t, rows, stride=CMP_STRIDE), :].astype(BF16) for t in range(CMP_STRIDE)],
                         axis=1)
    y0 = jnp.dot(xr, w1_ref[0, :half], preferred_element_type=F32)
    y1 = jnp.dot(xr, w1_ref[0, half:], preferred_element_type=F32)
    pe = jnp.dot(pe_ref[0], w1_ref[0], preferred_element_type=F32)[0:1]
    c = y0 + pltpu.roll(y1, rows - 1, 0) + pe
    gl = 0.5 * c * (1.0 + jnp.tanh(math.sqrt(2.0 / math.pi) * (c + 0.044715 * (c * c * c))))
    out = jnp.dot(gl.astype(BF16), w2_ref[0], preferred_element_type=F32)
    out = jnp.where(pl.program_id(0) == 0, _rms(out, gain_ref[...]), out)
    valid = lax.broadcasted_iota(jnp.int32, out.shape, 0) < n_cmp
    o_ref[0, 0, 0] = jnp.where(valid, out, 0.0).astype(o_ref.dtype)


def _compress(x3, pe, w1, w2, gain, n_cmp):
    B, T, _ = x3.shape
    rows = T // CMP_STRIDE
    wide = CMP_BLOCK * HEAD_DIM
    return pl.pallas_call(
        functools.partial(_compress_kernel, n_cmp=n_cmp),
        grid=(2, B, A_KV_HEADS),
        in_specs=[
            pl.BlockSpec((1, T, HEAD_DIM), lambda s, b, g: (b, 0, CMP_F32_SLOT0 + s * A_KV_HEADS + g)),
            pl.BlockSpec((1, 8, wide), lambda s, b, g: (s, 0, 0)),
            pl.BlockSpec((1, wide, HEAD_DIM), lambda s, b, g: (s, 0, 0)),
            pl.BlockSpec((1, HEAD_DIM, HEAD_DIM), lambda s, b, g: (s, 0, 0)),
            pl.BlockSpec((1, HEAD_DIM), lambda s, b, g: (0, 0)),
        ],
        out_specs=pl.BlockSpec((1, 1, 1, rows, HEAD_DIM), lambda s, b, g: (s, b, g, 0, 0)),
        out_shape=jax.ShapeDtypeStruct((2, B, A_KV_HEADS, rows, HEAD_DIM), BF16),
        compiler_params=_cparams(("arbitrary", "arbitrary", "arbitrary")),
        name="nsa_compress",
    )(x3, pe, w1, w2, gain)


def _cmp_select_kernel(q_ref, kc_ref, vc_ref, b_ref, ovl_ref, o_ref, sel_ref, rank_ref, *, n_slc, nsub):
    i0 = pl.program_id(2) * nsub
    rows = nsub * QBLK
    kc = kc_ref[0, 0, 0]
    vc = vc_ref[0, 0, 0]
    q_all = jnp.concatenate([q_ref[0, :, h * HEAD_DIM:(h + 1) * HEAD_DIM] for h in range(A_GRP)], axis=0)
    s_all = lax.dot_general(q_all, kc, (((1,), (1,)), ((), ())), preferred_element_type=F32)
    ps = []
    for h in range(A_GRP):
        s = s_all[h * rows:(h + 1) * rows] + b_ref[h]
        m = jnp.max(s, axis=-1, keepdims=True)
        e = jnp.exp2(s - jnp.where(m > 0.5 * NEG, m, 0.0))
        den = jnp.sum(e, axis=-1, keepdims=True)
        ps.append((e / jnp.where(den > 0, den, 1.0)).astype(BF16))
    r_all = jnp.dot(jnp.concatenate(ps, axis=0), jnp.concatenate([vc, ovl_ref[...]], axis=1),
                    preferred_element_type=F32)
    imp = jnp.zeros((rows, LANES), F32)
    for h in range(A_GRP):
        o_ref[0, :, h * HEAD_DIM:(h + 1) * HEAD_DIM] = r_all[h * rows:(h + 1) * rows, :HEAD_DIM].astype(o_ref.dtype)
        imp = imp + r_all[h * rows:(h + 1) * rows, HEAD_DIM:]
    t = i0 * QBLK + lax.broadcasted_iota(jnp.int32, (rows, LANES), 0)
    blk = lax.broadcasted_iota(jnp.int32, (rows, LANES), 1)
    cur = t // SLC_BLOCK
    imp = jnp.where(blk == 0, FORCED_SCORE, imp)
    imp = jnp.where(blk == cur, FORCED_SCORE, imp)
    imp = jnp.where(blk == cur - 1, FORCED_SCORE, imp)
    imp = jnp.where(blk * SLC_BLOCK <= t, imp, NEG)
    imp = jnp.where(blk < n_slc, imp, 2.0 * NEG)
    imp_t = jnp.concatenate([imp[j * QBLK:(j + 1) * QBLK].T for j in range(nsub)], axis=1)
    ngrp = -(-n_slc // SUBLANES)
    cands = [imp_t[r * SUBLANES:(r + 1) * SUBLANES] for r in range(ngrp)]
    row_id = lax.broadcasted_iota(jnp.int32, (SUBLANES, rows), 0)
    rank_ref[...] = jnp.zeros(rank_ref.shape, F32)
    last_blk = ((i0 + nsub) * QBLK - 1) // SLC_BLOCK
    for jg in range(ngrp):
        @pl.when(jg * SUBLANES <= last_blk)
        def _(jg=jg):
            for j in range(jg * SUBLANES, min((jg + 1) * SUBLANES, n_slc)):
                other = imp_t[j:j + 1, :]
                for r in range(ngrp):
                    if r * SUBLANES > j:
                        beats = other >= cands[r]
                    elif (r + 1) * SUBLANES <= j:
                        beats = other > cands[r]
                    else:
                        beats = jnp.where(row_id > j - r * SUBLANES, jnp.where(other >= cands[r], 1.0, 0.0),
                                          jnp.where(other > cands[r], 1.0, 0.0)) > 0.5
                    rs = slice(r * SUBLANES, (r + 1) * SUBLANES)
                    rank_ref[rs, :] = rank_ref[rs, :] + jnp.where(beats, 1.0, 0.0)
    ranks = [rank_ref[r * SUBLANES:(r + 1) * SUBLANES, :] for r in range(ngrp)]
    unsel = [jnp.where(r < float(min(SLC_TOPK, n_slc)), 0.0, 1.0) for r in ranks]
    if ngrp * SUBLANES > n_slc:
        unsel[-1] = jnp.where(row_id < n_slc - (ngrp - 1) * SUBLANES, unsel[-1], 0.0)
    sel_t = jnp.concatenate(unsel, axis=0)
    if ngrp * SUBLANES < LANES:
        sel_t = jnp.concatenate([sel_t, jnp.zeros((LANES - ngrp * SUBLANES, rows), F32)], axis=0)
    for j in range(nsub):
        sel_ref[0, 0, j * QBLK:(j + 1) * QBLK, :] = sel_t[:, j * QBLK:(j + 1) * QBLK].T.astype(sel_ref.dtype)


def _cmp_select(proj3, kvc, bias_c, ovl, n_slc):
    B, T, _ = proj3.shape
    ncp = kvc.shape[3]
    nsub = math.gcd(CMP_SUBBLOCKS, T // QBLK)
    rows = nsub * QBLK
    return pl.pallas_call(
        functools.partial(_cmp_select_kernel, n_slc=n_slc, nsub=nsub),
        grid=(B, A_KV_HEADS, T // rows),
        in_specs=[
            pl.BlockSpec((1, rows, A_GRP * HEAD_DIM), lambda b, g, i: (b, i, g)),
            pl.BlockSpec((1, 1, 1, ncp, HEAD_DIM), lambda b, g, i: (0, b, g, 0, 0)),
            pl.BlockSpec((1, 1, 1, ncp, HEAD_DIM), lambda b, g, i: (1, b, g, 0, 0)),
            pl.BlockSpec((A_GRP, rows, ncp), lambda b, g, i: (g, i, 0)),
            pl.BlockSpec((ncp, LANES), lambda b, g, i: (0, 0)),
        ],
        out_specs=(
            pl.BlockSpec((1, rows, A_GRP * HEAD_DIM), lambda b, g, i: (b, i, g)),
            pl.BlockSpec((1, 1, rows, LANES), lambda b, g, i: (b, g, i, 0)),
        ),
        out_shape=(
            jax.ShapeDtypeStruct((B, T, A_Q_HEADS * HEAD_DIM), MIXER_OUT),
            jax.ShapeDtypeStruct((B, A_KV_HEADS, T, LANES), BF16),
        ),
        scratch_shapes=[pltpu.VMEM((-(-n_slc // SUBLANES) * SUBLANES, rows), F32)],
        compiler_params=_cparams(("parallel", "parallel", "arbitrary")),
        name="nsa_cmp_select",
    )(proj3, kvc, kvc, bias_c, ovl)


def _slc_kernel(q_ref, k_ref, v_ref, sel_ref, e_ref, b_ref, o_ref, *scratch, n_tiles, nq):
    m_refs, l_refs, acc_refs = scratch[0::3], scratch[1::3], scratch[2::3]
    kw = SLC_CHUNK * QBLK
    for a in range(nq):
        it = pl.program_id(1) * nq + a
        rs = slice(a * QBLK, (a + 1) * QBLK)
        qs, sels = [], []
        for g in range(A_KV_HEADS):
            m_refs[g][...] = jnp.full(m_refs[g].shape, NEG, F32)
            l_refs[g][...] = jnp.zeros(l_refs[g].shape, F32)
            acc_refs[g][...] = jnp.zeros(acc_refs[g].shape, F32)
            qs.append(jnp.concatenate([q_ref[0, rs, (g * A_GRP + h) * HEAD_DIM:(g * A_GRP + h + 1) * HEAD_DIM]
                                       for h in range(A_GRP)], axis=0))
            sels.append(sel_ref[0, g, rs, :])

        def body(c, carry, it=it, qs=qs, sels=sels):
            start = pl.multiple_of(c * kw, kw)
            tidx = [jnp.clip(it - (c * SLC_CHUNK + j) + 1, 0, n_tiles - 1) for j in range(SLC_CHUNK)]
            kts = [k_ref[0, pl.ds(start, kw), g * HEAD_DIM:(g + 1) * HEAD_DIM] for g in range(A_KV_HEADS)]
            vts = [v_ref[0, pl.ds(start, kw), g * HEAD_DIM:(g + 1) * HEAD_DIM] for g in range(A_KV_HEADS)]
            ech = e_ref[:, pl.ds(start, kw)]
            ss = [lax.dot_general(qs[g], kts[g], (((1,), (1,)), ((), ())), preferred_element_type=F32)
                  for g in range(A_KV_HEADS)]
            madds = [jnp.dot(sels[g], ech, preferred_element_type=F32) for g in range(A_KV_HEADS)]
            pbs, alphas = [], []
            for g in range(A_KV_HEADS):
                rows = [[ss[g][h * QBLK:(h + 1) * QBLK, j * QBLK:(j + 1) * QBLK] + b_ref[g * A_GRP + h, tidx[j]]
                         + madds[g][:, j * QBLK:(j + 1) * QBLK] for j in range(SLC_CHUNK)] for h in range(A_GRP)]
                tile_max = jnp.concatenate([functools.reduce(jnp.maximum, r) for r in rows], axis=0)
                m_old = m_refs[g][...]
                m_new = jnp.maximum(m_old, jnp.max(tile_max, axis=-1, keepdims=True))
                alpha = jnp.exp2(m_old - m_new)
                p = [[jnp.exp2(t - m_new[h * QBLK:(h + 1) * QBLK]) for t in r] for h, r in enumerate(rows)]
                tile_sum = jnp.concatenate([functools.reduce(jnp.add, r) for r in p], axis=0)
                l_refs[g][...] = alpha * l_refs[g][...] + jnp.sum(tile_sum, axis=-1, keepdims=True)
                pbs.append(jnp.concatenate([jnp.concatenate([t.astype(BF16) for t in r], axis=1) for r in p], axis=0))
                alphas.append(alpha)
                m_refs[g][...] = m_new
            for g in range(A_KV_HEADS):
                acc_refs[g][...] = alphas[g] * acc_refs[g][...] + jnp.dot(pbs[g], vts[g], preferred_element_type=F32)
            return carry

        lax.fori_loop(0, it // SLC_CHUNK + 1, body, 0)
        for g in range(A_KV_HEADS):
            out = acc_refs[g][...] / l_refs[g][...]
            for h in range(A_GRP):
                col = (g * A_GRP + h) * HEAD_DIM
                o_ref[0, rs, col:col + HEAD_DIM] = out[h * QBLK:(h + 1) * QBLK].astype(o_ref.dtype)


def _slc_attn(proj3, sel, expand, bias):
    B, T, _ = proj3.shape
    n_tiles = bias.shape[1]
    nq = SLC_QTILES
    qrows = nq * QBLK
    assert T % (nq * QBLK) == 0 and T % (SLC_CHUNK * QBLK) == 0
    chain_rows = A_GRP * QBLK
    n_chains = A_KV_HEADS
    return pl.pallas_call(
        functools.partial(_slc_kernel, n_tiles=n_tiles, nq=nq),
        grid=(B, T // qrows),
        in_specs=[
            pl.BlockSpec((1, qrows, A_Q_HEADS * HEAD_DIM), lambda b, i: (b, i, COL_QA // A_Q_HEADS)),
            pl.BlockSpec((1, T, A_KV_HEADS * HEAD_DIM), lambda b, i: (b, 0, COL_KSA // A_KV_HEADS)),
            pl.BlockSpec((1, T, A_KV_HEADS * HEAD_DIM), lambda b, i: (b, 0, COL_VSA // A_KV_HEADS)),
            pl.BlockSpec((1, A_KV_HEADS, qrows, LANES), lambda b, i: (b, 0, i, 0)),
            pl.BlockSpec((LANES, T), lambda b, i: (0, 0)),
            pl.BlockSpec((A_Q_HEADS, n_tiles, QBLK, LANES), lambda b, i: (0, 0, 0, 0)),
        ],
        out_specs=pl.BlockSpec((1, qrows, A_Q_HEADS * HEAD_DIM), lambda b, i: (b, i, 0)),
        out_shape=jax.ShapeDtypeStruct((B, T, A_Q_HEADS * HEAD_DIM), MIXER_OUT),
        scratch_shapes=[pltpu.VMEM((chain_rows, 1), F32), pltpu.VMEM((chain_rows, 1), F32),
                        pltpu.VMEM((chain_rows, HEAD_DIM), F32)] * n_chains,
        compiler_params=_cparams(("parallel", "arbitrary")),
        name="nsa_slc_attn",
    )(proj3, proj3, proj3, sel, expand, bias)


def _mix_outproj_kernel(x_ref, gate_ref, ocmp_ref, oslc_ref, owin_ref, ob_ref, *rest):
    oc_refs, lse_refs = rest[:C_Q_HEADS], rest[C_Q_HEADS:2 * C_Q_HEADS]
    w_ref, o_ref, mix_ref = rest[2 * C_Q_HEADS:]
    tm = x_ref.shape[0]
    nsplit = MIX_ROW_SPLIT
    rows = tm // nsplit
    for part in range(nsplit):
        rs = slice(part * rows, (part + 1) * rows)
        gate = 1.0 / (1.0 + jnp.exp(-gate_ref[rs, :]))
        for h in range(A_Q_HEADS):
            sl = slice(h * HEAD_DIM, (h + 1) * HEAD_DIM)
            o = (gate[:, 3 * h:3 * h + 1] * ocmp_ref[rs, sl] + gate[:, 3 * h + 1:3 * h + 2] * oslc_ref[rs, sl]
                 + gate[:, 3 * h + 2:3 * h + 3] * owin_ref[rs, sl])
            mix_ref[rs, sl] = o.astype(mix_ref.dtype)
        base = A_Q_HEADS * HEAD_DIM
        width = B_Q_HEADS * HEAD_DIM
        mix_ref[rs, base:base + width] = ob_ref[rs, :].astype(mix_ref.dtype)
        base += width
        for hh in range(C_GRP):
            heads = [C_GRP * gidx + hh for gidx in range(len(DIL_PAIRS))]
            lses = [lse_refs[c][rs, :] for c in heads]
            mx = functools.reduce(jnp.maximum, lses)
            ws = [jnp.exp(x - mx) for x in lses]
            tot = functools.reduce(jnp.add, ws)
            for c, w in zip(heads, ws):
                mix_ref[rs, base + c * HEAD_DIM:base + (c + 1) * HEAD_DIM] = (oc_refs[c][rs, :] * (w / tot)).astype(mix_ref.dtype)
    for part in range(nsplit):
        rs = slice(part * rows, (part + 1) * rows)
        a = mix_ref[rs, :]
        for c0 in range(0, o_ref.shape[1], OUT_SUBTILE):
            sl = slice(c0, c0 + OUT_SUBTILE)
            o_ref[rs, sl] = x_ref[rs, sl] + jnp.dot(a, w_ref[:, sl], preferred_element_type=F32)


def _mix_outproj(x2d, gates, o_cmp, o_slc, o_win, o_b, o_cs, lses, w_all, layer, *, tm):
    m, d = x2d.shape
    k = w_all.shape[1]
    row = lambda w: pl.BlockSpec((tm, w), lambda i: (i, 0))
    head = lambda c: pl.BlockSpec((tm, HEAD_DIM), lambda i: (i, c))
    full = [x2d, gates, o_cmp, o_slc, o_win, o_b]
    return pl.pallas_call(
        _mix_outproj_kernel,
        grid=(m // tm,),
        in_specs=[row(a.shape[1]) for a in full] + [head(c) for _, c in o_cs + lses]
        + [pl.BlockSpec((None, k, d), lambda i: (layer, 0, 0), pipeline_mode=pl.Buffered(1))],
        out_specs=row(d),
        out_shape=jax.ShapeDtypeStruct((m, d), F32),
        scratch_shapes=[pltpu.VMEM((tm, k), BF16)],
        compiler_params=_cparams(("parallel",)),
        name="mix_out_proj",
    )(*full, *[a for a, _ in o_cs + lses], w_all)


def _mixers_outproj(x2d, proj3, gates, dil32, layer_params, tables, w_out_all, layer, *, tm):
    B, T, _ = proj3.shape
    cmp_pe, cmp_w1, cmp_w2, kc_gain, sinks = layer_params
    n_cmp = (T - CMP_BLOCK) // CMP_STRIDE + 1
    n_slc = T // SLC_BLOCK
    pe = jnp.broadcast_to(cmp_pe.reshape(2, 1, CMP_BLOCK * HEAD_DIM), (2, 8, CMP_BLOCK * HEAD_DIM)).astype(BF16)
    kvc = _compress(dil32, pe, cmp_w1.astype(BF16), cmp_w2.astype(BF16), kc_gain.reshape(1, HEAD_DIM), n_cmp)

    o_cmp, sel = _cmp_select(proj3, kvc, tables["bias_c"], tables["ovl"], n_slc)
    o_slc = _slc_attn(proj3, sel, tables["expand"], tables["bias_slc"])
    o_win = _band_attn(proj3, proj3, proj3, tables["bias_win"], n_kv=A_KV_HEADS, grp=A_GRP,
                       q_col=COL_QA // A_GRP, k_col=COL_KWA, v_col=COL_VWA)
    o_b = _band_attn(proj3, proj3, proj3, tables["bias_b"], n_kv=B_KV_HEADS, grp=B_GRP,
                     q_col=COL_QB // B_GRP, k_col=COL_KB, v_col=COL_VB, sinks=sinks)
    o_cs, lses = [], []
    for gidx, (_, dil) in enumerate(DIL_PAIRS):
        if dil == 1:
            o, lse = _band_attn(proj3, proj3, proj3, tables["bias_c%d" % gidx], n_kv=1, grp=C_GRP,
                                q_col=COL_QC // C_GRP + gidx, k_col=COL_KC + gidx, v_col=COL_VC + gidx,
                                with_lse=True)
            o_cs += [(o.reshape(B * T, -1), h) for h in range(C_GRP)]
            lses += [(lse.reshape(B * T, -1), h) for h in range(C_GRP)]
        else:
            os_, ls_ = _dil_attn(dil32, tables["bias_c%d" % gidx], col=DIL_F32_SLOT[COL_QC + C_GRP * gidx], dil=dil)
            o_cs += [(o.reshape(B * T, -1), 0) for o in os_]
            lses += [(lse.reshape(B * T, -1), 0) for lse in ls_]
    return _mix_outproj(x2d, gates, o_cmp.reshape(B * T, -1), o_slc.reshape(B * T, -1), o_win.reshape(B * T, -1),
                        o_b.reshape(B * T, -1), o_cs, lses, w_out_all, layer, tm=tm)


def _build_tables(rel_bias, T):
    n_cmp = (T - CMP_BLOCK) // CMP_STRIDE + 1
    n_slc = T // SLC_BLOCK
    ncp = T // CMP_STRIDE
    nq = T // QBLK
    tables = {}
    tables["bias_c"] = _cmp_bias(rel_bias, A_Q_HEADS, nq, ncp, n_cmp).reshape(A_Q_HEADS, T, ncp)
    n_sat = -(-(SAT_DIST + QBLK - 1) // QBLK) + 1
    tables["bias_slc"] = _bias_tiles(rel_bias, 0, A_Q_HEADS, min(n_sat, nq) + 1, koff=QBLK)
    win = NSA_WINDOW - 1
    tables["bias_win"] = _bias_tiles(rel_bias, 0, A_Q_HEADS, -(-win // QBLK) + 2, max_dist=win)
    swa = SWA_WINDOW - 1
    tables["bias_b"] = _bias_tiles(rel_bias, A_Q_HEADS, B_Q_HEADS, -(-swa // QBLK) + 2, max_dist=swa)
    for gidx, (w, dil) in enumerate(DIL_PAIRS):
        md = w // dil
        tables["bias_c%d" % gidx] = _bias_tiles(rel_bias, A_Q_HEADS + B_Q_HEADS + C_GRP * gidx, C_GRP,
                                                -(-md // QBLK) + 2, dscale=dil, max_dist=md)
    c0 = np.arange(ncp)[:, None] * CMP_STRIDE
    s0 = np.arange(LANES)[None, :] * SLC_BLOCK
    ovl = np.clip(np.minimum(c0 + CMP_BLOCK, s0 + SLC_BLOCK) - np.maximum(c0, s0), 0, None) / CMP_BLOCK
    ovl = ovl * (np.arange(ncp)[:, None] < n_cmp) * (np.arange(LANES)[None, :] < n_slc)
    tables["ovl"] = jnp.asarray(ovl, BF16)
    member = (np.arange(T)[None, :] // SLC_BLOCK) == np.arange(LANES)[:, None]
    tables["expand"] = jnp.asarray(member * MASK_WEIGHT, BF16)
    return tables


def _proj_gain(g):
    ones = jnp.ones((HEAD_DIM,), F32)
    spec = [(g[0] * (SCALE * LOG2E), 6), (ones, 4), (g[2], 2), (ones, 2), (g[3], 2), (ones, 2),
            (g[4] * (SCALE * LOG2E), 4), (g[5], 2), (ones, 2), (g[6] * (SCALE * LOG2E), 6), (g[7], 3), (ones, 3)]
    assert sum(n for _, n in spec) == N_MAIN_BLOCKS
    return jnp.concatenate([jnp.tile(v, n) for v, n in spec]).reshape(1, N_MAIN)


def kernel(x, norm_attn, w_in, qk_gain, cmp_pe, cmp_w1, cmp_w2, sinks, rel_bias, w_out, norm_ffn, w_gate, w_up, w_down):
    B, T, D = x.shape
    depth = w_in.shape[0]
    tables = _build_tables(rel_bias, T)
    x2 = x.reshape(B * T, D)
    w_all = _wprep(jnp.transpose(w_in, (0, 2, 1)).astype(BF16))
    w_out_b, w_gate_b, w_up_b, w_down_b = (w.astype(BF16) for w in (w_out, w_gate, w_up, w_down))
    tm = min(ROW_TILE, B * T)
    for l in range(depth):
        proj, gates, dil32 = _proj(x2, norm_attn[l].reshape(1, D), w_all, l, _proj_gain(qk_gain[l]), tm=tm)
        x2 = _mixers_outproj(x2, proj.reshape(B, T, N_MAIN), gates, dil32.reshape(B, T, DIL_F32_COLS),
                             (cmp_pe[l], cmp_w1[l], cmp_w2[l], qk_gain[l][1], sinks[l]), tables, w_out_b, l, tm=tm)
        x2 = _ffn(x2, norm_ffn[l].reshape(1, D), w_gate_b, w_up_b, w_down_b, l, tm=tm, tf=FFN_TILE)
    return x2.reshape(B, T, D)
```

```python
import functools
import math

import numpy as np
import jax
import jax.numpy as jnp
from jax import lax
from jax.experimental import pallas as pl
from jax.experimental.pallas import tpu as pltpu

F32 = jnp.float32
BF16 = jnp.bfloat16

HEAD_DIM = 128
MIXER_OUT = jnp.bfloat16
LANES = 128
SUBLANES = 8
QBLK = 128
BAND_SUBBLOCKS = 8
CMP_SUBBLOCKS = 4
A_Q_HEADS, A_KV_HEADS = 6, 2
A_GRP = A_Q_HEADS // A_KV_HEADS
B_Q_HEADS, B_KV_HEADS = 4, 2
B_GRP = B_Q_HEADS // B_KV_HEADS
DIL_PAIRS = ((128, 1), (512, 4), (2048, 16))
C_GRP = 2
C_Q_HEADS = C_GRP * len(DIL_PAIRS)
CMP_BLOCK, CMP_STRIDE = 32, 16
SLC_BLOCK, SLC_TOPK = 64, 16
SLC_CHUNK = 4
SLC_QTILES = 4
NSA_WINDOW, SWA_WINDOW = 512, 128
FORCED_SCORE = 1.0e4
REL_BUCKETS, REL_MAX_EXACT, REL_MAX_DIST = 32, 16, 2048
SCALE = HEAD_DIM ** -0.5
LOG2E = math.log2(math.e)
LN2 = math.log(2.0)
MASK_WEIGHT = -2.0 ** 100
EPS = 1e-6
NEG = -1e30
VMEM_LIMIT = 56 * 1024 * 1024

COL_QA, COL_KCA, COL_VCA, COL_KSA, COL_VSA, COL_KWA, COL_VWA = 0, 6, 8, 10, 12, 14, 16
COL_QB, COL_KB, COL_VB, COL_QC, COL_KC, COL_VC = 18, 22, 24, 26, 32, 35
N_MAIN_BLOCKS = 38
N_MAIN = N_MAIN_BLOCKS * LANES
GATE_START = 2304
N_GATES = A_Q_HEADS * 3
PROJ_NORMED = ([True] * 6 + [False] * 4 + [True] * 2 + [False] * 2 + [True] * 2 + [False] * 2
               + [True] * 6 + [False] * 2 + [True] * 9 + [False] * 3)
DIL_F32_SLOT = {}
for _g, (_, _d) in enumerate(DIL_PAIRS):
    if _d > 1:
        _base = 4 * len([1 for _, _dd in DIL_PAIRS[:_g] if _dd > 1])
        DIL_F32_SLOT.update({COL_QC + C_GRP * _g: _base, COL_QC + C_GRP * _g + 1: _base + 1,
                             COL_KC + _g: _base + 2, COL_VC + _g: _base + 3})
CMP_F32_SLOT0 = len(DIL_F32_SLOT)
DIL_F32_SLOT.update({COL_KCA + _i: CMP_F32_SLOT0 + _i for _i in range(2 * A_KV_HEADS)})
DIL_F32_COLS = LANES * len(DIL_F32_SLOT)
DIL_TILES_PER_STEP = 8
ROW_TILE = 512
FFN_ROW_TILE = 1024
FFN_TILE = 256
WPREP_ROWS = 384
PROJ_SUBTILE = 256
MIX_ROW_SPLIT = 4
OUT_SUBTILE = 512


def _bucket_starts():
    d = np.arange(0, 1 << 17)
    out = []
    for dt in (np.float32, np.float64):
        far = np.maximum(d, REL_MAX_EXACT).astype(dt)
        lb = REL_MAX_EXACT + (np.log(far / dt(REL_MAX_EXACT)) / dt(math.log(REL_MAX_DIST / REL_MAX_EXACT))
                              * dt(REL_BUCKETS - REL_MAX_EXACT)).astype(np.int64)
        out.append(np.where(d < REL_MAX_EXACT, d, np.minimum(lb, REL_BUCKETS - 1)))
    assert (out[0] == out[1]).all() and (np.diff(out[0]) >= 0).all()
    return [int(np.argmax(out[0] >= b)) for b in range(REL_BUCKETS)]


BUCKET_START = _bucket_starts()
SAT_DIST = BUCKET_START[REL_BUCKETS - 1]


def _cparams(sem, vmem=VMEM_LIMIT):
    return pltpu.CompilerParams(dimension_semantics=sem, vmem_limit_bytes=vmem)


def _bias_tile_kernel(tab_ref, o_ref, *, head0, koff, dscale, max_dist):
    h = pl.program_id(0) + head0
    r = lax.broadcasted_iota(jnp.int32, (QBLK, LANES), 0)
    c = lax.broadcasted_iota(jnp.int32, (QBLK, LANES), 1)
    for t in range(o_ref.shape[1]):
        dist = t * QBLK + r - c - koff
        d = dist * dscale
        val = jnp.full((QBLK, LANES), tab_ref[REL_BUCKETS - 1, h] * LOG2E, F32)
        for b in range(REL_BUCKETS - 2, -1, -1):
            val = jnp.where(d < BUCKET_START[b + 1], tab_ref[b, h] * LOG2E, val)
        val = jnp.where(dist >= 0, val, NEG)
        if max_dist is not None:
            val = jnp.where(dist <= max_dist, val, NEG)
        o_ref[0, t] = val


def _bias_tiles(rel_bias, head0, nheads, ntiles, *, koff=0, dscale=1, max_dist=None):
    kern = functools.partial(_bias_tile_kernel, head0=head0, koff=koff, dscale=dscale, max_dist=max_dist)
    return pl.pallas_call(
        kern,
        grid=(nheads,),
        in_specs=[pl.BlockSpec(memory_space=pltpu.SMEM)],
        out_specs=pl.BlockSpec((1, ntiles, QBLK, LANES), lambda h: (h, 0, 0, 0)),
        out_shape=jax.ShapeDtypeStruct((nheads, ntiles, QBLK, LANES), F32),
        compiler_params=_cparams(("parallel",)),
        name="bias_tiles",
    )(rel_bias)


def _cmp_bias_kernel(tab_ref, o_ref, *, n_cmp):
    h = pl.program_id(0)
    ntile, _, width = o_ref.shape[1:]
    r = lax.broadcasted_iota(jnp.int32, (QBLK, 2 * width), 0)
    c = lax.broadcasted_iota(jnp.int32, (QBLK, 2 * width), 1)
    dist = r - CMP_STRIDE * (c - width) - (CMP_BLOCK - 1)
    base = jnp.full((QBLK, 2 * width), tab_ref[REL_BUCKETS - 1, h] * LOG2E, F32)
    for b in range(REL_BUCKETS - 2, -1, -1):
        base = jnp.where(dist < BUCKET_START[b + 1], tab_ref[b, h] * LOG2E, base)
    base = jnp.where(dist >= 0, base, NEG)
    col = lax.broadcasted_iota(jnp.int32, (QBLK, width), 1)
    per_tile = QBLK // CMP_STRIDE
    for i in range(ntile):
        tile = pltpu.roll(base, (width + per_tile * i) % (2 * width), 1)[:, :width]
        o_ref[0, i] = jnp.where(col < n_cmp, tile, NEG)


def _cmp_bias(rel_bias, nheads, ntiles, width, n_cmp):
    return pl.pallas_call(
        functools.partial(_cmp_bias_kernel, n_cmp=n_cmp),
        grid=(nheads,),
        in_specs=[pl.BlockSpec(memory_space=pltpu.SMEM)],
        out_specs=pl.BlockSpec((1, ntiles, QBLK, width), lambda h: (h, 0, 0, 0)),
        out_shape=jax.ShapeDtypeStruct((nheads, ntiles, QBLK, width), F32),
        compiler_params=_cparams(("parallel",)),
        name="cmp_bias",
    )(rel_bias)


def _rms(x, w):
    ms = jnp.mean(x * x, axis=-1, keepdims=True)
    return x * lax.rsqrt(ms + EPS) * w


def _wprep_kernel(w_ref, nxt_ref, gate_ref, o_ref):
    j = pl.program_id(1)
    rows = o_ref.shape[1]
    gate_step = GATE_START // rows
    last = pl.num_programs(1) - 1

    @pl.when(j < gate_step)
    def _():
        o_ref[0] = w_ref[0]

    @pl.when((j >= gate_step) & (j < last))
    def _():
        o_ref[0, :rows - N_GATES] = w_ref[0, N_GATES:]
        o_ref[0, rows - N_GATES:] = nxt_ref[0, :N_GATES]

    @pl.when(j == last)
    def _():
        o_ref[0, :rows - LANES] = w_ref[0, N_GATES:N_GATES + rows - LANES]
        o_ref[0, rows - LANES:] = gate_ref[0, :LANES]


def _wprep(wt):
    depth, n_in, d = wt.shape
    rows = WPREP_ROWS
    assert GATE_START % rows == 0 and (N_MAIN + LANES) % rows == 0 and N_MAIN % rows == rows - LANES
    steps = (N_MAIN + LANES) // rows
    return pl.pallas_call(
        _wprep_kernel,
        grid=(depth, steps),
        in_specs=[pl.BlockSpec((1, rows, d), lambda l, j: (l, j, 0)),
                  pl.BlockSpec((1, rows, d), lambda l, j: (l, jnp.minimum(j + 1, steps - 1), 0)),
                  pl.BlockSpec((1, rows, d), lambda l, j: (l, GATE_START // rows, 0))],
        out_specs=pl.BlockSpec((1, rows, d), lambda l, j: (l, j, 0)),
        out_shape=jax.ShapeDtypeStruct((depth, N_MAIN + LANES, d), BF16),
        compiler_params=_cparams(("parallel", "parallel")),
        name="w_in_prep",
    )(wt, wt, wt)


def _proj_kernel(x_ref, nw_ref, w_ref, gain_ref, o_ref, gate_ref, dil_ref):
    h = _rms(x_ref[...], nw_ref[...]).astype(BF16)
    for c0 in range(0, N_MAIN, PROJ_SUBTILE):
        width = min(PROJ_SUBTILE, N_MAIN - c0)
        acc = lax.dot_general(h, w_ref[c0:c0 + width, :], (((1,), (1,)), ((), ())), preferred_element_type=F32)
        for c in range(width // LANES):
            blk = c0 // LANES + c
            sl = slice(blk * LANES, (blk + 1) * LANES)
            y = acc[:, c * LANES:(c + 1) * LANES]
            if PROJ_NORMED[blk]:
                ms = jnp.mean(y * y, axis=-1, keepdims=True)
                y = y * lax.rsqrt(ms + EPS) * gain_ref[:, sl]
            o_ref[:, sl] = y.astype(o_ref.dtype)
            if blk in DIL_F32_SLOT:
                dil_ref[:, DIL_F32_SLOT[blk] * LANES:(DIL_F32_SLOT[blk] + 1) * LANES] = y
    gate_ref[...] = lax.dot_general(h, w_ref[N_MAIN:, :], (((1,), (1,)), ((), ())), preferred_element_type=F32)


def _proj(x2d, norm_w, w_all, layer, gain, *, tm):
    m, d = x2d.shape
    n = w_all.shape[1]
    return pl.pallas_call(
        _proj_kernel,
        grid=(m // tm,),
        in_specs=[
            pl.BlockSpec((tm, d), lambda i: (i, 0)),
            pl.BlockSpec((1, d), lambda i: (0, 0)),
            pl.BlockSpec((None, n, d), lambda i: (layer, 0, 0), pipeline_mode=pl.Buffered(1)),
            pl.BlockSpec((1, N_MAIN), lambda i: (0, 0)),
        ],
        out_specs=(pl.BlockSpec((tm, N_MAIN), lambda i: (i, 0)), pl.BlockSpec((tm, LANES), lambda i: (i, 0)),
                   pl.BlockSpec((tm, DIL_F32_COLS), lambda i: (i, 0))),
        out_shape=(jax.ShapeDtypeStruct((m, N_MAIN), BF16), jax.ShapeDtypeStruct((m, LANES), F32),
                   jax.ShapeDtypeStruct((m, DIL_F32_COLS), F32)),
        compiler_params=_cparams(("parallel",)),
        name="in_proj",
    )(x2d, norm_w, w_all, gain)


def _ffn_kernel(x_ref, nw_ref, wg_ref, wu_ref, wd_ref, o_ref, h_ref):
    f = pl.program_id(1)

    @pl.when(f == 0)
    def _():
        x = x_ref[...]
        h_ref[...] = _rms(x, nw_ref[...]).astype(BF16)
        o_ref[...] = x

    h = h_ref[...]
    g = jnp.dot(h, wg_ref[...].astype(BF16), preferred_element_type=F32)
    u = jnp.dot(h, wu_ref[...].astype(BF16), preferred_element_type=F32)
    a = (g * (1.0 / (1.0 + jnp.exp(-g))) * u).astype(BF16)
    o_ref[...] += jnp.dot(a, wd_ref[...].astype(BF16), preferred_element_type=F32)


def _ffn(x2d, norm_w, wg, wu, wd, layer, *, tm, tf):
    m, d = x2d.shape
    dff = wg.shape[2]
    return pl.pallas_call(
        _ffn_kernel,
        grid=(m // tm, dff // tf),
        in_specs=[
            pl.BlockSpec((tm, d), lambda i, f: (i, 0)),
            pl.BlockSpec((1, d), lambda i, f: (0, 0)),
            pl.BlockSpec((None, d, tf), lambda i, f: (layer, 0, f)),
            pl.BlockSpec((None, d, tf), lambda i, f: (layer, 0, f)),
            pl.BlockSpec((None, tf, d), lambda i, f: (layer, f, 0)),
        ],
        out_specs=pl.BlockSpec((tm, d), lambda i, f: (i, 0)),
        out_shape=jax.ShapeDtypeStruct((m, d), F32),
        scratch_shapes=[pltpu.VMEM((tm, d), BF16)],
        compiler_params=_cparams(("parallel", "arbitrary")),
        name="ffn",
    )(x2d, norm_w, wg, wu, wd)


def _band_kernel(*refs, grp, n_off, nsub, has_sinks, with_lse):
    if has_sinks:
        sink_ref, q_ref, k_ref, v_ref, b_ref = refs[:5]
        outs = refs[5:]
    else:
        q_ref, k_ref, v_ref, b_ref = refs[:4]
        outs = refs[4:]
    o_ref = outs[0]
    g = pl.program_id(1)
    i0 = pl.program_id(2) * nsub
    tiles = {}

    def kv_tile(rel):
        if rel not in tiles:
            start = pl.multiple_of(jnp.maximum(i0 + rel, 0) * QBLK, QBLK)
            tiles[rel] = (k_ref[0, pl.ds(start, QBLK), :], v_ref[0, pl.ds(start, QBLK), :])
        return tiles[rel]

    def head_rows(x, h):
        return x[h * QBLK:(h + 1) * QBLK]

    kvs = [[kv_tile(j - off) for off in range(n_off)] for j in range(nsub)]
    tidxs = [[jnp.where(i0 + j - off >= 0, off, n_off) for off in range(n_off)] for j in range(nsub)]
    scores = []
    for j in range(nsub):
        rs = slice(j * QBLK, (j + 1) * QBLK)
        kcat = jnp.concatenate([t[0] for t in kvs[j]], axis=0)
        q = jnp.concatenate([q_ref[0, rs, h * HEAD_DIM:(h + 1) * HEAD_DIM] for h in range(grp)], axis=0)
        scores.append(lax.dot_general(q, kcat, (((1,), (1,)), ((), ())), preferred_element_type=F32))
    probs, stats = [], []
    for j in range(nsub):
        s = scores[j]
        st = [[head_rows(s, h)[:, off * QBLK:(off + 1) * QBLK] + b_ref[h, tidxs[j][off]] for off in range(n_off)]
              for h in range(grp)]
        m = jnp.max(jnp.concatenate([functools.reduce(jnp.maximum, r) for r in st], axis=0), axis=-1, keepdims=True)
        p = [[jnp.exp2(st[h][off] - head_rows(m, h)) for off in range(n_off)] for h in range(grp)]
        l = jnp.sum(jnp.concatenate([functools.reduce(jnp.add, r) for r in p], axis=0), axis=-1, keepdims=True)
        probs.append(jnp.concatenate([jnp.concatenate([t.astype(BF16) for t in r], axis=1) for r in p], axis=0))
        stats.append((m, l))
    for j in range(nsub):
        rs = slice(j * QBLK, (j + 1) * QBLK)
        m, l = stats[j]
        vcat = jnp.concatenate([t[1] for t in kvs[j]], axis=0)
        o = jnp.dot(probs[j], vcat, preferred_element_type=F32)
        den = l
        if has_sinks:
            sink = jnp.concatenate([jnp.full((QBLK, 1), sink_ref[g * grp + h] * LOG2E, F32) for h in range(grp)], axis=0)
            den = l + jnp.exp2(sink - m)
        o = o / den
        for h in range(grp):
            sl = slice(h * HEAD_DIM, (h + 1) * HEAD_DIM)
            o_ref[0, rs, sl] = head_rows(o, h).astype(o_ref.dtype)
            if with_lse:
                outs[1][0, rs, sl] = jnp.broadcast_to(head_rows(m * LN2 + jnp.log(l), h), (QBLK, HEAD_DIM))


def _band_attn(q_arr, k_arr, v_arr, bias, *, n_kv, grp, q_col, k_col, v_col, sinks=None, with_lse=False):
    n, L, _ = q_arr.shape
    n_off = bias.shape[1] - 1
    gw = grp * HEAD_DIM
    nsub = math.gcd(BAND_SUBBLOCKS, L // QBLK)
    rows = nsub * QBLK
    kern = functools.partial(_band_kernel, grp=grp, n_off=n_off, nsub=nsub, has_sinks=sinks is not None,
                             with_lse=with_lse)
    in_specs = [
        pl.BlockSpec((1, rows, gw), lambda b, g, i: (b, i, q_col + g)),
        pl.BlockSpec((1, L, HEAD_DIM), lambda b, g, i: (b, 0, k_col + g)),
        pl.BlockSpec((1, L, HEAD_DIM), lambda b, g, i: (b, 0, v_col + g)),
        pl.BlockSpec((grp, n_off + 1, QBLK, LANES), lambda b, g, i: (g, 0, 0, 0)),
    ]
    args = [q_arr, k_arr, v_arr, bias]
    if sinks is not None:
        in_specs = [pl.BlockSpec(memory_space=pltpu.SMEM)] + in_specs
        args = [sinks] + args
    o_spec = pl.BlockSpec((1, rows, gw), lambda b, g, i: (b, i, g))
    o_shape = jax.ShapeDtypeStruct((n, L, n_kv * gw), MIXER_OUT)
    lse_shape = jax.ShapeDtypeStruct((n, L, n_kv * gw), F32)
    return pl.pallas_call(
        kern,
        grid=(n, n_kv, L // rows),
        in_specs=in_specs,
        out_specs=(o_spec, o_spec) if with_lse else o_spec,
        out_shape=(o_shape, lse_shape) if with_lse else o_shape,
        compiler_params=_cparams(("parallel", "parallel", "arbitrary")),
        name="band_attn",
    )(*args)


def _dil_kernel(*refs, dil, nres):
    q_refs, (k_ref, v_ref, b_ref) = refs[:C_GRP], refs[C_GRP:C_GRP + 3]
    o_refs, lse_refs = refs[C_GRP + 3:2 * C_GRP + 3], refs[2 * C_GRP + 3:]
    ls = k_ref.shape[1] // dil
    ntile = ls // QBLK
    r0 = pl.program_id(1) * nres
    units = [(rr, j) for rr in range(nres) for j in range(ntile)]

    def rows(rr, j):
        return pl.ds(r0 + rr + dil * QBLK * j, QBLK, stride=dil)

    kv = {u: (k_ref[0, rows(*u), :].astype(BF16), v_ref[0, rows(*u), :].astype(BF16)) for u in units}
    n_off = b_ref.shape[1] - 1
    offs = {(rr, j): [off for off in range(n_off) if j - off >= 0] for rr, j in units}
    scores = []
    for rr, j in units:
        kcat = jnp.concatenate([kv[(rr, j - off)][0] for off in offs[(rr, j)]], axis=0)
        q = jnp.concatenate([q_ref[0, rows(rr, j), :].astype(BF16) for q_ref in q_refs], axis=0)
        scores.append(lax.dot_general(q, kcat, (((1,), (1,)), ((), ())), preferred_element_type=F32))
    probs, stats = [], []
    for u, s in zip(units, scores):
        st = [[s[h * QBLK:(h + 1) * QBLK, n * QBLK:(n + 1) * QBLK] + b_ref[h, off] for n, off in enumerate(offs[u])]
              for h in range(C_GRP)]
        m = jnp.max(jnp.concatenate([functools.reduce(jnp.maximum, r) for r in st], axis=0), axis=-1, keepdims=True)
        p = [[jnp.exp2(t - m[h * QBLK:(h + 1) * QBLK]) for t in r] for h, r in enumerate(st)]
        l = jnp.sum(jnp.concatenate([functools.reduce(jnp.add, r) for r in p], axis=0), axis=-1, keepdims=True)
        probs.append(jnp.concatenate([jnp.concatenate([t.astype(BF16) for t in r], axis=1) for r in p], axis=0))
        stats.append((m, l))
    for (rr, j), pb, (m, l) in zip(units, probs, stats):
        vcat = jnp.concatenate([kv[(rr, j - off)][1] for off in offs[(rr, j)]], axis=0)
        o = jnp.dot(pb, vcat, preferred_element_type=F32) / l
        lse = m * LN2 + jnp.log(l)
        for h in range(C_GRP):
            o_refs[h][0, rows(rr, j), :] = o[h * QBLK:(h + 1) * QBLK]
            lse_refs[h][0, rows(rr, j), :] = jnp.broadcast_to(lse[h * QBLK:(h + 1) * QBLK], (QBLK, HEAD_DIM))


def _dil_attn(x3, bias, *, col, dil):
    B, T, _ = x3.shape
    ntile = T // dil // QBLK
    nres = max(1, min(dil, DIL_TILES_PER_STEP // ntile))
    spec = pl.BlockSpec((1, T, HEAD_DIM), lambda b, r: (b, 0, 0))
    shape = jax.ShapeDtypeStruct((B, T, HEAD_DIM), F32)
    outs = pl.pallas_call(
        functools.partial(_dil_kernel, dil=dil, nres=nres),
        grid=(B, dil // nres),
        in_specs=[pl.BlockSpec((1, T, HEAD_DIM), functools.partial(lambda b, r, c: (b, 0, c), c=col + c))
                  for c in range(C_GRP + 2)]
        + [pl.BlockSpec((C_GRP, bias.shape[1], QBLK, LANES), lambda b, r: (0, 0, 0, 0))],
        out_specs=(spec,) * (2 * C_GRP),
        out_shape=(shape,) * (2 * C_GRP),
        compiler_params=_cparams(("parallel", "arbitrary")),
        name="dilated_attn",
    )(*([x3] * (C_GRP + 2)), bias)
    return outs[:C_GRP], outs[C_GRP:]


def _compress_kernel(x_ref, pe_ref, w1_ref, w2_ref, gain_ref, o_ref, *, n_cmp):
    half = w1_ref.shape[1] // 2
    rows = x_ref.shape[1] // CMP_STRIDE
    xr = jnp.concatenate([x_ref[0, pl.ds(t, rows, stride=CMP_STRIDE), :].astype(BF16) for t in range(CMP_STRIDE)],
                         axis=1)
    y0 = jnp.dot(xr, w1_ref[0, :half], preferred_element_type=F32)
    y1 = jnp.dot(xr, w1_ref[0, half:], preferred_element_type=F32)
    pe = jnp.dot(pe_ref[0], w1_ref[0], preferred_element_type=F32)[0:1]
    c = y0 + pltpu.roll(y1, rows - 1, 0) + pe
    gl = 0.5 * c * (1.0 + jnp.tanh(math.sqrt(2.0 / math.pi) * (c + 0.044715 * (c * c * c))))
    out = jnp.dot(gl.astype(BF16), w2_ref[0], preferred_element_type=F32)
    out = jnp.where(pl.program_id(0) == 0, _rms(out, gain_ref[...]), out)
    valid = lax.broadcasted_iota(jnp.int32, out.shape, 0) < n_cmp
    o_ref[0, 0, 0] = jnp.where(valid, out, 0.0).astype(o_ref.dtype)


def _compress(x3, pe, w1, w2, gain, n_cmp):
    B, T, _ = x3.shape
    rows = T // CMP_STRIDE
    wide = CMP_BLOCK * HEAD_DIM
    return pl.pallas_call(
        functools.partial(_compress_kernel, n_cmp=n_cmp),
        grid=(2, B, A_KV_HEADS),
        in_specs=[
            pl.BlockSpec((1, T, HEAD_DIM), lambda s, b, g: (b, 0, CMP_F32_SLOT0 + s * A_KV_HEADS + g)),
            pl.BlockSpec((1, 8, wide), lambda s, b, g: (s, 0, 0)),
            pl.BlockSpec((1, wide, HEAD_DIM), lambda s, b, g: (s, 0, 0)),
            pl.BlockSpec((1, HEAD_DIM, HEAD_DIM), lambda s, b, g: (s, 0, 0)),
            pl.BlockSpec((1, HEAD_DIM), lambda s, b, g: (0, 0)),
        ],
        out_specs=pl.BlockSpec((1, 1, 1, rows, HEAD_DIM), lambda s, b, g: (s, b, g, 0, 0)),
        out_shape=jax.ShapeDtypeStruct((2, B, A_KV_HEADS, rows, HEAD_DIM), BF16),
        compiler_params=_cparams(("arbitrary", "arbitrary", "arbitrary")),
        name="nsa_compress",
    )(x3, pe, w1, w2, gain)


def _cmp_select_kernel(q_ref, kc_ref, vc_ref, b_ref, ovl_ref, o_ref, sel_ref, rank_ref, *, n_slc, nsub):
    i0 = pl.program_id(2) * nsub
    rows = nsub * QBLK
    kc = kc_ref[0, 0, 0]
    vc = vc_ref[0, 0, 0]
    q_all = jnp.concatenate([q_ref[0, :, h * HEAD_DIM:(h + 1) * HEAD_DIM] for h in range(A_GRP)], axis=0)
    s_all = lax.dot_general(q_all, kc, (((1,), (1,)), ((), ())), preferred_element_type=F32)
    ps = []
    for h in range(A_GRP):
        s = s_all[h * rows:(h + 1) * rows] + b_ref[h]
        m = jnp.max(s, axis=-1, keepdims=True)
        e = jnp.exp2(s - jnp.where(m > 0.5 * NEG, m, 0.0))
        den = jnp.sum(e, axis=-1, keepdims=True)
        ps.append((e / jnp.where(den > 0, den, 1.0)).astype(BF16))
    r_all = jnp.dot(jnp.concatenate(ps, axis=0), jnp.concatenate([vc, ovl_ref[...]], axis=1),
                    preferred_element_type=F32)
    imp = jnp.zeros((rows, LANES), F32)
    for h in range(A_GRP):
        o_ref[0, :, h * HEAD_DIM:(h + 1) * HEAD_DIM] = r_all[h * rows:(h + 1) * rows, :HEAD_DIM].astype(o_ref.dtype)
        imp = imp + r_all[h * rows:(h + 1) * rows, HEAD_DIM:]
    t = i0 * QBLK + lax.broadcasted_iota(jnp.int32, (rows, LANES), 0)
    blk = lax.broadcasted_iota(jnp.int32, (rows, LANES), 1)
    cur = t // SLC_BLOCK
    imp = jnp.where(blk == 0, FORCED_SCORE, imp)
    imp = jnp.where(blk == cur, FORCED_SCORE, imp)
    imp = jnp.where(blk == cur - 1, FORCED_SCORE, imp)
    imp = jnp.where(blk * SLC_BLOCK <= t, imp, NEG)
    imp = jnp.where(blk < n_slc, imp, 2.0 * NEG)
    imp_t = jnp.concatenate([imp[j * QBLK:(j + 1) * QBLK].T for j in range(nsub)], axis=1)
    ngrp = -(-n_slc // SUBLANES)
    cands = [imp_t[r * SUBLANES:(r + 1) * SUBLANES] for r in range(ngrp)]
    row_id = lax.broadcasted_iota(jnp.int32, (SUBLANES, rows), 0)
    rank_ref[...] = jnp.zeros(rank_ref.shape, F32)
    last_blk = ((i0 + nsub) * QBLK - 1) // SLC_BLOCK
    for jg in range(ngrp):
        @pl.when(jg * SUBLANES <= last_blk)
        def _(jg=jg):
            for j in range(jg * SUBLANES, min((jg + 1) * SUBLANES, n_slc)):
                other = imp_t[j:j + 1, :]
                for r in range(ngrp):
                    if r * SUBLANES > j:
                        beats = other >= cands[r]
                    elif (r + 1) * SUBLANES <= j:
                        beats = other > cands[r]
                    else:
                        beats = jnp.where(row_id > j - r * SUBLANES, jnp.where(other >= cands[r], 1.0, 0.0),
                                          jnp.where(other > cands[r], 1.0, 0.0)) > 0.5
                    rs = slice(r * SUBLANES, (r + 1) * SUBLANES)
                    rank_ref[rs, :] = rank_ref[rs, :] + jnp.where(beats, 1.0, 0.0)
    ranks = [rank_ref[r * SUBLANES:(r + 1) * SUBLANES, :] for r in range(ngrp)]
    unsel = [jnp.where(r < float(min(SLC_TOPK, n_slc)), 0.0, 1.0) for r in ranks]
    if ngrp * SUBLANES > n_slc:
        unsel[-1] = jnp.where(row_id < n_slc - (ngrp - 1) * SUBLANES, unsel[-1], 0.0)
    sel_t = jnp.concatenate(unsel, axis=0)
    if ngrp * SUBLANES < LANES:
        sel_t = jnp.concatenate([sel_t, jnp.zeros((LANES - ngrp * SUBLANES, rows), F32)], axis=0)
    for j in range(nsub):
        sel_ref[0, 0, j * QBLK:(j + 1) * QBLK, :] = sel_t[:, j * QBLK:(j + 1) * QBLK].T.astype(sel_ref.dtype)


def _cmp_select(proj3, kvc, bias_c, ovl, n_slc):
    B, T, _ = proj3.shape
    ncp = kvc.shape[3]
    nsub = math.gcd(CMP_SUBBLOCKS, T // QBLK)
    rows = nsub * QBLK
    return pl.pallas_call(
        functools.partial(_cmp_select_kernel, n_slc=n_slc, nsub=nsub),
        grid=(B, A_KV_HEADS, T // rows),
        in_specs=[
            pl.BlockSpec((1, rows, A_GRP * HEAD_DIM), lambda b, g, i: (b, i, g)),
            pl.BlockSpec((1, 1, 1, ncp, HEAD_DIM), lambda b, g, i: (0, b, g, 0, 0)),
            pl.BlockSpec((1, 1, 1, ncp, HEAD_DIM), lambda b, g, i: (1, b, g, 0, 0)),
            pl.BlockSpec((A_GRP, rows, ncp), lambda b, g, i: (g, i, 0)),
            pl.BlockSpec((ncp, LANES), lambda b, g, i: (0, 0)),
        ],
        out_specs=(
            pl.BlockSpec((1, rows, A_GRP * HEAD_DIM), lambda b, g, i: (b, i, g)),
            pl.BlockSpec((1, 1, rows, LANES), lambda b, g, i: (b, g, i, 0)),
        ),
        out_shape=(
            jax.ShapeDtypeStruct((B, T, A_Q_HEADS * HEAD_DIM), MIXER_OUT),
            jax.ShapeDtypeStruct((B, A_KV_HEADS, T, LANES), BF16),
        ),
        scratch_shapes=[pltpu.VMEM((-(-n_slc // SUBLANES) * SUBLANES, rows), F32)],
        compiler_params=_cparams(("parallel", "parallel", "arbitrary")),
        name="nsa_cmp_select",
    )(proj3, kvc, kvc, bias_c, ovl)


def _slc_kernel(q_ref, k_ref, v_ref, sel_ref, e_ref, b_ref, o_ref, *scratch, n_tiles, nq):
    m_refs, l_refs, acc_refs = scratch[0::3], scratch[1::3], scratch[2::3]
    kw = SLC_CHUNK * QBLK
    for a in range(nq):
        it = pl.program_id(1) * nq + a
        rs = slice(a * QBLK, (a + 1) * QBLK)
        qs, sels = [], []
        for g in range(A_KV_HEADS):
            m_refs[g][...] = jnp.full(m_refs[g].shape, NEG, F32)
            l_refs[g][...] = jnp.zeros(l_refs[g].shape, F32)
            acc_refs[g][...] = jnp.zeros(acc_refs[g].shape, F32)
            qs.append(jnp.concatenate([q_ref[0, rs, (g * A_GRP + h) * HEAD_DIM:(g * A_GRP + h + 1) * HEAD_DIM]
                                       for h in range(A_GRP)], axis=0))
            sels.append(sel_ref[0, g, rs, :])

        def body(c, carry, it=it, qs=qs, sels=sels):
            start = pl.multiple_of(c * kw, kw)
            tidx = [jnp.clip(it - (c * SLC_CHUNK + j) + 1, 0, n_tiles - 1) for j in range(SLC_CHUNK)]
            kts = [k_ref[0, pl.ds(start, kw), g * HEAD_DIM:(g + 1) * HEAD_DIM] for g in range(A_KV_HEADS)]
            vts = [v_ref[0, pl.ds(start, kw), g * HEAD_DIM:(g + 1) * HEAD_DIM] for g in range(A_KV_HEADS)]
            ech = e_ref[:, pl.ds(start, kw)]
            ss = [lax.dot_general(qs[g], kts[g], (((1,), (1,)), ((), ())), preferred_element_type=F32)
                  for g in range(A_KV_HEADS)]
            madds = [jnp.dot(sels[g], ech, preferred_element_type=F32) for g in range(A_KV_HEADS)]
            pbs, alphas = [], []
            for g in range(A_KV_HEADS):
                rows = [[ss[g][h * QBLK:(h + 1) * QBLK, j * QBLK:(j + 1) * QBLK] + b_ref[g * A_GRP + h, tidx[j]]
                         + madds[g][:, j * QBLK:(j + 1) * QBLK] for j in range(SLC_CHUNK)] for h in range(A_GRP)]
                tile_max = jnp.concatenate([functools.reduce(jnp.maximum, r) for r in rows], axis=0)
                m_old = m_refs[g][...]
                m_new = jnp.maximum(m_old, jnp.max(tile_max, axis=-1, keepdims=True))
                alpha = jnp.exp2(m_old - m_new)
                p = [[jnp.exp2(t - m_new[h * QBLK:(h + 1) * QBLK]) for t in r] for h, r in enumerate(rows)]
                tile_sum = jnp.concatenate([functools.reduce(jnp.add, r) for r in p], axis=0)
                l_refs[g][...] = alpha * l_refs[g][...] + jnp.sum(tile_sum, axis=-1, keepdims=True)
                pbs.append(jnp.concatenate([jnp.concatenate([t.astype(BF16) for t in r], axis=1) for r in p], axis=0))
                alphas.append(alpha)
                m_refs[g][...] = m_new
            for g in range(A_KV_HEADS):
                acc_refs[g][...] = alphas[g] * acc_refs[g][...] + jnp.dot(pbs[g], vts[g], preferred_element_type=F32)
            return carry

        lax.fori_loop(0, it // SLC_CHUNK + 1, body, 0)
        for g in range(A_KV_HEADS):
            out = acc_refs[g][...] / l_refs[g][...]
            for h in range(A_GRP):
                col = (g * A_GRP + h) * HEAD_DIM
                o_ref[0, rs, col:col + HEAD_DIM] = out[h * QBLK:(h + 1) * QBLK].astype(o_ref.dtype)


def _slc_attn(proj3, sel, expand, bias):
    B, T, _ = proj3.shape
    n_tiles = bias.shape[1]
    nq = SLC_QTILES
    qrows = nq * QBLK
    assert T % (nq * QBLK) == 0 and T % (SLC_CHUNK * QBLK) == 0
    chain_rows = A_GRP * QBLK
    n_chains = A_KV_HEADS
    return pl.pallas_call(
        functools.partial(_slc_kernel, n_tiles=n_tiles, nq=nq),
        grid=(B, T // qrows),
        in_specs=[
            pl.BlockSpec((1, qrows, A_Q_HEADS * HEAD_DIM), lambda b, i: (b, i, COL_QA // A_Q_HEADS)),
            pl.BlockSpec((1, T, A_KV_HEADS * HEAD_DIM), lambda b, i: (b, 0, COL_KSA // A_KV_HEADS)),
            pl.BlockSpec((1, T, A_KV_HEADS * HEAD_DIM), lambda b, i: (b, 0, COL_VSA // A_KV_HEADS)),
            pl.BlockSpec((1, A_KV_HEADS, qrows, LANES), lambda b, i: (b, 0, i, 0)),
            pl.BlockSpec((LANES, T), lambda b, i: (0, 0)),
            pl.BlockSpec((A_Q_HEADS, n_tiles, QBLK, LANES), lambda b, i: (0, 0, 0, 0)),
        ],
        out_specs=pl.BlockSpec((1, qrows, A_Q_HEADS * HEAD_DIM), lambda b, i: (b, i, 0)),
        out_shape=jax.ShapeDtypeStruct((B, T, A_Q_HEADS * HEAD_DIM), MIXER_OUT),
        scratch_shapes=[pltpu.VMEM((chain_rows, 1), F32), pltpu.VMEM((chain_rows, 1), F32),
                        pltpu.VMEM((chain_rows, HEAD_DIM), F32)] * n_chains,
        compiler_params=_cparams(("parallel", "arbitrary")),
        name="nsa_slc_attn",
    )(proj3, proj3, proj3, sel, expand, bias)


def _mix_outproj_kernel(x_ref, gate_ref, ocmp_ref, oslc_ref, owin_ref, ob_ref, *rest):
    oc_refs, lse_refs = rest[:C_Q_HEADS], rest[C_Q_HEADS:2 * C_Q_HEADS]
    w_ref, o_ref, mix_ref = rest[2 * C_Q_HEADS:]
    tm = x_ref.shape[0]
    nsplit = MIX_ROW_SPLIT
    rows = tm // nsplit
    for part in range(nsplit):
        rs = slice(part * rows, (part + 1) * rows)
        gate = 1.0 / (1.0 + jnp.exp(-gate_ref[rs, :]))
        for h in range(A_Q_HEADS):
            sl = slice(h * HEAD_DIM, (h + 1) * HEAD_DIM)
            o = (gate[:, 3 * h:3 * h + 1] * ocmp_ref[rs, sl] + gate[:, 3 * h + 1:3 * h + 2] * oslc_ref[rs, sl]
                 + gate[:, 3 * h + 2:3 * h + 3] * owin_ref[rs, sl])
            mix_ref[rs, sl] = o.astype(mix_ref.dtype)
        base = A_Q_HEADS * HEAD_DIM
        width = B_Q_HEADS * HEAD_DIM
        mix_ref[rs, base:base + width] = ob_ref[rs, :].astype(mix_ref.dtype)
        base += width
        for hh in range(C_GRP):
            heads = [C_GRP * gidx + hh for gidx in range(len(DIL_PAIRS))]
            lses = [lse_refs[c][rs, :] for c in heads]
            mx = functools.reduce(jnp.maximum, lses)
            ws = [jnp.exp(x - mx) for x in lses]
            tot = functools.reduce(jnp.add, ws)
            for c, w in zip(heads, ws):
                mix_ref[rs, base + c * HEAD_DIM:base + (c + 1) * HEAD_DIM] = (oc_refs[c][rs, :] * (w / tot)).astype(mix_ref.dtype)
    for part in range(nsplit):
        rs = slice(part * rows, (part + 1) * rows)
        a = mix_ref[rs, :]
        for c0 in range(0, o_ref.shape[1], OUT_SUBTILE):
            sl = slice(c0, c0 + OUT_SUBTILE)
            o_ref[rs, sl] = x_ref[rs, sl] + jnp.dot(a, w_ref[:, sl], preferred_element_type=F32)


def _mix_outproj(x2d, gates, o_cmp, o_slc, o_win, o_b, o_cs, lses, w_all, layer, *, tm):
    m, d = x2d.shape
    k = w_all.shape[1]
    row = lambda w: pl.BlockSpec((tm, w), lambda i: (i, 0))
    head = lambda c: pl.BlockSpec((tm, HEAD_DIM), lambda i: (i, c))
    full = [x2d, gates, o_cmp, o_slc, o_win, o_b]
    return pl.pallas_call(
        _mix_outproj_kernel,
        grid=(m // tm,),
        in_specs=[row(a.shape[1]) for a in full] + [head(c) for _, c in o_cs + lses]
        + [pl.BlockSpec((None, k, d), lambda i: (layer, 0, 0), pipeline_mode=pl.Buffered(1))],
        out_specs=row(d),
        out_shape=jax.ShapeDtypeStruct((m, d), F32),
        scratch_shapes=[pltpu.VMEM((tm, k), BF16)],
        compiler_params=_cparams(("parallel",)),
        name="mix_out_proj",
    )(*full, *[a for a, _ in o_cs + lses], w_all)


def _mixers_outproj(x2d, proj3, gates, dil32, layer_params, tables, w_out_all, layer, *, tm):
    B, T, _ = proj3.shape
    cmp_pe, cmp_w1, cmp_w2, kc_gain, sinks = layer_params
    n_cmp = (T - CMP_BLOCK) // CMP_STRIDE + 1
    n_slc = T // SLC_BLOCK
    pe = jnp.broadcast_to(cmp_pe.reshape(2, 1, CMP_BLOCK * HEAD_DIM), (2, 8, CMP_BLOCK * HEAD_DIM)).astype(BF16)
    kvc = _compress(dil32, pe, cmp_w1.astype(BF16), cmp_w2.astype(BF16), kc_gain.reshape(1, HEAD_DIM), n_cmp)

    o_cmp, sel = _cmp_select(proj3, kvc, tables["bias_c"], tables["ovl"], n_slc)
    o_slc = _slc_attn(proj3, sel, tables["expand"], tables["bias_slc"])
    o_win = _band_attn(proj3, proj3, proj3, tables["bias_win"], n_kv=A_KV_HEADS, grp=A_GRP,
                       q_col=COL_QA // A_GRP, k_col=COL_KWA, v_col=COL_VWA)
    o_b = _band_attn(proj3, proj3, proj3, tables["bias_b"], n_kv=B_KV_HEADS, grp=B_GRP,
                     q_col=COL_QB // B_GRP, k_col=COL_KB, v_col=COL_VB, sinks=sinks)
    o_cs, lses = [], []
    for gidx, (_, dil) in enumerate(DIL_PAIRS):
        if dil == 1:
            o, lse = _band_attn(proj3, proj3, proj3, tables["bias_c%d" % gidx], n_kv=1, grp=C_GRP,
                                q_col=COL_QC // C_GRP + gidx, k_col=COL_KC + gidx, v_col=COL_VC + gidx,
                                with_lse=True)
            o_cs += [(o.reshape(B * T, -1), h) for h in range(C_GRP)]
            lses += [(lse.reshape(B * T, -1), h) for h in range(C_GRP)]
        else:
            os_, ls_ = _dil_attn(dil32, tables["bias_c%d" % gidx], col=DIL_F32_SLOT[COL_QC + C_GRP * gidx], dil=dil)
            o_cs += [(o.reshape(B * T, -1), 0) for o in os_]
            lses += [(lse.reshape(B * T, -1), 0) for lse in ls_]
    return _mix_outproj(x2d, gates, o_cmp.reshape(B * T, -1), o_slc.reshape(B * T, -1), o_win.reshape(B * T, -1),
                        o_b.reshape(B * T, -1), o_cs, lses, w_out_all, layer, tm=tm)


def _build_tables(rel_bias, T):
    n_cmp = (T - CMP_BLOCK) // CMP_STRIDE + 1
    n_slc = T // SLC_BLOCK
    ncp = T // CMP_STRIDE
    nq = T // QBLK
    tables = {}
    tables["bias_c"] = _cmp_bias(rel_bias, A_Q_HEADS, nq, ncp, n_cmp).reshape(A_Q_HEADS, T, ncp)
    n_sat = -(-(SAT_DIST + QBLK - 1) // QBLK) + 1
    tables["bias_slc"] = _bias_tiles(rel_bias, 0, A_Q_HEADS, min(n_sat, nq) + 1, koff=QBLK)
    win = NSA_WINDOW - 1
    tables["bias_win"] = _bias_tiles(rel_bias, 0, A_Q_HEADS, -(-win // QBLK) + 2, max_dist=win)
    swa = SWA_WINDOW - 1
    tables["bias_b"] = _bias_tiles(rel_bias, A_Q_HEADS, B_Q_HEADS, -(-swa // QBLK) + 2, max_dist=swa)
    for gidx, (w, dil) in enumerate(DIL_PAIRS):
        md = w // dil
        tables["bias_c%d" % gidx] = _bias_tiles(rel_bias, A_Q_HEADS + B_Q_HEADS + C_GRP * gidx, C_GRP,
                                                -(-md // QBLK) + 2, dscale=dil, max_dist=md)
    c0 = np.arange(ncp)[:, None] * CMP_STRIDE
    s0 = np.arange(LANES)[None, :] * SLC_BLOCK
    ovl = np.clip(np.minimum(c0 + CMP_BLOCK, s0 + SLC_BLOCK) - np.maximum(c0, s0), 0, None) / CMP_BLOCK
    ovl = ovl * (np.arange(ncp)[:, None] < n_cmp) * (np.arange(LANES)[None, :] < n_slc)
    tables["ovl"] = jnp.asarray(ovl, BF16)
    member = (np.arange(T)[None, :] // SLC_BLOCK) == np.arange(LANES)[:, None]
    tables["expand"] = jnp.asarray(member * MASK_WEIGHT, BF16)
    return tables


def _proj_gain(g):
    ones = jnp.ones((HEAD_DIM,), F32)
    spec = [(g[0] * (SCALE * LOG2E), 6), (ones, 4), (g[2], 2), (ones, 2), (g[3], 2), (ones, 2),
            (g[4] * (SCALE * LOG2E), 4), (g[5], 2), (ones, 2), (g[6] * (SCALE * LOG2E), 6), (g[7], 3), (ones, 3)]
    assert sum(n for _, n in spec) == N_MAIN_BLOCKS
    return jnp.concatenate([jnp.tile(v, n) for v, n in spec]).reshape(1, N_MAIN)


def kernel(x, norm_attn, w_in, qk_gain, cmp_pe, cmp_w1, cmp_w2, sinks, rel_bias, w_out, norm_ffn, w_gate, w_up, w_down):
    B, T, D = x.shape
    depth = w_in.shape[0]
    tables = _build_tables(rel_bias, T)
    x2 = x.reshape(B * T, D)
    w_all = _wprep(jnp.transpose(w_in, (0, 2, 1)).astype(BF16))
    w_out_b = w_out.astype(BF16)
    tm = min(ROW_TILE, B * T)
    for l in range(depth):
        proj, gates, dil32 = _proj(x2, norm_attn[l].reshape(1, D), w_all, l, _proj_gain(qk_gain[l]), tm=tm)
        x2 = _mixers_outproj(x2, proj.reshape(B, T, N_MAIN), gates, dil32.reshape(B, T, DIL_F32_COLS),
                             (cmp_pe[l], cmp_w1[l], cmp_w2[l], qk_gain[l][1], sinks[l]), tables, w_out_b, l, tm=tm)
        x2 = _ffn(x2, norm_ffn[l].reshape(1, D), w_gate, w_up, w_down, l, tm=min(FFN_ROW_TILE, B * T), tf=FFN_TILE)
    return x2.reshape(B, T, D)
```

```python
import functools
import math

import numpy as np
import jax
import jax.numpy as jnp
from jax import lax
from jax.experimental import pallas as pl
from jax.experimental.pallas import tpu as pltpu

F32 = jnp.float32
BF16 = jnp.bfloat16

HEAD_DIM = 128
MIXER_OUT = jnp.bfloat16
LANES = 128
SUBLANES = 8
QBLK = 128
BAND_SUBBLOCKS = 8
CMP_SUBBLOCKS = 4
A_Q_HEADS, A_KV_HEADS = 6, 2
A_GRP = A_Q_HEADS // A_KV_HEADS
B_Q_HEADS, B_KV_HEADS = 4, 2
B_GRP = B_Q_HEADS // B_KV_HEADS
DIL_PAIRS = ((128, 1), (512, 4), (2048, 16))
C_GRP = 2
C_Q_HEADS = C_GRP * len(DIL_PAIRS)
CMP_BLOCK, CMP_STRIDE = 32, 16
SLC_BLOCK, SLC_TOPK = 64, 16
SLC_CHUNK = 4
SLC_QTILES = SLC_CHUNK
NSA_WINDOW, SWA_WINDOW = 512, 128
FORCED_SCORE = 1.0e4
REL_BUCKETS, REL_MAX_EXACT, REL_MAX_DIST = 32, 16, 2048
SCALE = HEAD_DIM ** -0.5
LOG2E = math.log2(math.e)
LN2 = math.log(2.0)
MASK_WEIGHT = -2.0 ** 100
EPS = 1e-6
NEG = -1e30
VMEM_LIMIT = 56 * 1024 * 1024

COL_QA, COL_KCA, COL_VCA, COL_KSA, COL_VSA, COL_KWA, COL_VWA = 0, 6, 8, 10, 12, 14, 16
COL_QB, COL_KB, COL_VB, COL_QC, COL_KC, COL_VC = 18, 22, 24, 26, 32, 35
N_MAIN_BLOCKS = 38
N_MAIN = N_MAIN_BLOCKS * LANES
GATE_START = 2304
N_GATES = A_Q_HEADS * 3
PROJ_NORMED = ([True] * 6 + [False] * 4 + [True] * 2 + [False] * 2 + [True] * 2 + [False] * 2
               + [True] * 6 + [False] * 2 + [True] * 9 + [False] * 3)
DIL_F32_SLOT = {}
for _g, (_, _d) in enumerate(DIL_PAIRS):
    if _d > 1:
        _base = 4 * len([1 for _, _dd in DIL_PAIRS[:_g] if _dd > 1])
        DIL_F32_SLOT.update({COL_QC + C_GRP * _g: _base, COL_QC + C_GRP * _g + 1: _base + 1,
                             COL_KC + _g: _base + 2, COL_VC + _g: _base + 3})
CMP_F32_SLOT0 = len(DIL_F32_SLOT)
DIL_F32_SLOT.update({COL_KCA + _i: CMP_F32_SLOT0 + _i for _i in range(2 * A_KV_HEADS)})
DIL_F32_COLS = LANES * len(DIL_F32_SLOT)
DIL_TILES_PER_STEP = 8
ROW_TILE = 512
FFN_ROW_TILE = 1024
FFN_TILE = 256
WPREP_ROWS = 384
PROJ_SUBTILE = 256
MIX_ROW_SPLIT = 4
OUT_SUBTILE = 512


def _bucket_starts():
    d = np.arange(0, 1 << 17)
    out = []
    for dt in (np.float32, np.float64):
        far = np.maximum(d, REL_MAX_EXACT).astype(dt)
        lb = REL_MAX_EXACT + (np.log(far / dt(REL_MAX_EXACT)) / dt(math.log(REL_MAX_DIST / REL_MAX_EXACT))
                              * dt(REL_BUCKETS - REL_MAX_EXACT)).astype(np.int64)
        out.append(np.where(d < REL_MAX_EXACT, d, np.minimum(lb, REL_BUCKETS - 1)))
    assert (out[0] == out[1]).all() and (np.diff(out[0]) >= 0).all()
    return [int(np.argmax(out[0] >= b)) for b in range(REL_BUCKETS)]


BUCKET_START = _bucket_starts()
SAT_DIST = BUCKET_START[REL_BUCKETS - 1]


def _cparams(sem, vmem=VMEM_LIMIT):
    return pltpu.CompilerParams(dimension_semantics=sem, vmem_limit_bytes=vmem)


def _bias_tile_kernel(tab_ref, o_ref, *, head0, koff, dscale, max_dist):
    h = pl.program_id(0) + head0
    r = lax.broadcasted_iota(jnp.int32, (QBLK, LANES), 0)
    c = lax.broadcasted_iota(jnp.int32, (QBLK, LANES), 1)
    for t in range(o_ref.shape[1]):
        dist = t * QBLK + r - c - koff
        d = dist * dscale
        val = jnp.full((QBLK, LANES), tab_ref[REL_BUCKETS - 1, h] * LOG2E, F32)
        for b in range(REL_BUCKETS - 2, -1, -1):
            val = jnp.where(d < BUCKET_START[b + 1], tab_ref[b, h] * LOG2E, val)
        val = jnp.where(dist >= 0, val, NEG)
        if max_dist is not None:
            val = jnp.where(dist <= max_dist, val, NEG)
        o_ref[0, t] = val


def _bias_tiles(rel_bias, head0, nheads, ntiles, *, koff=0, dscale=1, max_dist=None):
    kern = functools.partial(_bias_tile_kernel, head0=head0, koff=koff, dscale=dscale, max_dist=max_dist)
    return pl.pallas_call(
        kern,
        grid=(nheads,),
        in_specs=[pl.BlockSpec(memory_space=pltpu.SMEM)],
        out_specs=pl.BlockSpec((1, ntiles, QBLK, LANES), lambda h: (h, 0, 0, 0)),
        out_shape=jax.ShapeDtypeStruct((nheads, ntiles, QBLK, LANES), F32),
        compiler_params=_cparams(("parallel",)),
        name="bias_tiles",
    )(rel_bias)


def _cmp_bias_kernel(tab_ref, o_ref, *, n_cmp):
    h = pl.program_id(0)
    ntile, _, width = o_ref.shape[1:]
    r = lax.broadcasted_iota(jnp.int32, (QBLK, 2 * width), 0)
    c = lax.broadcasted_iota(jnp.int32, (QBLK, 2 * width), 1)
    dist = r - CMP_STRIDE * (c - width) - (CMP_BLOCK - 1)
    base = jnp.full((QBLK, 2 * width), tab_ref[REL_BUCKETS - 1, h] * LOG2E, F32)
    for b in range(REL_BUCKETS - 2, -1, -1):
        base = jnp.where(dist < BUCKET_START[b + 1], tab_ref[b, h] * LOG2E, base)
    base = jnp.where(dist >= 0, base, NEG)
    col = lax.broadcasted_iota(jnp.int32, (QBLK, width), 1)
    per_tile = QBLK // CMP_STRIDE
    for i in range(ntile):
        tile = pltpu.roll(base, (width + per_tile * i) % (2 * width), 1)[:, :width]
        o_ref[0, i] = jnp.where(col < n_cmp, tile, NEG)


def _cmp_bias(rel_bias, nheads, ntiles, width, n_cmp):
    return pl.pallas_call(
        functools.partial(_cmp_bias_kernel, n_cmp=n_cmp),
        grid=(nheads,),
        in_specs=[pl.BlockSpec(memory_space=pltpu.SMEM)],
        out_specs=pl.BlockSpec((1, ntiles, QBLK, width), lambda h: (h, 0, 0, 0)),
        out_shape=jax.ShapeDtypeStruct((nheads, ntiles, QBLK, width), F32),
        compiler_params=_cparams(("parallel",)),
        name="cmp_bias",
    )(rel_bias)


def _rms(x, w):
    ms = jnp.mean(x * x, axis=-1, keepdims=True)
    return x * lax.rsqrt(ms + EPS) * w


def _wprep_kernel(w_ref, nxt_ref, gate_ref, o_ref):
    j = pl.program_id(1)
    rows = o_ref.shape[1]
    gate_step = GATE_START // rows
    last = pl.num_programs(1) - 1

    @pl.when(j < gate_step)
    def _():
        o_ref[0] = w_ref[0]

    @pl.when((j >= gate_step) & (j < last))
    def _():
        o_ref[0, :rows - N_GATES] = w_ref[0, N_GATES:]
        o_ref[0, rows - N_GATES:] = nxt_ref[0, :N_GATES]

    @pl.when(j == last)
    def _():
        o_ref[0, :rows - LANES] = w_ref[0, N_GATES:N_GATES + rows - LANES]
        o_ref[0, rows - LANES:] = gate_ref[0, :LANES]


def _wprep(wt):
    depth, n_in, d = wt.shape
    rows = WPREP_ROWS
    assert GATE_START % rows == 0 and (N_MAIN + LANES) % rows == 0 and N_MAIN % rows == rows - LANES
    steps = (N_MAIN + LANES) // rows
    return pl.pallas_call(
        _wprep_kernel,
        grid=(depth, steps),
        in_specs=[pl.BlockSpec((1, rows, d), lambda l, j: (l, j, 0)),
                  pl.BlockSpec((1, rows, d), lambda l, j: (l, jnp.minimum(j + 1, steps - 1), 0)),
                  pl.BlockSpec((1, rows, d), lambda l, j: (l, GATE_START // rows, 0))],
        out_specs=pl.BlockSpec((1, rows, d), lambda l, j: (l, j, 0)),
        out_shape=jax.ShapeDtypeStruct((depth, N_MAIN + LANES, d), BF16),
        compiler_params=_cparams(("parallel", "parallel")),
        name="w_in_prep",
    )(wt, wt, wt)


def _proj_kernel(x_ref, nw_ref, w_ref, gain_ref, o_ref, gate_ref, dil_ref):
    h = _rms(x_ref[...], nw_ref[...]).astype(BF16)
    for c0 in range(0, N_MAIN, PROJ_SUBTILE):
        width = min(PROJ_SUBTILE, N_MAIN - c0)
        acc = lax.dot_general(h, w_ref[c0:c0 + width, :], (((1,), (1,)), ((), ())), preferred_element_type=F32)
        for c in range(width // LANES):
            blk = c0 // LANES + c
            sl = slice(blk * LANES, (blk + 1) * LANES)
            y = acc[:, c * LANES:(c + 1) * LANES]
            if PROJ_NORMED[blk]:
                ms = jnp.mean(y * y, axis=-1, keepdims=True)
                y = y * lax.rsqrt(ms + EPS) * gain_ref[:, sl]
            o_ref[:, sl] = y.astype(o_ref.dtype)
            if blk in DIL_F32_SLOT:
                dil_ref[:, DIL_F32_SLOT[blk] * LANES:(DIL_F32_SLOT[blk] + 1) * LANES] = y
    gate_ref[...] = lax.dot_general(h, w_ref[N_MAIN:, :], (((1,), (1,)), ((), ())), preferred_element_type=F32)


def _proj(x2d, norm_w, w_all, layer, gain, *, tm):
    m, d = x2d.shape
    n = w_all.shape[1]
    return pl.pallas_call(
        _proj_kernel,
        grid=(m // tm,),
        in_specs=[
            pl.BlockSpec((tm, d), lambda i: (i, 0)),
            pl.BlockSpec((1, d), lambda i: (0, 0)),
            pl.BlockSpec((None, n, d), lambda i: (layer, 0, 0), pipeline_mode=pl.Buffered(1)),
            pl.BlockSpec((1, N_MAIN), lambda i: (0, 0)),
        ],
        out_specs=(pl.BlockSpec((tm, N_MAIN), lambda i: (i, 0)), pl.BlockSpec((tm, LANES), lambda i: (i, 0)),
                   pl.BlockSpec((tm, DIL_F32_COLS), lambda i: (i, 0))),
        out_shape=(jax.ShapeDtypeStruct((m, N_MAIN), BF16), jax.ShapeDtypeStruct((m, LANES), F32),
                   jax.ShapeDtypeStruct((m, DIL_F32_COLS), F32)),
        compiler_params=_cparams(("parallel",)),
        name="in_proj",
    )(x2d, norm_w, w_all, gain)


def _ffn_kernel(x_ref, nw_ref, wg_ref, wu_ref, wd_ref, o_ref, h_ref):
    f = pl.program_id(1)

    @pl.when(f == 0)
    def _():
        x = x_ref[...]
        h_ref[...] = _rms(x, nw_ref[...]).astype(BF16)
        o_ref[...] = x

    h = h_ref[...]
    g = jnp.dot(h, wg_ref[...].astype(BF16), preferred_element_type=F32)
    u = jnp.dot(h, wu_ref[...].astype(BF16), preferred_element_type=F32)
    a = (g * (1.0 / (1.0 + jnp.exp(-g))) * u).astype(BF16)
    o_ref[...] += jnp.dot(a, wd_ref[...].astype(BF16), preferred_element_type=F32)


def _ffn(x2d, norm_w, wg, wu, wd, layer, *, tm, tf):
    m, d = x2d.shape
    dff = wg.shape[2]
    return pl.pallas_call(
        _ffn_kernel,
        grid=(m // tm, dff // tf),
        in_specs=[
            pl.BlockSpec((tm, d), lambda i, f: (i, 0)),
            pl.BlockSpec((1, d), lambda i, f: (0, 0)),
            pl.BlockSpec((None, d, tf), lambda i, f: (layer, 0, f)),
            pl.BlockSpec((None, d, tf), lambda i, f: (layer, 0, f)),
            pl.BlockSpec((None, tf, d), lambda i, f: (layer, f, 0)),
        ],
        out_specs=pl.BlockSpec((tm, d), lambda i, f: (i, 0)),
        out_shape=jax.ShapeDtypeStruct((m, d), F32),
        scratch_shapes=[pltpu.VMEM((tm, d), BF16)],
        compiler_params=_cparams(("parallel", "arbitrary")),
        name="ffn",
    )(x2d, norm_w, wg, wu, wd)


def _band_kernel(*refs, grp, n_off, nsub, has_sinks, with_lse):
    if has_sinks:
        sink_ref, q_ref, k_ref, v_ref, b_ref = refs[:5]
        outs = refs[5:]
    else:
        q_ref, k_ref, v_ref, b_ref = refs[:4]
        outs = refs[4:]
    o_ref = outs[0]
    g = pl.program_id(1)
    i0 = pl.program_id(2) * nsub
    tiles = {}

    def kv_tile(rel):
        if rel not in tiles:
            start = pl.multiple_of(jnp.maximum(i0 + rel, 0) * QBLK, QBLK)
            tiles[rel] = (k_ref[0, pl.ds(start, QBLK), :], v_ref[0, pl.ds(start, QBLK), :])
        return tiles[rel]

    def head_rows(x, h):
        return x[h * QBLK:(h + 1) * QBLK]

    kvs = [[kv_tile(j - off) for off in range(n_off)] for j in range(nsub)]
    tidxs = [[jnp.where(i0 + j - off >= 0, off, n_off) for off in range(n_off)] for j in range(nsub)]
    scores = []
    for j in range(nsub):
        rs = slice(j * QBLK, (j + 1) * QBLK)
        kcat = jnp.concatenate([t[0] for t in kvs[j]], axis=0)
        q = jnp.concatenate([q_ref[0, rs, h * HEAD_DIM:(h + 1) * HEAD_DIM] for h in range(grp)], axis=0)
        scores.append(lax.dot_general(q, kcat, (((1,), (1,)), ((), ())), preferred_element_type=F32))
    probs, stats = [], []
    for j in range(nsub):
        s = scores[j]
        st = [[head_rows(s, h)[:, off * QBLK:(off + 1) * QBLK] + b_ref[h, tidxs[j][off]] for off in range(n_off)]
              for h in range(grp)]
        m = jnp.max(jnp.concatenate([functools.reduce(jnp.maximum, r) for r in st], axis=0), axis=-1, keepdims=True)
        p = [[jnp.exp2(st[h][off] - head_rows(m, h)) for off in range(n_off)] for h in range(grp)]
        l = jnp.sum(jnp.concatenate([functools.reduce(jnp.add, r) for r in p], axis=0), axis=-1, keepdims=True)
        probs.append(jnp.concatenate([jnp.concatenate([t.astype(BF16) for t in r], axis=1) for r in p], axis=0))
        stats.append((m, l))
    for j in range(nsub):
        rs = slice(j * QBLK, (j + 1) * QBLK)
        m, l = stats[j]
        vcat = jnp.concatenate([t[1] for t in kvs[j]], axis=0)
        o = jnp.dot(probs[j], vcat, preferred_element_type=F32)
        den = l
        if has_sinks:
            sink = jnp.concatenate([jnp.full((QBLK, 1), sink_ref[g * grp + h] * LOG2E, F32) for h in range(grp)], axis=0)
            den = l + jnp.exp2(sink - m)
        o = o / den
        for h in range(grp):
            sl = slice(h * HEAD_DIM, (h + 1) * HEAD_DIM)
            o_ref[0, rs, sl] = head_rows(o, h).astype(o_ref.dtype)
            if with_lse:
                outs[1][0, rs, sl] = jnp.broadcast_to(head_rows(m * LN2 + jnp.log(l), h), (QBLK, HEAD_DIM))


def _band_attn(q_arr, k_arr, v_arr, bias, *, n_kv, grp, q_col, k_col, v_col, sinks=None, with_lse=False):
    n, L, _ = q_arr.shape
    n_off = bias.shape[1] - 1
    gw = grp * HEAD_DIM
    nsub = math.gcd(BAND_SUBBLOCKS, L // QBLK)
    rows = nsub * QBLK
    kern = functools.partial(_band_kernel, grp=grp, n_off=n_off, nsub=nsub, has_sinks=sinks is not None,
                             with_lse=with_lse)
    in_specs = [
        pl.BlockSpec((1, rows, gw), lambda b, g, i: (b, i, q_col + g)),
        pl.BlockSpec((1, L, HEAD_DIM), lambda b, g, i: (b, 0, k_col + g)),
        pl.BlockSpec((1, L, HEAD_DIM), lambda b, g, i: (b, 0, v_col + g)),
        pl.BlockSpec((grp, n_off + 1, QBLK, LANES), lambda b, g, i: (g, 0, 0, 0)),
    ]
    args = [q_arr, k_arr, v_arr, bias]
    if sinks is not None:
        in_specs = [pl.BlockSpec(memory_space=pltpu.SMEM)] + in_specs
        args = [sinks] + args
    o_spec = pl.BlockSpec((1, rows, gw), lambda b, g, i: (b, i, g))
    o_shape = jax.ShapeDtypeStruct((n, L, n_kv * gw), MIXER_OUT)
    lse_shape = jax.ShapeDtypeStruct((n, L, n_kv * gw), F32)
    return pl.pallas_call(
        kern,
        grid=(n, n_kv, L // rows),
        in_specs=in_specs,
        out_specs=(o_spec, o_spec) if with_lse else o_spec,
        out_shape=(o_shape, lse_shape) if with_lse else o_shape,
        compiler_params=_cparams(("parallel", "parallel", "arbitrary")),
        name="band_attn",
    )(*args)


def _dil_kernel(*refs, dil, nres):
    q_refs, (k_ref, v_ref, b_ref) = refs[:C_GRP], refs[C_GRP:C_GRP + 3]
    o_refs, lse_refs = refs[C_GRP + 3:2 * C_GRP + 3], refs[2 * C_GRP + 3:]
    ls = k_ref.shape[1] // dil
    ntile = ls // QBLK
    r0 = pl.program_id(1) * nres
    units = [(rr, j) for rr in range(nres) for j in range(ntile)]

    def rows(rr, j):
        return pl.ds(r0 + rr + dil * QBLK * j, QBLK, stride=dil)

    kv = {u: (k_ref[0, rows(*u), :].astype(BF16), v_ref[0, rows(*u), :].astype(BF16)) for u in units}
    n_off = b_ref.shape[1] - 1
    offs = {(rr, j): [off for off in range(n_off) if j - off >= 0] for rr, j in units}
    scores = []
    for rr, j in units:
        kcat = jnp.concatenate([kv[(rr, j - off)][0] for off in offs[(rr, j)]], axis=0)
        q = jnp.concatenate([q_ref[0, rows(rr, j), :].astype(BF16) for q_ref in q_refs], axis=0)
        scores.append(lax.dot_general(q, kcat, (((1,), (1,)), ((), ())), preferred_element_type=F32))
    probs, stats = [], []
    for u, s in zip(units, scores):
        st = [[s[h * QBLK:(h + 1) * QBLK, n * QBLK:(n + 1) * QBLK] + b_ref[h, off] for n, off in enumerate(offs[u])]
              for h in range(C_GRP)]
        m = jnp.max(jnp.concatenate([functools.reduce(jnp.maximum, r) for r in st], axis=0), axis=-1, keepdims=True)
        p = [[jnp.exp2(t - m[h * QBLK:(h + 1) * QBLK]) for t in r] for h, r in enumerate(st)]
        l = jnp.sum(jnp.concatenate([functools.reduce(jnp.add, r) for r in p], axis=0), axis=-1, keepdims=True)
        probs.append(jnp.concatenate([jnp.concatenate([t.astype(BF16) for t in r], axis=1) for r in p], axis=0))
        stats.append((m, l))
    for (rr, j), pb, (m, l) in zip(units, probs, stats):
        vcat = jnp.concatenate([kv[(rr, j - off)][1] for off in offs[(rr, j)]], axis=0)
        o = jnp.dot(pb, vcat, preferred_element_type=F32) / l
        lse = m * LN2 + jnp.log(l)
        for h in range(C_GRP):
            o_refs[h][0, rows(rr, j), :] = o[h * QBLK:(h + 1) * QBLK]
            lse_refs[h][0, rows(rr, j), :] = jnp.broadcast_to(lse[h * QBLK:(h + 1) * QBLK], (QBLK, HEAD_DIM))


def _dil_attn(x3, bias, *, col, dil):
    B, T, _ = x3.shape
    ntile = T // dil // QBLK
    nres = max(1, min(dil, DIL_TILES_PER_STEP // ntile))
    spec = pl.BlockSpec((1, T, HEAD_DIM), lambda b, r: (b, 0, 0))
    shape = jax.ShapeDtypeStruct((B, T, HEAD_DIM), F32)
    outs = pl.pallas_call(
        functools.partial(_dil_kernel, dil=dil, nres=nres),
        grid=(B, dil // nres),
        in_specs=[pl.BlockSpec((1, T, HEAD_DIM), functools.partial(lambda b, r, c: (b, 0, c), c=col + c))
                  for c in range(C_GRP + 2)]
        + [pl.BlockSpec((C_GRP, bias.shape[1], QBLK, LANES), lambda b, r: (0, 0, 0, 0))],
        out_specs=(spec,) * (2 * C_GRP),
        out_shape=(shape,) * (2 * C_GRP),
        compiler_params=_cparams(("parallel", "arbitrary")),
        name="dilated_attn",
    )(*([x3] * (C_GRP + 2)), bias)
    return outs[:C_GRP], outs[C_GRP:]


def _compress_kernel(x_ref, pe_ref, w1_ref, w2_ref, gain_ref, o_ref, *, n_cmp):
    half = w1_ref.shape[1] // 2
    rows = x_ref.shape[1] // CMP_STRIDE
    xr = jnp.concatenate([x_ref[0, pl.ds(t, rows, stride=CMP_STRIDE), :].astype(BF16) for t in range(CMP_STRIDE)],
                         axis=1)
    y0 = jnp.dot(xr, w1_ref[0, :half], preferred_element_type=F32)
    y1 = jnp.dot(xr, w1_ref[0, half:], preferred_element_type=F32)
    pe = jnp.dot(pe_ref[0], w1_ref[0], preferred_element_type=F32)[0:1]
    c = y0 + pltpu.roll(y1, rows - 1, 0) + pe
    gl = 0.5 * c * (1.0 + jnp.tanh(math.sqrt(2.0 / math.pi) * (c + 0.044715 * (c * c * c))))
    out = jnp.dot(gl.astype(BF16), w2_ref[0], preferred_element_type=F32)
    out = jnp.where(pl.program_id(0) == 0, _rms(out, gain_ref[...]), out)
    valid = lax.broadcasted_iota(jnp.int32, out.shape, 0) < n_cmp
    o_ref[0, 0, 0] = jnp.where(valid, out, 0.0).astype(o_ref.dtype)


def _compress(x3, pe, w1, w2, gain, n_cmp):
    B, T, _ = x3.shape
    rows = T // CMP_STRIDE
    wide = CMP_BLOCK * HEAD_DIM
    return pl.pallas_call(
        functools.partial(_compress_kernel, n_cmp=n_cmp),
        grid=(2, B, A_KV_HEADS),
        in_specs=[
            pl.BlockSpec((1, T, HEAD_DIM), lambda s, b, g: (b, 0, CMP_F32_SLOT0 + s * A_KV_HEADS + g)),
            pl.BlockSpec((1, 8, wide), lambda s, b, g: (s, 0, 0)),
            pl.BlockSpec((1, wide, HEAD_DIM), lambda s, b, g: (s, 0, 0)),
            pl.BlockSpec((1, HEAD_DIM, HEAD_DIM), lambda s, b, g: (s, 0, 0)),
            pl.BlockSpec((1, HEAD_DIM), lambda s, b, g: (0, 0)),
        ],
        out_specs=pl.BlockSpec((1, 1, 1, rows, HEAD_DIM), lambda s, b, g: (s, b, g, 0, 0)),
        out_shape=jax.ShapeDtypeStruct((2, B, A_KV_HEADS, rows, HEAD_DIM), BF16),
        compiler_params=_cparams(("arbitrary", "arbitrary", "arbitrary")),
        name="nsa_compress",
    )(x3, pe, w1, w2, gain)


def _cmp_select_kernel(q_ref, kc_ref, vc_ref, b_ref, ovl_ref, o_ref, sel_ref, rank_ref, *, n_slc, nsub):
    i0 = pl.program_id(2) * nsub
    rows = nsub * QBLK
    kc = kc_ref[0, 0, 0]
    vc = vc_ref[0, 0, 0]
    q_all = jnp.concatenate([q_ref[0, :, h * HEAD_DIM:(h + 1) * HEAD_DIM] for h in range(A_GRP)], axis=0)
    s_all = lax.dot_general(q_all, kc, (((1,), (1,)), ((), ())), preferred_element_type=F32)
    ps = []
    for h in range(A_GRP):
        s = s_all[h * rows:(h + 1) * rows] + b_ref[h]
        m = jnp.max(s, axis=-1, keepdims=True)
        e = jnp.exp2(s - jnp.where(m > 0.5 * NEG, m, 0.0))
        den = jnp.sum(e, axis=-1, keepdims=True)
        ps.append((e / jnp.where(den > 0, den, 1.0)).astype(BF16))
    r_all = jnp.dot(jnp.concatenate(ps, axis=0), jnp.concatenate([vc, ovl_ref[...]], axis=1),
                    preferred_element_type=F32)
    imp = jnp.zeros((rows, LANES), F32)
    for h in range(A_GRP):
        o_ref[0, :, h * HEAD_DIM:(h + 1) * HEAD_DIM] = r_all[h * rows:(h + 1) * rows, :HEAD_DIM].astype(o_ref.dtype)
        imp = imp + r_all[h * rows:(h + 1) * rows, HEAD_DIM:]
    t = i0 * QBLK + lax.broadcasted_iota(jnp.int32, (rows, LANES), 0)
    blk = lax.broadcasted_iota(jnp.int32, (rows, LANES), 1)
    cur = t // SLC_BLOCK
    imp = jnp.where(blk == 0, FORCED_SCORE, imp)
    imp = jnp.where(blk == cur, FORCED_SCORE, imp)
    imp = jnp.where(blk == cur - 1, FORCED_SCORE, imp)
    imp = jnp.where(blk * SLC_BLOCK <= t, imp, NEG)
    imp = jnp.where(blk < n_slc, imp, 2.0 * NEG)
    imp_t = jnp.concatenate([imp[j * QBLK:(j + 1) * QBLK].T for j in range(nsub)], axis=1)
    ngrp = -(-n_slc // SUBLANES)
    cands = [imp_t[r * SUBLANES:(r + 1) * SUBLANES] for r in range(ngrp)]
    row_id = lax.broadcasted_iota(jnp.int32, (SUBLANES, rows), 0)
    rank_ref[...] = jnp.zeros(rank_ref.shape, F32)
    last_blk = ((i0 + nsub) * QBLK - 1) // SLC_BLOCK
    for jg in range(ngrp):
        @pl.when(jg * SUBLANES <= last_blk)
        def _(jg=jg):
            for j in range(jg * SUBLANES, min((jg + 1) * SUBLANES, n_slc)):
                other = imp_t[j:j + 1, :]
                for r in range(ngrp):
                    if r * SUBLANES > j:
                        beats = other >= cands[r]
                    elif (r + 1) * SUBLANES <= j:
                        beats = other > cands[r]
                    else:
                        beats = jnp.where(row_id > j - r * SUBLANES, jnp.where(other >= cands[r], 1.0, 0.0),
                                          jnp.where(other > cands[r], 1.0, 0.0)) > 0.5
                    rs = slice(r * SUBLANES, (r + 1) * SUBLANES)
                    rank_ref[rs, :] = rank_ref[rs, :] + jnp.where(beats, 1.0, 0.0)
    ranks = [rank_ref[r * SUBLANES:(r + 1) * SUBLANES, :] for r in range(ngrp)]
    unsel = [jnp.where(r < float(min(SLC_TOPK, n_slc)), 0.0, 1.0) for r in ranks]
    if ngrp * SUBLANES > n_slc:
        unsel[-1] = jnp.where(row_id < n_slc - (ngrp - 1) * SUBLANES, unsel[-1], 0.0)
    sel_t = jnp.concatenate(unsel, axis=0)
    if ngrp * SUBLANES < LANES:
        sel_t = jnp.concatenate([sel_t, jnp.zeros((LANES - ngrp * SUBLANES, rows), F32)], axis=0)
    for j in range(nsub):
        sel_ref[0, 0, j * QBLK:(j + 1) * QBLK, :] = sel_t[:, j * QBLK:(j + 1) * QBLK].T.astype(sel_ref.dtype)


def _cmp_select(proj3, kvc, bias_c, ovl, n_slc):
    B, T, _ = proj3.shape
    ncp = kvc.shape[3]
    nsub = math.gcd(CMP_SUBBLOCKS, T // QBLK)
    rows = nsub * QBLK
    return pl.pallas_call(
        functools.partial(_cmp_select_kernel, n_slc=n_slc, nsub=nsub),
        grid=(B, A_KV_HEADS, T // rows),
        in_specs=[
            pl.BlockSpec((1, rows, A_GRP * HEAD_DIM), lambda b, g, i: (b, i, g)),
            pl.BlockSpec((1, 1, 1, ncp, HEAD_DIM), lambda b, g, i: (0, b, g, 0, 0)),
            pl.BlockSpec((1, 1, 1, ncp, HEAD_DIM), lambda b, g, i: (1, b, g, 0, 0)),
            pl.BlockSpec((A_GRP, rows, ncp), lambda b, g, i: (g, i, 0)),
            pl.BlockSpec((ncp, LANES), lambda b, g, i: (0, 0)),
        ],
        out_specs=(
            pl.BlockSpec((1, rows, A_GRP * HEAD_DIM), lambda b, g, i: (b, i, g)),
            pl.BlockSpec((1, 1, rows, LANES), lambda b, g, i: (b, g, i, 0)),
        ),
        out_shape=(
            jax.ShapeDtypeStruct((B, T, A_Q_HEADS * HEAD_DIM), MIXER_OUT),
            jax.ShapeDtypeStruct((B, A_KV_HEADS, T, LANES), BF16),
        ),
        scratch_shapes=[pltpu.VMEM((-(-n_slc // SUBLANES) * SUBLANES, rows), F32)],
        compiler_params=_cparams(("parallel", "parallel", "arbitrary")),
        name="nsa_cmp_select",
    )(proj3, kvc, kvc, bias_c, ovl)


def _slc_kernel(q_ref, k_ref, v_ref, sel_ref, e_ref, b_ref, o_ref, *scratch, n_tiles, nq):
    m_refs, l_refs, acc_refs = scratch[0::3], scratch[1::3], scratch[2::3]
    ig = pl.program_id(1)
    for a in range(nq):
        it = ig * nq + a
        rs = slice(a * QBLK, (a + 1) * QBLK)
        qs, sels = [], []
        for g in range(A_KV_HEADS):
            m_refs[g][...] = jnp.full(m_refs[g].shape, NEG, F32)
            l_refs[g][...] = jnp.zeros(l_refs[g].shape, F32)
            acc_refs[g][...] = jnp.zeros(acc_refs[g].shape, F32)
            qs.append(jnp.concatenate([q_ref[0, rs, (g * A_GRP + h) * HEAD_DIM:(g * A_GRP + h + 1) * HEAD_DIM]
                                       for h in range(A_GRP)], axis=0))
            sels.append(sel_ref[0, g, rs, :])

        def chunk(c, ntile, it=it, qs=qs, sels=sels):
            kw = ntile * QBLK
            start = pl.multiple_of(c * (SLC_CHUNK * QBLK), SLC_CHUNK * QBLK)
            tidx = [jnp.clip(it - (c * SLC_CHUNK + j) + 1, 0, n_tiles - 1) for j in range(ntile)]
            kts = [k_ref[0, pl.ds(start, kw), g * HEAD_DIM:(g + 1) * HEAD_DIM] for g in range(A_KV_HEADS)]
            vts = [v_ref[0, pl.ds(start, kw), g * HEAD_DIM:(g + 1) * HEAD_DIM] for g in range(A_KV_HEADS)]
            ech = e_ref[:, pl.ds(start, kw)]
            ss = [lax.dot_general(qs[g], kts[g], (((1,), (1,)), ((), ())), preferred_element_type=F32)
                  for g in range(A_KV_HEADS)]
            madds = [jnp.dot(sels[g], ech, preferred_element_type=F32) for g in range(A_KV_HEADS)]
            pbs, alphas = [], []
            for g in range(A_KV_HEADS):
                rows = [[ss[g][h * QBLK:(h + 1) * QBLK, j * QBLK:(j + 1) * QBLK] + b_ref[g * A_GRP + h, tidx[j]]
                         + madds[g][:, j * QBLK:(j + 1) * QBLK] for j in range(ntile)] for h in range(A_GRP)]
                tile_max = jnp.concatenate([functools.reduce(jnp.maximum, r) for r in rows], axis=0)
                m_old = m_refs[g][...]
                m_new = jnp.maximum(m_old, jnp.max(tile_max, axis=-1, keepdims=True))
                alpha = jnp.exp2(m_old - m_new)
                p = [[jnp.exp2(t - m_new[h * QBLK:(h + 1) * QBLK]) for t in r] for h, r in enumerate(rows)]
                tile_sum = jnp.concatenate([functools.reduce(jnp.add, r) for r in p], axis=0)
                l_refs[g][...] = alpha * l_refs[g][...] + jnp.sum(tile_sum, axis=-1, keepdims=True)
                pbs.append(jnp.concatenate([jnp.concatenate([t.astype(BF16) for t in r], axis=1) for r in p], axis=0))
                alphas.append(alpha)
                m_refs[g][...] = m_new
            for g in range(A_KV_HEADS):
                acc_refs[g][...] = alphas[g] * acc_refs[g][...] + jnp.dot(pbs[g], vts[g], preferred_element_type=F32)

        def body(c, carry, chunk=chunk):
            chunk(c, SLC_CHUNK)
            return carry

        lax.fori_loop(0, ig, body, 0)
        chunk(ig, a + 1)
        for g in range(A_KV_HEADS):
            out = acc_refs[g][...] / l_refs[g][...]
            for h in range(A_GRP):
                col = (g * A_GRP + h) * HEAD_DIM
                o_ref[0, rs, col:col + HEAD_DIM] = out[h * QBLK:(h + 1) * QBLK].astype(o_ref.dtype)


def _slc_attn(proj3, sel, expand, bias):
    B, T, _ = proj3.shape
    n_tiles = bias.shape[1]
    nq = SLC_QTILES
    qrows = nq * QBLK
    assert nq == SLC_CHUNK and T % (SLC_CHUNK * QBLK) == 0
    chain_rows = A_GRP * QBLK
    n_chains = A_KV_HEADS
    return pl.pallas_call(
        functools.partial(_slc_kernel, n_tiles=n_tiles, nq=nq),
        grid=(B, T // qrows),
        in_specs=[
            pl.BlockSpec((1, qrows, A_Q_HEADS * HEAD_DIM), lambda b, i: (b, i, COL_QA // A_Q_HEADS)),
            pl.BlockSpec((1, T, A_KV_HEADS * HEAD_DIM), lambda b, i: (b, 0, COL_KSA // A_KV_HEADS)),
            pl.BlockSpec((1, T, A_KV_HEADS * HEAD_DIM), lambda b, i: (b, 0, COL_VSA // A_KV_HEADS)),
            pl.BlockSpec((1, A_KV_HEADS, qrows, LANES), lambda b, i: (b, 0, i, 0)),
            pl.BlockSpec((LANES, T), lambda b, i: (0, 0)),
            pl.BlockSpec((A_Q_HEADS, n_tiles, QBLK, LANES), lambda b, i: (0, 0, 0, 0)),
        ],
        out_specs=pl.BlockSpec((1, qrows, A_Q_HEADS * HEAD_DIM), lambda b, i: (b, i, 0)),
        out_shape=jax.ShapeDtypeStruct((B, T, A_Q_HEADS * HEAD_DIM), MIXER_OUT),
        scratch_shapes=[pltpu.VMEM((chain_rows, 1), F32), pltpu.VMEM((chain_rows, 1), F32),
                        pltpu.VMEM((chain_rows, HEAD_DIM), F32)] * n_chains,
        compiler_params=_cparams(("parallel", "arbitrary")),
        name="nsa_slc_attn",
    )(proj3, proj3, proj3, sel, expand, bias)


def _mix_outproj_kernel(x_ref, gate_ref, ocmp_ref, oslc_ref, owin_ref, ob_ref, *rest):
    oc_refs, lse_refs = rest[:C_Q_HEADS], rest[C_Q_HEADS:2 * C_Q_HEADS]
    w_ref, o_ref, mix_ref = rest[2 * C_Q_HEADS:]
    tm = x_ref.shape[0]
    nsplit = MIX_ROW_SPLIT
    rows = tm // nsplit
    for part in range(nsplit):
        rs = slice(part * rows, (part + 1) * rows)
        gate = 1.0 / (1.0 + jnp.exp(-gate_ref[rs, :]))
        for h in range(A_Q_HEADS):
            sl = slice(h * HEAD_DIM, (h + 1) * HEAD_DIM)
            o = (gate[:, 3 * h:3 * h + 1] * ocmp_ref[rs, sl] + gate[:, 3 * h + 1:3 * h + 2] * oslc_ref[rs, sl]
                 + gate[:, 3 * h + 2:3 * h + 3] * owin_ref[rs, sl])
            mix_ref[rs, sl] = o.astype(mix_ref.dtype)
        base = A_Q_HEADS * HEAD_DIM
        width = B_Q_HEADS * HEAD_DIM
        mix_ref[rs, base:base + width] = ob_ref[rs, :].astype(mix_ref.dtype)
        base += width
        for hh in range(C_GRP):
            heads = [C_GRP * gidx + hh for gidx in range(len(DIL_PAIRS))]
            lses = [lse_refs[c][rs, :] for c in heads]
            mx = functools.reduce(jnp.maximum, lses)
            ws = [jnp.exp(x - mx) for x in lses]
            tot = functools.reduce(jnp.add, ws)
            for c, w in zip(heads, ws):
                mix_ref[rs, base + c * HEAD_DIM:base + (c + 1) * HEAD_DIM] = (oc_refs[c][rs, :] * (w / tot)).astype(mix_ref.dtype)
    for part in range(nsplit):
        rs = slice(part * rows, (part + 1) * rows)
        a = mix_ref[rs, :]
        for c0 in range(0, o_ref.shape[1], OUT_SUBTILE):
            sl = slice(c0, c0 + OUT_SUBTILE)
            o_ref[rs, sl] = x_ref[rs, sl] + jnp.dot(a, w_ref[:, sl], preferred_element_type=F32)


def _mix_outproj(x2d, gates, o_cmp, o_slc, o_win, o_b, o_cs, lses, w_all, layer, *, tm):
    m, d = x2d.shape
    k = w_all.shape[1]
    row = lambda w: pl.BlockSpec((tm, w), lambda i: (i, 0))
    head = lambda c: pl.BlockSpec((tm, HEAD_DIM), lambda i: (i, c))
    full = [x2d, gates, o_cmp, o_slc, o_win, o_b]
    return pl.pallas_call(
        _mix_outproj_kernel,
        grid=(m // tm,),
        in_specs=[row(a.shape[1]) for a in full] + [head(c) for _, c in o_cs + lses]
        + [pl.BlockSpec((None, k, d), lambda i: (layer, 0, 0), pipeline_mode=pl.Buffered(1))],
        out_specs=row(d),
        out_shape=jax.ShapeDtypeStruct((m, d), F32),
        scratch_shapes=[pltpu.VMEM((tm, k), BF16)],
        compiler_params=_cparams(("parallel",)),
        name="mix_out_proj",
    )(*full, *[a for a, _ in o_cs + lses], w_all)


def _mixers_outproj(x2d, proj3, gates, dil32, layer_params, tables, w_out_all, layer, *, tm):
    B, T, _ = proj3.shape
    cmp_pe, cmp_w1, cmp_w2, kc_gain, sinks = layer_params
    n_cmp = (T - CMP_BLOCK) // CMP_STRIDE + 1
    n_slc = T // SLC_BLOCK
    pe = jnp.broadcast_to(cmp_pe.reshape(2, 1, CMP_BLOCK * HEAD_DIM), (2, 8, CMP_BLOCK * HEAD_DIM)).astype(BF16)
    kvc = _compress(dil32, pe, cmp_w1.astype(BF16), cmp_w2.astype(BF16), kc_gain.reshape(1, HEAD_DIM), n_cmp)

    o_cmp, sel = _cmp_select(proj3, kvc, tables["bias_c"], tables["ovl"], n_slc)
    o_slc = _slc_attn(proj3, sel, tables["expand"], tables["bias_slc"])
    o_win = _band_attn(proj3, proj3, proj3, tables["bias_win"], n_kv=A_KV_HEADS, grp=A_GRP,
                       q_col=COL_QA // A_GRP, k_col=COL_KWA, v_col=COL_VWA)
    o_b = _band_attn(proj3, proj3, proj3, tables["bias_b"], n_kv=B_KV_HEADS, grp=B_GRP,
                     q_col=COL_QB // B_GRP, k_col=COL_KB, v_col=COL_VB, sinks=sinks)
    o_cs, lses = [], []
    for gidx, (_, dil) in enumerate(DIL_PAIRS):
        if dil == 1:
            o, lse = _band_attn(proj3, proj3, proj3, tables["bias_c%d" % gidx], n_kv=1, grp=C_GRP,
                                q_col=COL_QC // C_GRP + gidx, k_col=COL_KC + gidx, v_col=COL_VC + gidx,
                                with_lse=True)
            o_cs += [(o.reshape(B * T, -1), h) for h in range(C_GRP)]
            lses += [(lse.reshape(B * T, -1), h) for h in range(C_GRP)]
        else:
            os_, ls_ = _dil_attn(dil32, tables["bias_c%d" % gidx], col=DIL_F32_SLOT[COL_QC + C_GRP * gidx], dil=dil)
            o_cs += [(o.reshape(B * T, -1), 0) for o in os_]
            lses += [(lse.reshape(B * T, -1), 0) for lse in ls_]
    return _mix_outproj(x2d, gates, o_cmp.reshape(B * T, -1), o_slc.reshape(B * T, -1), o_win.reshape(B * T, -1),
                        o_b.reshape(B * T, -1), o_cs, lses, w_out_all, layer, tm=tm)


def _build_tables(rel_bias, T):
    n_cmp = (T - CMP_BLOCK) // CMP_STRIDE + 1
    n_slc = T // SLC_BLOCK
    ncp = T // CMP_STRIDE
    nq = T // QBLK
    tables = {}
    tables["bias_c"] = _cmp_bias(rel_bias, A_Q_HEADS, nq, ncp, n_cmp).reshape(A_Q_HEADS, T, ncp)
    n_sat = -(-(SAT_DIST + QBLK - 1) // QBLK) + 1
    tables["bias_slc"] = _bias_tiles(rel_bias, 0, A_Q_HEADS, min(n_sat, nq) + 1, koff=QBLK)
    win = NSA_WINDOW - 1
    tables["bias_win"] = _bias_tiles(rel_bias, 0, A_Q_HEADS, -(-win // QBLK) + 2, max_dist=win)
    swa = SWA_WINDOW - 1
    tables["bias_b"] = _bias_tiles(rel_bias, A_Q_HEADS, B_Q_HEADS, -(-swa // QBLK) + 2, max_dist=swa)
    for gidx, (w, dil) in enumerate(DIL_PAIRS):
        md = w // dil
        tables["bias_c%d" % gidx] = _bias_tiles(rel_bias, A_Q_HEADS + B_Q_HEADS + C_GRP * gidx, C_GRP,
                                                -(-md // QBLK) + 2, dscale=dil, max_dist=md)
    c0 = np.arange(ncp)[:, None] * CMP_STRIDE
    s0 = np.arange(LANES)[None, :] * SLC_BLOCK
    ovl = np.clip(np.minimum(c0 + CMP_BLOCK, s0 + SLC_BLOCK) - np.maximum(c0, s0), 0, None) / CMP_BLOCK
    ovl = ovl * (np.arange(ncp)[:, None] < n_cmp) * (np.arange(LANES)[None, :] < n_slc)
    tables["ovl"] = jnp.asarray(ovl, BF16)
    member = (np.arange(T)[None, :] // SLC_BLOCK) == np.arange(LANES)[:, None]
    tables["expand"] = jnp.asarray(member * MASK_WEIGHT, BF16)
    return tables


def _proj_gain(g):
    ones = jnp.ones((HEAD_DIM,), F32)
    spec = [(g[0] * (SCALE * LOG2E), 6), (ones, 4), (g[2], 2), (ones, 2), (g[3], 2), (ones, 2),
            (g[4] * (SCALE * LOG2E), 4), (g[5], 2), (ones, 2), (g[6] * (SCALE * LOG2E), 6), (g[7], 3), (ones, 3)]
    assert sum(n for _, n in spec) == N_MAIN_BLOCKS
    return jnp.concatenate([jnp.tile(v, n) for v, n in spec]).reshape(1, N_MAIN)


def kernel(x, norm_attn, w_in, qk_gain, cmp_pe, cmp_w1, cmp_w2, sinks, rel_bias, w_out, norm_ffn, w_gate, w_up, w_down):
    B, T, D = x.shape
    depth = w_in.shape[0]
    tables = _build_tables(rel_bias, T)
    x2 = x.reshape(B * T, D)
    w_all = _wprep(jnp.transpose(w_in, (0, 2, 1)).astype(BF16))
    w_out_b = w_out.astype(BF16)
    tm = min(ROW_TILE, B * T)
    for l in range(depth):
        proj, gates, dil32 = _proj(x2, norm_attn[l].reshape(1, D), w_all, l, _proj_gain(qk_gain[l]), tm=tm)
        x2 = _mixers_outproj(x2, proj.reshape(B, T, N_MAIN), gates, dil32.reshape(B, T, DIL_F32_COLS),
                             (cmp_pe[l], cmp_w1[l], cmp_w2[l], qk_gain[l][1], sinks[l]), tables, w_out_b, l, tm=tm)
        x2 = _ffn(x2, norm_ffn[l].reshape(1, D), w_gate, w_up, w_down, l, tm=min(FFN_ROW_TILE, B * T), tf=FFN_TILE)
    return x2.reshape(B, T, D)
```

```python
import functools
import math

import numpy as np
import jax
import jax.numpy as jnp
from jax import lax
from jax.experimental import pallas as pl
from jax.experimental.pallas import tpu as pltpu

F32 = jnp.float32
BF16 = jnp.bfloat16

HEAD_DIM = 128
MIXER_OUT = jnp.bfloat16
LANES = 128
SUBLANES = 8
QBLK = 128
BAND_SUBBLOCKS = 8
CMP_SUBBLOCKS = 16
A_Q_HEADS, A_KV_HEADS = 6, 2
A_GRP = A_Q_HEADS // A_KV_HEADS
B_Q_HEADS, B_KV_HEADS = 4, 2
B_GRP = B_Q_HEADS // B_KV_HEADS
DIL_PAIRS = ((128, 1), (512, 4), (2048, 16))
C_GRP = 2
C_Q_HEADS = C_GRP * len(DIL_PAIRS)
CMP_BLOCK, CMP_STRIDE = 32, 16
SLC_BLOCK, SLC_TOPK = 64, 16
SLC_CHUNK = 4
SLC_QTILES = SLC_CHUNK
NSA_WINDOW, SWA_WINDOW = 512, 128
FORCED_SCORE = 1.0e4
REL_BUCKETS, REL_MAX_EXACT, REL_MAX_DIST = 32, 16, 2048
SCALE = HEAD_DIM ** -0.5
LOG2E = math.log2(math.e)
LN2 = math.log(2.0)
MASK_WEIGHT = -2.0 ** 100
EPS = 1e-6
NEG = -1e30
VMEM_LIMIT = 56 * 1024 * 1024

COL_QA, COL_KCA, COL_VCA, COL_KSA, COL_VSA, COL_KWA, COL_VWA = 0, 6, 8, 10, 12, 14, 16
COL_QB, COL_KB, COL_VB, COL_QC, COL_KC, COL_VC = 18, 22, 24, 26, 32, 35
N_MAIN_BLOCKS = 38
N_MAIN = N_MAIN_BLOCKS * LANES
GATE_START = 2304
N_GATES = A_Q_HEADS * 3
PROJ_NORMED = ([True] * 6 + [False] * 4 + [True] * 2 + [False] * 2 + [True] * 2 + [False] * 2
               + [True] * 6 + [False] * 2 + [True] * 9 + [False] * 3)
DIL_F32_SLOT = {}
for _g, (_, _d) in enumerate(DIL_PAIRS):
    if _d > 1:
        _base = 4 * len([1 for _, _dd in DIL_PAIRS[:_g] if _dd > 1])
        DIL_F32_SLOT.update({COL_QC + C_GRP * _g: _base, COL_QC + C_GRP * _g + 1: _base + 1,
                             COL_KC + _g: _base + 2, COL_VC + _g: _base + 3})
CMP_F32_SLOT0 = len(DIL_F32_SLOT)
DIL_F32_SLOT.update({COL_KCA + _i: CMP_F32_SLOT0 + _i for _i in range(2 * A_KV_HEADS)})
DIL_F32_COLS = LANES * len(DIL_F32_SLOT)
DIL_TILES_PER_STEP = 8
ROW_TILE = 512
FFN_ROW_TILE = 1024
FFN_TILE = 256
WPREP_ROWS = 384
PROJ_SUBTILE = 256
MIX_ROW_SPLIT = 4
OUT_SUBTILE = 512


def _bucket_starts():
    d = np.arange(0, 1 << 17)
    out = []
    for dt in (np.float32, np.float64):
        far = np.maximum(d, REL_MAX_EXACT).astype(dt)
        lb = REL_MAX_EXACT + (np.log(far / dt(REL_MAX_EXACT)) / dt(math.log(REL_MAX_DIST / REL_MAX_EXACT))
                              * dt(REL_BUCKETS - REL_MAX_EXACT)).astype(np.int64)
        out.append(np.where(d < REL_MAX_EXACT, d, np.minimum(lb, REL_BUCKETS - 1)))
    assert (out[0] == out[1]).all() and (np.diff(out[0]) >= 0).all()
    return [int(np.argmax(out[0] >= b)) for b in range(REL_BUCKETS)]


BUCKET_START = _bucket_starts()
SAT_DIST = BUCKET_START[REL_BUCKETS - 1]


def _cparams(sem, vmem=VMEM_LIMIT):
    return pltpu.CompilerParams(dimension_semantics=sem, vmem_limit_bytes=vmem)


def _bias_tile_kernel(tab_ref, o_ref, *, head0, koff, dscale, max_dist):
    h = pl.program_id(0) + head0
    r = lax.broadcasted_iota(jnp.int32, (QBLK, LANES), 0)
    c = lax.broadcasted_iota(jnp.int32, (QBLK, LANES), 1)
    for t in range(o_ref.shape[1]):
        dist = t * QBLK + r - c - koff
        d = dist * dscale
        val = jnp.full((QBLK, LANES), tab_ref[REL_BUCKETS - 1, h] * LOG2E, F32)
        for b in range(REL_BUCKETS - 2, -1, -1):
            val = jnp.where(d < BUCKET_START[b + 1], tab_ref[b, h] * LOG2E, val)
        val = jnp.where(dist >= 0, val, NEG)
        if max_dist is not None:
            val = jnp.where(dist <= max_dist, val, NEG)
        o_ref[0, t] = val


def _bias_tiles(rel_bias, head0, nheads, ntiles, *, koff=0, dscale=1, max_dist=None):
    kern = functools.partial(_bias_tile_kernel, head0=head0, koff=koff, dscale=dscale, max_dist=max_dist)
    return pl.pallas_call(
        kern,
        grid=(nheads,),
        in_specs=[pl.BlockSpec(memory_space=pltpu.SMEM)],
        out_specs=pl.BlockSpec((1, ntiles, QBLK, LANES), lambda h: (h, 0, 0, 0)),
        out_shape=jax.ShapeDtypeStruct((nheads, ntiles, QBLK, LANES), F32),
        compiler_params=_cparams(("parallel",)),
        name="bias_tiles",
    )(rel_bias)


def _cmp_bias_kernel(tab_ref, o_ref, *, n_cmp):
    h = pl.program_id(0)
    ntile, _, width = o_ref.shape[1:]
    r = lax.broadcasted_iota(jnp.int32, (QBLK, 2 * width), 0)
    c = lax.broadcasted_iota(jnp.int32, (QBLK, 2 * width), 1)
    dist = r - CMP_STRIDE * (c - width) - (CMP_BLOCK - 1)
    base = jnp.full((QBLK, 2 * width), tab_ref[REL_BUCKETS - 1, h] * LOG2E, F32)
    for b in range(REL_BUCKETS - 2, -1, -1):
        base = jnp.where(dist < BUCKET_START[b + 1], tab_ref[b, h] * LOG2E, base)
    base = jnp.where(dist >= 0, base, NEG)
    col = lax.broadcasted_iota(jnp.int32, (QBLK, width), 1)
    per_tile = QBLK // CMP_STRIDE
    for i in range(ntile):
        tile = pltpu.roll(base, (width + per_tile * i) % (2 * width), 1)[:, :width]
        o_ref[0, i] = jnp.where(col < n_cmp, tile, NEG)


def _cmp_bias(rel_bias, nheads, ntiles, width, n_cmp):
    return pl.pallas_call(
        functools.partial(_cmp_bias_kernel, n_cmp=n_cmp),
        grid=(nheads,),
        in_specs=[pl.BlockSpec(memory_space=pltpu.SMEM)],
        out_specs=pl.BlockSpec((1, ntiles, QBLK, width), lambda h: (h, 0, 0, 0)),
        out_shape=jax.ShapeDtypeStruct((nheads, ntiles, QBLK, width), F32),
        compiler_params=_cparams(("parallel",)),
        name="cmp_bias",
    )(rel_bias)


def _rms(x, w):
    ms = jnp.mean(x * x, axis=-1, keepdims=True)
    return x * lax.rsqrt(ms + EPS) * w


def _wprep_kernel(w_ref, nxt_ref, gate_ref, o_ref):
    j = pl.program_id(1)
    rows = o_ref.shape[1]
    gate_step = GATE_START // rows
    last = pl.num_programs(1) - 1

    @pl.when(j < gate_step)
    def _():
        o_ref[0] = w_ref[0]

    @pl.when((j >= gate_step) & (j < last))
    def _():
        o_ref[0, :rows - N_GATES] = w_ref[0, N_GATES:]
        o_ref[0, rows - N_GATES:] = nxt_ref[0, :N_GATES]

    @pl.when(j == last)
    def _():
        o_ref[0, :rows - LANES] = w_ref[0, N_GATES:N_GATES + rows - LANES]
        o_ref[0, rows - LANES:] = gate_ref[0, :LANES]


def _wprep(wt):
    depth, n_in, d = wt.shape
    rows = WPREP_ROWS
    assert GATE_START % rows == 0 and (N_MAIN + LANES) % rows == 0 and N_MAIN % rows == rows - LANES
    steps = (N_MAIN + LANES) // rows
    return pl.pallas_call(
        _wprep_kernel,
        grid=(depth, steps),
        in_specs=[pl.BlockSpec((1, rows, d), lambda l, j: (l, j, 0)),
                  pl.BlockSpec((1, rows, d), lambda l, j: (l, jnp.minimum(j + 1, steps - 1), 0)),
                  pl.BlockSpec((1, rows, d), lambda l, j: (l, GATE_START // rows, 0))],
        out_specs=pl.BlockSpec((1, rows, d), lambda l, j: (l, j, 0)),
        out_shape=jax.ShapeDtypeStruct((depth, N_MAIN + LANES, d), BF16),
        compiler_params=_cparams(("parallel", "parallel")),
        name="w_in_prep",
    )(wt, wt, wt)


def _proj_kernel(x_ref, nw_ref, w_ref, gain_ref, o_ref, gate_ref, dil_ref):
    h = _rms(x_ref[...], nw_ref[...]).astype(BF16)
    for c0 in range(0, N_MAIN, PROJ_SUBTILE):
        width = min(PROJ_SUBTILE, N_MAIN - c0)
        acc = lax.dot_general(h, w_ref[c0:c0 + width, :], (((1,), (1,)), ((), ())), preferred_element_type=F32)
        for c in range(width // LANES):
            blk = c0 // LANES + c
            sl = slice(blk * LANES, (blk + 1) * LANES)
            y = acc[:, c * LANES:(c + 1) * LANES]
            if PROJ_NORMED[blk]:
                ms = jnp.mean(y * y, axis=-1, keepdims=True)
                y = y * lax.rsqrt(ms + EPS) * gain_ref[:, sl]
            o_ref[:, sl] = y.astype(o_ref.dtype)
            if blk in DIL_F32_SLOT:
                dil_ref[:, DIL_F32_SLOT[blk] * LANES:(DIL_F32_SLOT[blk] + 1) * LANES] = y
    gate_ref[...] = lax.dot_general(h, w_ref[N_MAIN:, :], (((1,), (1,)), ((), ())), preferred_element_type=F32)


def _proj(x2d, norm_w, w_all, layer, gain, *, tm):
    m, d = x2d.shape
    n = w_all.shape[1]
    return pl.pallas_call(
        _proj_kernel,
        grid=(m // tm,),
        in_specs=[
            pl.BlockSpec((tm, d), lambda i: (i, 0)),
            pl.BlockSpec((1, d), lambda i: (0, 0)),
            pl.BlockSpec((None, n, d), lambda i: (layer, 0, 0), pipeline_mode=pl.Buffered(1)),
            pl.BlockSpec((1, N_MAIN), lambda i: (0, 0)),
        ],
        out_specs=(pl.BlockSpec((tm, N_MAIN), lambda i: (i, 0)), pl.BlockSpec((tm, LANES), lambda i: (i, 0)),
                   pl.BlockSpec((tm, DIL_F32_COLS), lambda i: (i, 0))),
        out_shape=(jax.ShapeDtypeStruct((m, N_MAIN), BF16), jax.ShapeDtypeStruct((m, LANES), F32),
                   jax.ShapeDtypeStruct((m, DIL_F32_COLS), F32)),
        compiler_params=_cparams(("parallel",)),
        name="in_proj",
    )(x2d, norm_w, w_all, gain)


def _ffn_kernel(x_ref, nw_ref, wg_ref, wu_ref, wd_ref, o_ref, h_ref):
    f = pl.program_id(1)

    @pl.when(f == 0)
    def _():
        x = x_ref[...]
        h_ref[...] = _rms(x, nw_ref[...]).astype(BF16)
        o_ref[...] = x

    h = h_ref[...]
    g = jnp.dot(h, wg_ref[...].astype(BF16), preferred_element_type=F32)
    u = jnp.dot(h, wu_ref[...].astype(BF16), preferred_element_type=F32)
    a = (g * (1.0 / (1.0 + jnp.exp(-g))) * u).astype(BF16)
    o_ref[...] += jnp.dot(a, wd_ref[...].astype(BF16), preferred_element_type=F32)


def _ffn(x2d, norm_w, wg, wu, wd, layer, *, tm, tf):
    m, d = x2d.shape
    dff = wg.shape[2]
    return pl.pallas_call(
        _ffn_kernel,
        grid=(m // tm, dff // tf),
        in_specs=[
            pl.BlockSpec((tm, d), lambda i, f: (i, 0)),
            pl.BlockSpec((1, d), lambda i, f: (0, 0)),
            pl.BlockSpec((None, d, tf), lambda i, f: (layer, 0, f)),
            pl.BlockSpec((None, d, tf), lambda i, f: (layer, 0, f)),
            pl.BlockSpec((None, tf, d), lambda i, f: (layer, f, 0)),
        ],
        out_specs=pl.BlockSpec((tm, d), lambda i, f: (i, 0)),
        out_shape=jax.ShapeDtypeStruct((m, d), F32),
        scratch_shapes=[pltpu.VMEM((tm, d), BF16)],
        compiler_params=_cparams(("parallel", "arbitrary")),
        name="ffn",
    )(x2d, norm_w, wg, wu, wd)


def _band_kernel(*refs, grp, n_off, nsub, has_sinks, with_lse):
    if has_sinks:
        sink_ref, q_ref, k_ref, v_ref, b_ref = refs[:5]
        outs = refs[5:]
    else:
        q_ref, k_ref, v_ref, b_ref = refs[:4]
        outs = refs[4:]
    o_ref = outs[0]
    g = pl.program_id(1)
    i0 = pl.program_id(2) * nsub
    tiles = {}

    def kv_tile(rel):
        if rel not in tiles:
            start = pl.multiple_of(jnp.maximum(i0 + rel, 0) * QBLK, QBLK)
            tiles[rel] = (k_ref[0, pl.ds(start, QBLK), :], v_ref[0, pl.ds(start, QBLK), :])
        return tiles[rel]

    def head_rows(x, h):
        return x[h * QBLK:(h + 1) * QBLK]

    kvs = [[kv_tile(j - off) for off in range(n_off)] for j in range(nsub)]
    tidxs = [[jnp.where(i0 + j - off >= 0, off, n_off) for off in range(n_off)] for j in range(nsub)]
    scores = []
    for j in range(nsub):
        rs = slice(j * QBLK, (j + 1) * QBLK)
        kcat = jnp.concatenate([t[0] for t in kvs[j]], axis=0)
        q = jnp.concatenate([q_ref[0, rs, h * HEAD_DIM:(h + 1) * HEAD_DIM] for h in range(grp)], axis=0)
        scores.append(lax.dot_general(q, kcat, (((1,), (1,)), ((), ())), preferred_element_type=F32))
    probs, stats = [], []
    for j in range(nsub):
        s = scores[j]
        st = [[head_rows(s, h)[:, off * QBLK:(off + 1) * QBLK] + b_ref[h, tidxs[j][off]] for off in range(n_off)]
              for h in range(grp)]
        m = jnp.max(jnp.concatenate([functools.reduce(jnp.maximum, r) for r in st], axis=0), axis=-1, keepdims=True)
        p = [[jnp.exp2(st[h][off] - head_rows(m, h)) for off in range(n_off)] for h in range(grp)]
        l = jnp.sum(jnp.concatenate([functools.reduce(jnp.add, r) for r in p], axis=0), axis=-1, keepdims=True)
        probs.append(jnp.concatenate([jnp.concatenate([t.astype(BF16) for t in r], axis=1) for r in p], axis=0))
        stats.append((m, l))
    for j in range(nsub):
        rs = slice(j * QBLK, (j + 1) * QBLK)
        m, l = stats[j]
        vcat = jnp.concatenate([t[1] for t in kvs[j]], axis=0)
        o = jnp.dot(probs[j], vcat, preferred_element_type=F32)
        den = l
        if has_sinks:
            sink = jnp.concatenate([jnp.full((QBLK, 1), sink_ref[g * grp + h] * LOG2E, F32) for h in range(grp)], axis=0)
            den = l + jnp.exp2(sink - m)
        o = o / den
        for h in range(grp):
            sl = slice(h * HEAD_DIM, (h + 1) * HEAD_DIM)
            o_ref[0, rs, sl] = head_rows(o, h).astype(o_ref.dtype)
            if with_lse:
                outs[1][0, rs, sl] = jnp.broadcast_to(head_rows(m * LN2 + jnp.log(l), h), (QBLK, HEAD_DIM))


def _band_attn(q_arr, k_arr, v_arr, bias, *, n_kv, grp, q_col, k_col, v_col, sinks=None, with_lse=False):
    n, L, _ = q_arr.shape
    n_off = bias.shape[1] - 1
    gw = grp * HEAD_DIM
    nsub = math.gcd(BAND_SUBBLOCKS, L // QBLK)
    rows = nsub * QBLK
    kern = functools.partial(_band_kernel, grp=grp, n_off=n_off, nsub=nsub, has_sinks=sinks is not None,
                             with_lse=with_lse)
    in_specs = [
        pl.BlockSpec((1, rows, gw), lambda b, g, i: (b, i, q_col + g)),
        pl.BlockSpec((1, L, HEAD_DIM), lambda b, g, i: (b, 0, k_col + g)),
        pl.BlockSpec((1, L, HEAD_DIM), lambda b, g, i: (b, 0, v_col + g)),
        pl.BlockSpec((grp, n_off + 1, QBLK, LANES), lambda b, g, i: (g, 0, 0, 0)),
    ]
    args = [q_arr, k_arr, v_arr, bias]
    if sinks is not None:
        in_specs = [pl.BlockSpec(memory_space=pltpu.SMEM)] + in_specs
        args = [sinks] + args
    o_spec = pl.BlockSpec((1, rows, gw), lambda b, g, i: (b, i, g))
    o_shape = jax.ShapeDtypeStruct((n, L, n_kv * gw), MIXER_OUT)
    lse_shape = jax.ShapeDtypeStruct((n, L, n_kv * gw), F32)
    return pl.pallas_call(
        kern,
        grid=(n, n_kv, L // rows),
        in_specs=in_specs,
        out_specs=(o_spec, o_spec) if with_lse else o_spec,
        out_shape=(o_shape, lse_shape) if with_lse else o_shape,
        compiler_params=_cparams(("parallel", "parallel", "arbitrary")),
        name="band_attn",
    )(*args)


def _dil_kernel(*refs, dil, nres):
    q_refs, (k_ref, v_ref, b_ref) = refs[:C_GRP], refs[C_GRP:C_GRP + 3]
    o_refs, lse_refs = refs[C_GRP + 3:2 * C_GRP + 3], refs[2 * C_GRP + 3:]
    ls = k_ref.shape[1] // dil
    ntile = ls // QBLK
    r0 = pl.program_id(1) * nres
    units = [(rr, j) for rr in range(nres) for j in range(ntile)]

    def rows(rr, j):
        return pl.ds(r0 + rr + dil * QBLK * j, QBLK, stride=dil)

    kv = {u: (k_ref[0, rows(*u), :].astype(BF16), v_ref[0, rows(*u), :].astype(BF16)) for u in units}
    n_off = b_ref.shape[1] - 1
    offs = {(rr, j): [off for off in range(n_off) if j - off >= 0] for rr, j in units}
    scores = []
    for rr, j in units:
        kcat = jnp.concatenate([kv[(rr, j - off)][0] for off in offs[(rr, j)]], axis=0)
        q = jnp.concatenate([q_ref[0, rows(rr, j), :].astype(BF16) for q_ref in q_refs], axis=0)
        scores.append(lax.dot_general(q, kcat, (((1,), (1,)), ((), ())), preferred_element_type=F32))
    probs, stats = [], []
    for u, s in zip(units, scores):
        st = [[s[h * QBLK:(h + 1) * QBLK, n * QBLK:(n + 1) * QBLK] + b_ref[h, off] for n, off in enumerate(offs[u])]
              for h in range(C_GRP)]
        m = jnp.max(jnp.concatenate([functools.reduce(jnp.maximum, r) for r in st], axis=0), axis=-1, keepdims=True)
        p = [[jnp.exp2(t - m[h * QBLK:(h + 1) * QBLK]) for t in r] for h, r in enumerate(st)]
        l = jnp.sum(jnp.concatenate([functools.reduce(jnp.add, r) for r in p], axis=0), axis=-1, keepdims=True)
        probs.append(jnp.concatenate([jnp.concatenate([t.astype(BF16) for t in r], axis=1) for r in p], axis=0))
        stats.append((m, l))
    for (rr, j), pb, (m, l) in zip(units, probs, stats):
        vcat = jnp.concatenate([kv[(rr, j - off)][1] for off in offs[(rr, j)]], axis=0)
        o = jnp.dot(pb, vcat, preferred_element_type=F32) / l
        lse = m * LN2 + jnp.log(l)
        for h in range(C_GRP):
            o_refs[h][0, rows(rr, j), :] = o[h * QBLK:(h + 1) * QBLK]
            lse_refs[h][0, rows(rr, j), :] = jnp.broadcast_to(lse[h * QBLK:(h + 1) * QBLK], (QBLK, HEAD_DIM))


def _dil_attn(x3, bias, *, col, dil):
    B, T, _ = x3.shape
    ntile = T // dil // QBLK
    nres = max(1, min(dil, DIL_TILES_PER_STEP // ntile))
    spec = pl.BlockSpec((1, T, HEAD_DIM), lambda b, r: (b, 0, 0))
    shape = jax.ShapeDtypeStruct((B, T, HEAD_DIM), F32)
    outs = pl.pallas_call(
        functools.partial(_dil_kernel, dil=dil, nres=nres),
        grid=(B, dil // nres),
        in_specs=[pl.BlockSpec((1, T, HEAD_DIM), functools.partial(lambda b, r, c: (b, 0, c), c=col + c))
                  for c in range(C_GRP + 2)]
        + [pl.BlockSpec((C_GRP, bias.shape[1], QBLK, LANES), lambda b, r: (0, 0, 0, 0))],
        out_specs=(spec,) * (2 * C_GRP),
        out_shape=(shape,) * (2 * C_GRP),
        compiler_params=_cparams(("parallel", "arbitrary")),
        name="dilated_attn",
    )(*([x3] * (C_GRP + 2)), bias)
    return outs[:C_GRP], outs[C_GRP:]


def _compress_kernel(x_ref, pe_ref, w1_ref, w2_ref, gain_ref, o_ref, *, n_cmp):
    half = w1_ref.shape[1] // 2
    rows = x_ref.shape[1] // CMP_STRIDE
    xr = jnp.concatenate([x_ref[0, pl.ds(t, rows, stride=CMP_STRIDE), :].astype(BF16) for t in range(CMP_STRIDE)],
                         axis=1)
    y0 = jnp.dot(xr, w1_ref[0, :half], preferred_element_type=F32)
    y1 = jnp.dot(xr, w1_ref[0, half:], preferred_element_type=F32)
    pe = jnp.dot(pe_ref[0], w1_ref[0], preferred_element_type=F32)[0:1]
    c = y0 + pltpu.roll(y1, rows - 1, 0) + pe
    gl = 0.5 * c * (1.0 + jnp.tanh(math.sqrt(2.0 / math.pi) * (c + 0.044715 * (c * c * c))))
    out = jnp.dot(gl.astype(BF16), w2_ref[0], preferred_element_type=F32)
    out = jnp.where(pl.program_id(0) == 0, _rms(out, gain_ref[...]), out)
    valid = lax.broadcasted_iota(jnp.int32, out.shape, 0) < n_cmp
    o_ref[0, 0, 0] = jnp.where(valid, out, 0.0).astype(o_ref.dtype)


def _compress(x3, pe, w1, w2, gain, n_cmp):
    B, T, _ = x3.shape
    rows = T // CMP_STRIDE
    wide = CMP_BLOCK * HEAD_DIM
    return pl.pallas_call(
        functools.partial(_compress_kernel, n_cmp=n_cmp),
        grid=(2, B, A_KV_HEADS),
        in_specs=[
            pl.BlockSpec((1, T, HEAD_DIM), lambda s, b, g: (b, 0, CMP_F32_SLOT0 + s * A_KV_HEADS + g)),
            pl.BlockSpec((1, 8, wide), lambda s, b, g: (s, 0, 0)),
            pl.BlockSpec((1, wide, HEAD_DIM), lambda s, b, g: (s, 0, 0)),
            pl.BlockSpec((1, HEAD_DIM, HEAD_DIM), lambda s, b, g: (s, 0, 0)),
            pl.BlockSpec((1, HEAD_DIM), lambda s, b, g: (0, 0)),
        ],
        out_specs=pl.BlockSpec((1, 1, 1, rows, HEAD_DIM), lambda s, b, g: (s, b, g, 0, 0)),
        out_shape=jax.ShapeDtypeStruct((2, B, A_KV_HEADS, rows, HEAD_DIM), BF16),
        compiler_params=_cparams(("arbitrary", "arbitrary", "arbitrary")),
        name="nsa_compress",
    )(x3, pe, w1, w2, gain)


def _cmp_select_kernel(q_ref, kc_ref, vc_ref, b_ref, ovl_ref, o_ref, sel_ref, rank_ref, *, n_slc, nsub):
    i0 = pl.program_id(2) * nsub
    rows = nsub * QBLK
    kc = kc_ref[0, 0, 0]
    vc = vc_ref[0, 0, 0]
    q_all = jnp.concatenate([q_ref[0, :, h * HEAD_DIM:(h + 1) * HEAD_DIM] for h in range(A_GRP)], axis=0)
    s_all = lax.dot_general(q_all, kc, (((1,), (1,)), ((), ())), preferred_element_type=F32)
    ps = []
    for h in range(A_GRP):
        s = s_all[h * rows:(h + 1) * rows] + b_ref[h]
        m = jnp.max(s, axis=-1, keepdims=True)
        e = jnp.exp2(s - jnp.where(m > 0.5 * NEG, m, 0.0))
        den = jnp.sum(e, axis=-1, keepdims=True)
        ps.append((e / jnp.where(den > 0, den, 1.0)).astype(BF16))
    r_all = jnp.dot(jnp.concatenate(ps, axis=0), jnp.concatenate([vc, ovl_ref[...]], axis=1),
                    preferred_element_type=F32)
    imp = jnp.zeros((rows, LANES), F32)
    for h in range(A_GRP):
        o_ref[0, :, h * HEAD_DIM:(h + 1) * HEAD_DIM] = r_all[h * rows:(h + 1) * rows, :HEAD_DIM].astype(o_ref.dtype)
        imp = imp + r_all[h * rows:(h + 1) * rows, HEAD_DIM:]
    t = i0 * QBLK + lax.broadcasted_iota(jnp.int32, (rows, LANES), 0)
    blk = lax.broadcasted_iota(jnp.int32, (rows, LANES), 1)
    cur = t // SLC_BLOCK
    imp = jnp.where(blk == 0, FORCED_SCORE, imp)
    imp = jnp.where(blk == cur, FORCED_SCORE, imp)
    imp = jnp.where(blk == cur - 1, FORCED_SCORE, imp)
    imp = jnp.where(blk * SLC_BLOCK <= t, imp, NEG)
    imp = jnp.where(blk < n_slc, imp, 2.0 * NEG)
    imp_t = jnp.concatenate([imp[j * QBLK:(j + 1) * QBLK].T for j in range(nsub)], axis=1)
    ngrp = -(-n_slc // SUBLANES)
    cands = [imp_t[r * SUBLANES:(r + 1) * SUBLANES] for r in range(ngrp)]
    row_id = lax.broadcasted_iota(jnp.int32, (SUBLANES, rows), 0)
    rank_ref[...] = jnp.zeros(rank_ref.shape, F32)
    last_blk = ((i0 + nsub) * QBLK - 1) // SLC_BLOCK
    for jg in range(ngrp):
        @pl.when(jg * SUBLANES <= last_blk)
        def _(jg=jg):
            for j in range(jg * SUBLANES, min((jg + 1) * SUBLANES, n_slc)):
                other = imp_t[j:j + 1, :]
                for r in range(ngrp):
                    if r * SUBLANES > j:
                        beats = other >= cands[r]
                    elif (r + 1) * SUBLANES <= j:
                        beats = other > cands[r]
                    else:
                        beats = jnp.where(row_id > j - r * SUBLANES, jnp.where(other >= cands[r], 1.0, 0.0),
                                          jnp.where(other > cands[r], 1.0, 0.0)) > 0.5
                    rs = slice(r * SUBLANES, (r + 1) * SUBLANES)
                    rank_ref[rs, :] = rank_ref[rs, :] + jnp.where(beats, 1.0, 0.0)
    ranks = [rank_ref[r * SUBLANES:(r + 1) * SUBLANES, :] for r in range(ngrp)]
    unsel = [jnp.where(r < float(min(SLC_TOPK, n_slc)), 0.0, 1.0) for r in ranks]
    if ngrp * SUBLANES > n_slc:
        unsel[-1] = jnp.where(row_id < n_slc - (ngrp - 1) * SUBLANES, unsel[-1], 0.0)
    sel_t = jnp.concatenate(unsel, axis=0)
    if ngrp * SUBLANES < LANES:
        sel_t = jnp.concatenate([sel_t, jnp.zeros((LANES - ngrp * SUBLANES, rows), F32)], axis=0)
    for j in range(nsub):
        sel_ref[0, 0, j * QBLK:(j + 1) * QBLK, :] = sel_t[:, j * QBLK:(j + 1) * QBLK].T.astype(sel_ref.dtype)


def _cmp_select(proj3, kvc, bias_c, ovl, n_slc):
    B, T, _ = proj3.shape
    ncp = kvc.shape[3]
    nsub = math.gcd(CMP_SUBBLOCKS, T // QBLK)
    rows = nsub * QBLK
    return pl.pallas_call(
        functools.partial(_cmp_select_kernel, n_slc=n_slc, nsub=nsub),
        grid=(B, A_KV_HEADS, T // rows),
        in_specs=[
            pl.BlockSpec((1, rows, A_GRP * HEAD_DIM), lambda b, g, i: (b, i, g)),
            pl.BlockSpec((1, 1, 1, ncp, HEAD_DIM), lambda b, g, i: (0, b, g, 0, 0)),
            pl.BlockSpec((1, 1, 1, ncp, HEAD_DIM), lambda b, g, i: (1, b, g, 0, 0)),
            pl.BlockSpec((A_GRP, rows, ncp), lambda b, g, i: (g, i, 0)),
            pl.BlockSpec((ncp, LANES), lambda b, g, i: (0, 0)),
        ],
        out_specs=(
            pl.BlockSpec((1, rows, A_GRP * HEAD_DIM), lambda b, g, i: (b, i, g)),
            pl.BlockSpec((1, 1, rows, LANES), lambda b, g, i: (b, g, i, 0)),
        ),
        out_shape=(
            jax.ShapeDtypeStruct((B, T, A_Q_HEADS * HEAD_DIM), MIXER_OUT),
            jax.ShapeDtypeStruct((B, A_KV_HEADS, T, LANES), BF16),
        ),
        scratch_shapes=[pltpu.VMEM((-(-n_slc // SUBLANES) * SUBLANES, rows), F32)],
        compiler_params=_cparams(("parallel", "parallel", "arbitrary")),
        name="nsa_cmp_select",
    )(proj3, kvc, kvc, bias_c, ovl)


def _slc_kernel(q_ref, k_ref, v_ref, sel_ref, e_ref, b_ref, o_ref, *scratch, n_tiles, nq):
    m_refs, l_refs, acc_refs = scratch[0::3], scratch[1::3], scratch[2::3]
    ig = pl.program_id(1)
    for a in range(nq):
        it = ig * nq + a
        rs = slice(a * QBLK, (a + 1) * QBLK)
        qs, sels = [], []
        for g in range(A_KV_HEADS):
            m_refs[g][...] = jnp.full(m_refs[g].shape, NEG, F32)
            l_refs[g][...] = jnp.zeros(l_refs[g].shape, F32)
            acc_refs[g][...] = jnp.zeros(acc_refs[g].shape, F32)
            qs.append(jnp.concatenate([q_ref[0, rs, (g * A_GRP + h) * HEAD_DIM:(g * A_GRP + h + 1) * HEAD_DIM]
                                       for h in range(A_GRP)], axis=0))
            sels.append(sel_ref[0, g, rs, :])

        def chunk(c, ntile, it=it, qs=qs, sels=sels):
            kw = ntile * QBLK
            start = pl.multiple_of(c * (SLC_CHUNK * QBLK), SLC_CHUNK * QBLK)
            tidx = [jnp.clip(it - (c * SLC_CHUNK + j) + 1, 0, n_tiles - 1) for j in range(ntile)]
            kts = [k_ref[0, pl.ds(start, kw), g * HEAD_DIM:(g + 1) * HEAD_DIM] for g in range(A_KV_HEADS)]
            vts = [v_ref[0, pl.ds(start, kw), g * HEAD_DIM:(g + 1) * HEAD_DIM] for g in range(A_KV_HEADS)]
            ech = e_ref[:, pl.ds(start, kw)]
            ss = [lax.dot_general(qs[g], kts[g], (((1,), (1,)), ((), ())), preferred_element_type=F32)
                  for g in range(A_KV_HEADS)]
            madds = [jnp.dot(sels[g], ech, preferred_element_type=F32) for g in range(A_KV_HEADS)]
            pbs, alphas = [], []
            for g in range(A_KV_HEADS):
                rows = [[ss[g][h * QBLK:(h + 1) * QBLK, j * QBLK:(j + 1) * QBLK] + b_ref[g * A_GRP + h, tidx[j]]
                         + madds[g][:, j * QBLK:(j + 1) * QBLK] for j in range(ntile)] for h in range(A_GRP)]
                tile_max = jnp.concatenate([functools.reduce(jnp.maximum, r) for r in rows], axis=0)
                m_old = m_refs[g][...]
                m_new = jnp.maximum(m_old, jnp.max(tile_max, axis=-1, keepdims=True))
                alpha = jnp.exp2(m_old - m_new)
                p = [[jnp.exp2(t - m_new[h * QBLK:(h + 1) * QBLK]) for t in r] for h, r in enumerate(rows)]
                tile_sum = jnp.concatenate([functools.reduce(jnp.add, r) for r in p], axis=0)
                l_refs[g][...] = alpha * l_refs[g][...] + jnp.sum(tile_sum, axis=-1, keepdims=True)
                pbs.append(jnp.concatenate([jnp.concatenate([t.astype(BF16) for t in r], axis=1) for r in p], axis=0))
                alphas.append(alpha)
                m_refs[g][...] = m_new
            for g in range(A_KV_HEADS):
                acc_refs[g][...] = alphas[g] * acc_refs[g][...] + jnp.dot(pbs[g], vts[g], preferred_element_type=F32)

        def body(c, carry, chunk=chunk):
            chunk(c, SLC_CHUNK)
            return carry

        lax.fori_loop(0, ig, body, 0)
        chunk(ig, a + 1)
        for g in range(A_KV_HEADS):
            out = acc_refs[g][...] / l_refs[g][...]
            for h in range(A_GRP):
                col = (g * A_GRP + h) * HEAD_DIM
                o_ref[0, rs, col:col + HEAD_DIM] = out[h * QBLK:(h + 1) * QBLK].astype(o_ref.dtype)


def _slc_attn(proj3, sel, expand, bias):
    B, T, _ = proj3.shape
    n_tiles = bias.shape[1]
    nq = SLC_QTILES
    qrows = nq * QBLK
    assert nq == SLC_CHUNK and T % (SLC_CHUNK * QBLK) == 0
    chain_rows = A_GRP * QBLK
    n_chains = A_KV_HEADS
    return pl.pallas_call(
        functools.partial(_slc_kernel, n_tiles=n_tiles, nq=nq),
        grid=(B, T // qrows),
        in_specs=[
            pl.BlockSpec((1, qrows, A_Q_HEADS * HEAD_DIM), lambda b, i: (b, i, COL_QA // A_Q_HEADS)),
            pl.BlockSpec((1, T, A_KV_HEADS * HEAD_DIM), lambda b, i: (b, 0, COL_KSA // A_KV_HEADS)),
            pl.BlockSpec((1, T, A_KV_HEADS * HEAD_DIM), lambda b, i: (b, 0, COL_VSA // A_KV_HEADS)),
            pl.BlockSpec((1, A_KV_HEADS, qrows, LANES), lambda b, i: (b, 0, i, 0)),
            pl.BlockSpec((LANES, T), lambda b, i: (0, 0)),
            pl.BlockSpec((A_Q_HEADS, n_tiles, QBLK, LANES), lambda b, i: (0, 0, 0, 0)),
        ],
        out_specs=pl.BlockSpec((1, qrows, A_Q_HEADS * HEAD_DIM), lambda b, i: (b, i, 0)),
        out_shape=jax.ShapeDtypeStruct((B, T, A_Q_HEADS * HEAD_DIM), MIXER_OUT),
        scratch_shapes=[pltpu.VMEM((chain_rows, 1), F32), pltpu.VMEM((chain_rows, 1), F32),
                        pltpu.VMEM((chain_rows, HEAD_DIM), F32)] * n_chains,
        compiler_params=_cparams(("parallel", "arbitrary")),
        name="nsa_slc_attn",
    )(proj3, proj3, proj3, sel, expand, bias)


def _mix_outproj_kernel(x_ref, gate_ref, ocmp_ref, oslc_ref, owin_ref, ob_ref, *rest):
    oc_refs, lse_refs = rest[:C_Q_HEADS], rest[C_Q_HEADS:2 * C_Q_HEADS]
    w_ref, o_ref, mix_ref = rest[2 * C_Q_HEADS:]
    tm = x_ref.shape[0]
    nsplit = MIX_ROW_SPLIT
    rows = tm // nsplit
    for part in range(nsplit):
        rs = slice(part * rows, (part + 1) * rows)
        gate = 1.0 / (1.0 + jnp.exp(-gate_ref[rs, :]))
        for h in range(A_Q_HEADS):
            sl = slice(h * HEAD_DIM, (h + 1) * HEAD_DIM)
            o = (gate[:, 3 * h:3 * h + 1] * ocmp_ref[rs, sl] + gate[:, 3 * h + 1:3 * h + 2] * oslc_ref[rs, sl]
                 + gate[:, 3 * h + 2:3 * h + 3] * owin_ref[rs, sl])
            mix_ref[rs, sl] = o.astype(mix_ref.dtype)
        base = A_Q_HEADS * HEAD_DIM
        width = B_Q_HEADS * HEAD_DIM
        mix_ref[rs, base:base + width] = ob_ref[rs, :].astype(mix_ref.dtype)
        base += width
        for hh in range(C_GRP):
            heads = [C_GRP * gidx + hh for gidx in range(len(DIL_PAIRS))]
            lses = [lse_refs[c][rs, :] for c in heads]
            mx = functools.reduce(jnp.maximum, lses)
            ws = [jnp.exp(x - mx) for x in lses]
            tot = functools.reduce(jnp.add, ws)
            for c, w in zip(heads, ws):
                mix_ref[rs, base + c * HEAD_DIM:base + (c + 1) * HEAD_DIM] = (oc_refs[c][rs, :] * (w / tot)).astype(mix_ref.dtype)
    for part in range(nsplit):
        rs = slice(part * rows, (part + 1) * rows)
        a = mix_ref[rs, :]
        for c0 in range(0, o_ref.shape[1], OUT_SUBTILE):
            sl = slice(c0, c0 + OUT_SUBTILE)
            o_ref[rs, sl] = x_ref[rs, sl] + jnp.dot(a, w_ref[:, sl], preferred_element_type=F32)


def _mix_outproj(x2d, gates, o_cmp, o_slc, o_win, o_b, o_cs, lses, w_all, layer, *, tm):
    m, d = x2d.shape
    k = w_all.shape[1]
    row = lambda w: pl.BlockSpec((tm, w), lambda i: (i, 0))
    head = lambda c: pl.BlockSpec((tm, HEAD_DIM), lambda i: (i, c))
    full = [x2d, gates, o_cmp, o_slc, o_win, o_b]
    return pl.pallas_call(
        _mix_outproj_kernel,
        grid=(m // tm,),
        in_specs=[row(a.shape[1]) for a in full] + [head(c) for _, c in o_cs + lses]
        + [pl.BlockSpec((None, k, d), lambda i: (layer, 0, 0), pipeline_mode=pl.Buffered(1))],
        out_specs=row(d),
        out_shape=jax.ShapeDtypeStruct((m, d), F32),
        scratch_shapes=[pltpu.VMEM((tm, k), BF16)],
        compiler_params=_cparams(("parallel",)),
        name="mix_out_proj",
    )(*full, *[a for a, _ in o_cs + lses], w_all)


def _mixers_outproj(x2d, proj3, gates, dil32, layer_params, tables, w_out_all, layer, *, tm):
    B, T, _ = proj3.shape
    cmp_pe, cmp_w1, cmp_w2, kc_gain, sinks = layer_params
    n_cmp = (T - CMP_BLOCK) // CMP_STRIDE + 1
    n_slc = T // SLC_BLOCK
    pe = jnp.broadcast_to(cmp_pe.reshape(2, 1, CMP_BLOCK * HEAD_DIM), (2, 8, CMP_BLOCK * HEAD_DIM)).astype(BF16)
    kvc = _compress(dil32, pe, cmp_w1.astype(BF16), cmp_w2.astype(BF16), kc_gain.reshape(1, HEAD_DIM), n_cmp)

    o_cmp, sel = _cmp_select(proj3, kvc, tables["bias_c"], tables["ovl"], n_slc)
    o_slc = _slc_attn(proj3, sel, tables["expand"], tables["bias_slc"])
    o_win = _band_attn(proj3, proj3, proj3, tables["bias_win"], n_kv=A_KV_HEADS, grp=A_GRP,
                       q_col=COL_QA // A_GRP, k_col=COL_KWA, v_col=COL_VWA)
    o_b = _band_attn(proj3, proj3, proj3, tables["bias_b"], n_kv=B_KV_HEADS, grp=B_GRP,
                     q_col=COL_QB // B_GRP, k_col=COL_KB, v_col=COL_VB, sinks=sinks)
    o_cs, lses = [], []
    for gidx, (_, dil) in enumerate(DIL_PAIRS):
        if dil == 1:
            o, lse = _band_attn(proj3, proj3, proj3, tables["bias_c%d" % gidx], n_kv=1, grp=C_GRP,
                                q_col=COL_QC // C_GRP + gidx, k_col=COL_KC + gidx, v_col=COL_VC + gidx,
                                with_lse=True)
            o_cs += [(o.reshape(B * T, -1), h) for h in range(C_GRP)]
            lses += [(lse.reshape(B * T, -1), h) for h in range(C_GRP)]
        else:
            os_, ls_ = _dil_attn(dil32, tables["bias_c%d" % gidx], col=DIL_F32_SLOT[COL_QC + C_GRP * gidx], dil=dil)
            o_cs += [(o.reshape(B * T, -1), 0) for o in os_]
            lses += [(lse.reshape(B * T, -1), 0) for lse in ls_]
    return _mix_outproj(x2d, gates, o_cmp.reshape(B * T, -1), o_slc.reshape(B * T, -1), o_win.reshape(B * T, -1),
                        o_b.reshape(B * T, -1), o_cs, lses, w_out_all, layer, tm=tm)


def _build_tables(rel_bias, T):
    n_cmp = (T - CMP_BLOCK) // CMP_STRIDE + 1
    n_slc = T // SLC_BLOCK
    ncp = T // CMP_STRIDE
    nq = T // QBLK
    tables = {}
    tables["bias_c"] = _cmp_bias(rel_bias, A_Q_HEADS, nq, ncp, n_cmp).reshape(A_Q_HEADS, T, ncp)
    n_sat = -(-(SAT_DIST + QBLK - 1) // QBLK) + 1
    tables["bias_slc"] = _bias_tiles(rel_bias, 0, A_Q_HEADS, min(n_sat, nq) + 1, koff=QBLK)
    win = NSA_WINDOW - 1
    tables["bias_win"] = _bias_tiles(rel_bias, 0, A_Q_HEADS, -(-win // QBLK) + 2, max_dist=win)
    swa = SWA_WINDOW - 1
    tables["bias_b"] = _bias_tiles(rel_bias, A_Q_HEADS, B_Q_HEADS, -(-swa // QBLK) + 2, max_dist=swa)
    for gidx, (w, dil) in enumerate(DIL_PAIRS):
        md = w // dil
        tables["bias_c%d" % gidx] = _bias_tiles(rel_bias, A_Q_HEADS + B_Q_HEADS + C_GRP * gidx, C_GRP,
                                                -(-md // QBLK) + 2, dscale=dil, max_dist=md)
    c0 = np.arange(ncp)[:, None] * CMP_STRIDE
    s0 = np.arange(LANES)[None, :] * SLC_BLOCK
    ovl = np.clip(np.minimum(c0 + CMP_BLOCK, s0 + SLC_BLOCK) - np.maximum(c0, s0), 0, None) / CMP_BLOCK
    ovl = ovl * (np.arange(ncp)[:, None] < n_cmp) * (np.arange(LANES)[None, :] < n_slc)
    tables["ovl"] = jnp.asarray(ovl, BF16)
    member = (np.arange(T)[None, :] // SLC_BLOCK) == np.arange(LANES)[:, None]
    tables["expand"] = jnp.asarray(member * MASK_WEIGHT, BF16)
    return tables


def _proj_gain(g):
    ones = jnp.ones((HEAD_DIM,), F32)
    spec = [(g[0] * (SCALE * LOG2E), 6), (ones, 4), (g[2], 2), (ones, 2), (g[3], 2), (ones, 2),
            (g[4] * (SCALE * LOG2E), 4), (g[5], 2), (ones, 2), (g[6] * (SCALE * LOG2E), 6), (g[7], 3), (ones, 3)]
    assert sum(n for _, n in spec) == N_MAIN_BLOCKS
    return jnp.concatenate([jnp.tile(v, n) for v, n in spec]).reshape(1, N_MAIN)


def kernel(x, norm_attn, w_in, qk_gain, cmp_pe, cmp_w1, cmp_w2, sinks, rel_bias, w_out, norm_ffn, w_gate, w_up, w_down):
    B, T, D = x.shape
    depth = w_in.shape[0]
    tables = _build_tables(rel_bias, T)
    x2 = x.reshape(B * T, D)
    w_all = _wprep(jnp.transpose(w_in, (0, 2, 1)).astype(BF16))
    w_out_b = w_out.astype(BF16)
    tm = min(ROW_TILE, B * T)
    for l in range(depth):
        proj, gates, dil32 = _proj(x2, norm_attn[l].reshape(1, D), w_all, l, _proj_gain(qk_gain[l]), tm=tm)
        x2 = _mixers_outproj(x2, proj.reshape(B, T, N_MAIN), gates, dil32.reshape(B, T, DIL_F32_COLS),
                             (cmp_pe[l], cmp_w1[l], cmp_w2[l], qk_gain[l][1], sinks[l]), tables, w_out_b, l, tm=tm)
        x2 = _ffn(x2, norm_ffn[l].reshape(1, D), w_gate, w_up, w_down, l, tm=min(FFN_ROW_TILE, B * T), tf=FFN_TILE)
    return x2.reshape(B, T, D)
```
